```python
import math
import jax, jax.numpy as jnp
from jax import lax
import numpy as np

D_MODEL = 1024
BATCH = 8
SEQ = 2048
DEPTH = 1
DEC_BATCH = 8
DEC_SEQ = 64
PAST_LEN = 4096

CHUNK = 64
N_META = 16
SSD_HEADS = 16
SSD_HEAD_DIM = 64
SSD_INNER = SSD_HEADS * SSD_HEAD_DIM
SSD_GROUPS = 4
SSD_STATE = 128
SSD_BLOCK = 64
CONV_WIDTH = 4
CONV_DIM = SSD_INNER + 2 * SSD_GROUPS * SSD_STATE
N_HEADS = 8
N_KV_HEADS = 2
HEAD_DIM = 128
ATT_INNER = N_HEADS * HEAD_DIM
KV_DIM = N_KV_HEADS * HEAD_DIM
IDX_HEADS = 8
IDX_DIM = 64
TOPK_MAX = 256
Q_BLOCK = 128
REL_BUCKETS = 32
REL_MAX_DIST = 128
D_FF = -(-8 * D_MODEL // (3 * 256)) * 256
IN_WIDTHS = (SSD_INNER, CONV_DIM, SSD_HEADS, ATT_INNER, KV_DIM, KV_DIM, IDX_HEADS * IDX_DIM, IDX_DIM, IDX_HEADS, D_MODEL, D_MODEL)
IN_DIM = sum(IN_WIDTHS)
EPS = 1e-6

kernel_name = 'hybrid_ssd_dsa_streaming_step'


def rmsnorm(x, w):
    xf = x.astype(jnp.float32)
    y = xf * lax.rsqrt(jnp.mean(xf * xf, axis=-1, keepdims=True) + EPS)
    return (y * w.astype(jnp.float32)).astype(x.dtype)


def split_columns(h):
    outs = []
    o = 0
    for w in IN_WIDTHS:
        outs.append(h[..., o:o + w])
        o += w
    return outs


def t5_bucket(rel):
    nb = REL_BUCKETS // 2
    max_exact = nb // 2
    ret = jnp.where(rel > 0, nb, 0)
    n = jnp.abs(rel)
    nf = jnp.maximum(n, 1).astype(jnp.float32)
    large = max_exact + (jnp.log(nf / max_exact) / math.log(REL_MAX_DIST / max_exact) * (nb - max_exact)).astype(jnp.int32)
    large = jnp.minimum(large, nb - 1)
    return ret + jnp.where(n < max_exact, n, large)


def causal_conv(xbc, prev, w, b):
    T = xbc.shape[1]
    xpad = jnp.concatenate([prev.astype(xbc.dtype), xbc], axis=1)
    y = b + sum(xpad[:, i:i + T] * w[i] for i in range(CONV_WIDTH))
    return jax.nn.silu(y), xpad[:, xpad.shape[1] - (CONV_WIDTH - 1):]


def ssd_scan(x, dt, a, bm, cm, s0):
    Bsz, T = x.shape[:2]
    pad = (-T) % SSD_BLOCK
    nc = (T + pad) // SSD_BLOCK
    R = SSD_HEADS // SSD_GROUPS

    def blocks(u, tail):
        u = jnp.pad(u.astype(jnp.float32), [(0, 0), (0, pad)] + [(0, 0)] * (u.ndim - 2))
        return u.reshape((Bsz, nc, SSD_BLOCK) + tail)

    x = blocks(x, (SSD_GROUPS, R, SSD_HEAD_DIM))
    dt = blocks(dt, (SSD_GROUPS, R))
    bm = blocks(bm, (SSD_GROUPS, SSD_STATE))
    cm = blocks(cm, (SSD_GROUPS, SSD_STATE))
    acum = jnp.cumsum(dt * a.astype(jnp.float32).reshape(SSD_GROUPS, R), axis=2)
    causal = jnp.tril(jnp.ones((SSD_BLOCK, SSD_BLOCK), bool))[:, :, None, None]
    seg = acum[:, :, :, None] - acum[:, :, None, :]
    lmat = jnp.exp(jnp.where(causal, seg, -jnp.inf))
    cb = jnp.einsum('bcign,bcjgn->bcijg', cm, bm)
    wmat = cb[..., None] * lmat * dt[:, :, None]
    y_diag = jnp.einsum('bcijgr,bcjgrp->bcigrp', wmat, x)
    decay_end = jnp.exp(acum[:, :, -1:] - acum)
    st = jnp.einsum('bcjgn,bcjgrp->bcgrpn', bm, x * (decay_end * dt)[..., None])
    blk_decay = jnp.exp(acum[:, :, -1])

    def step(s, inp):
        st_c, dec_c = inp
        return s * dec_c[..., None, None] + st_c, s

    s0 = s0.astype(jnp.float32).reshape(Bsz, SSD_GROUPS, R, SSD_HEAD_DIM, SSD_STATE)
    s_fin, s_in = lax.scan(step, s0, (jnp.moveaxis(st, 1, 0), jnp.moveaxis(blk_decay, 1, 0)))
    s_in = jnp.moveaxis(s_in, 0, 1)
    y_off = jnp.einsum('bcign,bcgrpn->bcigrp', cm, s_in) * jnp.exp(acum)[..., None]
    y = (y_diag + y_off).reshape(Bsz, nc * SSD_BLOCK, SSD_HEADS, SSD_HEAD_DIM)[:, :T]
    return y, s_fin.reshape(Bsz, SSD_HEADS, SSD_HEAD_DIM, SSD_STATE)


def ssd_mixer(z, xbc, dt_raw, conv_prev, ssm_prev, conv_w, conv_b, dt_bias, a_log, d_skip, norm_w):
    Bsz, T = z.shape[:2]
    xbc, conv_new = causal_conv(xbc, conv_prev, conv_w, conv_b)
    gn = SSD_GROUPS * SSD_STATE
    xs = xbc[..., :SSD_INNER].reshape(Bsz, T, SSD_HEADS, SSD_HEAD_DIM)
    bm = xbc[..., SSD_INNER:SSD_INNER + gn].reshape(Bsz, T, SSD_GROUPS, SSD_STATE)
    cm = xbc[..., SSD_INNER + gn:].reshape(Bsz, T, SSD_GROUPS, SSD_STATE)
    dt = jax.nn.softplus(dt_raw.astype(jnp.float32) + dt_bias.astype(jnp.float32))
    a = -jnp.exp(a_log.astype(jnp.float32))
    y, s_new = ssd_scan(xs, dt, a, bm, cm, ssm_prev)
    y = y + xs.astype(jnp.float32) * d_skip.astype(jnp.float32)[:, None]
    y = y.reshape(Bsz, T, SSD_INNER) * jax.nn.silu(z.astype(jnp.float32))
    return rmsnorm(y, norm_w).astype(z.dtype), conv_new, s_new.astype(ssm_prev.dtype)


def sparse_attn_block(q, qi, wi, qpos, qchunk, k_all, v_all, ki_all, kpos, kchunk, n_sel, rel_bias):
    Bsz, Tq = q.shape[:2]
    R = N_HEADS // N_KV_HEADS
    s_idx = jnp.einsum('bthd,bsd->bths', qi.astype(jnp.float32), ki_all.astype(jnp.float32)) * IDX_DIM ** -0.5
    score = jnp.einsum('bth,bths->bts', wi.astype(jnp.float32), jax.nn.relu(s_idx))
    admissible = kchunk[None, :] <= qchunk[:, None]
    score = jnp.where(admissible[None], score, -jnp.inf)
    _, idx = lax.top_k(score, n_sel)
    take = jax.vmap(lambda rows, ids: rows[ids])
    k_sel = take(k_all, idx)
    v_sel = take(v_all, idx)
    valid = kchunk[idx] <= qchunk[None, :, None]
    bias = rel_bias[t5_bucket(kpos[idx] - qpos[None, :, None])]
    bias = bias.reshape(Bsz, Tq, n_sel, N_KV_HEADS, R).transpose(0, 1, 3, 4, 2)
    qg = q.reshape(Bsz, Tq, N_KV_HEADS, R, HEAD_DIM)
    logits = jnp.einsum('btgrd,btngd->btgrn', qg, k_sel, preferred_element_type=jnp.float32) * HEAD_DIM ** -0.5 + bias.astype(jnp.float32)
    logits = jnp.where(valid[:, :, None, None, :], logits, -1e30)
    p = jax.nn.softmax(logits, axis=-1)
    out = jnp.einsum('btgrn,btngd->btgrd', p.astype(v_sel.dtype), v_sel)
    return out.reshape(Bsz, Tq, ATT_INNER)


def attend(q, qi, wi, qpos, qchunk, k_all, v_all, ki_all, kpos, kchunk, n_sel, rel_bias):
    Bsz, T = q.shape[:2]
    if T <= Q_BLOCK:
        return sparse_attn_block(q, qi, wi, qpos, qchunk, k_all, v_all, ki_all, kpos, kchunk, n_sel, rel_bias)
    pad = (-T) % Q_BLOCK
    nb = (T + pad) // Q_BLOCK

    def blk(u):
        u = jnp.pad(u, [(0, 0), (0, pad)] + [(0, 0)] * (u.ndim - 2))
        return jnp.moveaxis(u.reshape((Bsz, nb, Q_BLOCK) + u.shape[2:]), 1, 0)

    qpos_b = jnp.pad(qpos, (0, pad), mode='edge').reshape(nb, Q_BLOCK)
    qchunk_b = jnp.pad(qchunk, (0, pad), mode='edge').reshape(nb, Q_BLOCK)
    out = lax.map(lambda u: sparse_attn_block(u[0], u[1], u[2], u[3], u[4], k_all, v_all, ki_all, kpos, kchunk, n_sel, rel_bias),
                  (blk(q), blk(qi), blk(wi), qpos_b, qchunk_b))
    return jnp.moveaxis(out, 0, 1).reshape(Bsz, nb * Q_BLOCK, ATT_INNER)[:, :T]


def trunk_layer(x, conv_prev, ssm_prev, k_past, v_past, ki_past, qpos, qchunk, kpos, kchunk, n_sel, rel_bias,
                norm1_w, w_in, conv_w, conv_b, dt_bias, a_log, d_skip, ssd_norm_w, q_norm_w, k_norm_w, idx_k_norm_w,
                w_br_ssd, w_br_att, w_out, norm2_w, w_gate, w_up, w_down):
    Bsz, T, _ = x.shape
    hn = rmsnorm(x, norm1_w)
    z, xbc, dt_raw, q, k, v, qi, ki, wi, g_ssd, g_att = split_columns(hn @ w_in)
    y_ssd, conv_new, ssm_new = ssd_mixer(z, xbc, dt_raw, conv_prev, ssm_prev, conv_w, conv_b, dt_bias, a_log, d_skip, ssd_norm_w)
    q = rmsnorm(q.reshape(Bsz, T, N_HEADS, HEAD_DIM), q_norm_w)
    k = rmsnorm(k.reshape(Bsz, T, N_KV_HEADS, HEAD_DIM), k_norm_w)
    v = v.reshape(Bsz, T, N_KV_HEADS, HEAD_DIM)
    qi = qi.reshape(Bsz, T, IDX_HEADS, IDX_DIM)
    ki = rmsnorm(ki, idx_k_norm_w)
    wi = wi * IDX_HEADS ** -0.5
    if k_past is None:
        k_all, v_all, ki_all = k, v, ki
    else:
        k_all = jnp.concatenate([k_past.astype(k.dtype), k], axis=1)
        v_all = jnp.concatenate([v_past.astype(v.dtype), v], axis=1)
        ki_all = jnp.concatenate([ki_past.astype(ki.dtype), ki], axis=1)
    y_att = attend(q, qi, wi, qpos, qchunk, k_all, v_all, ki_all, kpos, kchunk, n_sel, rel_bias)
    merged = jax.nn.sigmoid(g_ssd) * (y_ssd @ w_br_ssd) + jax.nn.sigmoid(g_att) * (y_att @ w_br_att)
    h = x + merged @ w_out
    hn2 = rmsnorm(h, norm2_w)
    y = h + (jax.nn.silu(hn2 @ w_gate) * (hn2 @ w_up)) @ w_down
    return y, k, v, ki, ssm_new, conv_new


def setup_inputs(seed: int = 0) -> dict:
    key = jax.random.key(seed)
    ks = jax.random.split(key, 32)
    f32 = jnp.float32

    def nrm(k, shape, scale):
        return jax.random.normal(k, shape, f32) * scale

    dt0 = jnp.exp(jax.random.uniform(ks[13], (DEPTH, SSD_HEADS), f32, math.log(1e-3), math.log(1e-1)))
    return {
        'x_prompt': nrm(ks[0], (BATCH, SEQ, D_MODEL), 1.0),
        'x_sample': nrm(ks[1], (DEC_BATCH, DEC_SEQ, D_MODEL), 1.0),
        'cache_k': nrm(ks[2], (DEPTH, DEC_BATCH, PAST_LEN, N_KV_HEADS, HEAD_DIM), 1.0),
        'cache_v': nrm(ks[3], (DEPTH, DEC_BATCH, PAST_LEN, N_KV_HEADS, HEAD_DIM), 1.0),
        'cache_kidx': nrm(ks[4], (DEPTH, DEC_BATCH, PAST_LEN, IDX_DIM), 1.0),
        'state_ssm': nrm(ks[5], (DEPTH, DEC_BATCH, SSD_HEADS, SSD_HEAD_DIM, SSD_STATE), 0.2),
        'state_conv': nrm(ks[6], (DEPTH, DEC_BATCH, CONV_WIDTH - 1, CONV_DIM), 1.0),
        'meta_tokens': nrm(ks[7], (N_META, D_MODEL), 1.0),
        'rel_bias': nrm(ks[8], (REL_BUCKETS, N_HEADS), 0.5),
        'norm1_w': 1.0 + nrm(ks[9], (DEPTH, D_MODEL), 0.02),
        'w_in': nrm(ks[10], (DEPTH, D_MODEL, IN_DIM), D_MODEL ** -0.5),
        'conv_w': nrm(ks[11], (DEPTH, CONV_WIDTH, CONV_DIM), CONV_WIDTH ** -0.5),
        'conv_b': nrm(ks[12], (DEPTH, CONV_DIM), 0.02),
        'dt_bias': dt0 + jnp.log(-jnp.expm1(-dt0)),
        'a_log': jnp.log(jax.random.uniform(ks[14], (DEPTH, SSD_HEADS), f32, 1.0, 16.0)),
        'd_skip': 1.0 + nrm(ks[15], (DEPTH, SSD_HEADS), 0.1),
        'ssd_norm_w': 1.0 + nrm(ks[16], (DEPTH, SSD_INNER), 0.02),
        'q_norm_w': 1.0 + nrm(ks[17], (DEPTH, HEAD_DIM), 0.02),
        'k_norm_w': 1.0 + nrm(ks[18], (DEPTH, HEAD_DIM), 0.02),
        'idx_k_norm_w': 1.0 + nrm(ks[19], (DEPTH, IDX_DIM), 0.02),
        'w_br_ssd': nrm(ks[20], (DEPTH, SSD_INNER, D_MODEL), SSD_INNER ** -0.5),
        'w_br_att': nrm(ks[21], (DEPTH, ATT_INNER, D_MODEL), ATT_INNER ** -0.5),
        'w_out': nrm(ks[22], (DEPTH, D_MODEL, D_MODEL), D_MODEL ** -0.5),
        'norm2_w': 1.0 + nrm(ks[23], (DEPTH, D_MODEL), 0.02),
        'w_gate': nrm(ks[24], (DEPTH, D_MODEL, D_FF), D_MODEL ** -0.5),
        'w_up': nrm(ks[25], (DEPTH, D_MODEL, D_FF), D_MODEL ** -0.5),
        'w_down': nrm(ks[26], (DEPTH, D_FF, D_MODEL), D_FF ** -0.5),
    }


def reference(x_prompt, x_sample, cache_k, cache_v, cache_kidx, state_ssm, state_conv, meta_tokens, rel_bias,
              norm1_w, w_in, conv_w, conv_b, dt_bias, a_log, d_skip, ssd_norm_w, q_norm_w, k_norm_w, idx_k_norm_w,
              w_br_ssd, w_br_att, w_out, norm2_w, w_gate, w_up, w_down):
    bp, sp = x_prompt.shape[:2]
    hp = jnp.concatenate([jnp.broadcast_to(meta_tokens.astype(x_prompt.dtype)[None], (bp, N_META, D_MODEL)), x_prompt], axis=1)
    pos_p = jnp.arange(N_META + sp)
    chunk_p = jnp.where(pos_p < N_META, 0, (pos_p - N_META) // CHUNK + 1)
    n_sel_p = min(TOPK_MAX, sp // 4)
    conv0 = jnp.zeros((bp, CONV_WIDTH - 1, CONV_DIM), x_prompt.dtype)
    ssm0 = jnp.zeros((bp, SSD_HEADS, SSD_HEAD_DIM, SSD_STATE), state_ssm.dtype)
    past = cache_k.shape[2]
    ts = x_sample.shape[1]
    kpos_s = jnp.arange(past + ts)
    kchunk_s = kpos_s // CHUNK
    n_sel_s = min(TOPK_MAX, (past + ts) // 4)
    hs = x_sample
    kp_l, vp_l, kip_l, sp_l, cp_l = [], [], [], [], []
    ks_l, vs_l, kis_l, ss_l, cs_l = [], [], [], [], []
    for l in range(DEPTH):
        lw = (norm1_w[l], w_in[l], conv_w[l], conv_b[l], dt_bias[l], a_log[l], d_skip[l], ssd_norm_w[l],
              q_norm_w[l], k_norm_w[l], idx_k_norm_w[l], w_br_ssd[l], w_br_att[l], w_out[l], norm2_w[l],
              w_gate[l], w_up[l], w_down[l])
        hp, kp, vp, kip, ssp, cvp = trunk_layer(hp, conv0, ssm0, None, None, None, pos_p, chunk_p, pos_p, chunk_p,
                                               n_sel_p, rel_bias, *lw)
        hs, kss, vss, kis, sss, cvs = trunk_layer(hs, state_conv[l], state_ssm[l], cache_k[l], cache_v[l], cache_kidx[l],
                                                 kpos_s[past:], kchunk_s[past:], kpos_s, kchunk_s, n_sel_s, rel_bias, *lw)
        kp_l.append(kp); vp_l.append(vp); kip_l.append(kip); sp_l.append(ssp); cp_l.append(cvp)
        ks_l.append(kss); vs_l.append(vss); kis_l.append(kis); ss_l.append(sss); cs_l.append(cvs)
    y_prompt = hp[:, N_META:]
    y_sample = hs
    k_prompt = jnp.stack(kp_l, 0)
    v_prompt = jnp.stack(vp_l, 0)
    kidx_prompt = jnp.stack(kip_l, 0)
    ssm_prompt = jnp.stack(sp_l, 0)
    conv_prompt = jnp.stack(cp_l, 0)
    k_sample = jnp.stack(ks_l, 0)
    v_sample = jnp.stack(vs_l, 0)
    kidx_sample = jnp.stack(kis_l, 0)
    ssm_sample = jnp.stack(ss_l, 0)
    conv_sample = jnp.stack(cs_l, 0)
    return (y_prompt, y_sample, k_prompt, v_prompt, kidx_prompt, ssm_prompt, conv_prompt, k_sample, v_sample, kidx_sample, ssm_sample, conv_sample)
```

```python
import functools
import math

import jax
import jax.numpy as jnp
from jax import lax
from jax.experimental import pallas as pl
from jax.experimental.pallas import tpu as pltpu

F32 = jnp.float32
BF16 = jnp.bfloat16

D_MODEL = 1024
CHUNK = 64
N_META = 16
SSD_HEADS = 16
SSD_HEAD_DIM = 64
SSD_INNER = SSD_HEADS * SSD_HEAD_DIM
SSD_GROUPS = 4
SSD_STATE = 128
CONV_WIDTH = 4
CONV_DIM = SSD_INNER + 2 * SSD_GROUPS * SSD_STATE
N_HEADS = 8
N_KV_HEADS = 2
HEAD_DIM = 128
ATT_INNER = N_HEADS * HEAD_DIM
KV_DIM = N_KV_HEADS * HEAD_DIM
IDX_HEADS = 8
IDX_DIM = 64
TOPK_MAX = 256
REL_BUCKETS = 32
REL_MAX_DIST = 128
IN_WIDTHS = (SSD_INNER, CONV_DIM, SSD_HEADS, ATT_INNER, KV_DIM, KV_DIM, IDX_HEADS * IDX_DIM, IDX_DIM, IDX_HEADS,
             D_MODEL, D_MODEL)
EPS = 1e-6

LANES = 128
SUBLANES = 8
VMEM_LIMIT_BYTES = 56 * 1024 * 1024

BLK = LANES
ROW_TILE = 256
HALO = SUBLANES

C_Z = 0
C_XBC = C_Z + SSD_INNER
C_Q = C_XBC + CONV_DIM
C_K = C_Q + ATT_INNER
C_V = C_K + KV_DIM
C_QI = C_V + KV_DIM
C_GS = C_QI + IDX_HEADS * IDX_DIM
C_GA = C_GS + D_MODEL
C_SM = C_GA + D_MODEL
IN_PAD = C_SM + LANES
SM_KI = 0
SM_DT = SM_KI + IDX_DIM
SM_WI = SM_DT + SSD_HEADS

BISECT_STEPS = 26
NEG_BIG = -1e30


def _cparams(sem):
    return pltpu.CompilerParams(dimension_semantics=sem, vmem_limit_bytes=VMEM_LIMIT_BYTES)


def _const_spec(shape):
    nd = len(shape)
    return pl.BlockSpec(shape, lambda *_: (0,) * nd)


def _rms(x, w):
    return x * lax.rsqrt(jnp.mean(x * x, axis=-1, keepdims=True) + EPS) * w


def _silu(x):
    return x * jax.nn.sigmoid(x)


def _in_proj_kernel(x_ref, n1_ref, w_ref, qn_ref, kn_ref, kin_ref,
                    z_ref, xbc_ref, q_ref, k_ref, v_ref, kb_ref, vb_ref, qi_ref, sm_ref, smb_ref, gs_ref, ga_ref):
    hn = _rms(x_ref[...], n1_ref[...]).astype(BF16)

    def mm(lo, hi):
        return jnp.dot(hn, w_ref[:, lo:hi], preferred_element_type=F32)

    z_ref[...] = mm(C_Z, C_XBC)
    xbc_ref[...] = mm(C_XBC, C_Q)
    q = mm(C_Q, C_K)
    for h in range(N_HEADS):
        sl = slice(h * HEAD_DIM, (h + 1) * HEAD_DIM)
        q_ref[:, sl] = _rms(q[:, sl], qn_ref[...]).astype(BF16)
    k = mm(C_K, C_V)
    for h in range(N_KV_HEADS):
        sl = slice(h * HEAD_DIM, (h + 1) * HEAD_DIM)
        kh = _rms(k[:, sl], kn_ref[...])
        k_ref[:, sl] = kh
        kb_ref[:, sl] = kh.astype(BF16)
    v = mm(C_V, C_QI)
    v_ref[...] = v
    vb_ref[...] = v.astype(BF16)
    qi_ref[...] = mm(C_QI, C_GS).astype(BF16)
    gs_ref[...] = mm(C_GS, C_GA)
    ga_ref[...] = mm(C_GA, C_SM)
    sm = mm(C_SM, IN_PAD)
    lane = lax.broadcasted_iota(jnp.int32, sm.shape, 1)
    is_ki = lane < SM_KI + IDX_DIM
    ms = jnp.sum(jnp.where(is_ki, sm * sm, 0.0), axis=-1, keepdims=True) * (1.0 / IDX_DIM)
    ki = sm * lax.rsqrt(ms + EPS) * kin_ref[...]
    is_wi = (lane >= SM_WI) & (lane < SM_WI + IDX_HEADS)
    out = jnp.where(is_ki, ki, jnp.where(is_wi, sm * (IDX_HEADS ** -0.5), sm))
    sm_ref[...] = out
    smb_ref[...] = out.astype(BF16)


def _in_proj(x2d, n1, w_perm, qn, kn, kin_pad):
    n = x2d.shape[0]
    tm = ROW_TILE
    row = lambda width: pl.BlockSpec((tm, width), lambda i: (i, 0))
    outs = [
        (SSD_INNER, F32), (CONV_DIM, F32), (ATT_INNER, BF16), (KV_DIM, F32), (KV_DIM, F32), (KV_DIM, BF16),
        (KV_DIM, BF16), (IDX_HEADS * IDX_DIM, BF16), (LANES, F32), (LANES, BF16), (D_MODEL, F32), (D_MODEL, F32),
    ]
    return pl.pallas_call(
        _in_proj_kernel,
        grid=(n // tm,),
        in_specs=[row(D_MODEL), _const_spec((1, D_MODEL)),
                  pl.BlockSpec((D_MODEL, IN_PAD), lambda i: (0, 0), pipeline_mode=pl.Buffered(1)),
                  _const_spec((1, HEAD_DIM)), _const_spec((1, HEAD_DIM)), _const_spec((1, LANES))],
        out_specs=[row(w) for w, _ in outs],
        out_shape=[jax.ShapeDtypeStruct((n, w), dt) for w, dt in outs],
        compiler_params=_cparams(("parallel",)),
        name="in_proj",
    )(x2d, n1, w_perm, qn, kn, kin_pad)


def _softplus(x):
    return jnp.maximum(x, 0.0) + jnp.log1p(jnp.exp(-jnp.abs(x)))


def _ssd_kernel(xbc_ref, z_ref, sm_ref, cprev_ref, sprev_ref, cw_ref, cb_ref, dtb_ref, dtbt_ref, alog_ref, alogt_ref,
                dsk_ref, nw_ref, exp_ref,
                y_ref, snew_ref, cnew_ref,
                s_scr, xpad_scr, xc_scr, y_scr, *, n_chunks, t_valid):
    c = pl.program_id(1)
    q = BLK
    hp = lax.Precision.HIGHEST
    gw = SSD_HEADS // SSD_GROUPS * SSD_HEAD_DIM

    @pl.when(c == 0)
    def _():
        for g in range(SSD_GROUPS):
            s_scr[g] = sprev_ref[g].T
        xpad_scr[0:HALO, :] = cprev_ref[...]

    xpad_scr[HALO:HALO + q, :] = xbc_ref[...]

    slab = 512
    for cc in range(CONV_DIM // slab):
        sl = slice(cc * slab, (cc + 1) * slab)
        acc = xpad_scr[HALO - 3:HALO - 3 + q, sl] * cw_ref[0:1, sl]
        for i in range(1, CONV_WIDTH):
            acc = acc + xpad_scr[HALO - 3 + i:HALO - 3 + i + q, sl] * cw_ref[i:i + 1, sl]
        xc_scr[:, sl] = _silu(cb_ref[:, sl] + acc)

    n_last = t_valid - (n_chunks - 1) * q

    @pl.when(c == n_chunks - 1)
    def _():
        cnew_ref[...] = xpad_scr[n_last:n_last + HALO, :]

    xpad_scr[0:HALO, :] = xpad_scr[q:q + HALO, :]

    sm = sm_ref[...]
    smt = sm.T
    lane = lax.broadcasted_iota(jnp.int32, (q, LANES), 1)
    row = lax.broadcasted_iota(jnp.int32, (q, LANES), 0)
    is_dt = (lane >= SM_DT) & (lane < SM_DT + SSD_HEADS) & (row + c * q < t_valid)
    is_dt_t = (row >= SM_DT) & (row < SM_DT + SSD_HEADS) & (lane + c * q < t_valid)
    dt = jnp.where(is_dt, _softplus(sm + dtb_ref[...]), 0.0)
    dtt = jnp.where(is_dt_t, _softplus(smt + dtbt_ref[...]), 0.0)
    da = dt * (-jnp.exp(alog_ref[...]))
    dat = dtt * (-jnp.exp(alogt_ref[...]))
    ii = lax.broadcasted_iota(jnp.int32, (q, q), 0)
    jj = lax.broadcasted_iota(jnp.int32, (q, q), 1)
    causal = jj <= ii
    tri = causal.astype(F32)
    acum = jnp.dot(tri, da, precision=hp, preferred_element_type=F32)
    acumt = jnp.dot(dat, (ii <= jj).astype(F32), precision=hp, preferred_element_type=F32)
    a_last = acum[q - 1:q, :]
    expand = exp_ref[...]
    ea_x = jnp.dot(jnp.exp(acum), expand, precision=hp, preferred_element_type=F32)
    wdt_x = jnp.dot(jnp.exp(a_last - acum) * dt, expand, precision=hp, preferred_element_type=F32)
    dec_x = jnp.dot(jnp.broadcast_to(jnp.exp(a_last), (SUBLANES, LANES)), expand, precision=hp,
                    preferred_element_type=F32)[0:1, :]

    for g in range(SSD_GROUPS):
        gsl = slice(g * gw, (g + 1) * gw)
        bsl = slice(SSD_INNER + g * SSD_STATE, SSD_INNER + (g + 1) * SSD_STATE)
        csl = slice(SSD_INNER + SSD_GROUPS * SSD_STATE + g * SSD_STATE,
                    SSD_INNER + SSD_GROUPS * SSD_STATE + (g + 1) * SSD_STATE)
        bmf = xc_scr[:, bsl]
        bm = bmf.astype(BF16)
        cm = xc_scr[:, csl].astype(BF16)
        xg = xc_scr[:, gsl]
        xgb = xg.astype(BF16)
        cbm = lax.dot_general(cm, bm, (((1,), (1,)), ((), ())), preferred_element_type=F32)
        xw = (xg * wdt_x[:, gsl]).astype(BF16)
        st = jnp.dot(bmf.T.astype(BF16), xw, preferred_element_type=F32)
        s_in = s_scr[g]
        y_off = jnp.dot(cm, s_in.astype(BF16), preferred_element_type=F32) * ea_x[:, gsl]
        s_scr[g] = s_in * dec_x[:, gsl] + st
        for r in range(SSD_HEADS // SSD_GROUPS):
            h = g * (SSD_HEADS // SSD_GROUPS) + r
            seg = acum[:, SM_DT + h:SM_DT + h + 1] - acumt[SM_DT + h:SM_DT + h + 1, :]
            lmat = jnp.exp(jnp.where(causal, seg, -jnp.inf))
            wmat = (cbm * lmat * dtt[SM_DT + h:SM_DT + h + 1, :]).astype(BF16)
            rsl = slice(r * SSD_HEAD_DIM, (r + 1) * SSD_HEAD_DIM)
            hsl = slice(h * SSD_HEAD_DIM, (h + 1) * SSD_HEAD_DIM)
            y_diag = jnp.dot(wmat, xgb[:, rsl], preferred_element_type=F32)
            y_scr[:, hsl] = y_diag + y_off[:, rsl] + xg[:, rsl] * dsk_ref[:, hsl]

    @pl.when(c == n_chunks - 1)
    def _():
        for g in range(SSD_GROUPS):
            snew_ref[g] = s_scr[g].T

    y = y_scr[...] * _silu(z_ref[...])
    y_ref[...] = _rms(y, nw_ref[...]).astype(BF16)


def _ssd(xbc, z, sm, conv_prev8, ssm_prev, p, t_valid):
    b, tp, _ = xbc.shape
    nc = tp // BLK
    gw = SSD_HEADS // SSD_GROUPS * SSD_HEAD_DIM
    seq = lambda width: pl.BlockSpec((None, BLK, width), lambda i, c: (i, c, 0))
    kern = functools.partial(_ssd_kernel, n_chunks=nc, t_valid=t_valid)
    return pl.pallas_call(
        kern,
        grid=(b, nc),
        in_specs=[seq(CONV_DIM), seq(SSD_INNER), seq(LANES),
                  pl.BlockSpec((None, HALO, CONV_DIM), lambda i, c: (i, 0, 0)),
                  pl.BlockSpec((None, SSD_GROUPS, gw, SSD_STATE), lambda i, c: (i, 0, 0, 0)),
                  _const_spec((CONV_WIDTH, CONV_DIM)), _const_spec((1, CONV_DIM)),
                  _const_spec((1, LANES)), _const_spec((LANES, 1)), _const_spec((1, LANES)), _const_spec((LANES, 1)),
                  _const_spec((1, SSD_INNER)), _const_spec((1, SSD_INNER)), _const_spec((LANES, SSD_INNER))],
        out_specs=[seq(SSD_INNER),
                   pl.BlockSpec((None, SSD_GROUPS, gw, SSD_STATE), lambda i, c: (i, 0, 0, 0)),
                   pl.BlockSpec((None, HALO, CONV_DIM), lambda i, c: (i, 0, 0))],
        out_shape=[jax.ShapeDtypeStruct((b, tp, SSD_INNER), BF16),
                   jax.ShapeDtypeStruct((b, SSD_GROUPS, gw, SSD_STATE), F32),
                   jax.ShapeDtypeStruct((b, HALO, CONV_DIM), F32)],
        scratch_shapes=[pltpu.VMEM((SSD_GROUPS, SSD_STATE, gw), F32),
                        pltpu.VMEM((BLK + 2 * HALO, CONV_DIM), F32),
                        pltpu.VMEM((BLK, CONV_DIM), F32),
                        pltpu.VMEM((BLK, SSD_INNER), F32)],
        compiler_params=_cparams(("parallel", "arbitrary")),
        name="ssd",
    )(xbc, z, sm, conv_prev8, ssm_prev, p["conv_w"], p["conv_b"], p["dtb"], p["dtb_t"], p["alog"], p["alog_t"],
      p["dskip_x"], p["ssd_norm_w"], p["expand"])


N_BIAS_TILES = 5


def _bias_kernel(rb_ref, bt_ref):
    nb = REL_BUCKETS // 2
    max_exact = nb // 2
    qq = lax.broadcasted_iota(jnp.int32, (BLK, BLK), 0)
    kk = lax.broadcasted_iota(jnp.int32, (BLK, BLK), 1)
    for u in range(N_BIAS_TILES):
        rel = (u - 2) * BLK + kk - qq
        n = jnp.abs(rel)
        nf = jnp.maximum(n, 1).astype(F32)
        large = max_exact + (jnp.log(nf / max_exact) / math.log(REL_MAX_DIST / max_exact)
                             * (nb - max_exact)).astype(jnp.int32)
        large = jnp.minimum(large, nb - 1)
        bucket = jnp.where(rel > 0, nb, 0) + jnp.where(n < max_exact, n, large)
        for h in range(N_HEADS):
            acc = jnp.zeros((BLK, BLK), F32)
            for bkt in range(REL_BUCKETS):
                acc = jnp.where(bucket == bkt, rb_ref[bkt, h], acc)
            bt_ref[h * N_BIAS_TILES + u] = acc


def _bias_tiles(rel_bias):
    return pl.pallas_call(
        _bias_kernel,
        in_specs=[pl.BlockSpec(memory_space=pltpu.SMEM)],
        out_specs=pl.BlockSpec(memory_space=pltpu.VMEM),
        out_shape=jax.ShapeDtypeStruct((N_HEADS * N_BIAS_TILES, BLK, BLK), F32),
        name="bias_tiles",
    )(rel_bias)


def _attn_kernel(qn_ref, qi_ref, sm_ref, ki_ref, k_ref, v_ref, bt_ref, o_ref, st_scr, m_scr, lg_scr, *,
                 nkb_total, qblk0, chunk_off, l_valid, n_sel):
    i = pl.program_id(1)
    r = BLK
    qb = qblk0 + i
    q0 = qb * BLK
    k_end = CHUNK * ((q0 + BLK - 1 + chunk_off) // CHUNK + 1) - chunk_off
    nkb = jnp.minimum(nkb_total, (k_end + BLK - 1) // BLK)

    nt = (((1,), (1,)), ((), ()))
    wit = sm_ref[...].T
    qpos = q0 + lax.broadcasted_iota(jnp.int32, (BLK, r), 1)
    qchunk = (qpos + chunk_off) // CHUNK
    krow = lax.broadcasted_iota(jnp.int32, (BLK, r), 0)

    def score_body(j, carry):
        mn, mx, cnt = carry
        kij = ki_ref[j][:, SM_KI:SM_KI + IDX_DIM]
        acc = jnp.zeros((BLK, r), F32)
        for h in range(IDX_HEADS):
            sh = lax.dot_general(kij, qi_ref[:, h * IDX_DIM:(h + 1) * IDX_DIM], nt, preferred_element_type=F32)
            acc = acc + wit[SM_WI + h:SM_WI + h + 1, :] * jnp.maximum(sh, 0.0)
        kpos = j * BLK + krow
        adm = ((kpos + chunk_off) // CHUNK <= qchunk) & (kpos < l_valid)
        s = jnp.where(adm, acc * (IDX_DIM ** -0.5), -jnp.inf)
        st_scr[j] = s
        mn = jnp.minimum(mn, jnp.where(adm, s, jnp.inf))
        mx = jnp.maximum(mx, s)
        cnt = cnt + jnp.where(adm, 1.0, 0.0)
        return mn, mx, cnt

    init = (jnp.full((BLK, r), jnp.inf, F32), jnp.full((BLK, r), -jnp.inf, F32), jnp.zeros((BLK, r), F32))
    mn, mx, cnt = lax.fori_loop(0, nkb, score_body, init)
    lo0 = jnp.min(mn, axis=0, keepdims=True)
    hi0 = jnp.max(mx, axis=0, keepdims=True)
    kk = jnp.minimum(jnp.sum(cnt, axis=0, keepdims=True), float(n_sel))

    def count(pred):
        def body(j, acc):
            return acc + jnp.where(pred(st_scr[j], j), 1.0, 0.0)
        acc = lax.fori_loop(0, nkb, body, jnp.zeros((BLK, r), F32))
        return jnp.sum(acc, axis=0, keepdims=True)

    def bisect(_, carry):
        lo, hi, ub = carry
        mid = 0.5 * lo + 0.5 * hi
        ok = count(lambda s, j: s >= mid) >= kk
        return jnp.where(ok, mid, lo), jnp.where(ok, hi, mid), jnp.where(ok, ub, mid)

    _, _, ub = lax.fori_loop(0, BISECT_STEPS, bisect, (lo0, hi0, jnp.full((1, r), jnp.inf, F32)))

    def next_below(ub):
        def body(j, acc):
            s = st_scr[j]
            return jnp.maximum(acc, jnp.where(s < ub, s, -jnp.inf))
        acc = lax.fori_loop(0, nkb, body, jnp.full((BLK, r), -jnp.inf, F32))
        return jnp.max(acc, axis=0, keepdims=True)

    n_keys = nkb_total * BLK

    def descend_cond(carry):
        _, _, pending, it = carry
        return (pending > 0) & (it < n_keys)

    def descend_body(carry):
        ub, _, _, it = carry
        t = next_below(ub)
        done = count(lambda s, j: s >= t) >= kk
        pending = jnp.sum(jnp.where(done, 0, 1))
        return jnp.where(done, ub, t), t, pending, it + 1

    _, thr, _, _ = lax.while_loop(descend_cond, descend_body, (ub, lo0, jnp.int32(1), jnp.int32(0)))

    n_gt = count(lambda s, j: s > thr)
    n_eq = count(lambda s, j: s == thr)
    need = kk - n_gt

    def tie_search(_, carry):
        lo, hi = carry
        mid = (lo + hi) // 2
        ok = count(lambda s, j: (s == thr) & (j * BLK + krow < mid)) >= need
        return jnp.where(ok, lo, mid), jnp.where(ok, mid, hi)

    tie_steps =max(1, math.ceil(math.log2(n_keys + 1)))
    has_extra_ties = jnp.sum(jnp.where(n_eq > need, 1, 0)) > 0
    key_limit = lax.cond(
        has_extra_ties,
        lambda: lax.fori_loop(0, tie_steps, tie_search,
                              (jnp.zeros((1, r), jnp.int32), jnp.full((1, r), n_keys, jnp.int32)))[1],
        lambda: jnp.full((1, r), n_keys, jnp.int32))

    def mask_body(j, carry):
        s = st_scr[j]
        sel = (s > thr) | ((s == thr) & (j * BLK + krow < key_limit))
        m_scr[j] = jnp.where(sel, 1.0, 0.0).T
        return carry

    lax.fori_loop(0, nkb, mask_body, 0)

    scale = HEAD_DIM ** -0.5
    rep = N_HEADS // N_KV_HEADS
    for h in range(N_HEADS):
        g = h // rep
        hsl = slice(h * HEAD_DIM, (h + 1) * HEAD_DIM)
        gsl = slice(g * HEAD_DIM, (g + 1) * HEAD_DIM)
        qh = qn_ref[:, hsl]

        def logit_body(j, mrun):
            lt = lax.dot_general(qh, k_ref[j, :, gsl], nt, preferred_element_type=F32)
            u = jnp.clip(j - qb + 2, 0, N_BIAS_TILES - 1)
            lg = lt * scale + bt_ref[h * N_BIAS_TILES + u]
            lg = jnp.where(m_scr[j] > 0.0, lg, NEG_BIG)
            lg_scr[j] = lg
            return jnp.maximum(mrun, lg)

        mrun = lax.fori_loop(0, nkb, logit_body, jnp.full((r, BLK), -jnp.inf, F32))
        mrow = jnp.max(mrun, axis=1, keepdims=True)

        def pv_body(j, carry):
            lrun, acc = carry
            e = jnp.exp(lg_scr[j] - mrow)
            acc = acc + jnp.dot(e.astype(BF16), v_ref[j, :, gsl], preferred_element_type=F32)
            return lrun + e, acc

        lrun, acc = lax.fori_loop(0, nkb, pv_body, (jnp.zeros((r, BLK), F32), jnp.zeros((r, HEAD_DIM), F32)))
        o_ref[:, hsl] = (acc / jnp.sum(lrun, axis=1, keepdims=True)).astype(BF16)


def _attn(qn, qi, sm, ki_b, k_b, v_b, bias_tiles, *, qblk0, chunk_off, l_valid, n_sel):
    b, tq, _ = qn.shape
    nkb_total = k_b.shape[1]
    seq = lambda width: pl.BlockSpec((None, BLK, width), lambda bi, i: (bi, i, 0))
    keys = lambda width: pl.BlockSpec((None, nkb_total, BLK, width), lambda bi, i: (bi, 0, 0, 0))
    kern = functools.partial(_attn_kernel, nkb_total=nkb_total, qblk0=qblk0, chunk_off=chunk_off, l_valid=l_valid,
                             n_sel=n_sel)
    return pl.pallas_call(
        kern,
        grid=(b, tq // BLK),
        in_specs=[seq(ATT_INNER), seq(IDX_HEADS * IDX_DIM), seq(LANES), keys(LANES), keys(KV_DIM), keys(KV_DIM),
                  _const_spec((N_HEADS * N_BIAS_TILES, BLK, BLK))],
        out_specs=seq(ATT_INNER),
        out_shape=jax.ShapeDtypeStruct((b, tq, ATT_INNER), BF16),
        scratch_shapes=[pltpu.VMEM((nkb_total, BLK, BLK), F32)] * 3,
        compiler_params=_cparams(("parallel", "arbitrary")),
        name="attn",
    )(qn, qi, sm, ki_b, k_b, v_b, bias_tiles)


def _out_ffn_kernel(x_ref, ys_ref, ya_ref, gs_ref, ga_ref, wbs_ref, wba_ref, wo_ref, n2_ref, wg_ref, wu_ref, wd_ref,
                    y_ref):
    dot = functools.partial(jnp.dot, preferred_element_type=F32)
    merged = (jax.nn.sigmoid(gs_ref[...]) * dot(ys_ref[...], wbs_ref[...])
              + jax.nn.sigmoid(ga_ref[...]) * dot(ya_ref[...], wba_ref[...]))
    h = x_ref[...] + dot(merged.astype(BF16), wo_ref[...])
    hn = _rms(h, n2_ref[...]).astype(BF16)
    act = (_silu(dot(hn, wg_ref[...])) * dot(hn, wu_ref[...])).astype(BF16)
    y_ref[...] = h + dot(act, wd_ref[...])


def _out_ffn(x2d, ys, ya, gs, ga, p):
    n = x2d.shape[0]
    tm = ROW_TILE
    d_ff = p["w_gate"].shape[1]
    row = lambda width: pl.BlockSpec((tm, width), lambda i: (i, 0))
    wspec = lambda shape: pl.BlockSpec(shape, lambda i: (0, 0), pipeline_mode=pl.Buffered(1))
    return pl.pallas_call(
        _out_ffn_kernel,
        grid=(n // tm,),
        in_specs=[row(D_MODEL), row(SSD_INNER), row(ATT_INNER), row(D_MODEL), row(D_MODEL),
                  wspec((SSD_INNER, D_MODEL)), wspec((ATT_INNER, D_MODEL)), wspec((D_MODEL, D_MODEL)),
                  _const_spec((1, D_MODEL)), wspec((D_MODEL, d_ff)), wspec((D_MODEL, d_ff)), wspec((d_ff, D_MODEL))],
        out_specs=row(D_MODEL),
        out_shape=jax.ShapeDtypeStruct((n, D_MODEL), F32),
        compiler_params=_cparams(("parallel",)),
        name="out_ffn",
    )(x2d, ys, ya, gs, ga, p["w_br_ssd"], p["w_br_att"], p["w_out"], p["norm2_w"], p["w_gate"], p["w_up"],
      p["w_down"])


def _layer(x_pad, t_valid, conv_prev, ssm_prev, past, p, bias_tiles, *, qblk0, chunk_off, n_sel):
    b, tp, _ = x_pad.shape
    x2d = x_pad.reshape(b * tp, D_MODEL)
    z, xbc, qn, k32, v32, kb, vb, qi, sm, smb, gs, ga = _in_proj(
        x2d, p["norm1_w"], p["w_in"], p["q_norm_w"], p["k_norm_w"], p["idx_k_norm_w"])
    seq = lambda a: a.reshape(b, tp, a.shape[-1])

    conv_prev8 = jnp.pad(conv_prev.astype(F32), ((0, 0), (HALO - (CONV_WIDTH - 1), 0), (0, 0)))
    gw = SSD_HEADS // SSD_GROUPS * SSD_HEAD_DIM
    y_ssd, ssm_new, conv_new8 = _ssd(seq(xbc), seq(z), seq(sm), conv_prev8,
                                     ssm_prev.astype(F32).reshape(b, SSD_GROUPS, gw, SSD_STATE), p, t_valid)

    kb, vb, kib = seq(kb), seq(vb), seq(smb)
    if past is not None:
        pk, pv, pki = past
        n_past = pk.shape[1]
        kb = jnp.concatenate([pk.reshape(b, n_past, KV_DIM).astype(BF16), kb], axis=1)
        vb = jnp.concatenate([pv.reshape(b, n_past, KV_DIM).astype(BF16), vb], axis=1)
        pki = jnp.pad(pki.astype(BF16), ((0, 0), (0, 0), (SM_KI, LANES - SM_KI - IDX_DIM)))
        kib = jnp.concatenate([pki, kib], axis=1)
    else:
        n_past = 0
    blocks = lambda a: a.reshape(b, a.shape[1] // BLK, BLK, a.shape[-1])
    y_att = _attn(seq(qn), seq(qi), seq(sm), blocks(kib), blocks(kb), blocks(vb), bias_tiles,
                  qblk0=qblk0, chunk_off=chunk_off, l_valid=n_past + t_valid, n_sel=n_sel)

    y = _out_ffn(x2d, y_ssd.reshape(b * tp, SSD_INNER), y_att.reshape(b * tp, ATT_INNER), gs, ga, p)

    k_new = seq(k32)[:, :t_valid].reshape(b, t_valid, N_KV_HEADS, HEAD_DIM)
    v_new = seq(v32)[:, :t_valid].reshape(b, t_valid, N_KV_HEADS, HEAD_DIM)
    ki_new = seq(sm)[:, :t_valid, SM_KI:SM_KI + IDX_DIM]
    ssm_new = ssm_new.reshape(b, SSD_HEADS, SSD_HEAD_DIM, SSD_STATE)
    conv_new = conv_new8[:, HALO - (CONV_WIDTH - 1):]
    return y.reshape(b, tp, D_MODEL), k_new, v_new, ki_new, ssm_new, conv_new


def _prepare_params(l, norm1_w, w_in, conv_w, conv_b, dt_bias, a_log, d_skip, ssd_norm_w, q_norm_w, k_norm_w,
                    idx_k_norm_w, w_br_ssd, w_br_att, w_out, norm2_w, w_gate, w_up, w_down):
    offs = [0]
    for w in IN_WIDTHS:
        offs.append(offs[-1] + w)
    seg = lambda i: w_in[l][:, offs[i]:offs[i + 1]]
    i_z, i_xbc, i_dt, i_q, i_k, i_v, i_qi, i_ki, i_wi, i_gs, i_ga = range(11)
    pad = jnp.zeros((D_MODEL, LANES - IDX_DIM - SSD_HEADS - IDX_HEADS), w_in.dtype)
    w_perm = jnp.concatenate([seg(i_z), seg(i_xbc), seg(i_q), seg(i_k), seg(i_v), seg(i_qi), seg(i_gs), seg(i_ga),
                              seg(i_ki), seg(i_dt), seg(i_wi), pad], axis=1).astype(BF16)

    def lanes_at(vec, start):
        return jnp.zeros((1, LANES), F32).at[0, start:start + vec.shape[0]].set(vec.astype(F32))

    dtb = lanes_at(dt_bias[l], SM_DT)
    alog = lanes_at(a_log[l], SM_DT)
    head_of_channel = jnp.arange(SSD_INNER) // SSD_HEAD_DIM
    expand = (jnp.arange(LANES)[:, None] == head_of_channel[None, :] + SM_DT).astype(F32)
    row = lambda v: v.astype(F32).reshape(1, -1)
    return dict(
        norm1_w=row(norm1_w[l]), w_in=w_perm, conv_w=conv_w[l].astype(F32), conv_b=row(conv_b[l]),
        dtb=dtb, dtb_t=dtb.reshape(LANES, 1), alog=alog, alog_t=alog.reshape(LANES, 1),
        dskip_x=row(jnp.repeat(d_skip[l], SSD_HEAD_DIM)), ssd_norm_w=row(ssd_norm_w[l]), expand=expand,
        q_norm_w=row(q_norm_w[l]), k_norm_w=row(k_norm_w[l]),
        idx_k_norm_w=jnp.ones((1, LANES), F32).at[0, SM_KI:SM_KI + IDX_DIM].set(idx_k_norm_w[l].astype(F32)),
        w_br_ssd=w_br_ssd[l].astype(BF16), w_br_att=w_br_att[l].astype(BF16), w_out=w_out[l].astype(BF16),
        norm2_w=row(norm2_w[l]), w_gate=w_gate[l].astype(BF16), w_up=w_up[l].astype(BF16),
        w_down=w_down[l].astype(BF16))


def _pad_rows(x, tp):
    return jnp.pad(x, ((0, 0), (0, tp - x.shape[1]), (0, 0)))


def kernel(x_prompt, x_sample, cache_k, cache_v, cache_kidx, state_ssm, state_conv, meta_tokens, rel_bias, norm1_w,
           w_in, conv_w, conv_b, dt_bias, a_log, d_skip, ssd_norm_w, q_norm_w, k_norm_w, idx_k_norm_w, w_br_ssd,
           w_br_att, w_out, norm2_w, w_gate, w_up, w_down):
    bp, sp, _ = x_prompt.shape
    bs, ts, _ = x_sample.shape
    depth = w_in.shape[0]
    past = cache_k.shape[2]
    assert past % BLK == 0 and BLK % CHUNK == 0 and N_META <= CHUNK

    tq_p = N_META + sp
    tp_p = -(-tq_p // BLK) * BLK
    tp_s = -(-ts // BLK) * BLK
    n_sel_p = min(TOPK_MAX, sp // 4)
    n_sel_s = min(TOPK_MAX, (past + ts) // 4)

    hp = jnp.concatenate([jnp.broadcast_to(meta_tokens.astype(x_prompt.dtype)[None], (bp, N_META, D_MODEL)),
                          x_prompt], axis=1)
    hp = _pad_rows(hp, tp_p)
    hs = _pad_rows(x_sample, tp_s)
    conv0 = jnp.zeros((bp, CONV_WIDTH - 1, CONV_DIM), F32)
    ssm0 = jnp.zeros((bp, SSD_HEADS, SSD_HEAD_DIM, SSD_STATE), F32)
    bias_tiles = _bias_tiles(rel_bias.astype(F32))

    outs_p, outs_s = [], []
    for l in range(depth):
        p = _prepare_params(l, norm1_w, w_in, conv_w, conv_b, dt_bias, a_log, d_skip, ssd_norm_w, q_norm_w, k_norm_w,
                            idx_k_norm_w, w_br_ssd, w_br_att, w_out, norm2_w, w_gate, w_up, w_down)
        hp, *rest_p = _layer(hp, tq_p, conv0, ssm0, None, p, bias_tiles,
                             qblk0=0, chunk_off=CHUNK - N_META, n_sel=n_sel_p)
        hs, *rest_s = _layer(hs, ts, state_conv[l], state_ssm[l], (cache_k[l], cache_v[l], cache_kidx[l]), p,
                             bias_tiles, qblk0=past // BLK, chunk_off=0, n_sel=n_sel_s)
        outs_p.append(rest_p)
        outs_s.append(rest_s)

    stack = lambda outs, idx, dt: jnp.stack([o[idx] for o in outs], 0).astype(dt)
    y_prompt = hp[:, N_META:tq_p]
    y_sample = hs[:, :ts]
    return (y_prompt, y_sample,
            stack(outs_p, 0, x_prompt.dtype), stack(outs_p, 1, x_prompt.dtype), stack(outs_p, 2, x_prompt.dtype),
            stack(outs_p, 3, state_ssm.dtype), stack(outs_p, 4, x_prompt.dtype),
            stack(outs_s, 0, x_sample.dtype), stack(outs_s, 1, x_sample.dtype), stack(outs_s, 2, x_sample.dtype),
            stack(outs_s, 3, state_ssm.dtype), stack(outs_s, 4, x_sample.dtype))
```

```python
import functools
import math

import jax
import jax.numpy as jnp
from jax import lax
from jax.experimental import pallas as pl
from jax.experimental.pallas import tpu as pltpu

F32 = jnp.float32
BF16 = jnp.bfloat16

D_MODEL = 1024
CHUNK = 64
N_META = 16
SSD_HEADS = 16
SSD_HEAD_DIM = 64
SSD_INNER = SSD_HEADS * SSD_HEAD_DIM
SSD_GROUPS = 4
SSD_STATE = 128
CONV_WIDTH = 4
CONV_DIM = SSD_INNER + 2 * SSD_GROUPS * SSD_STATE
N_HEADS = 8
N_KV_HEADS = 2
HEAD_DIM = 128
ATT_INNER = N_HEADS * HEAD_DIM
KV_DIM = N_KV_HEADS * HEAD_DIM
IDX_HEADS = 8
IDX_DIM = 64
TOPK_MAX = 256
REL_BUCKETS = 32
REL_MAX_DIST = 128
IN_WIDTHS = (SSD_INNER, CONV_DIM, SSD_HEADS, ATT_INNER, KV_DIM, KV_DIM, IDX_HEADS * IDX_DIM, IDX_DIM, IDX_HEADS,
             D_MODEL, D_MODEL)
EPS = 1e-6

LANES = 128
SUBLANES = 8
VMEM_LIMIT_BYTES = 56 * 1024 * 1024

BLK = LANES
ROW_TILE = 256
HALO = SUBLANES

C_Z = 0
C_XBC = C_Z + SSD_INNER
C_Q = C_XBC + CONV_DIM
C_K = C_Q + ATT_INNER
C_V = C_K + KV_DIM
C_QI = C_V + KV_DIM
C_GS = C_QI + IDX_HEADS * IDX_DIM
C_GA = C_GS + D_MODEL
C_SM = C_GA + D_MODEL
IN_PAD = C_SM + LANES
SM_KI = 0
SM_DT = SM_KI + IDX_DIM
SM_WI = SM_DT + SSD_HEADS

BISECT_STEPS = 26
NEG_BIG = -1e30


def _cparams(sem):
    return pltpu.CompilerParams(dimension_semantics=sem, vmem_limit_bytes=VMEM_LIMIT_BYTES)


def _const_spec(shape):
    nd = len(shape)
    return pl.BlockSpec(shape, lambda *_: (0,) * nd)


def _rms(x, w):
    return x * lax.rsqrt(jnp.mean(x * x, axis=-1, keepdims=True) + EPS) * w


def _silu(x):
    return x * jax.nn.sigmoid(x)


def _in_proj_kernel(x_ref, n1_ref, w_ref, qn_ref, kn_ref, kin_ref,
                    z_ref, xbc_ref, q_ref, k_ref, v_ref, kb_ref, vb_ref, qi_ref, sm_ref, smb_ref, gs_ref, ga_ref):
    hn = _rms(x_ref[...], n1_ref[...]).astype(BF16)

    def mm(lo, hi):
        return jnp.dot(hn, w_ref[:, lo:hi], preferred_element_type=F32)

    z_ref[...] = mm(C_Z, C_XBC)
    xbc_ref[...] = mm(C_XBC, C_Q)
    q = mm(C_Q, C_K)
    for h in range(N_HEADS):
        sl = slice(h * HEAD_DIM, (h + 1) * HEAD_DIM)
        q_ref[:, sl] = _rms(q[:, sl], qn_ref[...]).astype(BF16)
    k = mm(C_K, C_V)
    for h in range(N_KV_HEADS):
        sl = slice(h * HEAD_DIM, (h + 1) * HEAD_DIM)
        kh = _rms(k[:, sl], kn_ref[...])
        k_ref[:, sl] = kh
        kb_ref[:, sl] = kh.astype(BF16)
    v = mm(C_V, C_QI)
    v_ref[...] = v
    vb_ref[...] = v.astype(BF16)
    qi_ref[...] = mm(C_QI, C_GS).astype(BF16)
    gs_ref[...] = mm(C_GS, C_GA)
    ga_ref[...] = mm(C_GA, C_SM)
    sm = mm(C_SM, IN_PAD)
    lane = lax.broadcasted_iota(jnp.int32, sm.shape, 1)
    is_ki = lane < SM_KI + IDX_DIM
    ms = jnp.sum(jnp.where(is_ki, sm * sm, 0.0), axis=-1, keepdims=True) * (1.0 / IDX_DIM)
    ki = sm * lax.rsqrt(ms + EPS) * kin_ref[...]
    is_wi = (lane >= SM_WI) & (lane < SM_WI + IDX_HEADS)
    out = jnp.where(is_ki, ki, jnp.where(is_wi, sm * (IDX_HEADS ** -0.5), sm))
    sm_ref[...] = out
    smb_ref[...] = out.astype(BF16)


def _in_proj(x2d, n1, w_perm, qn, kn, kin_pad):
    n = x2d.shape[0]
    tm = ROW_TILE
    row = lambda width: pl.BlockSpec((tm, width), lambda i: (i, 0))
    outs = [
        (SSD_INNER, F32), (CONV_DIM, F32), (ATT_INNER, BF16), (KV_DIM, F32), (KV_DIM, F32), (KV_DIM, BF16),
        (KV_DIM, BF16), (IDX_HEADS * IDX_DIM, BF16), (LANES, F32), (LANES, BF16), (D_MODEL, F32), (D_MODEL, F32),
    ]
    return pl.pallas_call(
        _in_proj_kernel,
        grid=(n // tm,),
        in_specs=[row(D_MODEL), _const_spec((1, D_MODEL)),
                  pl.BlockSpec((D_MODEL, IN_PAD), lambda i: (0, 0), pipeline_mode=pl.Buffered(1)),
                  _const_spec((1, HEAD_DIM)), _const_spec((1, HEAD_DIM)), _const_spec((1, LANES))],
        out_specs=[row(w) for w, _ in outs],
        out_shape=[jax.ShapeDtypeStruct((n, w), dt) for w, dt in outs],
        compiler_params=_cparams(("parallel",)),
        name="in_proj",
    )(x2d, n1, w_perm, qn, kn, kin_pad)


def _softplus(x):
    return jnp.maximum(x, 0.0) + jnp.log1p(jnp.exp(-jnp.abs(x)))


def _ssd_kernel(xbc_ref, z_ref, sm_ref, cprev_ref, sprev_ref, cw_ref, cb_ref, dtb_ref, dtbt_ref, alog_ref, alogt_ref,
                dsk_ref, nw_ref, exp_ref,
                y_ref, snew_ref, cnew_ref,
                s_scr, xpad_scr, xc_scr, y_scr, *, n_chunks, t_valid):
    c = pl.program_id(1)
    q = BLK
    hp = lax.Precision.HIGHEST
    gw = SSD_HEADS // SSD_GROUPS * SSD_HEAD_DIM

    @pl.when(c == 0)
    def _():
        for g in range(SSD_GROUPS):
            s_scr[g] = sprev_ref[g].T
        xpad_scr[0:HALO, :] = cprev_ref[...]

    xpad_scr[HALO:HALO + q, :] = xbc_ref[...]

    slab = 512
    for cc in range(CONV_DIM // slab):
        sl = slice(cc * slab, (cc + 1) * slab)
        acc = xpad_scr[HALO - 3:HALO - 3 + q, sl] * cw_ref[0:1, sl]
        for i in range(1, CONV_WIDTH):
            acc = acc + xpad_scr[HALO - 3 + i:HALO - 3 + i + q, sl] * cw_ref[i:i + 1, sl]
        xc_scr[:, sl] = _silu(cb_ref[:, sl] + acc)

    n_last = t_valid - (n_chunks - 1) * q

    @pl.when(c == n_chunks - 1)
    def _():
        cnew_ref[...] = xpad_scr[n_last:n_last + HALO, :]

    xpad_scr[0:HALO, :] = xpad_scr[q:q + HALO, :]

    sm = sm_ref[...]
    smt = sm.T
    lane = lax.broadcasted_iota(jnp.int32, (q, LANES), 1)
    row = lax.broadcasted_iota(jnp.int32, (q, LANES), 0)
    is_dt = (lane >= SM_DT) & (lane < SM_DT + SSD_HEADS) & (row + c * q < t_valid)
    is_dt_t = (row >= SM_DT) & (row < SM_DT + SSD_HEADS) & (lane + c * q < t_valid)
    dt = jnp.where(is_dt, _softplus(sm + dtb_ref[...]), 0.0)
    dtt = jnp.where(is_dt_t, _softplus(smt + dtbt_ref[...]), 0.0)
    da = dt * (-jnp.exp(alog_ref[...]))
    dat = dtt * (-jnp.exp(alogt_ref[...]))
    ii = lax.broadcasted_iota(jnp.int32, (q, q), 0)
    jj = lax.broadcasted_iota(jnp.int32, (q, q), 1)
    causal = jj <= ii
    tri = causal.astype(F32)
    acum = jnp.dot(tri, da, precision=hp, preferred_element_type=F32)
    acumt = jnp.dot(dat, (ii <= jj).astype(F32), precision=hp, preferred_element_type=F32)
    a_last = acum[q - 1:q, :]
    expand = exp_ref[...]
    ea_x = jnp.dot(jnp.exp(acum), expand, precision=hp, preferred_element_type=F32)
    wdt_x = jnp.dot(jnp.exp(a_last - acum) * dt, expand, precision=hp, preferred_element_type=F32)
    dec_x = jnp.dot(jnp.broadcast_to(jnp.exp(a_last), (SUBLANES, LANES)), expand, precision=hp,
                    preferred_element_type=F32)[0:1, :]

    for g in range(SSD_GROUPS):
        gsl = slice(g * gw, (g + 1) * gw)
        bsl = slice(SSD_INNER + g * SSD_STATE, SSD_INNER + (g + 1) * SSD_STATE)
        csl = slice(SSD_INNER + SSD_GROUPS * SSD_STATE + g * SSD_STATE,
                    SSD_INNER + SSD_GROUPS * SSD_STATE + (g + 1) * SSD_STATE)
        bmf = xc_scr[:, bsl]
        bm = bmf.astype(BF16)
        cm = xc_scr[:, csl].astype(BF16)
        xg = xc_scr[:, gsl]
        xgb = xg.astype(BF16)
        cbm = lax.dot_general(cm, bm, (((1,), (1,)), ((), ())), preferred_element_type=F32)
        xw = (xg * wdt_x[:, gsl]).astype(BF16)
        st = jnp.dot(bmf.T.astype(BF16), xw, preferred_element_type=F32)
        s_in = s_scr[g]
        y_off = jnp.dot(cm, s_in.astype(BF16), preferred_element_type=F32) * ea_x[:, gsl]
        s_scr[g] = s_in * dec_x[:, gsl] + st
        for r in range(SSD_HEADS // SSD_GROUPS):
            h = g * (SSD_HEADS // SSD_GROUPS) + r
            seg = acum[:, SM_DT + h:SM_DT + h + 1] - acumt[SM_DT + h:SM_DT + h + 1, :]
            lmat = jnp.exp(jnp.where(causal, seg, -jnp.inf))
            wmat = (cbm * lmat * dtt[SM_DT + h:SM_DT + h + 1, :]).astype(BF16)
            rsl = slice(r * SSD_HEAD_DIM, (r + 1) * SSD_HEAD_DIM)
            hsl = slice(h * SSD_HEAD_DIM, (h + 1) * SSD_HEAD_DIM)
            y_diag = jnp.dot(wmat, xgb[:, rsl], preferred_element_type=F32)
            y_scr[:, hsl] = y_diag + y_off[:, rsl] + xg[:, rsl] * dsk_ref[:, hsl]

    @pl.when(c == n_chunks - 1)
    def _():
        for g in range(SSD_GROUPS):
            snew_ref[g] = s_scr[g].T

    y = y_scr[...] * _silu(z_ref[...])
    y_ref[...] = _rms(y, nw_ref[...]).astype(BF16)


def _ssd(xbc, z, sm, conv_prev8, ssm_prev, p, t_valid):
    b, tp, _ = xbc.shape
    nc = tp // BLK
    gw = SSD_HEADS // SSD_GROUPS * SSD_HEAD_DIM
    seq = lambda width: pl.BlockSpec((None, BLK, width), lambda i, c: (i, c, 0))
    kern = functools.partial(_ssd_kernel, n_chunks=nc, t_valid=t_valid)
    return pl.pallas_call(
        kern,
        grid=(b, nc),
        in_specs=[seq(CONV_DIM), seq(SSD_INNER), seq(LANES),
                  pl.BlockSpec((None, HALO, CONV_DIM), lambda i, c: (i, 0, 0)),
                  pl.BlockSpec((None, SSD_GROUPS, gw, SSD_STATE), lambda i, c: (i, 0, 0, 0)),
                  _const_spec((CONV_WIDTH, CONV_DIM)), _const_spec((1, CONV_DIM)),
                  _const_spec((1, LANES)), _const_spec((LANES, 1)), _const_spec((1, LANES)), _const_spec((LANES, 1)),
                  _const_spec((1, SSD_INNER)), _const_spec((1, SSD_INNER)), _const_spec((LANES, SSD_INNER))],
        out_specs=[seq(SSD_INNER),
                   pl.BlockSpec((None, SSD_GROUPS, gw, SSD_STATE), lambda i, c: (i, 0, 0, 0)),
                   pl.BlockSpec((None, HALO, CONV_DIM), lambda i, c: (i, 0, 0))],
        out_shape=[jax.ShapeDtypeStruct((b, tp, SSD_INNER), BF16),
                   jax.ShapeDtypeStruct((b, SSD_GROUPS, gw, SSD_STATE), F32),
                   jax.ShapeDtypeStruct((b, HALO, CONV_DIM), F32)],
        scratch_shapes=[pltpu.VMEM((SSD_GROUPS, SSD_STATE, gw), F32),
                        pltpu.VMEM((BLK + 2 * HALO, CONV_DIM), F32),
                        pltpu.VMEM((BLK, CONV_DIM), F32),
                        pltpu.VMEM((BLK, SSD_INNER), F32)],
        compiler_params=_cparams(("parallel", "arbitrary")),
        name="ssd",
    )(xbc, z, sm, conv_prev8, ssm_prev, p["conv_w"], p["conv_b"], p["dtb"], p["dtb_t"], p["alog"], p["alog_t"],
      p["dskip_x"], p["ssd_norm_w"], p["expand"])


N_BIAS_TILES = 5


def _bias_kernel(rb_ref, bt_ref):
    nb = REL_BUCKETS // 2
    max_exact = nb // 2
    qq = lax.broadcasted_iota(jnp.int32, (BLK, BLK), 0)
    kk = lax.broadcasted_iota(jnp.int32, (BLK, BLK), 1)
    for u in range(N_BIAS_TILES):
        rel = (u - 2) * BLK + kk - qq
        n = jnp.abs(rel)
        nf = jnp.maximum(n, 1).astype(F32)
        large = max_exact + (jnp.log(nf / max_exact) / math.log(REL_MAX_DIST / max_exact)
                             * (nb - max_exact)).astype(jnp.int32)
        large = jnp.minimum(large, nb - 1)
        bucket = jnp.where(rel > 0, nb, 0) + jnp.where(n < max_exact, n, large)
        for h in range(N_HEADS):
            acc = jnp.zeros((BLK, BLK), F32)
            for bkt in range(REL_BUCKETS):
                acc = jnp.where(bucket == bkt, rb_ref[bkt, h], acc)
            bt_ref[h * N_BIAS_TILES + u] = acc


def _bias_tiles(rel_bias):
    return pl.pallas_call(
        _bias_kernel,
        in_specs=[pl.BlockSpec(memory_space=pltpu.SMEM)],
        out_specs=pl.BlockSpec(memory_space=pltpu.VMEM),
        out_shape=jax.ShapeDtypeStruct((N_HEADS * N_BIAS_TILES, BLK, BLK), F32),
        name="bias_tiles",
    )(rel_bias)


def _attn_kernel(qn_ref, qi_ref, sm_ref, ki_ref, k_ref, v_ref, bt_ref, o_ref,
                 st_scr, m_scr, lg_scr, qis_scr, qs_scr, mrun_scr, lrun_scr, acc_scr, *,
                 nkb_total, qblk0, chunk_off, l_valid, n_sel):
    i = pl.program_id(1)
    r = BLK
    qb = qblk0 + i
    q0 = qb * BLK
    k_end = CHUNK * ((q0 + BLK - 1 + chunk_off) // CHUNK + 1) - chunk_off
    nkb = jnp.minimum(nkb_total, (k_end + BLK - 1) // BLK)

    nt = (((1,), (1,)), ((), ()))
    wit = sm_ref[...].T
    qpos = q0 + lax.broadcasted_iota(jnp.int32, (BLK, r), 1)
    qchunk = (qpos + chunk_off) // CHUNK
    krow = lax.broadcasted_iota(jnp.int32, (BLK, r), 0)

    for h in range(IDX_HEADS):
        qis_scr[h * r:(h + 1) * r, :] = qi_ref[:, h * IDX_DIM:(h + 1) * IDX_DIM]

    def fold(x, op):
        return op(x.reshape(BLK // SUBLANES, SUBLANES, r), axis=0)

    def score_body(j, carry):
        mn, mx = carry
        kij = ki_ref[j][:, SM_KI:SM_KI + IDX_DIM]
        acc = jnp.zeros((BLK, r), F32)
        for hp in range(IDX_HEADS // 2):
            sh = lax.dot_general(kij, qis_scr[2 * hp * r:2 * (hp + 1) * r, :], nt, preferred_element_type=F32)
            for e in range(2):
                h = 2 * hp + e
                acc = acc + wit[SM_WI + h:SM_WI + h + 1, :] * jnp.maximum(sh[:, e * r:(e + 1) * r], 0.0)
        kpos = j * BLK + krow
        adm = ((kpos + chunk_off) // CHUNK <= qchunk) & (kpos < l_valid)
        s = jnp.where(adm, acc * (IDX_DIM ** -0.5), -jnp.inf)
        st_scr[j] = s
        mn = jnp.minimum(mn, fold(jnp.where(adm, s, jnp.inf), jnp.min))
        mx = jnp.maximum(mx, fold(s, jnp.max))
        return mn, mx

    init = (jnp.full((SUBLANES, r), jnp.inf, F32), jnp.full((SUBLANES, r), -jnp.inf, F32))
    mn, mx = lax.fori_loop(0, nkb, score_body, init)
    lo0 = jnp.min(mn, axis=0, keepdims=True)
    hi0 = jnp.max(mx, axis=0, keepdims=True)
    n_adm = jnp.minimum(CHUNK * (qchunk[0:1, :] + 1) - chunk_off, l_valid)
    kk = jnp.minimum(n_adm, n_sel).astype(F32)

    def count(pred):
        def body(j, acc):
            return acc + jnp.where(pred(st_scr[j], j), 1.0, 0.0)
        acc = lax.fori_loop(0, nkb, body, jnp.zeros((BLK, r), F32))
        return jnp.sum(acc, axis=0, keepdims=True)

    def bisect(_, carry):
        lo, hi, ub = carry
        mid = 0.5 * lo + 0.5 * hi
        ok = count(lambda s, j: s >= mid) >= kk
        return jnp.where(ok, mid, lo), jnp.where(ok, hi, mid), jnp.where(ok, ub, mid)

    _, _, ub = lax.fori_loop(0, BISECT_STEPS, bisect, (lo0, hi0, jnp.full((1, r), jnp.inf, F32)))

    def next_below(ub):
        def body(j, acc):
            s = st_scr[j]
            return jnp.maximum(acc, jnp.where(s < ub, s, -jnp.inf))
        acc = lax.fori_loop(0, nkb, body, jnp.full((BLK, r), -jnp.inf, F32))
        return jnp.max(acc, axis=0, keepdims=True)

    n_keys = nkb_total * BLK

    def descend_cond(carry):
        _, _, pending, it = carry
        return (pending > 0) & (it < n_keys)

    def descend_body(carry):
        ub, _, _, it = carry
        t = next_below(ub)
        done = count(lambda s, j: s >= t) >= kk
        pending = jnp.sum(jnp.where(done, 0, 1))
        return jnp.where(done, ub, t), t, pending, it + 1

    _, thr, _, _ = lax.while_loop(descend_cond, descend_body, (ub, lo0, jnp.int32(1), jnp.int32(0)))

    n_gt = count(lambda s, j: s > thr)
    n_eq = count(lambda s, j: s == thr)
    need = kk - n_gt

    def tie_search(_, carry):
        lo, hi = carry
        mid = (lo + hi) // 2
        ok = count(lambda s, j: (s == thr) & (j * BLK + krow < mid)) >= need
        return jnp.where(ok, lo, mid), jnp.where(ok, mid, hi)

    tie_steps =max(1, math.ceil(math.log2(n_keys + 1)))
    has_extra_ties = jnp.sum(jnp.where(n_eq > need, 1, 0)) > 0
    key_limit = lax.cond(
        has_extra_ties,
        lambda: lax.fori_loop(0, tie_steps, tie_search,
                              (jnp.zeros((1, r), jnp.int32), jnp.full((1, r), n_keys, jnp.int32)))[1],
        lambda: jnp.full((1, r), n_keys, jnp.int32))

    def mask_body(j, carry):
        s = st_scr[j]
        sel = (s > thr) | ((s == thr) & (j * BLK + krow < key_limit))
        m_scr[j] = jnp.where(sel, 0.0, NEG_BIG).T
        return carry

    lax.fori_loop(0, nkb, mask_body, 0)

    scale = HEAD_DIM ** -0.5
    rep = N_HEADS // N_KV_HEADS
    for h in range(N_HEADS):
        qs_scr[h // rep, (h % rep) * r:(h % rep + 1) * r, :] = qn_ref[:, h * HEAD_DIM:(h + 1) * HEAD_DIM]
    mrun_scr[...] = jnp.full(mrun_scr.shape, -jnp.inf, F32)
    lrun_scr[...] = jnp.zeros(lrun_scr.shape, F32)
    acc_scr[...] = jnp.zeros(acc_scr.shape, F32)

    def logit_body(j, carry):
        u = jnp.clip(j - qb + 2, 0, N_BIAS_TILES - 1)
        madd = m_scr[j]
        for g in range(N_KV_HEADS):
            lt = lax.dot_general(qs_scr[g], k_ref[j, :, g * HEAD_DIM:(g + 1) * HEAD_DIM], nt,
                                 preferred_element_type=F32)
            for e in range(rep):
                h = g * rep + e
                lg = lt[e * r:(e + 1) * r, :] * scale + bt_ref[h * N_BIAS_TILES + u] + madd
                lg_scr[h, j] = lg
                mrun_scr[h] = jnp.maximum(mrun_scr[h], lg)
        return carry

    lax.fori_loop(0, nkb, logit_body, 0)
    for h in range(N_HEADS):
        mrun_scr[h] = jnp.broadcast_to(jnp.max(mrun_scr[h], axis=1, keepdims=True), (r, BLK))

    def pv_body(j, carry):
        for g in range(N_KV_HEADS):
            es = []
            for e in range(rep):
                h = g * rep + e
                ex = jnp.exp(lg_scr[h, j] - mrun_scr[h])
                lrun_scr[h] = lrun_scr[h] + ex
                es.append(ex.astype(BF16))
            acc_scr[g] = acc_scr[g] + jnp.dot(jnp.concatenate(es, axis=0),
                                              v_ref[j, :, g * HEAD_DIM:(g + 1) * HEAD_DIM],
                                              preferred_element_type=F32)
        return carry

    lax.fori_loop(0, nkb, pv_body, 0)
    for h in range(N_HEADS):
        g, e = h // rep, h % rep
        den = jnp.sum(lrun_scr[h], axis=1, keepdims=True)
        o_ref[:, h * HEAD_DIM:(h + 1) * HEAD_DIM] = (acc_scr[g, e * r:(e + 1) * r, :] / den).astype(BF16)


def _attn(qn, qi, sm, ki_b, k_b, v_b, bias_tiles, *, qblk0, chunk_off, l_valid, n_sel):
    b, tq, _ = qn.shape
    nkb_total = k_b.shape[1]
    seq = lambda width: pl.BlockSpec((None, BLK, width), lambda bi, i: (bi, i, 0))
    keys = lambda width: pl.BlockSpec((None, nkb_total, BLK, width), lambda bi, i: (bi, 0, 0, 0))
    kern = functools.partial(_attn_kernel, nkb_total=nkb_total, qblk0=qblk0, chunk_off=chunk_off, l_valid=l_valid,
                             n_sel=n_sel)
    return pl.pallas_call(
        kern,
        grid=(b, tq // BLK),
        in_specs=[seq(ATT_INNER), seq(IDX_HEADS * IDX_DIM), seq(LANES), keys(LANES), keys(KV_DIM), keys(KV_DIM),
                  _const_spec((N_HEADS * N_BIAS_TILES, BLK, BLK))],
        out_specs=seq(ATT_INNER),
        out_shape=jax.ShapeDtypeStruct((b, tq, ATT_INNER), BF16),
        scratch_shapes=[pltpu.VMEM((nkb_total, BLK, BLK), F32),
                        pltpu.VMEM((nkb_total, BLK, BLK), F32),
                        pltpu.VMEM((N_HEADS, nkb_total, BLK, BLK), F32),
                        pltpu.VMEM((IDX_HEADS * BLK, IDX_DIM), BF16),
                        pltpu.VMEM((N_KV_HEADS, N_HEADS // N_KV_HEADS * BLK, HEAD_DIM), BF16),
                        pltpu.VMEM((N_HEADS, BLK, BLK), F32),
                        pltpu.VMEM((N_HEADS, BLK, BLK), F32),
                        pltpu.VMEM((N_KV_HEADS, N_HEADS // N_KV_HEADS * BLK, HEAD_DIM), F32)],
        compiler_params=_cparams(("parallel", "arbitrary")),
        name="attn",
    )(qn, qi, sm, ki_b, k_b, v_b, bias_tiles)


def _out_ffn_kernel(x_ref, ys_ref, ya_ref, gs_ref, ga_ref, wbs_ref, wba_ref, wo_ref, n2_ref, wg_ref, wu_ref, wd_ref,
                    y_ref):
    dot = functools.partial(jnp.dot, preferred_element_type=F32)
    merged = (jax.nn.sigmoid(gs_ref[...]) * dot(ys_ref[...], wbs_ref[...])
              + jax.nn.sigmoid(ga_ref[...]) * dot(ya_ref[...], wba_ref[...]))
    h = x_ref[...] + dot(merged.astype(BF16), wo_ref[...])
    hn = _rms(h, n2_ref[...]).astype(BF16)
    act = (_silu(dot(hn, wg_ref[...])) * dot(hn, wu_ref[...])).astype(BF16)
    y_ref[...] = h + dot(act, wd_ref[...])


def _out_ffn(x2d, ys, ya, gs, ga, p):
    n = x2d.shape[0]
    tm = ROW_TILE
    d_ff = p["w_gate"].shape[1]
    row = lambda width: pl.BlockSpec((tm, width), lambda i: (i, 0))
    wspec = lambda shape: pl.BlockSpec(shape, lambda i: (0, 0), pipeline_mode=pl.Buffered(1))
    return pl.pallas_call(
        _out_ffn_kernel,
        grid=(n // tm,),
        in_specs=[row(D_MODEL), row(SSD_INNER), row(ATT_INNER), row(D_MODEL), row(D_MODEL),
                  wspec((SSD_INNER, D_MODEL)), wspec((ATT_INNER, D_MODEL)), wspec((D_MODEL, D_MODEL)),
                  _const_spec((1, D_MODEL)), wspec((D_MODEL, d_ff)), wspec((D_MODEL, d_ff)), wspec((d_ff, D_MODEL))],
        out_specs=row(D_MODEL),
        out_shape=jax.ShapeDtypeStruct((n, D_MODEL), F32),
        compiler_params=_cparams(("parallel",)),
        name="out_ffn",
    )(x2d, ys, ya, gs, ga, p["w_br_ssd"], p["w_br_att"], p["w_out"], p["norm2_w"], p["w_gate"], p["w_up"],
      p["w_down"])


def _layer(x_pad, t_valid, conv_prev, ssm_prev, past, p, bias_tiles, *, qblk0, chunk_off, n_sel):
    b, tp, _ = x_pad.shape
    x2d = x_pad.reshape(b * tp, D_MODEL)
    z, xbc, qn, k32, v32, kb, vb, qi, sm, smb, gs, ga = _in_proj(
        x2d, p["norm1_w"], p["w_in"], p["q_norm_w"], p["k_norm_w"], p["idx_k_norm_w"])
    seq = lambda a: a.reshape(b, tp, a.shape[-1])

    conv_prev8 = jnp.pad(conv_prev.astype(F32), ((0, 0), (HALO - (CONV_WIDTH - 1), 0), (0, 0)))
    gw = SSD_HEADS // SSD_GROUPS * SSD_HEAD_DIM
    y_ssd, ssm_new, conv_new8 = _ssd(seq(xbc), seq(z), seq(sm), conv_prev8,
                                     ssm_prev.astype(F32).reshape(b, SSD_GROUPS, gw, SSD_STATE), p, t_valid)

    kb, vb, kib = seq(kb), seq(vb), seq(smb)
    if past is not None:
        pk, pv, pki = past
        n_past = pk.shape[1]
        kb = jnp.concatenate([pk.reshape(b, n_past, KV_DIM).astype(BF16), kb], axis=1)
        vb = jnp.concatenate([pv.reshape(b, n_past, KV_DIM).astype(BF16), vb], axis=1)
        pki = jnp.pad(pki.astype(BF16), ((0, 0), (0, 0), (SM_KI, LANES - SM_KI - IDX_DIM)))
        kib = jnp.concatenate([pki, kib], axis=1)
    else:
        n_past = 0
    blocks = lambda a: a.reshape(b, a.shape[1] // BLK, BLK, a.shape[-1])
    y_att = _attn(seq(qn), seq(qi), seq(sm), blocks(kib), blocks(kb), blocks(vb), bias_tiles,
                  qblk0=qblk0, chunk_off=chunk_off, l_valid=n_past + t_valid, n_sel=n_sel)

    y = _out_ffn(x2d, y_ssd.reshape(b * tp, SSD_INNER), y_att.reshape(b * tp, ATT_INNER), gs, ga, p)

    k_new = seq(k32)[:, :t_valid].reshape(b, t_valid, N_KV_HEADS, HEAD_DIM)
    v_new = seq(v32)[:, :t_valid].reshape(b, t_valid, N_KV_HEADS, HEAD_DIM)
    ki_new = seq(sm)[:, :t_valid, SM_KI:SM_KI + IDX_DIM]
    ssm_new = ssm_new.reshape(b, SSD_HEADS, SSD_HEAD_DIM, SSD_STATE)
    conv_new = conv_new8[:, HALO - (CONV_WIDTH - 1):]
    return y.reshape(b, tp, D_MODEL), k_new, v_new, ki_new, ssm_new, conv_new


def _prepare_params(l, norm1_w, w_in, conv_w, conv_b, dt_bias, a_log, d_skip, ssd_norm_w, q_norm_w, k_norm_w,
                    idx_k_norm_w, w_br_ssd, w_br_att, w_out, norm2_w, w_gate, w_up, w_down):
    offs = [0]
    for w in IN_WIDTHS:
        offs.append(offs[-1] + w)
    seg = lambda i: w_in[l][:, offs[i]:offs[i + 1]]
    i_z, i_xbc, i_dt, i_q, i_k, i_v, i_qi, i_ki, i_wi, i_gs, i_ga = range(11)
    pad = jnp.zeros((D_MODEL, LANES - IDX_DIM - SSD_HEADS - IDX_HEADS), w_in.dtype)
    w_perm = jnp.concatenate([seg(i_z), seg(i_xbc), seg(i_q), seg(i_k), seg(i_v), seg(i_qi), seg(i_gs), seg(i_ga),
                              seg(i_ki), seg(i_dt), seg(i_wi), pad], axis=1).astype(BF16)

    def lanes_at(vec, start):
        return jnp.zeros((1, LANES), F32).at[0, start:start + vec.shape[0]].set(vec.astype(F32))

    dtb = lanes_at(dt_bias[l], SM_DT)
    alog = lanes_at(a_log[l], SM_DT)
    head_of_channel = jnp.arange(SSD_INNER) // SSD_HEAD_DIM
    expand = (jnp.arange(LANES)[:, None] == head_of_channel[None, :] + SM_DT).astype(F32)
    row = lambda v: v.astype(F32).reshape(1, -1)
    return dict(
        norm1_w=row(norm1_w[l]), w_in=w_perm, conv_w=conv_w[l].astype(F32), conv_b=row(conv_b[l]),
        dtb=dtb, dtb_t=dtb.reshape(LANES, 1), alog=alog, alog_t=alog.reshape(LANES, 1),
        dskip_x=row(jnp.repeat(d_skip[l], SSD_HEAD_DIM)), ssd_norm_w=row(ssd_norm_w[l]), expand=expand,
        q_norm_w=row(q_norm_w[l]), k_norm_w=row(k_norm_w[l]),
        idx_k_norm_w=jnp.ones((1, LANES), F32).at[0, SM_KI:SM_KI + IDX_DIM].set(idx_k_norm_w[l].astype(F32)),
        w_br_ssd=w_br_ssd[l].astype(BF16), w_br_att=w_br_att[l].astype(BF16), w_out=w_out[l].astype(BF16),
        norm2_w=row(norm2_w[l]), w_gate=w_gate[l].astype(BF16), w_up=w_up[l].astype(BF16),
        w_down=w_down[l].astype(BF16))


def _pad_rows(x, tp):
    return jnp.pad(x, ((0, 0), (0, tp - x.shape[1]), (0, 0)))


def kernel(x_prompt, x_sample, cache_k, cache_v, cache_kidx, state_ssm, state_conv, meta_tokens, rel_bias, norm1_w,
           w_in, conv_w, conv_b, dt_bias, a_log, d_skip, ssd_norm_w, q_norm_w, k_norm_w, idx_k_norm_w, w_br_ssd,
           w_br_att, w_out, norm2_w, w_gate, w_up, w_down):
    bp, sp, _ = x_prompt.shape
    bs, ts, _ = x_sample.shape
    depth = w_in.shape[0]
    past = cache_k.shape[2]
    assert past % BLK == 0 and BLK % CHUNK == 0 and N_META <= CHUNK

    tq_p = N_META + sp
    tp_p = -(-tq_p // BLK) * BLK
    tp_s = -(-ts // BLK) * BLK
    n_sel_p = min(TOPK_MAX, sp // 4)
    n_sel_s = min(TOPK_MAX, (past + ts) // 4)

    hp = jnp.concatenate([jnp.broadcast_to(meta_tokens.astype(x_prompt.dtype)[None], (bp, N_META, D_MODEL)),
                          x_prompt], axis=1)
    hp = _pad_rows(hp, tp_p)
    hs = _pad_rows(x_sample, tp_s)
    conv0 = jnp.zeros((bp, CONV_WIDTH - 1, CONV_DIM), F32)
    ssm0 = jnp.zeros((bp, SSD_HEADS, SSD_HEAD_DIM, SSD_STATE), F32)
    bias_tiles = _bias_tiles(rel_bias.astype(F32))

    outs_p, outs_s = [], []
    for l in range(depth):
        p = _prepare_params(l, norm1_w, w_in, conv_w, conv_b, dt_bias, a_log, d_skip, ssd_norm_w, q_norm_w, k_norm_w,
                            idx_k_norm_w, w_br_ssd, w_br_att, w_out, norm2_w, w_gate, w_up, w_down)
        hp, *rest_p = _layer(hp, tq_p, conv0, ssm0, None, p, bias_tiles,
                             qblk0=0, chunk_off=CHUNK - N_META, n_sel=n_sel_p)
        hs, *rest_s = _layer(hs, ts, state_conv[l], state_ssm[l], (cache_k[l], cache_v[l], cache_kidx[l]), p,
                             bias_tiles, qblk0=past // BLK, chunk_off=0, n_sel=n_sel_s)
        outs_p.append(rest_p)
        outs_s.append(rest_s)

    stack = lambda outs, idx, dt: jnp.stack([o[idx] for o in outs], 0).astype(dt)
    y_prompt = hp[:, N_META:tq_p]
    y_sample = hs[:, :ts]
    return (y_prompt, y_sample,
            stack(outs_p, 0, x_prompt.dtype), stack(outs_p, 1, x_prompt.dtype), stack(outs_p, 2, x_prompt.dtype),
            stack(outs_p, 3, state_ssm.dtype), stack(outs_p, 4, x_prompt.dtype),
            stack(outs_s, 0, x_sample.dtype), stack(outs_s, 1, x_sample.dtype), stack(outs_s, 2, x_sample.dtype),
            stack(outs_s, 3, state_ssm.dtype), stack(outs_s, 4, x_sample.dtype))
```

```python
import functools
import math

import jax
import jax.numpy as jnp
from jax import lax
from jax.experimental import pallas as pl
from jax.experimental.pallas import tpu as pltpu

F32 = jnp.float32
BF16 = jnp.bfloat16

D_MODEL = 1024
CHUNK = 64
N_META = 16
SSD_HEADS = 16
SSD_HEAD_DIM = 64
SSD_INNER = SSD_HEADS * SSD_HEAD_DIM
SSD_GROUPS = 4
SSD_STATE = 128
CONV_WIDTH = 4
CONV_DIM = SSD_INNER + 2 * SSD_GROUPS * SSD_STATE
N_HEADS = 8
N_KV_HEADS = 2
HEAD_DIM = 128
ATT_INNER = N_HEADS * HEAD_DIM
KV_DIM = N_KV_HEADS * HEAD_DIM
IDX_HEADS = 8
IDX_DIM = 64
TOPK_MAX = 256
REL_BUCKETS = 32
REL_MAX_DIST = 128
IN_WIDTHS = (SSD_INNER, CONV_DIM, SSD_HEADS, ATT_INNER, KV_DIM, KV_DIM, IDX_HEADS * IDX_DIM, IDX_DIM, IDX_HEADS,
             D_MODEL, D_MODEL)
EPS = 1e-6

LANES = 128
SUBLANES = 8
VMEM_LIMIT_BYTES = 56 * 1024 * 1024

BLK = LANES
ROW_TILE = 256
HALO = SUBLANES

C_Z = 0
C_XBC = C_Z + SSD_INNER
C_Q = C_XBC + CONV_DIM
C_K = C_Q + ATT_INNER
C_V = C_K + KV_DIM
C_QI = C_V + KV_DIM
C_GS = C_QI + IDX_HEADS * IDX_DIM
C_GA = C_GS + D_MODEL
C_SM = C_GA + D_MODEL
IN_PAD = C_SM + LANES
SM_KI = 0
SM_DT = SM_KI + IDX_DIM
SM_WI = SM_DT + SSD_HEADS

BISECT_STEPS = 24
NEG_BIG = -1e30


def _cparams(sem):
    return pltpu.CompilerParams(dimension_semantics=sem, vmem_limit_bytes=VMEM_LIMIT_BYTES)


def _const_spec(shape):
    nd = len(shape)
    return pl.BlockSpec(shape, lambda *_: (0,) * nd)


def _rms(x, w):
    return x * lax.rsqrt(jnp.mean(x * x, axis=-1, keepdims=True) + EPS) * w


def _silu(x):
    return x * jax.nn.sigmoid(x)


def _in_proj_kernel(x_ref, n1_ref, w_ref, qn_ref, kn_ref, kin_ref,
                    z_ref, xbc_ref, q_ref, k_ref, v_ref, kb_ref, vb_ref, qi_ref, sm_ref, smb_ref, gs_ref, ga_ref):
    hn = _rms(x_ref[...], n1_ref[...]).astype(BF16)

    def mm(lo, hi):
        return jnp.dot(hn, w_ref[:, lo:hi], preferred_element_type=F32)

    z_ref[...] = mm(C_Z, C_XBC)
    xbc_ref[...] = mm(C_XBC, C_Q)
    q = mm(C_Q, C_K)
    for h in range(N_HEADS):
        sl = slice(h * HEAD_DIM, (h + 1) * HEAD_DIM)
        q_ref[:, sl] = _rms(q[:, sl], qn_ref[...]).astype(BF16)
    k = mm(C_K, C_V)
    for h in range(N_KV_HEADS):
        sl = slice(h * HEAD_DIM, (h + 1) * HEAD_DIM)
        kh = _rms(k[:, sl], kn_ref[...])
        k_ref[:, sl] = kh
        kb_ref[:, sl] = kh.astype(BF16)
    v = mm(C_V, C_QI)
    v_ref[...] = v
    vb_ref[...] = v.astype(BF16)
    qi_ref[...] = mm(C_QI, C_GS).astype(BF16)
    gs_ref[...] = mm(C_GS, C_GA)
    ga_ref[...] = mm(C_GA, C_SM)
    sm = mm(C_SM, IN_PAD)
    lane = lax.broadcasted_iota(jnp.int32, sm.shape, 1)
    is_ki = lane < SM_KI + IDX_DIM
    ms = jnp.sum(jnp.where(is_ki, sm * sm, 0.0), axis=-1, keepdims=True) * (1.0 / IDX_DIM)
    ki = sm * lax.rsqrt(ms + EPS) * kin_ref[...]
    is_wi = (lane >= SM_WI) & (lane < SM_WI + IDX_HEADS)
    out = jnp.where(is_ki, ki, jnp.where(is_wi, sm * (IDX_HEADS ** -0.5), sm))
    sm_ref[...] = out
    smb_ref[...] = out.astype(BF16)


def _in_proj(x2d, n1, w_perm, qn, kn, kin_pad):
    n = x2d.shape[0]
    tm = ROW_TILE
    row = lambda width: pl.BlockSpec((tm, width), lambda i: (i, 0))
    outs = [
        (SSD_INNER, F32), (CONV_DIM, F32), (ATT_INNER, BF16), (KV_DIM, F32), (KV_DIM, F32), (KV_DIM, BF16),
        (KV_DIM, BF16), (IDX_HEADS * IDX_DIM, BF16), (LANES, F32), (LANES, BF16), (D_MODEL, F32), (D_MODEL, F32),
    ]
    return pl.pallas_call(
        _in_proj_kernel,
        grid=(n // tm,),
        in_specs=[row(D_MODEL), _const_spec((1, D_MODEL)),
                  pl.BlockSpec((D_MODEL, IN_PAD), lambda i: (0, 0), pipeline_mode=pl.Buffered(1)),
                  _const_spec((1, HEAD_DIM)), _const_spec((1, HEAD_DIM)), _const_spec((1, LANES))],
        out_specs=[row(w) for w, _ in outs],
        out_shape=[jax.ShapeDtypeStruct((n, w), dt) for w, dt in outs],
        compiler_params=_cparams(("parallel",)),
        name="in_proj",
    )(x2d, n1, w_perm, qn, kn, kin_pad)


def _softplus(x):
    return jnp.maximum(x, 0.0) + jnp.log1p(jnp.exp(-jnp.abs(x)))


def _ssd_kernel(xbc_ref, z_ref, sm_ref, cprev_ref, sprev_ref, cw_ref, cb_ref, dtb_ref, dtbt_ref, alog_ref, alogt_ref,
                dsk_ref, nw_ref, exp_ref,
                y_ref, snew_ref, cnew_ref,
                s_scr, xpad_scr, xc_scr, y_scr, *, n_chunks, t_valid):
    c = pl.program_id(1)
    q = BLK
    hp = lax.Precision.HIGHEST
    gw = SSD_HEADS // SSD_GROUPS * SSD_HEAD_DIM

    @pl.when(c == 0)
    def _():
        for g in range(SSD_GROUPS):
            s_scr[g] = sprev_ref[g].T
        xpad_scr[0:HALO, :] = cprev_ref[...]

    xpad_scr[HALO:HALO + q, :] = xbc_ref[...]

    slab = 512
    for cc in range(CONV_DIM // slab):
        sl = slice(cc * slab, (cc + 1) * slab)
        acc = xpad_scr[HALO - 3:HALO - 3 + q, sl] * cw_ref[0:1, sl]
        for i in range(1, CONV_WIDTH):
            acc = acc + xpad_scr[HALO - 3 + i:HALO - 3 + i + q, sl] * cw_ref[i:i + 1, sl]
        xc_scr[:, sl] = _silu(cb_ref[:, sl] + acc)

    n_last = t_valid - (n_chunks - 1) * q

    @pl.when(c == n_chunks - 1)
    def _():
        cnew_ref[...] = xpad_scr[n_last:n_last + HALO, :]

    xpad_scr[0:HALO, :] = xpad_scr[q:q + HALO, :]

    sm = sm_ref[...]
    smt = sm.T
    lane = lax.broadcasted_iota(jnp.int32, (q, LANES), 1)
    row = lax.broadcasted_iota(jnp.int32, (q, LANES), 0)
    is_dt = (lane >= SM_DT) & (lane < SM_DT + SSD_HEADS) & (row + c * q < t_valid)
    is_dt_t = (row >= SM_DT) & (row < SM_DT + SSD_HEADS) & (lane + c * q < t_valid)
    dt = jnp.where(is_dt, _softplus(sm + dtb_ref[...]), 0.0)
    dtt = jnp.where(is_dt_t, _softplus(smt + dtbt_ref[...]), 0.0)
    da = dt * (-jnp.exp(alog_ref[...]))
    dat = dtt * (-jnp.exp(alogt_ref[...]))
    ii = lax.broadcasted_iota(jnp.int32, (q, q), 0)
    jj = lax.broadcasted_iota(jnp.int32, (q, q), 1)
    causal = jj <= ii
    tri = causal.astype(F32)
    acum = jnp.dot(tri, da, precision=hp, preferred_element_type=F32)
    acumt = jnp.dot(dat, (ii <= jj).astype(F32), precision=hp, preferred_element_type=F32)
    a_last = acum[q - 1:q, :]
    expand = exp_ref[...]
    ea_x = jnp.dot(jnp.exp(acum), expand, precision=hp, preferred_element_type=F32)
    wdt_x = jnp.dot(jnp.exp(a_last - acum) * dt, expand, precision=hp, preferred_element_type=F32)
    dec_x = jnp.dot(jnp.broadcast_to(jnp.exp(a_last), (SUBLANES, LANES)), expand, precision=hp,
                    preferred_element_type=F32)[0:1, :]

    for g in range(SSD_GROUPS):
        gsl = slice(g * gw, (g + 1) * gw)
        bsl = slice(SSD_INNER + g * SSD_STATE, SSD_INNER + (g + 1) * SSD_STATE)
        csl = slice(SSD_INNER + SSD_GROUPS * SSD_STATE + g * SSD_STATE,
                    SSD_INNER + SSD_GROUPS * SSD_STATE + (g + 1) * SSD_STATE)
        bmf = xc_scr[:, bsl]
        bm = bmf.astype(BF16)
        cm = xc_scr[:, csl].astype(BF16)
        xg = xc_scr[:, gsl]
        xgb = xg.astype(BF16)
        cbm = lax.dot_general(cm, bm, (((1,), (1,)), ((), ())), preferred_element_type=F32)
        xw = (xg * wdt_x[:, gsl]).astype(BF16)
        st = jnp.dot(bmf.T.astype(BF16), xw, preferred_element_type=F32)
        s_in = s_scr[g]
        y_off = jnp.dot(cm, s_in.astype(BF16), preferred_element_type=F32) * ea_x[:, gsl]
        s_scr[g] = s_in * dec_x[:, gsl] + st
        for r in range(SSD_HEADS // SSD_GROUPS):
            h = g * (SSD_HEADS // SSD_GROUPS) + r
            seg = acum[:, SM_DT + h:SM_DT + h + 1] - acumt[SM_DT + h:SM_DT + h + 1, :]
            lmat = jnp.exp(jnp.where(causal, seg, -jnp.inf))
            wmat = (cbm * lmat * dtt[SM_DT + h:SM_DT + h + 1, :]).astype(BF16)
            rsl = slice(r * SSD_HEAD_DIM, (r + 1) * SSD_HEAD_DIM)
            hsl = slice(h * SSD_HEAD_DIM, (h + 1) * SSD_HEAD_DIM)
            y_diag = jnp.dot(wmat, xgb[:, rsl], preferred_element_type=F32)
            y_scr[:, hsl] = y_diag + y_off[:, rsl] + xg[:, rsl] * dsk_ref[:, hsl]

    @pl.when(c == n_chunks - 1)
    def _():
        for g in range(SSD_GROUPS):
            snew_ref[g] = s_scr[g].T

    y = y_scr[...] * _silu(z_ref[...])
    y_ref[...] = _rms(y, nw_ref[...]).astype(BF16)


def _ssd(xbc, z, sm, conv_prev8, ssm_prev, p, t_valid):
    b, tp, _ = xbc.shape
    nc = tp // BLK
    gw = SSD_HEADS // SSD_GROUPS * SSD_HEAD_DIM
    seq = lambda width: pl.BlockSpec((None, BLK, width), lambda i, c: (i, c, 0))
    kern = functools.partial(_ssd_kernel, n_chunks=nc, t_valid=t_valid)
    return pl.pallas_call(
        kern,
        grid=(b, nc),
        in_specs=[seq(CONV_DIM), seq(SSD_INNER), seq(LANES),
                  pl.BlockSpec((None, HALO, CONV_DIM), lambda i, c: (i, 0, 0)),
                  pl.BlockSpec((None, SSD_GROUPS, gw, SSD_STATE), lambda i, c: (i, 0, 0, 0)),
                  _const_spec((CONV_WIDTH, CONV_DIM)), _const_spec((1, CONV_DIM)),
                  _const_spec((1, LANES)), _const_spec((LANES, 1)), _const_spec((1, LANES)), _const_spec((LANES, 1)),
                  _const_spec((1, SSD_INNER)), _const_spec((1, SSD_INNER)), _const_spec((LANES, SSD_INNER))],
        out_specs=[seq(SSD_INNER),
                   pl.BlockSpec((None, SSD_GROUPS, gw, SSD_STATE), lambda i, c: (i, 0, 0, 0)),
                   pl.BlockSpec((None, HALO, CONV_DIM), lambda i, c: (i, 0, 0))],
        out_shape=[jax.ShapeDtypeStruct((b, tp, SSD_INNER), BF16),
                   jax.ShapeDtypeStruct((b, SSD_GROUPS, gw, SSD_STATE), F32),
                   jax.ShapeDtypeStruct((b, HALO, CONV_DIM), F32)],
        scratch_shapes=[pltpu.VMEM((SSD_GROUPS, SSD_STATE, gw), F32),
                        pltpu.VMEM((BLK + 2 * HALO, CONV_DIM), F32),
                        pltpu.VMEM((BLK, CONV_DIM), F32),
                        pltpu.VMEM((BLK, SSD_INNER), F32)],
        compiler_params=_cparams(("parallel", "arbitrary")),
        name="ssd",
    )(xbc, z, sm, conv_prev8, ssm_prev, p["conv_w"], p["conv_b"], p["dtb"], p["dtb_t"], p["alog"], p["alog_t"],
      p["dskip_x"], p["ssd_norm_w"], p["expand"])


N_BIAS_TILES = 5


def _bias_kernel(rb_ref, bt_ref):
    nb = REL_BUCKETS // 2
    max_exact = nb // 2
    qq = lax.broadcasted_iota(jnp.int32, (BLK, BLK), 0)
    kk = lax.broadcasted_iota(jnp.int32, (BLK, BLK), 1)
    for u in range(N_BIAS_TILES):
        rel = (u - 2) * BLK + kk - qq
        n = jnp.abs(rel)
        nf = jnp.maximum(n, 1).astype(F32)
        large = max_exact + (jnp.log(nf / max_exact) / math.log(REL_MAX_DIST / max_exact)
                             * (nb - max_exact)).astype(jnp.int32)
        large = jnp.minimum(large, nb - 1)
        bucket = jnp.where(rel > 0, nb, 0) + jnp.where(n < max_exact, n, large)
        for h in range(N_HEADS):
            acc = jnp.zeros((BLK, BLK), F32)
            for bkt in range(REL_BUCKETS):
                acc = jnp.where(bucket == bkt, rb_ref[bkt, h], acc)
            bt_ref[h * N_BIAS_TILES + u] = acc


def _bias_tiles(rel_bias):
    return pl.pallas_call(
        _bias_kernel,
        in_specs=[pl.BlockSpec(memory_space=pltpu.SMEM)],
        out_specs=pl.BlockSpec(memory_space=pltpu.VMEM),
        out_shape=jax.ShapeDtypeStruct((N_HEADS * N_BIAS_TILES, BLK, BLK), F32),
        name="bias_tiles",
    )(rel_bias)


KB = 2 * BLK


def _attn_kernel(qn_ref, qi_ref, sm_ref, ki_ref, k_ref, v_ref, bt_ref, o_ref,
                 st_scr, m_scr, lg_scr, qis_scr, qs_scr, mrun_scr, lrun_scr, acc_scr, *,
                 nkp_total, qblk0, chunk_off, l_valid, n_sel):
    i = pl.program_id(1)
    r = BLK
    qb = qblk0 + i
    q0 = qb * BLK
    k_end = CHUNK * ((q0 + BLK - 1 + chunk_off) // CHUNK + 1) - chunk_off
    nkp = jnp.minimum(nkp_total, (k_end + KB - 1) // KB)
    n_keys = nkp_total * KB

    nt = (((1,), (1,)), ((), ()))
    wit = sm_ref[...].T
    qpos = q0 + lax.broadcasted_iota(jnp.int32, (KB, r), 1)
    qchunk = (qpos + chunk_off) // CHUNK
    krow = lax.broadcasted_iota(jnp.int32, (KB, r), 0)

    fold_rows = 8 * SUBLANES

    def fold(x, op):
        return op(x.reshape(KB // fold_rows, fold_rows, r), axis=0)

    for h in range(IDX_HEADS):
        qis_scr[h * r:(h + 1) * r, :] = qi_ref[:, h * IDX_DIM:(h + 1) * IDX_DIM]

    def score_body(jp, carry):
        mn, mx = carry
        kij = ki_ref[jp][:, SM_KI:SM_KI + IDX_DIM]
        acc = jnp.zeros((KB, r), F32)
        for hp in range(IDX_HEADS // 2):
            sh = lax.dot_general(kij, qis_scr[2 * hp * r:2 * (hp + 1) * r, :], nt, preferred_element_type=F32)
            for e in range(2):
                h = 2 * hp + e
                acc = acc + wit[SM_WI + h:SM_WI + h + 1, :] * jnp.maximum(sh[:, e * r:(e + 1) * r], 0.0)
        kpos = jp * KB + krow
        adm = ((kpos + chunk_off) // CHUNK <= qchunk) & (kpos < l_valid)
        s = jnp.where(adm, acc * (IDX_DIM ** -0.5), -jnp.inf)
        st_scr[jp] = s
        mn = jnp.minimum(mn, fold(jnp.where(adm, s, jnp.inf), jnp.min))
        mx = jnp.maximum(mx, fold(s, jnp.max))
        return mn, mx

    init = (jnp.full((fold_rows, r), jnp.inf, F32), jnp.full((fold_rows, r), -jnp.inf, F32))
    mn, mx = lax.fori_loop(0, nkp, score_body, init)
    lo0 = jnp.min(mn, axis=0, keepdims=True)
    hi0 = jnp.max(mx, axis=0, keepdims=True)
    n_adm = jnp.minimum(CHUNK * (qchunk[0:1, :] + 1) - chunk_off, l_valid)
    kk = jnp.minimum(n_adm, n_sel).astype(F32)

    def count(pred):
        def body(jp, acc):
            return acc + fold(jnp.where(pred(st_scr[jp], jp), 1.0, 0.0), jnp.sum)
        acc = lax.fori_loop(0, nkp, body, jnp.zeros((fold_rows, r), F32))
        return jnp.sum(acc, axis=0, keepdims=True)

    def bisect_body(_, carry):
        lo, hi, ub, _ = carry
        mid = 0.5 * lo + 0.5 * hi
        cnt = count(lambda s, jp: s >= mid)
        ok = cnt >= kk
        hit = cnt == kk
        lo = jnp.where(ok, mid, lo)
        hi = jnp.where(ok & ~hit, hi, mid)
        ub = jnp.where(ok, ub, mid)
        return lo, hi, ub, jnp.where(hit, 1.0, 0.0)

    all_adm = n_adm.astype(F32) == kk
    hi_init = jnp.where(all_adm, lo0, hi0)
    lo_f, hi_f, ub, hit_f = lax.fori_loop(
        0, BISECT_STEPS, bisect_body, (lo0, hi_init, jnp.full((1, r), jnp.inf, F32), jnp.zeros((1, r), F32)))
    mid_f = 0.5 * lo_f + 0.5 * hi_f
    pending = jnp.sum(jnp.where(hit_f > 0.0, 0, 1))

    def exact_path():
        hit = hit_f > 0.0

        def next_below(ub):
            def body(jp, acc):
                s = st_scr[jp]
                return jnp.maximum(acc, fold(jnp.where(s < ub, s, -jnp.inf), jnp.max))
            acc = lax.fori_loop(0, nkp, body, jnp.full((fold_rows, r), -jnp.inf, F32))
            return jnp.max(acc, axis=0, keepdims=True)

        def descend_cond(carry):
            _, _, todo, it = carry
            return (todo > 0) & (it < n_keys)

        def descend_body(carry):
            ub, _, _, it = carry
            t = next_below(ub)
            done = hit | (count(lambda s, jp: s >= t) >= kk)
            return jnp.where(done, ub, t), t, jnp.sum(jnp.where(done, 0, 1)), it + 1

        _, t, _, _ = lax.while_loop(descend_cond, descend_body, (ub, lo0, jnp.int32(1), jnp.int32(0)))
        thr = jnp.where(hit, mid_f, t)
        n_gt = count(lambda s, jp: s > thr)
        n_eq = count(lambda s, jp: s == thr)
        need = kk - n_gt

        def tie_search(_, carry):
            lo, hi = carry
            mid = (lo + hi) // 2
            ok = count(lambda s, jp: (s == thr) & (jp * KB + krow < mid)) >= need
            return jnp.where(ok, lo, mid), jnp.where(ok, mid, hi)

        tie_steps = max(1, math.ceil(math.log2(n_keys + 1)))
        key_limit = lax.cond(
            jnp.sum(jnp.where(n_eq > need, 1, 0)) > 0,
            lambda: lax.fori_loop(0, tie_steps, tie_search,
                                  (jnp.zeros((1, r), jnp.int32), jnp.full((1, r), n_keys, jnp.int32)))[1],
            lambda: jnp.full((1, r), n_keys, jnp.int32))
        return thr, key_limit

    thr, key_limit = lax.cond(pending > 0, exact_path, lambda: (mid_f, jnp.full((1, r), n_keys, jnp.int32)))

    def mask_body(jp, carry):
        s = st_scr[jp]
        sel = (s > thr) | ((s == thr) & (jp * KB + krow < key_limit))
        m_scr[jp] = jnp.where(sel, 0.0, NEG_BIG).T
        return carry

    lax.fori_loop(0, nkp, mask_body, 0)

    scale = HEAD_DIM ** -0.5
    rep = N_HEADS // N_KV_HEADS
    for h in range(N_HEADS):
        qs_scr[h // rep, (h % rep) * r:(h % rep + 1) * r, :] = qn_ref[:, h * HEAD_DIM:(h + 1) * HEAD_DIM]
    mrun_scr[...] = jnp.full(mrun_scr.shape, -jnp.inf, F32)
    lrun_scr[...] = jnp.zeros(lrun_scr.shape, F32)
    acc_scr[...] = jnp.zeros(acc_scr.shape, F32)

    def logit_body(jp, carry):
        u0 = jnp.clip(2 * jp - qb + 2, 0, N_BIAS_TILES - 1)
        u1 = jnp.clip(2 * jp + 1 - qb + 2, 0, N_BIAS_TILES - 1)
        madd = m_scr[jp]
        for g in range(N_KV_HEADS):
            lt = lax.dot_general(qs_scr[g], k_ref[jp, :, g * HEAD_DIM:(g + 1) * HEAD_DIM], nt,
                                 preferred_element_type=F32)
            for e in range(rep):
                h = g * rep + e
                bias = jnp.concatenate([bt_ref[h * N_BIAS_TILES + u0], bt_ref[h * N_BIAS_TILES + u1]], axis=1)
                lg = lt[e * r:(e + 1) * r, :] * scale + bias + madd
                lg_scr[h, jp] = lg
                mrun_scr[h] = jnp.maximum(mrun_scr[h], jnp.maximum(lg[:, :BLK], lg[:, BLK:]))
        return carry

    lax.fori_loop(0, nkp, logit_body, 0)
    for h in range(N_HEADS):
        mrun_scr[h] = jnp.broadcast_to(jnp.max(mrun_scr[h], axis=1, keepdims=True), (r, BLK))

    def pv_body(jp, carry):
        for g in range(N_KV_HEADS):
            es = []
            for e in range(rep):
                h = g * rep + e
                mrow = mrun_scr[h]
                ex = jnp.exp(lg_scr[h, jp] - jnp.concatenate([mrow, mrow], axis=1))
                lrun_scr[h] = lrun_scr[h] + (ex[:, :BLK] + ex[:, BLK:])
                es.append(ex.astype(BF16))
            acc_scr[g] = acc_scr[g] + jnp.dot(jnp.concatenate(es, axis=0),
                                              v_ref[jp, :, g * HEAD_DIM:(g + 1) * HEAD_DIM],
                                              preferred_element_type=F32)
        return carry

    lax.fori_loop(0, nkp, pv_body, 0)
    for h in range(N_HEADS):
        g, e = h // rep, h % rep
        den = jnp.sum(lrun_scr[h], axis=1, keepdims=True)
        o_ref[:, h * HEAD_DIM:(h + 1) * HEAD_DIM] = (acc_scr[g, e * r:(e + 1) * r, :] / den).astype(BF16)


def _attn(qn, qi, sm, ki_b, k_b, v_b, bias_tiles, *, qblk0, chunk_off, l_valid, n_sel):
    b, tq, _ = qn.shape
    nkp_total = k_b.shape[1]
    rep = N_HEADS // N_KV_HEADS
    seq = lambda width: pl.BlockSpec((None, BLK, width), lambda bi, i: (bi, i, 0))
    keys = lambda width: pl.BlockSpec((None, nkp_total, KB, width), lambda bi, i: (bi, 0, 0, 0))
    kern = functools.partial(_attn_kernel, nkp_total=nkp_total, qblk0=qblk0, chunk_off=chunk_off, l_valid=l_valid,
                             n_sel=n_sel)
    return pl.pallas_call(
        kern,
        grid=(b, tq // BLK),
        in_specs=[seq(ATT_INNER), seq(IDX_HEADS * IDX_DIM), seq(LANES), keys(LANES), keys(KV_DIM), keys(KV_DIM),
                  _const_spec((N_HEADS * N_BIAS_TILES, BLK, BLK))],
        out_specs=seq(ATT_INNER),
        out_shape=jax.ShapeDtypeStruct((b, tq, ATT_INNER), BF16),
        scratch_shapes=[pltpu.VMEM((nkp_total, KB, BLK), F32),
                        pltpu.VMEM((nkp_total, BLK, KB), F32),
                        pltpu.VMEM((N_HEADS, nkp_total, BLK, KB), F32),
                        pltpu.VMEM((IDX_HEADS * BLK, IDX_DIM), BF16),
                        pltpu.VMEM((N_KV_HEADS, rep * BLK, HEAD_DIM), BF16),
                        pltpu.VMEM((N_HEADS, BLK, BLK), F32),
                        pltpu.VMEM((N_HEADS, BLK, BLK), F32),
                        pltpu.VMEM((N_KV_HEADS, rep * BLK, HEAD_DIM), F32)],
        compiler_params=_cparams(("parallel", "arbitrary")),
        name="attn",
    )(qn, qi, sm, ki_b, k_b, v_b, bias_tiles)


def _out_ffn_kernel(x_ref, ys_ref, ya_ref, gs_ref, ga_ref, wbs_ref, wba_ref, wo_ref, n2_ref, wg_ref, wu_ref, wd_ref,
                    y_ref):
    dot = functools.partial(jnp.dot, preferred_element_type=F32)
    merged = (jax.nn.sigmoid(gs_ref[...]) * dot(ys_ref[...], wbs_ref[...])
              + jax.nn.sigmoid(ga_ref[...]) * dot(ya_ref[...], wba_ref[...]))
    h = x_ref[...] + dot(merged.astype(BF16), wo_ref[...])
    hn = _rms(h, n2_ref[...]).astype(BF16)
    act = (_silu(dot(hn, wg_ref[...])) * dot(hn, wu_ref[...])).astype(BF16)
    y_ref[...] = h + dot(act, wd_ref[...])


def _out_ffn(x2d, ys, ya, gs, ga, p):
    n = x2d.shape[0]
    tm = ROW_TILE
    d_ff = p["w_gate"].shape[1]
    row = lambda width: pl.BlockSpec((tm, width), lambda i: (i, 0))
    wspec = lambda shape: pl.BlockSpec(shape, lambda i: (0, 0), pipeline_mode=pl.Buffered(1))
    return pl.pallas_call(
        _out_ffn_kernel,
        grid=(n // tm,),
        in_specs=[row(D_MODEL), row(SSD_INNER), row(ATT_INNER), row(D_MODEL), row(D_MODEL),
                  wspec((SSD_INNER, D_MODEL)), wspec((ATT_INNER, D_MODEL)), wspec((D_MODEL, D_MODEL)),
                  _const_spec((1, D_MODEL)), wspec((D_MODEL, d_ff)), wspec((D_MODEL, d_ff)), wspec((d_ff, D_MODEL))],
        out_specs=row(D_MODEL),
        out_shape=jax.ShapeDtypeStruct((n, D_MODEL), F32),
        compiler_params=_cparams(("parallel",)),
        name="out_ffn",
    )(x2d, ys, ya, gs, ga, p["w_br_ssd"], p["w_br_att"], p["w_out"], p["norm2_w"], p["w_gate"], p["w_up"],
      p["w_down"])


def _layer(x_pad, t_valid, conv_prev, ssm_prev, past, p, bias_tiles, *, qblk0, chunk_off, n_sel):
    b, tp, _ = x_pad.shape
    x2d = x_pad.reshape(b * tp, D_MODEL)
    z, xbc, qn, k32, v32, kb, vb, qi, sm, smb, gs, ga = _in_proj(
        x2d, p["norm1_w"], p["w_in"], p["q_norm_w"], p["k_norm_w"], p["idx_k_norm_w"])
    seq = lambda a: a.reshape(b, tp, a.shape[-1])

    conv_prev8 = jnp.pad(conv_prev.astype(F32), ((0, 0), (HALO - (CONV_WIDTH - 1), 0), (0, 0)))
    gw = SSD_HEADS // SSD_GROUPS * SSD_HEAD_DIM
    y_ssd, ssm_new, conv_new8 = _ssd(seq(xbc), seq(z), seq(sm), conv_prev8,
                                     ssm_prev.astype(F32).reshape(b, SSD_GROUPS, gw, SSD_STATE), p, t_valid)

    kb, vb, kib = seq(kb), seq(vb), seq(smb)
    if past is not None:
        pk, pv, pki = past
        n_past = pk.shape[1]
        kb = jnp.concatenate([pk.reshape(b, n_past, KV_DIM).astype(BF16), kb], axis=1)
        vb = jnp.concatenate([pv.reshape(b, n_past, KV_DIM).astype(BF16), vb], axis=1)
        pki = jnp.pad(pki.astype(BF16), ((0, 0), (0, 0), (SM_KI, LANES - SM_KI - IDX_DIM)))
        kib = jnp.concatenate([pki, kib], axis=1)
    else:
        n_past = 0
    def blocks(a):
        a = jnp.pad(a, ((0, 0), (0, -a.shape[1] % KB), (0, 0)))
        return a.reshape(b, a.shape[1] // KB, KB, a.shape[-1])

    y_att = _attn(seq(qn), seq(qi), seq(sm), blocks(kib), blocks(kb), blocks(vb), bias_tiles,
                  qblk0=qblk0, chunk_off=chunk_off, l_valid=n_past + t_valid, n_sel=n_sel)

    y = _out_ffn(x2d, y_ssd.reshape(b * tp, SSD_INNER), y_att.reshape(b * tp, ATT_INNER), gs, ga, p)

    k_new = seq(k32)[:, :t_valid].reshape(b, t_valid, N_KV_HEADS, HEAD_DIM)
    v_new = seq(v32)[:, :t_valid].reshape(b, t_valid, N_KV_HEADS, HEAD_DIM)
    ki_new = seq(sm)[:, :t_valid, SM_KI:SM_KI + IDX_DIM]
    ssm_new = ssm_new.reshape(b, SSD_HEADS, SSD_HEAD_DIM, SSD_STATE)
    conv_new = conv_new8[:, HALO - (CONV_WIDTH - 1):]
    return y.reshape(b, tp, D_MODEL), k_new, v_new, ki_new, ssm_new, conv_new


def _prepare_params(l, norm1_w, w_in, conv_w, conv_b, dt_bias, a_log, d_skip, ssd_norm_w, q_norm_w, k_norm_w,
                    idx_k_norm_w, w_br_ssd, w_br_att, w_out, norm2_w, w_gate, w_up, w_down):
    offs = [0]
    for w in IN_WIDTHS:
        offs.append(offs[-1] + w)
    seg = lambda i: w_in[l][:, offs[i]:offs[i + 1]]
    i_z, i_xbc, i_dt, i_q, i_k, i_v, i_qi, i_ki, i_wi, i_gs, i_ga = range(11)
    pad = jnp.zeros((D_MODEL, LANES - IDX_DIM - SSD_HEADS - IDX_HEADS), w_in.dtype)
    w_perm = jnp.concatenate([seg(i_z), seg(i_xbc), seg(i_q), seg(i_k), seg(i_v), seg(i_qi), seg(i_gs), seg(i_ga),
                              seg(i_ki), seg(i_dt), seg(i_wi), pad], axis=1).astype(BF16)

    def lanes_at(vec, start):
        return jnp.zeros((1, LANES), F32).at[0, start:start + vec.shape[0]].set(vec.astype(F32))

    dtb = lanes_at(dt_bias[l], SM_DT)
    alog = lanes_at(a_log[l], SM_DT)
    head_of_channel = jnp.arange(SSD_INNER) // SSD_HEAD_DIM
    expand = (jnp.arange(LANES)[:, None] == head_of_channel[None, :] + SM_DT).astype(F32)
    row = lambda v: v.astype(F32).reshape(1, -1)
    return dict(
        norm1_w=row(norm1_w[l]), w_in=w_perm, conv_w=conv_w[l].astype(F32), conv_b=row(conv_b[l]),
        dtb=dtb, dtb_t=dtb.reshape(LANES, 1), alog=alog, alog_t=alog.reshape(LANES, 1),
        dskip_x=row(jnp.repeat(d_skip[l], SSD_HEAD_DIM)), ssd_norm_w=row(ssd_norm_w[l]), expand=expand,
        q_norm_w=row(q_norm_w[l]), k_norm_w=row(k_norm_w[l]),
        idx_k_norm_w=jnp.ones((1, LANES), F32).at[0, SM_KI:SM_KI + IDX_DIM].set(idx_k_norm_w[l].astype(F32)),
        w_br_ssd=w_br_ssd[l].astype(BF16), w_br_att=w_br_att[l].astype(BF16), w_out=w_out[l].astype(BF16),
        norm2_w=row(norm2_w[l]), w_gate=w_gate[l].astype(BF16), w_up=w_up[l].astype(BF16),
        w_down=w_down[l].astype(BF16))


def _pad_rows(x, tp):
    return jnp.pad(x, ((0, 0), (0, tp - x.shape[1]), (0, 0)))


def kernel(x_prompt, x_sample, cache_k, cache_v, cache_kidx, state_ssm, state_conv, meta_tokens, rel_bias, norm1_w,
           w_in, conv_w, conv_b, dt_bias, a_log, d_skip, ssd_norm_w, q_norm_w, k_norm_w, idx_k_norm_w, w_br_ssd,
           w_br_att, w_out, norm2_w, w_gate, w_up, w_down):
    bp, sp, _ = x_prompt.shape
    bs, ts, _ = x_sample.shape
    depth = w_in.shape[0]
    past = cache_k.shape[2]
    assert past % BLK == 0 and BLK % CHUNK == 0 and N_META <= CHUNK

    tq_p = N_META + sp
    tp_p = -(-tq_p // BLK) * BLK
    tp_s = -(-ts // BLK) * BLK
    n_sel_p = min(TOPK_MAX, sp // 4)
    n_sel_s = min(TOPK_MAX, (past + ts) // 4)

    hp = jnp.concatenate([jnp.broadcast_to(meta_tokens.astype(x_prompt.dtype)[None], (bp, N_META, D_MODEL)),
                          x_prompt], axis=1)
    hp = _pad_rows(hp, tp_p)
    hs = _pad_rows(x_sample, tp_s)
    conv0 = jnp.zeros((bp, CONV_WIDTH - 1, CONV_DIM), F32)
    ssm0 = jnp.zeros((bp, SSD_HEADS, SSD_HEAD_DIM, SSD_STATE), F32)
    bias_tiles = _bias_tiles(rel_bias.astype(F32))

    outs_p, outs_s = [], []
    for l in range(depth):
        p = _prepare_params(l, norm1_w, w_in, conv_w, conv_b, dt_bias, a_log, d_skip, ssd_norm_w, q_norm_w, k_norm_w,
                            idx_k_norm_w, w_br_ssd, w_br_att, w_out, norm2_w, w_gate, w_up, w_down)
        hp, *rest_p = _layer(hp, tq_p, conv0, ssm0, None, p, bias_tiles,
                             qblk0=0, chunk_off=CHUNK - N_META, n_sel=n_sel_p)
        hs, *rest_s = _layer(hs, ts, state_conv[l], state_ssm[l], (cache_k[l], cache_v[l], cache_kidx[l]), p,
                             bias_tiles, qblk0=past // BLK, chunk_off=0, n_sel=n_sel_s)
        outs_p.append(rest_p)
        outs_s.append(rest_s)

    stack = lambda outs, idx, dt: jnp.stack([o[idx] for o in outs], 0).astype(dt)
    y_prompt = hp[:, N_META:tq_p]
    y_sample = hs[:, :ts]
    return (y_prompt, y_sample,
            stack(outs_p, 0, x_prompt.dtype), stack(outs_p, 1, x_prompt.dtype), stack(outs_p, 2, x_prompt.dtype),
            stack(outs_p, 3, state_ssm.dtype), stack(outs_p, 4, x_prompt.dtype),
            stack(outs_s, 0, x_sample.dtype), stack(outs_s, 1, x_sample.dtype), stack(outs_s, 2, x_sample.dtype),
            stack(outs_s, 3, state_ssm.dtype), stack(outs_s, 4, x_sample.dtype))
```

```python
import functools
import math

import jax
import jax.numpy as jnp
from jax import lax
from jax.experimental import pallas as pl
from jax.experimental.pallas import tpu as pltpu

F32 = jnp.float32
BF16 = jnp.bfloat16

D_MODEL = 1024
CHUNK = 64
N_META = 16
SSD_HEADS = 16
SSD_HEAD_DIM = 64
SSD_INNER = SSD_HEADS * SSD_HEAD_DIM
SSD_GROUPS = 4
SSD_STATE = 128
CONV_WIDTH = 4
CONV_DIM = SSD_INNER + 2 * SSD_GROUPS * SSD_STATE
N_HEADS = 8
N_KV_HEADS = 2
HEAD_DIM = 128
ATT_INNER = N_HEADS * HEAD_DIM
KV_DIM = N_KV_HEADS * HEAD_DIM
IDX_HEADS = 8
IDX_DIM = 64
TOPK_MAX = 256
REL_BUCKETS = 32
REL_MAX_DIST = 128
IN_WIDTHS = (SSD_INNER, CONV_DIM, SSD_HEADS, ATT_INNER, KV_DIM, KV_DIM, IDX_HEADS * IDX_DIM, IDX_DIM, IDX_HEADS,
             D_MODEL, D_MODEL)
EPS = 1e-6

LANES = 128
SUBLANES = 8
VMEM_LIMIT_BYTES = 56 * 1024 * 1024

BLK = LANES
ROW_TILE = 256
HALO = SUBLANES

C_Z = 0
C_XBC = C_Z + SSD_INNER
C_Q = C_XBC + CONV_DIM
C_K = C_Q + ATT_INNER
C_V = C_K + KV_DIM
C_QI = C_V + KV_DIM
C_GS = C_QI + IDX_HEADS * IDX_DIM
C_GA = C_GS + D_MODEL
C_SM = C_GA + D_MODEL
IN_PAD = C_SM + LANES
SM_KI = 0
SM_DT = SM_KI + IDX_DIM
SM_WI = SM_DT + SSD_HEADS

BISECT_STEPS = 24
NEG_BIG = -1e30


def _cparams(sem):
    return pltpu.CompilerParams(dimension_semantics=sem, vmem_limit_bytes=VMEM_LIMIT_BYTES)


def _const_spec(shape):
    nd = len(shape)
    return pl.BlockSpec(shape, lambda *_: (0,) * nd)


def _rms(x, w):
    return x * lax.rsqrt(jnp.mean(x * x, axis=-1, keepdims=True) + EPS) * w


def _silu(x):
    return x * jax.nn.sigmoid(x)


def _in_proj_kernel(x_ref, n1_ref, w_ref, qn_ref, kn_ref, kin_ref,
                    z_ref, xbc_ref, q_ref, k_ref, v_ref, kb_ref, vb_ref, qi_ref, sm_ref, smb_ref, gs_ref, ga_ref):
    hn = _rms(x_ref[...], n1_ref[...]).astype(BF16)

    def mm(lo, hi):
        return jnp.dot(hn, w_ref[:, lo:hi], preferred_element_type=F32)

    z_ref[...] = mm(C_Z, C_XBC)
    xbc_ref[...] = mm(C_XBC, C_Q)
    q = mm(C_Q, C_K)
    for h in range(N_HEADS):
        sl = slice(h * HEAD_DIM, (h + 1) * HEAD_DIM)
        q_ref[:, sl] = _rms(q[:, sl], qn_ref[...]).astype(BF16)
    k = mm(C_K, C_V)
    for h in range(N_KV_HEADS):
        sl = slice(h * HEAD_DIM, (h + 1) * HEAD_DIM)
        kh = _rms(k[:, sl], kn_ref[...])
        k_ref[:, sl] = kh
        kb_ref[:, sl] = kh.astype(BF16)
    v = mm(C_V, C_QI)
    v_ref[...] = v
    vb_ref[...] = v.astype(BF16)
    qi_ref[...] = mm(C_QI, C_GS).astype(BF16)
    gs_ref[...] = mm(C_GS, C_GA)
    ga_ref[...] = mm(C_GA, C_SM)
    sm = mm(C_SM, IN_PAD)
    lane = lax.broadcasted_iota(jnp.int32, sm.shape, 1)
    is_ki = lane < SM_KI + IDX_DIM
    ms = jnp.sum(jnp.where(is_ki, sm * sm, 0.0), axis=-1, keepdims=True) * (1.0 / IDX_DIM)
    ki = sm * lax.rsqrt(ms + EPS) * kin_ref[...]
    is_wi = (lane >= SM_WI) & (lane < SM_WI + IDX_HEADS)
    out = jnp.where(is_ki, ki, jnp.where(is_wi, sm * (IDX_HEADS ** -0.5), sm))
    sm_ref[...] = out
    smb_ref[...] = out.astype(BF16)


def _in_proj(x2d, n1, w_perm, qn, kn, kin_pad):
    n = x2d.shape[0]
    tm = ROW_TILE
    row = lambda width: pl.BlockSpec((tm, width), lambda i: (i, 0))
    outs = [
        (SSD_INNER, F32), (CONV_DIM, F32), (ATT_INNER, BF16), (KV_DIM, F32), (KV_DIM, F32), (KV_DIM, BF16),
        (KV_DIM, BF16), (IDX_HEADS * IDX_DIM, BF16), (LANES, F32), (LANES, BF16), (D_MODEL, F32), (D_MODEL, F32),
    ]
    return pl.pallas_call(
        _in_proj_kernel,
        grid=(n // tm,),
        in_specs=[row(D_MODEL), _const_spec((1, D_MODEL)),
                  pl.BlockSpec((D_MODEL, IN_PAD), lambda i: (0, 0), pipeline_mode=pl.Buffered(1)),
                  _const_spec((1, HEAD_DIM)), _const_spec((1, HEAD_DIM)), _const_spec((1, LANES))],
        out_specs=[row(w) for w, _ in outs],
        out_shape=[jax.ShapeDtypeStruct((n, w), dt) for w, dt in outs],
        compiler_params=_cparams(("parallel",)),
        name="in_proj",
    )(x2d, n1, w_perm, qn, kn, kin_pad)


def _softplus(x):
    return jnp.maximum(x, 0.0) + jnp.log1p(jnp.exp(-jnp.abs(x)))


def _split3(x):
    hi = x.astype(BF16)
    r1 = x - hi.astype(F32)
    mid = r1.astype(BF16)
    lo = (r1 - mid.astype(F32)).astype(BF16)
    return hi, mid, lo


def _ssd_kernel(xbc_ref, z_ref, sm_ref, cprev_ref, sprev_ref, cw_ref, cb_ref, dtb_ref, dtbt_ref, alog_ref, alogt_ref,
                dsk_ref, nw_ref, exp_ref,
                y_ref, snew_ref, cnew_ref,
                s_scr, xpad_scr, xc_scr, y_scr, *, n_chunks, t_valid):
    c = pl.program_id(1)
    q = BLK
    gw = SSD_HEADS // SSD_GROUPS * SSD_HEAD_DIM

    @pl.when(c == 0)
    def _():
        for g in range(SSD_GROUPS):
            s_scr[g] = sprev_ref[g].T
        xpad_scr[0:HALO, :] = cprev_ref[...]

    xpad_scr[HALO:HALO + q, :] = xbc_ref[...]

    slab = 512
    for cc in range(CONV_DIM // slab):
        sl = slice(cc * slab, (cc + 1) * slab)
        acc = xpad_scr[HALO - 3:HALO - 3 + q, sl] * cw_ref[0:1, sl]
        for i in range(1, CONV_WIDTH):
            acc = acc + xpad_scr[HALO - 3 + i:HALO - 3 + i + q, sl] * cw_ref[i:i + 1, sl]
        xc_scr[:, sl] = _silu(cb_ref[:, sl] + acc)

    n_last = t_valid - (n_chunks - 1) * q

    @pl.when(c == n_chunks - 1)
    def _():
        cnew_ref[...] = xpad_scr[n_last:n_last + HALO, :]

    xpad_scr[0:HALO, :] = xpad_scr[q:q + HALO, :]

    sm = sm_ref[...]
    smt = sm.T
    lane = lax.broadcasted_iota(jnp.int32, (q, LANES), 1)
    row = lax.broadcasted_iota(jnp.int32, (q, LANES), 0)
    is_dt = (lane >= SM_DT) & (lane < SM_DT + SSD_HEADS) & (row + c * q < t_valid)
    is_dt_t = (row >= SM_DT) & (row < SM_DT + SSD_HEADS) & (lane + c * q < t_valid)
    dt = jnp.where(is_dt, _softplus(sm + dtb_ref[...]), 0.0)
    dtt = jnp.where(is_dt_t, _softplus(smt + dtbt_ref[...]), 0.0)
    da = dt * (-jnp.exp(alog_ref[...]))
    dat = dtt * (-jnp.exp(alogt_ref[...]))
    ii = lax.broadcasted_iota(jnp.int32, (q, q), 0)
    jj = lax.broadcasted_iota(jnp.int32, (q, q), 1)
    causal = jj <= ii
    acum = sum(jnp.dot(causal.astype(BF16), p, preferred_element_type=F32) for p in _split3(da))
    acumt = sum(jnp.dot(p, (ii <= jj).astype(BF16), preferred_element_type=F32) for p in _split3(dat))
    a_last = acum[q - 1:q, :]
    expand = exp_ref[...]
    stacked = jnp.concatenate([jnp.exp(acum), jnp.exp(a_last - acum) * dt,
                               jnp.broadcast_to(jnp.exp(a_last), (SUBLANES, LANES))], axis=0)
    stacked_x = sum(jnp.dot(p, expand, preferred_element_type=F32) for p in _split3(stacked))
    ea_x = stacked_x[0:q]
    wdt_x = stacked_x[q:2 * q]
    dec_x = stacked_x[2 * q:2 * q + 1]

    for g in range(SSD_GROUPS):
        gsl = slice(g * gw, (g + 1) * gw)
        bsl = slice(SSD_INNER + g * SSD_STATE, SSD_INNER + (g + 1) * SSD_STATE)
        csl = slice(SSD_INNER + SSD_GROUPS * SSD_STATE + g * SSD_STATE,
                    SSD_INNER + SSD_GROUPS * SSD_STATE + (g + 1) * SSD_STATE)
        bmf = xc_scr[:, bsl]
        bm = bmf.astype(BF16)
        cm = xc_scr[:, csl].astype(BF16)
        xg = xc_scr[:, gsl]
        xgb = xg.astype(BF16)
        cbm = lax.dot_general(cm, bm, (((1,), (1,)), ((), ())), preferred_element_type=F32)
        xw = (xg * wdt_x[:, gsl]).astype(BF16)
        st = jnp.dot(bmf.T.astype(BF16), xw, preferred_element_type=F32)
        s_in = s_scr[g]
        y_off = jnp.dot(cm, s_in.astype(BF16), preferred_element_type=F32) * ea_x[:, gsl]
        s_scr[g] = s_in * dec_x[:, gsl] + st
        for r in range(SSD_HEADS // SSD_GROUPS):
            h = g * (SSD_HEADS // SSD_GROUPS) + r
            seg = acum[:, SM_DT + h:SM_DT + h + 1] - acumt[SM_DT + h:SM_DT + h + 1, :]
            lmat = jnp.exp(jnp.where(causal, seg, -jnp.inf))
            wmat = (cbm * lmat * dtt[SM_DT + h:SM_DT + h + 1, :]).astype(BF16)
            rsl = slice(r * SSD_HEAD_DIM, (r + 1) * SSD_HEAD_DIM)
            hsl = slice(h * SSD_HEAD_DIM, (h + 1) * SSD_HEAD_DIM)
            y_diag = jnp.dot(wmat, xgb[:, rsl], preferred_element_type=F32)
            y_scr[:, hsl] = y_diag + y_off[:, rsl] + xg[:, rsl] * dsk_ref[:, hsl]

    @pl.when(c == n_chunks - 1)
    def _():
        for g in range(SSD_GROUPS):
            snew_ref[g] = s_scr[g].T

    y = y_scr[...] * _silu(z_ref[...])
    y_ref[...] = _rms(y, nw_ref[...]).astype(BF16)


def _ssd(xbc, z, sm, conv_prev8, ssm_prev, p, t_valid):
    b, tp, _ = xbc.shape
    nc = tp // BLK
    gw = SSD_HEADS // SSD_GROUPS * SSD_HEAD_DIM
    seq = lambda width: pl.BlockSpec((None, BLK, width), lambda i, c: (i, c, 0))
    kern = functools.partial(_ssd_kernel, n_chunks=nc, t_valid=t_valid)
    return pl.pallas_call(
        kern,
        grid=(b, nc),
        in_specs=[seq(CONV_DIM), seq(SSD_INNER), seq(LANES),
                  pl.BlockSpec((None, HALO, CONV_DIM), lambda i, c: (i, 0, 0)),
                  pl.BlockSpec((None, SSD_GROUPS, gw, SSD_STATE), lambda i, c: (i, 0, 0, 0)),
                  _const_spec((CONV_WIDTH, CONV_DIM)), _const_spec((1, CONV_DIM)),
                  _const_spec((1, LANES)), _const_spec((LANES, 1)), _const_spec((1, LANES)), _const_spec((LANES, 1)),
                  _const_spec((1, SSD_INNER)), _const_spec((1, SSD_INNER)), _const_spec((LANES, SSD_INNER))],
        out_specs=[seq(SSD_INNER),
                   pl.BlockSpec((None, SSD_GROUPS, gw, SSD_STATE), lambda i, c: (i, 0, 0, 0)),
                   pl.BlockSpec((None, HALO, CONV_DIM), lambda i, c: (i, 0, 0))],
        out_shape=[jax.ShapeDtypeStruct((b, tp, SSD_INNER), BF16),
                   jax.ShapeDtypeStruct((b, SSD_GROUPS, gw, SSD_STATE), F32),
                   jax.ShapeDtypeStruct((b, HALO, CONV_DIM), F32)],
        scratch_shapes=[pltpu.VMEM((SSD_GROUPS, SSD_STATE, gw), F32),
                        pltpu.VMEM((BLK + 2 * HALO, CONV_DIM), F32),
                        pltpu.VMEM((BLK, CONV_DIM), F32),
                        pltpu.VMEM((BLK, SSD_INNER), F32)],
        compiler_params=_cparams(("parallel", "arbitrary")),
        name="ssd",
    )(xbc, z, sm, conv_prev8, ssm_prev, p["conv_w"], p["conv_b"], p["dtb"], p["dtb_t"], p["alog"], p["alog_t"],
      p["dskip_x"], p["ssd_norm_w"], p["expand"])


N_BIAS_TILES = 5


def _bias_kernel(rb_ref, bt_ref):
    nb = REL_BUCKETS // 2
    max_exact = nb // 2
    qq = lax.broadcasted_iota(jnp.int32, (BLK, BLK), 0)
    kk = lax.broadcasted_iota(jnp.int32, (BLK, BLK), 1)
    for u in range(N_BIAS_TILES):
        rel = (u - 2) * BLK + kk - qq
        n = jnp.abs(rel)
        nf = jnp.maximum(n, 1).astype(F32)
        large = max_exact + (jnp.log(nf / max_exact) / math.log(REL_MAX_DIST / max_exact)
                             * (nb - max_exact)).astype(jnp.int32)
        large = jnp.minimum(large, nb - 1)
        bucket = jnp.where(rel > 0, nb, 0) + jnp.where(n < max_exact, n, large)
        for h in range(N_HEADS):
            acc = jnp.zeros((BLK, BLK), F32)
            for bkt in range(REL_BUCKETS):
                acc = jnp.where(bucket == bkt, rb_ref[bkt, h], acc)
            bt_ref[h * N_BIAS_TILES + u] = acc


def _bias_tiles(rel_bias):
    return pl.pallas_call(
        _bias_kernel,
        in_specs=[pl.BlockSpec(memory_space=pltpu.SMEM)],
        out_specs=pl.BlockSpec(memory_space=pltpu.VMEM),
        out_shape=jax.ShapeDtypeStruct((N_HEADS * N_BIAS_TILES, BLK, BLK), F32),
        name="bias_tiles",
    )(rel_bias)


KB = 2 * BLK


def _attn_kernel(*refs, nkp_total, n_past, qblk0, chunk_off, l_valid, n_sel):
    if n_past:
        pki_ref, pk_ref, pv_ref, *refs = refs
    (qn_ref, qi_ref, sm_ref, ki_new_ref, k_new_ref, v_new_ref, bt_ref, o_ref,
     ki_scr, k_scr, v_scr, st_scr, m_scr, lg_scr, qis_scr, qs_scr, mrun_scr, lrun_scr, acc_scr) = refs
    i = pl.program_id(1)
    r = BLK

    @pl.when(i == 0)
    def _():
        if n_past:
            ki_scr[0:n_past, :] = pki_ref[...]
            k_scr[0:n_past, :] = pk_ref[...]
            v_scr[0:n_past, :] = pv_ref[...]
        n_new = k_new_ref.shape[0]
        ki_scr[n_past:n_past + n_new, :] = ki_new_ref[:, SM_KI:SM_KI + IDX_DIM]
        k_scr[n_past:n_past + n_new, :] = k_new_ref[...]
        v_scr[n_past:n_past + n_new, :] = v_new_ref[...]
        n_tail = nkp_total * KB - n_past - n_new
        if n_tail:
            for scr in (ki_scr, k_scr, v_scr):
                scr[n_past + n_new:, :] = jnp.zeros((n_tail, scr.shape[1]), BF16)

    qb = qblk0 + i
    q0 = qb * BLK
    k_end = CHUNK * ((q0 + BLK - 1 + chunk_off) // CHUNK + 1) - chunk_off
    nkp = jnp.minimum(nkp_total, (k_end + KB - 1) // KB)
    n_keys = nkp_total * KB

    nt = (((1,), (1,)), ((), ()))
    wit = sm_ref[...].T
    qpos = q0 + lax.broadcasted_iota(jnp.int32, (KB, r), 1)
    qchunk = (qpos + chunk_off) // CHUNK
    krow = lax.broadcasted_iota(jnp.int32, (KB, r), 0)

    fold_rows = 8 * SUBLANES

    def fold(x, op):
        return op(x.reshape(KB // fold_rows, fold_rows, r), axis=0)

    for h in range(IDX_HEADS):
        qis_scr[h * r:(h + 1) * r, :] = qi_ref[:, h * IDX_DIM:(h + 1) * IDX_DIM]

    def score_body(jp, carry):
        mn, mx = carry
        kij = ki_scr[pl.ds(pl.multiple_of(jp * KB, KB), KB), :]
        acc = jnp.zeros((KB, r), F32)
        for hp in range(IDX_HEADS // 2):
            sh = lax.dot_general(kij, qis_scr[2 * hp * r:2 * (hp + 1) * r, :], nt, preferred_element_type=F32)
            for e in range(2):
                h = 2 * hp + e
                acc = acc + wit[SM_WI + h:SM_WI + h + 1, :] * jnp.maximum(sh[:, e * r:(e + 1) * r], 0.0)
        kpos = jp * KB + krow
        adm = ((kpos + chunk_off) // CHUNK <= qchunk) & (kpos < l_valid)
        s = jnp.where(adm, acc * (IDX_DIM ** -0.5), -jnp.inf)
        st_scr[jp] = s
        mn = jnp.minimum(mn, fold(jnp.where(adm, s, jnp.inf), jnp.min))
        mx = jnp.maximum(mx, fold(s, jnp.max))
        return mn, mx

    init = (jnp.full((fold_rows, r), jnp.inf, F32), jnp.full((fold_rows, r), -jnp.inf, F32))
    mn, mx = lax.fori_loop(0, nkp, score_body, init)
    lo0 = jnp.min(mn, axis=0, keepdims=True)
    hi0 = jnp.max(mx, axis=0, keepdims=True)
    n_adm = jnp.minimum(CHUNK * (qchunk[0:1, :] + 1) - chunk_off, l_valid)
    kk = jnp.minimum(n_adm, n_sel).astype(F32)

    def count(pred):
        def body(jp, acc):
            return acc + fold(jnp.where(pred(st_scr[jp], jp), 1.0, 0.0), jnp.sum)
        acc = lax.fori_loop(0, nkp, body, jnp.zeros((fold_rows, r), F32))
        return jnp.sum(acc, axis=0, keepdims=True)

    def bisect_body(_, carry):
        lo, hi, ub, _ = carry
        mid = 0.5 * lo + 0.5 * hi
        cnt = count(lambda s, jp: s >= mid)
        ok = cnt >= kk
        hit = cnt == kk
        lo = jnp.where(ok, mid, lo)
        hi = jnp.where(ok & ~hit, hi, mid)
        ub = jnp.where(ok, ub, mid)
        return lo, hi, ub, jnp.where(hit, 1.0, 0.0)

    all_adm = n_adm.astype(F32) == kk
    hi_init = jnp.where(all_adm, lo0, hi0)
    lo_f, hi_f, ub, hit_f = lax.fori_loop(
        0, BISECT_STEPS, bisect_body, (lo0, hi_init, jnp.full((1, r), jnp.inf, F32), jnp.zeros((1, r), F32)))
    mid_f = 0.5 * lo_f + 0.5 * hi_f
    pending = jnp.sum(jnp.where(hit_f > 0.0, 0, 1))

    def exact_path():
        hit = hit_f > 0.0

        def next_below(ub):
            def body(jp, acc):
                s = st_scr[jp]
                return jnp.maximum(acc, fold(jnp.where(s < ub, s, -jnp.inf), jnp.max))
            acc = lax.fori_loop(0, nkp, body, jnp.full((fold_rows, r), -jnp.inf, F32))
            return jnp.max(acc, axis=0, keepdims=True)

        def descend_cond(carry):
            _, _, todo, it = carry
            return (todo > 0) & (it < n_keys)

        def descend_body(carry):
            ub, _, _, it = carry
            t = next_below(ub)
            done = hit | (count(lambda s, jp: s >= t) >= kk)
            return jnp.where(done, ub, t), t, jnp.sum(jnp.where(done, 0, 1)), it + 1

        _, t, _, _ = lax.while_loop(descend_cond, descend_body, (ub, lo0, jnp.int32(1), jnp.int32(0)))
        thr = jnp.where(hit, mid_f, t)
        n_gt = count(lambda s, jp: s > thr)
        n_eq = count(lambda s, jp: s == thr)
        need = kk - n_gt

        def tie_search(_, carry):
            lo, hi = carry
            mid = (lo + hi) // 2
            ok = count(lambda s, jp: (s == thr) & (jp * KB + krow < mid)) >= need
            return jnp.where(ok, lo, mid), jnp.where(ok, mid, hi)

        tie_steps = max(1, math.ceil(math.log2(n_keys + 1)))
        key_limit = lax.cond(
            jnp.sum(jnp.where(n_eq > need, 1, 0)) > 0,
            lambda: lax.fori_loop(0, tie_steps, tie_search,
                                  (jnp.zeros((1, r), jnp.int32), jnp.full((1, r), n_keys, jnp.int32)))[1],
            lambda: jnp.full((1, r), n_keys, jnp.int32))
        return thr, key_limit

    thr, key_limit = lax.cond(pending > 0, exact_path, lambda: (mid_f, jnp.full((1, r), n_keys, jnp.int32)))

    def mask_body(jp, carry):
        s = st_scr[jp]
        sel = (s > thr) | ((s == thr) & (jp * KB + krow < key_limit))
        m_scr[jp] = jnp.where(sel, 0.0, NEG_BIG).T
        return carry

    lax.fori_loop(0, nkp, mask_body, 0)

    scale = HEAD_DIM ** -0.5
    rep = N_HEADS // N_KV_HEADS
    for h in range(N_HEADS):
        qs_scr[h // rep, (h % rep) * r:(h % rep + 1) * r, :] = qn_ref[:, h * HEAD_DIM:(h + 1) * HEAD_DIM]
    mrun_scr[...] = jnp.full(mrun_scr.shape, -jnp.inf, F32)
    lrun_scr[...] = jnp.zeros(lrun_scr.shape, F32)
    acc_scr[...] = jnp.zeros(acc_scr.shape, F32)

    def logit_body(jp, carry):
        u0 = jnp.clip(2 * jp - qb + 2, 0, N_BIAS_TILES - 1)
        u1 = jnp.clip(2 * jp + 1 - qb + 2, 0, N_BIAS_TILES - 1)
        madd = m_scr[jp]
        for g in range(N_KV_HEADS):
            lt = lax.dot_general(qs_scr[g], k_scr[pl.ds(pl.multiple_of(jp * KB, KB), KB),
                                                  g * HEAD_DIM:(g + 1) * HEAD_DIM], nt,
                                 preferred_element_type=F32)
            for e in range(rep):
                h = g * rep + e
                bias = jnp.concatenate([bt_ref[h * N_BIAS_TILES + u0], bt_ref[h * N_BIAS_TILES + u1]], axis=1)
                lg = lt[e * r:(e + 1) * r, :] * scale + bias + madd
                lg_scr[h, jp] = lg
                mrun_scr[h] = jnp.maximum(mrun_scr[h], jnp.maximum(lg[:, :BLK], lg[:, BLK:]))
        return carry

    lax.fori_loop(0, nkp, logit_body, 0)
    for h in range(N_HEADS):
        mrun_scr[h] = jnp.broadcast_to(jnp.max(mrun_scr[h], axis=1, keepdims=True), (r, BLK))

    def pv_body(jp, carry):
        for g in range(N_KV_HEADS):
            es = []
            for e in range(rep):
                h = g * rep + e
                mrow = mrun_scr[h]
                ex = jnp.exp(lg_scr[h, jp] - jnp.concatenate([mrow, mrow], axis=1))
                lrun_scr[h] = lrun_scr[h] + (ex[:, :BLK] + ex[:, BLK:])
                es.append(ex.astype(BF16))
            acc_scr[g] = acc_scr[g] + jnp.dot(jnp.concatenate(es, axis=0),
                                              v_scr[pl.ds(pl.multiple_of(jp * KB, KB), KB),
                                                    g * HEAD_DIM:(g + 1) * HEAD_DIM],
                                              preferred_element_type=F32)
        return carry

    lax.fori_loop(0, nkp, pv_body, 0)
    for h in range(N_HEADS):
        g, e = h // rep, h % rep
        den = jnp.sum(lrun_scr[h], axis=1, keepdims=True)
        o_ref[:, h * HEAD_DIM:(h + 1) * HEAD_DIM] = (acc_scr[g, e * r:(e + 1) * r, :] / den).astype(BF16)


def _attn(qn, qi, sm, ki_new, k_new, v_new, past, bias_tiles, *, qblk0, chunk_off, t_valid, n_sel):
    b, tq, _ = qn.shape
    n_past = past[1].shape[1] if past is not None else 0
    n_keys = -(-(n_past + tq) // KB) * KB
    nkp_total = n_keys // KB
    rep = N_HEADS // N_KV_HEADS
    seq = lambda width: pl.BlockSpec((None, BLK, width), lambda bi, i: (bi, i, 0))
    rows = lambda n, width: pl.BlockSpec((None, n, width), lambda bi, i: (bi, 0, 0))
    kern = functools.partial(_attn_kernel, nkp_total=nkp_total, n_past=n_past, qblk0=qblk0, chunk_off=chunk_off,
                             l_valid=n_past + t_valid, n_sel=n_sel)
    past_specs = [rows(n_past, IDX_DIM), rows(n_past, KV_DIM), rows(n_past, KV_DIM)] if n_past else []
    return pl.pallas_call(
        kern,
        grid=(b, tq // BLK),
        in_specs=past_specs + [seq(ATT_INNER), seq(IDX_HEADS * IDX_DIM), seq(LANES),
                               rows(tq, LANES), rows(tq, KV_DIM), rows(tq, KV_DIM),
                               _const_spec((N_HEADS * N_BIAS_TILES, BLK, BLK))],
        out_specs=seq(ATT_INNER),
        out_shape=jax.ShapeDtypeStruct((b, tq, ATT_INNER), BF16),
        scratch_shapes=[pltpu.VMEM((n_keys, IDX_DIM), BF16),
                        pltpu.VMEM((n_keys, KV_DIM), BF16),
                        pltpu.VMEM((n_keys, KV_DIM), BF16),
                        pltpu.VMEM((nkp_total, KB, BLK), F32),
                        pltpu.VMEM((nkp_total, BLK, KB), F32),
                        pltpu.VMEM((N_HEADS, nkp_total, BLK, KB), F32),
                        pltpu.VMEM((IDX_HEADS * BLK, IDX_DIM), BF16),
                        pltpu.VMEM((N_KV_HEADS, rep * BLK, HEAD_DIM), BF16),
                        pltpu.VMEM((N_HEADS, BLK, BLK), F32),
                        pltpu.VMEM((N_HEADS, BLK, BLK), F32),
                        pltpu.VMEM((N_KV_HEADS, rep * BLK, HEAD_DIM), F32)],
        compiler_params=_cparams(("parallel", "arbitrary")),
        name="attn",
    )(*(past or ()), qn, qi, sm, ki_new, k_new, v_new, bias_tiles)


def _out_ffn_kernel(x_ref, ys_ref, ya_ref, gs_ref, ga_ref, wbs_ref, wba_ref, wo_ref, n2_ref, wg_ref, wu_ref, wd_ref,
                    y_ref):
    dot = functools.partial(jnp.dot, preferred_element_type=F32)
    merged = (jax.nn.sigmoid(gs_ref[...]) * dot(ys_ref[...], wbs_ref[...])
              + jax.nn.sigmoid(ga_ref[...]) * dot(ya_ref[...], wba_ref[...]))
    h = x_ref[...] + dot(merged.astype(BF16), wo_ref[...])
    hn = _rms(h, n2_ref[...]).astype(BF16)
    act = (_silu(dot(hn, wg_ref[...])) * dot(hn, wu_ref[...])).astype(BF16)
    y_ref[...] = h + dot(act, wd_ref[...])


def _out_ffn(x2d, ys, ya, gs, ga, p):
    n = x2d.shape[0]
    tm = ROW_TILE
    d_ff = p["w_gate"].shape[1]
    row = lambda width: pl.BlockSpec((tm, width), lambda i: (i, 0))
    wspec = lambda shape: pl.BlockSpec(shape, lambda i: (0, 0), pipeline_mode=pl.Buffered(1))
    return pl.pallas_call(
        _out_ffn_kernel,
        grid=(n // tm,),
        in_specs=[row(D_MODEL), row(SSD_INNER), row(ATT_INNER), row(D_MODEL), row(D_MODEL),
                  wspec((SSD_INNER, D_MODEL)), wspec((ATT_INNER, D_MODEL)), wspec((D_MODEL, D_MODEL)),
                  _const_spec((1, D_MODEL)), wspec((D_MODEL, d_ff)), wspec((D_MODEL, d_ff)), wspec((d_ff, D_MODEL))],
        out_specs=row(D_MODEL),
        out_shape=jax.ShapeDtypeStruct((n, D_MODEL), F32),
        compiler_params=_cparams(("parallel",)),
        name="out_ffn",
    )(x2d, ys, ya, gs, ga, p["w_br_ssd"], p["w_br_att"], p["w_out"], p["norm2_w"], p["w_gate"], p["w_up"],
      p["w_down"])


def _layer(x_pad, t_valid, conv_prev, ssm_prev, past, p, bias_tiles, *, qblk0, chunk_off, n_sel):
    b, tp, _ = x_pad.shape
    x2d = x_pad.reshape(b * tp, D_MODEL)
    z, xbc, qn, k32, v32, kb, vb, qi, sm, smb, gs, ga = _in_proj(
        x2d, p["norm1_w"], p["w_in"], p["q_norm_w"], p["k_norm_w"], p["idx_k_norm_w"])
    seq = lambda a: a.reshape(b, tp, a.shape[-1])

    conv_prev8 = jnp.pad(conv_prev.astype(F32), ((0, 0), (HALO - (CONV_WIDTH - 1), 0), (0, 0)))
    gw = SSD_HEADS // SSD_GROUPS * SSD_HEAD_DIM
    y_ssd, ssm_new, conv_new8 = _ssd(seq(xbc), seq(z), seq(sm), conv_prev8,
                                     ssm_prev.astype(F32).reshape(b, SSD_GROUPS, gw, SSD_STATE), p, t_valid)

    if past is not None:
        pk, pv, pki = past
        n_past = pk.shape[1]
        past = (pki.astype(BF16), pk.reshape(b, n_past, KV_DIM).astype(BF16),
                pv.reshape(b, n_past, KV_DIM).astype(BF16))
    y_att = _attn(seq(qn), seq(qi), seq(sm), seq(smb), seq(kb), seq(vb), past, bias_tiles,
                  qblk0=qblk0, chunk_off=chunk_off, t_valid=t_valid, n_sel=n_sel)

    y = _out_ffn(x2d, y_ssd.reshape(b * tp, SSD_INNER), y_att.reshape(b * tp, ATT_INNER), gs, ga, p)

    k_new = seq(k32)[:, :t_valid].reshape(b, t_valid, N_KV_HEADS, HEAD_DIM)
    v_new = seq(v32)[:, :t_valid].reshape(b, t_valid, N_KV_HEADS, HEAD_DIM)
    ki_new = seq(sm)[:, :t_valid, SM_KI:SM_KI + IDX_DIM]
    ssm_new = ssm_new.reshape(b, SSD_HEADS, SSD_HEAD_DIM, SSD_STATE)
    conv_new = conv_new8[:, HALO - (CONV_WIDTH - 1):]
    return y.reshape(b, tp, D_MODEL), k_new, v_new, ki_new, ssm_new, conv_new


def _prepare_params(l, norm1_w, w_in, conv_w, conv_b, dt_bias, a_log, d_skip, ssd_norm_w, q_norm_w, k_norm_w,
                    idx_k_norm_w, w_br_ssd, w_br_att, w_out, norm2_w, w_gate, w_up, w_down):
    offs = [0]
    for w in IN_WIDTHS:
        offs.append(offs[-1] + w)
    seg = lambda i: w_in[l][:, offs[i]:offs[i + 1]]
    i_z, i_xbc, i_dt, i_q, i_k, i_v, i_qi, i_ki, i_wi, i_gs, i_ga = range(11)
    pad = jnp.zeros((D_MODEL, LANES - IDX_DIM - SSD_HEADS - IDX_HEADS), w_in.dtype)
    w_perm = jnp.concatenate([seg(i_z), seg(i_xbc), seg(i_q), seg(i_k), seg(i_v), seg(i_qi), seg(i_gs), seg(i_ga),
                              seg(i_ki), seg(i_dt), seg(i_wi), pad], axis=1).astype(BF16)

    def lanes_at(vec, start):
        return jnp.zeros((1, LANES), F32).at[0, start:start + vec.shape[0]].set(vec.astype(F32))

    dtb = lanes_at(dt_bias[l], SM_DT)
    alog = lanes_at(a_log[l], SM_DT)
    head_of_channel = jnp.arange(SSD_INNER) // SSD_HEAD_DIM
    expand = (jnp.arange(LANES)[:, None] == head_of_channel[None, :] + SM_DT).astype(BF16)
    row = lambda v: v.astype(F32).reshape(1, -1)
    return dict(
        norm1_w=row(norm1_w[l]), w_in=w_perm, conv_w=conv_w[l].astype(F32), conv_b=row(conv_b[l]),
        dtb=dtb, dtb_t=dtb.reshape(LANES, 1), alog=alog, alog_t=alog.reshape(LANES, 1),
        dskip_x=row(jnp.repeat(d_skip[l], SSD_HEAD_DIM)), ssd_norm_w=row(ssd_norm_w[l]), expand=expand,
        q_norm_w=row(q_norm_w[l]), k_norm_w=row(k_norm_w[l]),
        idx_k_norm_w=jnp.ones((1, LANES), F32).at[0, SM_KI:SM_KI + IDX_DIM].set(idx_k_norm_w[l].astype(F32)),
        w_br_ssd=w_br_ssd[l].astype(BF16), w_br_att=w_br_att[l].astype(BF16), w_out=w_out[l].astype(BF16),
        norm2_w=row(norm2_w[l]), w_gate=w_gate[l].astype(BF16), w_up=w_up[l].astype(BF16),
        w_down=w_down[l].astype(BF16))


def _pad_rows(x, tp):
    return jnp.pad(x, ((0, 0), (0, tp - x.shape[1]), (0, 0)))


def kernel(x_prompt, x_sample, cache_k, cache_v, cache_kidx, state_ssm, state_conv, meta_tokens, rel_bias, norm1_w,
           w_in, conv_w, conv_b, dt_bias, a_log, d_skip, ssd_norm_w, q_norm_w, k_norm_w, idx_k_norm_w, w_br_ssd,
           w_br_att, w_out, norm2_w, w_gate, w_up, w_down):
    bp, sp, _ = x_prompt.shape
    bs, ts, _ = x_sample.shape
    depth = w_in.shape[0]
    past = cache_k.shape[2]
    assert past % BLK == 0 and BLK % CHUNK == 0 and N_META <= CHUNK

    tq_p = N_META + sp
    tp_p = -(-tq_p // BLK) * BLK
    tp_s = -(-ts // BLK) * BLK
    n_sel_p = min(TOPK_MAX, sp // 4)
    n_sel_s = min(TOPK_MAX, (past + ts) // 4)

    hp = jnp.concatenate([jnp.broadcast_to(meta_tokens.astype(x_prompt.dtype)[None], (bp, N_META, D_MODEL)),
                          x_prompt], axis=1)
    hp = _pad_rows(hp, tp_p)
    hs = _pad_rows(x_sample, tp_s)
    conv0 = jnp.zeros((bp, CONV_WIDTH - 1, CONV_DIM), F32)
    ssm0 = jnp.zeros((bp, SSD_HEADS, SSD_HEAD_DIM, SSD_STATE), F32)
    bias_tiles = _bias_tiles(rel_bias.astype(F32))

    outs_p, outs_s = [], []
    for l in range(depth):
        p = _prepare_params(l, norm1_w, w_in, conv_w, conv_b, dt_bias, a_log, d_skip, ssd_norm_w, q_norm_w, k_norm_w,
                            idx_k_norm_w, w_br_ssd, w_br_att, w_out, norm2_w, w_gate, w_up, w_down)
        hp, *rest_p = _layer(hp, tq_p, conv0, ssm0, None, p, bias_tiles,
                             qblk0=0, chunk_off=CHUNK - N_META, n_sel=n_sel_p)
        hs, *rest_s = _layer(hs, ts, state_conv[l], state_ssm[l], (cache_k[l], cache_v[l], cache_kidx[l]), p,
                             bias_tiles, qblk0=past // BLK, chunk_off=0, n_sel=n_sel_s)
        outs_p.append(rest_p)
        outs_s.append(rest_s)

    stack = lambda outs, idx, dt: jnp.stack([o[idx] for o in outs], 0).astype(dt)
    y_prompt = hp[:, N_META:tq_p]
    y_sample = hs[:, :ts]
    return (y_prompt, y_sample,
            stack(outs_p, 0, x_prompt.dtype), stack(outs_p, 1, x_prompt.dtype), stack(outs_p, 2, x_prompt.dtype),
            stack(outs_p, 3, state_ssm.dtype), stack(outs_p, 4, x_prompt.dtype),
            stack(outs_s, 0, x_sample.dtype), stack(outs_s, 1, x_sample.dtype), stack(outs_s, 2, x_sample.dtype),
            stack(outs_s, 3, state_ssm.dtype), stack(outs_s, 4, x_sample.dtype))
```

```python
import functools
import math

import jax
import jax.numpy as jnp
from jax import lax
from jax.experimental import pallas as pl
from jax.experimental.pallas import tpu as pltpu

F32 = jnp.float32
BF16 = jnp.bfloat16

D_MODEL = 1024
CHUNK = 64
N_META = 16
SSD_HEADS = 16
SSD_HEAD_DIM = 64
SSD_INNER = SSD_HEADS * SSD_HEAD_DIM
SSD_GROUPS = 4
SSD_STATE = 128
CONV_WIDTH = 4
CONV_DIM = SSD_INNER + 2 * SSD_GROUPS * SSD_STATE
N_HEADS = 8
N_KV_HEADS = 2
HEAD_DIM = 128
ATT_INNER = N_HEADS * HEAD_DIM
KV_DIM = N_KV_HEADS * HEAD_DIM
IDX_HEADS = 8
IDX_DIM = 64
TOPK_MAX = 256
REL_BUCKETS = 32
REL_MAX_DIST = 128
IN_WIDTHS = (SSD_INNER, CONV_DIM, SSD_HEADS, ATT_INNER, KV_DIM, KV_DIM, IDX_HEADS * IDX_DIM, IDX_DIM, IDX_HEADS,
             D_MODEL, D_MODEL)
EPS = 1e-6

LANES = 128
SUBLANES = 8
VMEM_LIMIT_BYTES = 56 * 1024 * 1024

BLK = LANES
ROW_TILE = 256
HALO = SUBLANES

C_Z = 0
C_XBC = C_Z + SSD_INNER
C_Q = C_XBC + CONV_DIM
C_K = C_Q + ATT_INNER
C_V = C_K + KV_DIM
C_QI = C_V + KV_DIM
C_GS = C_QI + IDX_HEADS * IDX_DIM
C_GA = C_GS + D_MODEL
C_SM = C_GA + D_MODEL
IN_PAD = C_SM + LANES
SM_KI = 0
SM_DT = SM_KI + IDX_DIM
SM_WI = SM_DT + SSD_HEADS

BISECT_STEPS = 24
NEG_BIG = -1e30


def _cparams(sem):
    return pltpu.CompilerParams(dimension_semantics=sem, vmem_limit_bytes=VMEM_LIMIT_BYTES)


def _const_spec(shape):
    nd = len(shape)
    return pl.BlockSpec(shape, lambda *_: (0,) * nd)


def _rms(x, w):
    return x * lax.rsqrt(jnp.mean(x * x, axis=-1, keepdims=True) + EPS) * w


def _silu(x):
    return x * jax.nn.sigmoid(x)


def _in_proj_kernel(x_ref, n1_ref, w_ref, qn_ref, kn_ref, kin_ref,
                    z_ref, xbc_ref, q_ref, k_ref, v_ref, kb_ref, vb_ref, qi_ref, sm_ref, smb_ref, gs_ref, ga_ref):
    hn = _rms(x_ref[...], n1_ref[...]).astype(BF16)

    def mm(lo, hi):
        return jnp.dot(hn, w_ref[:, lo:hi], preferred_element_type=F32)

    z_ref[...] = mm(C_Z, C_XBC)
    xbc_ref[...] = mm(C_XBC, C_Q)
    q = mm(C_Q, C_K)
    for h in range(N_HEADS):
        sl = slice(h * HEAD_DIM, (h + 1) * HEAD_DIM)
        q_ref[:, sl] = _rms(q[:, sl], qn_ref[...]).astype(BF16)
    k = mm(C_K, C_V)
    for h in range(N_KV_HEADS):
        sl = slice(h * HEAD_DIM, (h + 1) * HEAD_DIM)
        kh = _rms(k[:, sl], kn_ref[...])
        k_ref[:, sl] = kh
        kb_ref[:, sl] = kh.astype(BF16)
    v = mm(C_V, C_QI)
    v_ref[...] = v
    vb_ref[...] = v.astype(BF16)
    qi_ref[...] = mm(C_QI, C_GS).astype(BF16)
    gs_ref[...] = mm(C_GS, C_GA)
    ga_ref[...] = mm(C_GA, C_SM)
    sm = mm(C_SM, IN_PAD)
    lane = lax.broadcasted_iota(jnp.int32, sm.shape, 1)
    is_ki = lane < SM_KI + IDX_DIM
    ms = jnp.sum(jnp.where(is_ki, sm * sm, 0.0), axis=-1, keepdims=True) * (1.0 / IDX_DIM)
    ki = sm * lax.rsqrt(ms + EPS) * kin_ref[...]
    is_wi = (lane >= SM_WI) & (lane < SM_WI + IDX_HEADS)
    out = jnp.where(is_ki, ki, jnp.where(is_wi, sm * (IDX_HEADS ** -0.5), sm))
    sm_ref[...] = out
    smb_ref[...] = out.astype(BF16)


def _in_proj(x2d, n1, w_perm, qn, kn, kin_pad):
    n = x2d.shape[0]
    tm = ROW_TILE
    row = lambda width: pl.BlockSpec((tm, width), lambda i: (i, 0))
    outs = [
        (SSD_INNER, F32), (CONV_DIM, F32), (ATT_INNER, BF16), (KV_DIM, F32), (KV_DIM, F32), (KV_DIM, BF16),
        (KV_DIM, BF16), (IDX_HEADS * IDX_DIM, BF16), (LANES, F32), (LANES, BF16), (D_MODEL, F32), (D_MODEL, F32),
    ]
    return pl.pallas_call(
        _in_proj_kernel,
        grid=(n // tm,),
        in_specs=[row(D_MODEL), _const_spec((1, D_MODEL)),
                  pl.BlockSpec((D_MODEL, IN_PAD), lambda i: (0, 0), pipeline_mode=pl.Buffered(1)),
                  _const_spec((1, HEAD_DIM)), _const_spec((1, HEAD_DIM)), _const_spec((1, LANES))],
        out_specs=[row(w) for w, _ in outs],
        out_shape=[jax.ShapeDtypeStruct((n, w), dt) for w, dt in outs],
        compiler_params=_cparams(("parallel",)),
        name="in_proj",
    )(x2d, n1, w_perm, qn, kn, kin_pad)


def _softplus(x):
    return jnp.maximum(x, 0.0) + jnp.log1p(jnp.exp(-jnp.abs(x)))


def _split3(x):
    hi = x.astype(BF16)
    r1 = x - hi.astype(F32)
    mid = r1.astype(BF16)
    lo = (r1 - mid.astype(F32)).astype(BF16)
    return hi, mid, lo


def _ssd_kernel(xbc_ref, z_ref, sm_ref, cprev_ref, sprev_ref, cw_ref, cb_ref, dtb_ref, dtbt_ref, alog_ref, alogt_ref,
                dsk_ref, nw_ref, exp_ref,
                y_ref, snew_ref, cnew_ref,
                s_scr, xpad_scr, xc_scr, y_scr, *, n_chunks, t_valid):
    c = pl.program_id(1)
    q = BLK
    gw = SSD_HEADS // SSD_GROUPS * SSD_HEAD_DIM

    @pl.when(c == 0)
    def _():
        for g in range(SSD_GROUPS):
            s_scr[g] = sprev_ref[g].T
        xpad_scr[0:HALO, :] = cprev_ref[...]

    xpad_scr[HALO:HALO + q, :] = xbc_ref[...]

    slab = 512
    for cc in range(CONV_DIM // slab):
        sl = slice(cc * slab, (cc + 1) * slab)
        acc = xpad_scr[HALO - 3:HALO - 3 + q, sl] * cw_ref[0:1, sl]
        for i in range(1, CONV_WIDTH):
            acc = acc + xpad_scr[HALO - 3 + i:HALO - 3 + i + q, sl] * cw_ref[i:i + 1, sl]
        xc_scr[:, sl] = _silu(cb_ref[:, sl] + acc)

    n_last = t_valid - (n_chunks - 1) * q

    @pl.when(c == n_chunks - 1)
    def _():
        cnew_ref[...] = xpad_scr[n_last:n_last + HALO, :]

    xpad_scr[0:HALO, :] = xpad_scr[q:q + HALO, :]

    sm = sm_ref[...]
    smt = sm.T
    lane = lax.broadcasted_iota(jnp.int32, (q, LANES), 1)
    row = lax.broadcasted_iota(jnp.int32, (q, LANES), 0)
    is_dt = (lane >= SM_DT) & (lane < SM_DT + SSD_HEADS) & (row + c * q < t_valid)
    is_dt_t = (row >= SM_DT) & (row < SM_DT + SSD_HEADS) & (lane + c * q < t_valid)
    dt = jnp.where(is_dt, _softplus(sm + dtb_ref[...]), 0.0)
    dtt = jnp.where(is_dt_t, _softplus(smt + dtbt_ref[...]), 0.0)
    da = dt * (-jnp.exp(alog_ref[...]))
    dat = dtt * (-jnp.exp(alogt_ref[...]))
    ii = lax.broadcasted_iota(jnp.int32, (q, q), 0)
    jj = lax.broadcasted_iota(jnp.int32, (q, q), 1)
    causal = jj <= ii
    acum = sum(jnp.dot(causal.astype(BF16), p, preferred_element_type=F32) for p in _split3(da))
    acumt = sum(jnp.dot(p, (ii <= jj).astype(BF16), preferred_element_type=F32) for p in _split3(dat))
    a_last = acum[q - 1:q, :]
    expand = exp_ref[...]
    stacked = jnp.concatenate([jnp.exp(acum), jnp.exp(a_last - acum) * dt,
                               jnp.broadcast_to(jnp.exp(a_last), (SUBLANES, LANES))], axis=0)
    stacked_x = sum(jnp.dot(p, expand, preferred_element_type=F32) for p in _split3(stacked))
    ea_x = stacked_x[0:q]
    wdt_x = stacked_x[q:2 * q]
    dec_x = stacked_x[2 * q:2 * q + 1]

    for g in range(SSD_GROUPS):
        gsl = slice(g * gw, (g + 1) * gw)
        bsl = slice(SSD_INNER + g * SSD_STATE, SSD_INNER + (g + 1) * SSD_STATE)
        csl = slice(SSD_INNER + SSD_GROUPS * SSD_STATE + g * SSD_STATE,
                    SSD_INNER + SSD_GROUPS * SSD_STATE + (g + 1) * SSD_STATE)
        bmf = xc_scr[:, bsl]
        bm = bmf.astype(BF16)
        cm = xc_scr[:, csl].astype(BF16)
        xg = xc_scr[:, gsl]
        xgb = xg.astype(BF16)
        cbm = lax.dot_general(cm, bm, (((1,), (1,)), ((), ())), preferred_element_type=F32)
        xw = (xg * wdt_x[:, gsl]).astype(BF16)
        st = jnp.dot(bmf.T.astype(BF16), xw, preferred_element_type=F32)
        s_in = s_scr[g]
        y_off = jnp.dot(cm, s_in.astype(BF16), preferred_element_type=F32) * ea_x[:, gsl]
        s_scr[g] = s_in * dec_x[:, gsl] + st
        for r in range(SSD_HEADS // SSD_GROUPS):
            h = g * (SSD_HEADS // SSD_GROUPS) + r
            seg = acum[:, SM_DT + h:SM_DT + h + 1] - acumt[SM_DT + h:SM_DT + h + 1, :]
            lmat = jnp.exp(jnp.where(causal, seg, -jnp.inf))
            wmat = (cbm * lmat * dtt[SM_DT + h:SM_DT + h + 1, :]).astype(BF16)
            rsl = slice(r * SSD_HEAD_DIM, (r + 1) * SSD_HEAD_DIM)
            hsl = slice(h * SSD_HEAD_DIM, (h + 1) * SSD_HEAD_DIM)
            y_diag = jnp.dot(wmat, xgb[:, rsl], preferred_element_type=F32)
            y_scr[:, hsl] = y_diag + y_off[:, rsl] + xg[:, rsl] * dsk_ref[:, hsl]

    @pl.when(c == n_chunks - 1)
    def _():
        for g in range(SSD_GROUPS):
            snew_ref[g] = s_scr[g].T

    y = y_scr[...] * _silu(z_ref[...])
    y_ref[...] = _rms(y, nw_ref[...]).astype(BF16)


def _ssd(xbc, z, sm, conv_prev8, ssm_prev, p, t_valid):
    b, tp, _ = xbc.shape
    nc = tp // BLK
    gw = SSD_HEADS // SSD_GROUPS * SSD_HEAD_DIM
    seq = lambda width: pl.BlockSpec((None, BLK, width), lambda i, c: (i, c, 0))
    kern = functools.partial(_ssd_kernel, n_chunks=nc, t_valid=t_valid)
    return pl.pallas_call(
        kern,
        grid=(b, nc),
        in_specs=[seq(CONV_DIM), seq(SSD_INNER), seq(LANES),
                  pl.BlockSpec((None, HALO, CONV_DIM), lambda i, c: (i, 0, 0)),
                  pl.BlockSpec((None, SSD_GROUPS, gw, SSD_STATE), lambda i, c: (i, 0, 0, 0)),
                  _const_spec((CONV_WIDTH, CONV_DIM)), _const_spec((1, CONV_DIM)),
                  _const_spec((1, LANES)), _const_spec((LANES, 1)), _const_spec((1, LANES)), _const_spec((LANES, 1)),
                  _const_spec((1, SSD_INNER)), _const_spec((1, SSD_INNER)), _const_spec((LANES, SSD_INNER))],
        out_specs=[seq(SSD_INNER),
                   pl.BlockSpec((None, SSD_GROUPS, gw, SSD_STATE), lambda i, c: (i, 0, 0, 0)),
                   pl.BlockSpec((None, HALO, CONV_DIM), lambda i, c: (i, 0, 0))],
        out_shape=[jax.ShapeDtypeStruct((b, tp, SSD_INNER), BF16),
                   jax.ShapeDtypeStruct((b, SSD_GROUPS, gw, SSD_STATE), F32),
                   jax.ShapeDtypeStruct((b, HALO, CONV_DIM), F32)],
        scratch_shapes=[pltpu.VMEM((SSD_GROUPS, SSD_STATE, gw), F32),
                        pltpu.VMEM((BLK + 2 * HALO, CONV_DIM), F32),
                        pltpu.VMEM((BLK, CONV_DIM), F32),
                        pltpu.VMEM((BLK, SSD_INNER), F32)],
        compiler_params=_cparams(("parallel", "arbitrary")),
        name="ssd",
    )(xbc, z, sm, conv_prev8, ssm_prev, p["conv_w"], p["conv_b"], p["dtb"], p["dtb_t"], p["alog"], p["alog_t"],
      p["dskip_x"], p["ssd_norm_w"], p["expand"])


N_BIAS_TILES = 5


def _log_bucket_starts():
    nb = REL_BUCKETS // 2
    max_exact = nb // 2
    s = nb - max_exact
    starts = []
    for m in range(1, s):
        n = max_exact
        while n ** s * max_exact ** m < max_exact ** s * REL_MAX_DIST ** m:
            n += 1
        starts.append(n)
    return starts


def _bias_kernel(rb_ref, bt_ref):
    nb = REL_BUCKETS // 2
    max_exact = nb // 2
    qq = lax.broadcasted_iota(jnp.int32, (BLK, BLK), 0)
    kk = lax.broadcasted_iota(jnp.int32, (BLK, BLK), 1)
    for u in range(N_BIAS_TILES):
        rel = (u - 2) * BLK + kk - qq
        n = jnp.abs(rel)
        large = max_exact + sum(jnp.where(n >= start, 1, 0) for start in _log_bucket_starts())
        bucket = jnp.where(rel > 0, nb, 0) + jnp.where(n < max_exact, n, large)
        for h in range(N_HEADS):
            acc = jnp.zeros((BLK, BLK), F32)
            for bkt in range(REL_BUCKETS):
                acc = jnp.where(bucket == bkt, rb_ref[bkt, h], acc)
            bt_ref[h * N_BIAS_TILES + u] = acc


def _bias_tiles(rel_bias):
    return pl.pallas_call(
        _bias_kernel,
        in_specs=[pl.BlockSpec(memory_space=pltpu.SMEM)],
        out_specs=pl.BlockSpec(memory_space=pltpu.VMEM),
        out_shape=jax.ShapeDtypeStruct((N_HEADS * N_BIAS_TILES, BLK, BLK), F32),
        name="bias_tiles",
    )(rel_bias)


KB = 2 * BLK


def _attn_kernel(*refs, nkp_total, n_past, qblk0, chunk_off, l_valid, n_sel):
    if n_past:
        pki_ref, pk_ref, pv_ref, *refs = refs
    (qn_ref, qi_ref, sm_ref, ki_new_ref, k_new_ref, v_new_ref, bt_ref, o_ref,
     ki_scr, k_scr, v_scr, st_scr, m_scr, lg_scr, qis_scr, qs_scr, mrun_scr, lrun_scr, acc_scr) = refs
    i = pl.program_id(1)
    r = BLK

    @pl.when(i == 0)
    def _():
        if n_past:
            def load_past(c, carry):
                rows = pl.ds(pl.multiple_of(c * KB, KB), KB)
                ki_scr[rows, :] = pki_ref[rows, :].astype(BF16)
                for g in range(N_KV_HEADS):
                    src = pl.ds(pl.multiple_of(c * KB * N_KV_HEADS, KB) + g, KB, stride=N_KV_HEADS)
                    k_scr[rows, g * HEAD_DIM:(g + 1) * HEAD_DIM] = pk_ref[src, :].astype(BF16)
                    v_scr[rows, g * HEAD_DIM:(g + 1) * HEAD_DIM] = pv_ref[src, :].astype(BF16)
                return carry

            lax.fori_loop(0, n_past // KB, load_past, 0)
        n_new = k_new_ref.shape[0]
        ki_scr[n_past:n_past + n_new, :] = ki_new_ref[:, SM_KI:SM_KI + IDX_DIM]
        k_scr[n_past:n_past + n_new, :] = k_new_ref[...]
        v_scr[n_past:n_past + n_new, :] = v_new_ref[...]
        n_tail = nkp_total * KB - n_past - n_new
        if n_tail:
            for scr in (ki_scr, k_scr, v_scr):
                scr[n_past + n_new:, :] = jnp.zeros((n_tail, scr.shape[1]), BF16)

    qb = qblk0 + i
    q0 = qb * BLK
    k_end = CHUNK * ((q0 + BLK - 1 + chunk_off) // CHUNK + 1) - chunk_off
    nkp = jnp.minimum(nkp_total, (k_end + KB - 1) // KB)
    n_keys = nkp_total * KB

    nt = (((1,), (1,)), ((), ()))
    wit = sm_ref[...].T
    qpos = q0 + lax.broadcasted_iota(jnp.int32, (1, r), 1)
    n_adm = jnp.minimum(CHUNK * ((qpos + chunk_off) // CHUNK + 1) - chunk_off, l_valid)
    krow = lax.broadcasted_iota(jnp.int32, (KB, r), 0)

    fold_rows = 8 * SUBLANES

    def fold(x, op):
        return op(x.reshape(KB // fold_rows, fold_rows, r), axis=0)

    def for_key_steps(body, init):
        carry = lax.fori_loop(0, nkp // 2, lambda t, c: body(2 * t + 1, body(2 * t, c)), init)
        return lax.cond(nkp % 2 == 1, lambda c: body(nkp - 1, c), lambda c: c, carry)

    for h in range(IDX_HEADS):
        qis_scr[h * r:(h + 1) * r, :] = qi_ref[:, h * IDX_DIM:(h + 1) * IDX_DIM]

    def score_body(jp, carry):
        mn, mx = carry
        kij = ki_scr[pl.ds(pl.multiple_of(jp * KB, KB), KB), :]
        acc = jnp.zeros((KB, r), F32)
        for hp in range(IDX_HEADS // 2):
            sh = lax.dot_general(kij, qis_scr[2 * hp * r:2 * (hp + 1) * r, :], nt, preferred_element_type=F32)
            for e in range(2):
                h = 2 * hp + e
                acc = acc + wit[SM_WI + h:SM_WI + h + 1, :] * jnp.maximum(sh[:, e * r:(e + 1) * r], 0.0)
        adm = krow < n_adm - jp * KB
        s = jnp.where(adm, acc * (IDX_DIM ** -0.5), -jnp.inf)
        st_scr[jp] = s
        mn = jnp.minimum(mn, fold(jnp.where(adm, s, jnp.inf), jnp.min))
        mx = jnp.maximum(mx, fold(s, jnp.max))
        return mn, mx

    init = (jnp.full((fold_rows, r), jnp.inf, F32), jnp.full((fold_rows, r), -jnp.inf, F32))
    mn, mx = for_key_steps(score_body, init)
    lo0 = jnp.min(mn, axis=0, keepdims=True)
    hi0 = jnp.max(mx, axis=0, keepdims=True)
    kk = jnp.minimum(n_adm, n_sel).astype(F32)

    def count(pred):
        def body(jp, acc):
            return acc + fold(jnp.where(pred(st_scr[jp], jp), 1.0, 0.0), jnp.sum)
        acc = lax.fori_loop(0, nkp, body, jnp.zeros((fold_rows, r), F32))
        return jnp.sum(acc, axis=0, keepdims=True)

    def bisect_body(_, carry):
        lo, hi, ub, _ = carry
        mid = 0.5 * lo + 0.5 * hi
        cnt = count(lambda s, jp: s >= mid)
        ok = cnt >= kk
        hit = cnt == kk
        lo = jnp.where(ok, mid, lo)
        hi = jnp.where(ok & ~hit, hi, mid)
        ub = jnp.where(ok, ub, mid)
        return lo, hi, ub, jnp.where(hit, 1.0, 0.0)

    all_adm = n_adm.astype(F32) == kk
    hi_init = jnp.where(all_adm, lo0, hi0)
    lo_f, hi_f, ub, hit_f = lax.fori_loop(
        0, BISECT_STEPS, bisect_body, (lo0, hi_init, jnp.full((1, r), jnp.inf, F32), jnp.zeros((1, r), F32)))
    mid_f = 0.5 * lo_f + 0.5 * hi_f
    pending = jnp.sum(jnp.where(hit_f > 0.0, 0, 1))

    def exact_path():
        hit = hit_f > 0.0

        def next_below(ub):
            def body(jp, acc):
                s = st_scr[jp]
                return jnp.maximum(acc, fold(jnp.where(s < ub, s, -jnp.inf), jnp.max))
            acc = lax.fori_loop(0, nkp, body, jnp.full((fold_rows, r), -jnp.inf, F32))
            return jnp.max(acc, axis=0, keepdims=True)

        def descend_cond(carry):
            _, _, todo, it = carry
            return (todo > 0) & (it < n_keys)

        def descend_body(carry):
            ub, _, _, it = carry
            t = next_below(ub)
            done = hit | (count(lambda s, jp: s >= t) >= kk)
            return jnp.where(done, ub, t), t, jnp.sum(jnp.where(done, 0, 1)), it + 1

        _, t, _, _ = lax.while_loop(descend_cond, descend_body, (ub, lo0, jnp.int32(1), jnp.int32(0)))
        thr = jnp.where(hit, mid_f, t)
        n_gt = count(lambda s, jp: s > thr)
        n_eq = count(lambda s, jp: s == thr)
        need = kk - n_gt

        def tie_search(_, carry):
            lo, hi = carry
            mid = (lo + hi) // 2
            ok = count(lambda s, jp: (s == thr) & (krow < mid - jp * KB)) >= need
            return jnp.where(ok, lo, mid), jnp.where(ok, mid, hi)

        tie_steps = max(1, math.ceil(math.log2(n_keys + 1)))
        key_limit = lax.cond(
            jnp.sum(jnp.where(n_eq > need, 1, 0)) > 0,
            lambda: lax.fori_loop(0, tie_steps, tie_search,
                                  (jnp.zeros((1, r), jnp.int32), jnp.full((1, r), n_keys, jnp.int32)))[1],
            lambda: jnp.full((1, r), n_keys, jnp.int32))
        return thr, key_limit

    thr, key_limit = lax.cond(pending > 0, exact_path, lambda: (mid_f, jnp.full((1, r), n_keys, jnp.int32)))

    def mask_body(jp, carry):
        s = st_scr[jp]
        sel = (s > thr) | ((s == thr) & (krow < key_limit - jp * KB))
        m_scr[jp] = jnp.where(sel, 0.0, NEG_BIG).T
        return carry

    lax.fori_loop(0, nkp, mask_body, 0)

    scale = HEAD_DIM ** -0.5
    rep = N_HEADS // N_KV_HEADS
    for h in range(N_HEADS):
        qs_scr[h // rep, (h % rep) * r:(h % rep + 1) * r, :] = qn_ref[:, h * HEAD_DIM:(h + 1) * HEAD_DIM]
    mrun_scr[...] = jnp.full(mrun_scr.shape, -jnp.inf, F32)
    lrun_scr[...] = jnp.zeros(lrun_scr.shape, F32)
    acc_scr[...] = jnp.zeros(acc_scr.shape, F32)

    def logit_body(jp, carry):
        u0 = jnp.clip(2 * jp - qb + 2, 0, N_BIAS_TILES - 1)
        u1 = jnp.clip(2 * jp + 1 - qb + 2, 0, N_BIAS_TILES - 1)
        madd = m_scr[jp]
        for g in range(N_KV_HEADS):
            lt = lax.dot_general(qs_scr[g], k_scr[pl.ds(pl.multiple_of(jp * KB, KB), KB),
                                                  g * HEAD_DIM:(g + 1) * HEAD_DIM], nt,
                                 preferred_element_type=F32)
            for e in range(rep):
                h = g * rep + e
                bias = jnp.concatenate([bt_ref[h * N_BIAS_TILES + u0], bt_ref[h * N_BIAS_TILES + u1]], axis=1)
                lg = lt[e * r:(e + 1) * r, :] * scale + bias + madd
                lg_scr[h, jp] = lg
                mrun_scr[h] = jnp.maximum(mrun_scr[h], jnp.maximum(lg[:, :BLK], lg[:, BLK:]))
        return carry

    for_key_steps(logit_body, 0)
    for h in range(N_HEADS):
        mrun_scr[h] = jnp.broadcast_to(jnp.max(mrun_scr[h], axis=1, keepdims=True), (r, BLK))

    def pv_body(jp, carry):
        for g in range(N_KV_HEADS):
            es = []
            for e in range(rep):
                h = g * rep + e
                mrow = mrun_scr[h]
                ex = jnp.exp(lg_scr[h, jp] - jnp.concatenate([mrow, mrow], axis=1))
                lrun_scr[h] = lrun_scr[h] + (ex[:, :BLK] + ex[:, BLK:])
                es.append(ex.astype(BF16))
            acc_scr[g] = acc_scr[g] + jnp.dot(jnp.concatenate(es, axis=0),
                                              v_scr[pl.ds(pl.multiple_of(jp * KB, KB), KB),
                                                    g * HEAD_DIM:(g + 1) * HEAD_DIM],
                                              preferred_element_type=F32)
        return carry

    for_key_steps(pv_body, 0)
    for h in range(N_HEADS):
        g, e = h // rep, h % rep
        den = jnp.sum(lrun_scr[h], axis=1, keepdims=True)
        o_ref[:, h * HEAD_DIM:(h + 1) * HEAD_DIM] = (acc_scr[g, e * r:(e + 1) * r, :] / den).astype(BF16)


def _attn(qn, qi, sm, ki_new, k_new, v_new, past, bias_tiles, *, qblk0, chunk_off, t_valid, n_sel):
    b, tq, _ = qn.shape
    n_past = past[0].shape[1] if past is not None else 0
    assert n_past % KB == 0
    n_keys = -(-(n_past + tq) // KB) * KB
    nkp_total = n_keys // KB
    rep = N_HEADS // N_KV_HEADS
    seq = lambda width: pl.BlockSpec((None, BLK, width), lambda bi, i: (bi, i, 0))
    rows = lambda n, width: pl.BlockSpec((None, n, width), lambda bi, i: (bi, 0, 0))
    kern = functools.partial(_attn_kernel, nkp_total=nkp_total, n_past=n_past, qblk0=qblk0, chunk_off=chunk_off,
                             l_valid=n_past + t_valid, n_sel=n_sel)
    once = lambda n, width: pl.BlockSpec((None, n, width), lambda bi, i: (bi, 0, 0), pipeline_mode=pl.Buffered(1))
    past_specs = [once(n_past, IDX_DIM), once(n_past * N_KV_HEADS, HEAD_DIM),
                  once(n_past * N_KV_HEADS, HEAD_DIM)] if n_past else []
    return pl.pallas_call(
        kern,
        grid=(b, tq // BLK),
        in_specs=past_specs + [seq(ATT_INNER), seq(IDX_HEADS * IDX_DIM), seq(LANES),
                               rows(tq, LANES), rows(tq, KV_DIM), rows(tq, KV_DIM),
                               _const_spec((N_HEADS * N_BIAS_TILES, BLK, BLK))],
        out_specs=seq(ATT_INNER),
        out_shape=jax.ShapeDtypeStruct((b, tq, ATT_INNER), BF16),
        scratch_shapes=[pltpu.VMEM((n_keys, IDX_DIM), BF16),
                        pltpu.VMEM((n_keys, KV_DIM), BF16),
                        pltpu.VMEM((n_keys, KV_DIM), BF16),
                        pltpu.VMEM((nkp_total, KB, BLK), F32),
                        pltpu.VMEM((nkp_total, BLK, KB), F32),
                        pltpu.VMEM((N_HEADS, nkp_total, BLK, KB), F32),
                        pltpu.VMEM((IDX_HEADS * BLK, IDX_DIM), BF16),
                        pltpu.VMEM((N_KV_HEADS, rep * BLK, HEAD_DIM), BF16),
                        pltpu.VMEM((N_HEADS, BLK, BLK), F32),
                        pltpu.VMEM((N_HEADS, BLK, BLK), F32),
                        pltpu.VMEM((N_KV_HEADS, rep * BLK, HEAD_DIM), F32)],
        compiler_params=_cparams(("parallel", "arbitrary")),
        name="attn",
    )(*(past or ()), qn, qi, sm, ki_new, k_new, v_new, bias_tiles)


def _out_ffn_kernel(x_ref, ys_ref, ya_ref, gs_ref, ga_ref, wbs_ref, wba_ref, wo_ref, n2_ref, wg_ref, wu_ref, wd_ref,
                    y_ref):
    dot = functools.partial(jnp.dot, preferred_element_type=F32)
    merged = (jax.nn.sigmoid(gs_ref[...]) * dot(ys_ref[...], wbs_ref[...])
              + jax.nn.sigmoid(ga_ref[...]) * dot(ya_ref[...], wba_ref[...]))
    h = x_ref[...] + dot(merged.astype(BF16), wo_ref[...])
    hn = _rms(h, n2_ref[...]).astype(BF16)
    act = (_silu(dot(hn, wg_ref[...])) * dot(hn, wu_ref[...])).astype(BF16)
    y_ref[...] = h + dot(act, wd_ref[...])


def _out_ffn(x2d, ys, ya, gs, ga, p):
    n = x2d.shape[0]
    tm = ROW_TILE
    d_ff = p["w_gate"].shape[1]
    row = lambda width: pl.BlockSpec((tm, width), lambda i: (i, 0))
    wspec = lambda shape: pl.BlockSpec(shape, lambda i: (0, 0), pipeline_mode=pl.Buffered(1))
    return pl.pallas_call(
        _out_ffn_kernel,
        grid=(n // tm,),
        in_specs=[row(D_MODEL), row(SSD_INNER), row(ATT_INNER), row(D_MODEL), row(D_MODEL),
                  wspec((SSD_INNER, D_MODEL)), wspec((ATT_INNER, D_MODEL)), wspec((D_MODEL, D_MODEL)),
                  _const_spec((1, D_MODEL)), wspec((D_MODEL, d_ff)), wspec((D_MODEL, d_ff)), wspec((d_ff, D_MODEL))],
        out_specs=row(D_MODEL),
        out_shape=jax.ShapeDtypeStruct((n, D_MODEL), F32),
        compiler_params=_cparams(("parallel",)),
        name="out_ffn",
    )(x2d, ys, ya, gs, ga, p["w_br_ssd"], p["w_br_att"], p["w_out"], p["norm2_w"], p["w_gate"], p["w_up"],
      p["w_down"])


def _layer(x_pad, t_valid, conv_prev, ssm_prev, past, p, bias_tiles, *, qblk0, chunk_off, n_sel):
    b, tp, _ = x_pad.shape
    x2d = x_pad.reshape(b * tp, D_MODEL)
    z, xbc, qn, k32, v32, kb, vb, qi, sm, smb, gs, ga = _in_proj(
        x2d, p["norm1_w"], p["w_in"], p["q_norm_w"], p["k_norm_w"], p["idx_k_norm_w"])
    seq = lambda a: a.reshape(b, tp, a.shape[-1])

    conv_prev8 = jnp.pad(conv_prev.astype(F32), ((0, 0), (HALO - (CONV_WIDTH - 1), 0), (0, 0)))
    gw = SSD_HEADS // SSD_GROUPS * SSD_HEAD_DIM
    y_ssd, ssm_new, conv_new8 = _ssd(seq(xbc), seq(z), seq(sm), conv_prev8,
                                     ssm_prev.astype(F32).reshape(b, SSD_GROUPS, gw, SSD_STATE), p, t_valid)

    if past is not None:
        pk, pv, pki = past
        n_past = pk.shape[1]
        past = (pki.astype(F32), pk.astype(F32).reshape(b, n_past * N_KV_HEADS, HEAD_DIM),
                pv.astype(F32).reshape(b, n_past * N_KV_HEADS, HEAD_DIM))
    y_att = _attn(seq(qn), seq(qi), seq(sm), seq(smb), seq(kb), seq(vb), past, bias_tiles,
                  qblk0=qblk0, chunk_off=chunk_off, t_valid=t_valid, n_sel=n_sel)

    y = _out_ffn(x2d, y_ssd.reshape(b * tp, SSD_INNER), y_att.reshape(b * tp, ATT_INNER), gs, ga, p)

    k_new = seq(k32)[:, :t_valid].reshape(b, t_valid, N_KV_HEADS, HEAD_DIM)
    v_new = seq(v32)[:, :t_valid].reshape(b, t_valid, N_KV_HEADS, HEAD_DIM)
    ki_new = seq(sm)[:, :t_valid, SM_KI:SM_KI + IDX_DIM]
    ssm_new = ssm_new.reshape(b, SSD_HEADS, SSD_HEAD_DIM, SSD_STATE)
    conv_new = conv_new8[:, HALO - (CONV_WIDTH - 1):]
    return y.reshape(b, tp, D_MODEL), k_new, v_new, ki_new, ssm_new, conv_new


def _prepare_params(l, norm1_w, w_in, conv_w, conv_b, dt_bias, a_log, d_skip, ssd_norm_w, q_norm_w, k_norm_w,
                    idx_k_norm_w, w_br_ssd, w_br_att, w_out, norm2_w, w_gate, w_up, w_down):
    offs = [0]
    for w in IN_WIDTHS:
        offs.append(offs[-1] + w)
    seg = lambda i: w_in[l][:, offs[i]:offs[i + 1]]
    i_z, i_xbc, i_dt, i_q, i_k, i_v, i_qi, i_ki, i_wi, i_gs, i_ga = range(11)
    pad = jnp.zeros((D_MODEL, LANES - IDX_DIM - SSD_HEADS - IDX_HEADS), w_in.dtype)
    w_perm = jnp.concatenate([seg(i_z), seg(i_xbc), seg(i_q), seg(i_k), seg(i_v), seg(i_qi), seg(i_gs), seg(i_ga),
                              seg(i_ki), seg(i_dt), seg(i_wi), pad], axis=1).astype(BF16)

    def lanes_at(vec, start):
        return jnp.zeros((1, LANES), F32).at[0, start:start + vec.shape[0]].set(vec.astype(F32))

    dtb = lanes_at(dt_bias[l], SM_DT)
    alog = lanes_at(a_log[l], SM_DT)
    head_of_channel = jnp.arange(SSD_INNER) // SSD_HEAD_DIM
    expand = (jnp.arange(LANES)[:, None] == head_of_channel[None, :] + SM_DT).astype(BF16)
    row = lambda v: v.astype(F32).reshape(1, -1)
    return dict(
        norm1_w=row(norm1_w[l]), w_in=w_perm, conv_w=conv_w[l].astype(F32), conv_b=row(conv_b[l]),
        dtb=dtb, dtb_t=dtb.reshape(LANES, 1), alog=alog, alog_t=alog.reshape(LANES, 1),
        dskip_x=row(jnp.repeat(d_skip[l], SSD_HEAD_DIM)), ssd_norm_w=row(ssd_norm_w[l]), expand=expand,
        q_norm_w=row(q_norm_w[l]), k_norm_w=row(k_norm_w[l]),
        idx_k_norm_w=jnp.ones((1, LANES), F32).at[0, SM_KI:SM_KI + IDX_DIM].set(idx_k_norm_w[l].astype(F32)),
        w_br_ssd=w_br_ssd[l].astype(BF16), w_br_att=w_br_att[l].astype(BF16), w_out=w_out[l].astype(BF16),
        norm2_w=row(norm2_w[l]), w_gate=w_gate[l].astype(BF16), w_up=w_up[l].astype(BF16),
        w_down=w_down[l].astype(BF16))


def _pad_rows(x, tp):
    return jnp.pad(x, ((0, 0), (0, tp - x.shape[1]), (0, 0)))


def kernel(x_prompt, x_sample, cache_k, cache_v, cache_kidx, state_ssm, state_conv, meta_tokens, rel_bias, norm1_w,
           w_in, conv_w, conv_b, dt_bias, a_log, d_skip, ssd_norm_w, q_norm_w, k_norm_w, idx_k_norm_w, w_br_ssd,
           w_br_att, w_out, norm2_w, w_gate, w_up, w_down):
    bp, sp, _ = x_prompt.shape
    bs, ts, _ = x_sample.shape
    depth = w_in.shape[0]
    past = cache_k.shape[2]
    assert past % BLK == 0 and BLK % CHUNK == 0 and N_META <= CHUNK

    tq_p = N_META + sp
    tp_p = -(-tq_p // BLK) * BLK
    tp_s = -(-ts // BLK) * BLK
    n_sel_p = min(TOPK_MAX, sp // 4)
    n_sel_s = min(TOPK_MAX, (past + ts) // 4)

    hp = jnp.concatenate([jnp.broadcast_to(meta_tokens.astype(x_prompt.dtype)[None], (bp, N_META, D_MODEL)),
                          x_prompt, jnp.zeros((bp, tp_p - tq_p, D_MODEL), x_prompt.dtype)], axis=1)
    hs = _pad_rows(x_sample, tp_s)
    conv0 = jnp.zeros((bp, CONV_WIDTH - 1, CONV_DIM), F32)
    ssm0 = jnp.zeros((bp, SSD_HEADS, SSD_HEAD_DIM, SSD_STATE), F32)
    bias_tiles = _bias_tiles(rel_bias.astype(F32))

    outs_p, outs_s = [], []
    for l in range(depth):
        p = _prepare_params(l, norm1_w, w_in, conv_w, conv_b, dt_bias, a_log, d_skip, ssd_norm_w, q_norm_w, k_norm_w,
                            idx_k_norm_w, w_br_ssd, w_br_att, w_out, norm2_w, w_gate, w_up, w_down)
        hp, *rest_p = _layer(hp, tq_p, conv0, ssm0, None, p, bias_tiles,
                             qblk0=0, chunk_off=CHUNK - N_META, n_sel=n_sel_p)
        hs, *rest_s = _layer(hs, ts, state_conv[l], state_ssm[l], (cache_k[l], cache_v[l], cache_kidx[l]), p,
                             bias_tiles, qblk0=past // BLK, chunk_off=0, n_sel=n_sel_s)
        outs_p.append(rest_p)
        outs_s.append(rest_s)

    stack = lambda outs, idx, dt: jnp.stack([o[idx] for o in outs], 0).astype(dt)
    y_prompt = hp[:, N_META:tq_p]
    y_sample = hs[:, :ts]
    return (y_prompt, y_sample,
            stack(outs_p, 0, x_prompt.dtype), stack(outs_p, 1, x_prompt.dtype), stack(outs_p, 2, x_prompt.dtype),
            stack(outs_p, 3, state_ssm.dtype), stack(outs_p, 4, x_prompt.dtype),
            stack(outs_s, 0, x_sample.dtype), stack(outs_s, 1, x_sample.dtype), stack(outs_s, 2, x_sample.dtype),
            stack(outs_s, 3, state_ssm.dtype), stack(outs_s, 4, x_sample.dtype))
```

```python
import functools

import jax
import jax.numpy as jnp
from jax import lax
from jax.experimental import pallas as pl
from jax.experimental.pallas import tpu as pltpu

F32 = jnp.float32
BF16 = jnp.bfloat16

D_MODEL = 1024
CHUNK = 64
N_META = 16
SSD_HEADS = 16
SSD_HEAD_DIM = 64
SSD_INNER = SSD_HEADS * SSD_HEAD_DIM
SSD_GROUPS = 4
SSD_STATE = 128
CONV_WIDTH = 4
CONV_DIM = SSD_INNER + 2 * SSD_GROUPS * SSD_STATE
N_HEADS = 8
N_KV_HEADS = 2
HEAD_DIM = 128
ATT_INNER = N_HEADS * HEAD_DIM
KV_DIM = N_KV_HEADS * HEAD_DIM
IDX_HEADS = 8
IDX_DIM = 64
TOPK_MAX = 256
REL_BUCKETS = 32
REL_MAX_DIST = 128
IN_WIDTHS = (SSD_INNER, CONV_DIM, SSD_HEADS, ATT_INNER, KV_DIM, KV_DIM, IDX_HEADS * IDX_DIM, IDX_DIM, IDX_HEADS,
             D_MODEL, D_MODEL)
EPS = 1e-6

LANES = 128
SUBLANES = 8
VMEM_LIMIT_BYTES = 56 * 1024 * 1024

BLK = LANES
ROW_TILE = 256
HALO = SUBLANES

C_Z = 0
C_XBC = C_Z + SSD_INNER
C_Q = C_XBC + CONV_DIM
C_K = C_Q + ATT_INNER
C_V = C_K + KV_DIM
C_QI = C_V + KV_DIM
C_GS = C_QI + IDX_HEADS * IDX_DIM
C_GA = C_GS + D_MODEL
C_SM = C_GA + D_MODEL
IN_PAD = C_SM + LANES
SM_KI = 0
SM_DT = SM_KI + IDX_DIM
SM_WI = SM_DT + SSD_HEADS

SEARCH_STEPS = 16
SEARCH_CLAMP = 1.0 / 16
NEG_BIG = -1e30


def _cparams(sem):
    return pltpu.CompilerParams(dimension_semantics=sem, vmem_limit_bytes=VMEM_LIMIT_BYTES)


def _const_spec(shape):
    nd = len(shape)
    return pl.BlockSpec(shape, lambda *_: (0,) * nd)


def _rms(x, w):
    return x * lax.rsqrt(jnp.mean(x * x, axis=-1, keepdims=True) + EPS) * w


def _silu(x):
    return x * jax.nn.sigmoid(x)


def _in_proj_kernel(x_ref, n1_ref, w_ref, qn_ref, kn_ref, kin_ref,
                    z_ref, xbc_ref, q_ref, k_ref, v_ref, kb_ref, vb_ref, qi_ref, sm_ref, smb_ref, gs_ref, ga_ref):
    hn = _rms(x_ref[...], n1_ref[...]).astype(BF16)
    tm = x_ref.shape[0]

    def mm(lo, hi):
        return jnp.dot(hn, w_ref[:, lo:hi], preferred_element_type=F32)

    z_ref[...] = mm(C_Z, C_XBC)
    xbc_ref[...] = mm(C_XBC, C_Q)
    q = mm(C_Q, C_K)
    for h in range(N_HEADS):
        sl = slice(h * HEAD_DIM, (h + 1) * HEAD_DIM)
        q_ref[:, sl] = _rms(q[:, sl], qn_ref[...]).astype(BF16)
    k = mm(C_K, C_V)
    for h in range(N_KV_HEADS):
        sl = slice(h * HEAD_DIM, (h + 1) * HEAD_DIM)
        kh = _rms(k[:, sl], kn_ref[...])
        k_ref[pl.ds(h, tm, stride=N_KV_HEADS), :] = kh
        kb_ref[:, sl] = kh.astype(BF16)
    v = mm(C_V, C_QI)
    for h in range(N_KV_HEADS):
        v_ref[pl.ds(h, tm, stride=N_KV_HEADS), :] = v[:, h * HEAD_DIM:(h + 1) * HEAD_DIM]
    vb_ref[...] = v.astype(BF16)
    qi_ref[...] = mm(C_QI, C_GS).astype(BF16)
    gs_ref[...] = mm(C_GS, C_GA)
    ga_ref[...] = mm(C_GA, C_SM)
    sm = mm(C_SM, IN_PAD)
    lane = lax.broadcasted_iota(jnp.int32, sm.shape, 1)
    is_ki = lane < SM_KI + IDX_DIM
    ms = jnp.sum(jnp.where(is_ki, sm * sm, 0.0), axis=-1, keepdims=True) * (1.0 / IDX_DIM)
    ki = sm * lax.rsqrt(ms + EPS) * kin_ref[...]
    is_wi = (lane >= SM_WI) & (lane < SM_WI + IDX_HEADS)
    out = jnp.where(is_ki, ki, jnp.where(is_wi, sm * (IDX_HEADS ** -0.5), sm))
    sm_ref[...] = out
    smb_ref[...] = out.astype(BF16)


def _in_proj(x2d, n1, w_perm, qn, kn, kin_pad):
    n = x2d.shape[0]
    tm = ROW_TILE
    row = lambda width, per_token=1: pl.BlockSpec((tm * per_token, width), lambda i: (i, 0))
    outs = [
        (SSD_INNER, F32, 1), (CONV_DIM, F32, 1), (ATT_INNER, BF16, 1), (HEAD_DIM, F32, N_KV_HEADS),
        (HEAD_DIM, F32, N_KV_HEADS), (KV_DIM, BF16, 1), (KV_DIM, BF16, 1), (IDX_HEADS * IDX_DIM, BF16, 1),
        (LANES, F32, 1), (LANES, BF16, 1), (D_MODEL, F32, 1), (D_MODEL, F32, 1),
    ]
    return pl.pallas_call(
        _in_proj_kernel,
        grid=(n // tm,),
        in_specs=[row(D_MODEL), _const_spec((1, D_MODEL)),
                  pl.BlockSpec((D_MODEL, IN_PAD), lambda i: (0, 0), pipeline_mode=pl.Buffered(1)),
                  _const_spec((1, HEAD_DIM)), _const_spec((1, HEAD_DIM)), _const_spec((1, LANES))],
        out_specs=[row(w, per) for w, _, per in outs],
        out_shape=[jax.ShapeDtypeStruct((n * per, w), dt) for w, dt, per in outs],
        compiler_params=_cparams(("parallel",)),
        name="in_proj",
    )(x2d, n1, w_perm, qn, kn, kin_pad)


def _softplus(x):
    return jnp.maximum(x, 0.0) + jnp.log1p(jnp.exp(-jnp.abs(x)))


def _split3(x):
    hi = x.astype(BF16)
    r1 = x - hi.astype(F32)
    mid = r1.astype(BF16)
    lo = (r1 - mid.astype(F32)).astype(BF16)
    return hi, mid, lo


def _ssd_kernel(xbc_ref, z_ref, sm_ref, cprev_ref, sprev_ref, cw_ref, cb_ref, dtb_ref, dtbt_ref, alog_ref, alogt_ref,
                dsk_ref, nw_ref, exp_ref,
                y_ref, snew_ref, cnew_ref,
                s_scr, xpad_scr, xc_scr, y_scr, *, n_chunks, t_valid):
    c = pl.program_id(1)
    q = BLK
    gw = SSD_HEADS // SSD_GROUPS * SSD_HEAD_DIM

    @pl.when(c == 0)
    def _():
        for g in range(SSD_GROUPS):
            s_scr[g] = sprev_ref[g].T
        xpad_scr[0:HALO, :] = cprev_ref[...]

    xpad_scr[HALO:HALO + q, :] = xbc_ref[...]

    slab = 512
    for cc in range(CONV_DIM // slab):
        sl = slice(cc * slab, (cc + 1) * slab)
        acc = xpad_scr[HALO - 3:HALO - 3 + q, sl] * cw_ref[0:1, sl]
        for i in range(1, CONV_WIDTH):
            acc = acc + xpad_scr[HALO - 3 + i:HALO - 3 + i + q, sl] * cw_ref[i:i + 1, sl]
        xc_scr[:, sl] = _silu(cb_ref[:, sl] + acc)

    n_last = t_valid - (n_chunks - 1) * q

    @pl.when(c == n_chunks - 1)
    def _():
        cnew_ref[...] = xpad_scr[n_last:n_last + HALO, :]

    xpad_scr[0:HALO, :] = xpad_scr[q:q + HALO, :]

    sm = sm_ref[...]
    smt = sm.T
    lane = lax.broadcasted_iota(jnp.int32, (q, LANES), 1)
    row = lax.broadcasted_iota(jnp.int32, (q, LANES), 0)
    is_dt = (lane >= SM_DT) & (lane < SM_DT + SSD_HEADS) & (row + c * q < t_valid)
    is_dt_t = (row >= SM_DT) & (row < SM_DT + SSD_HEADS) & (lane + c * q < t_valid)
    dt = jnp.where(is_dt, _softplus(sm + dtb_ref[...]), 0.0)
    dtt = jnp.where(is_dt_t, _softplus(smt + dtbt_ref[...]), 0.0)
    da = dt * (-jnp.exp(alog_ref[...]))
    dat = dtt * (-jnp.exp(alogt_ref[...]))
    ii = lax.broadcasted_iota(jnp.int32, (q, q), 0)
    jj = lax.broadcasted_iota(jnp.int32, (q, q), 1)
    causal = jj <= ii
    acum = sum(jnp.dot(causal.astype(BF16), p, preferred_element_type=F32) for p in _split3(da))
    acumt = sum(jnp.dot(p, (ii <= jj).astype(BF16), preferred_element_type=F32) for p in _split3(dat))
    a_last = acum[q - 1:q, :]
    expand = exp_ref[...]
    stacked = jnp.concatenate([jnp.exp(acum), jnp.exp(a_last - acum) * dt,
                               jnp.broadcast_to(jnp.exp(a_last), (SUBLANES, LANES))], axis=0)
    stacked_x = sum(jnp.dot(p, expand, preferred_element_type=F32) for p in _split3(stacked))
    ea_x = stacked_x[0:q]
    wdt_x = stacked_x[q:2 * q]
    dec_x = stacked_x[2 * q:2 * q + 1]

    for g in range(SSD_GROUPS):
        gsl = slice(g * gw, (g + 1) * gw)
        bsl = slice(SSD_INNER + g * SSD_STATE, SSD_INNER + (g + 1) * SSD_STATE)
        csl = slice(SSD_INNER + SSD_GROUPS * SSD_STATE + g * SSD_STATE,
                    SSD_INNER + SSD_GROUPS * SSD_STATE + (g + 1) * SSD_STATE)
        bmf = xc_scr[:, bsl]
        bm = bmf.astype(BF16)
        cm = xc_scr[:, csl].astype(BF16)
        xg = xc_scr[:, gsl]
        xgb = xg.astype(BF16)
        cbm = lax.dot_general(cm, bm, (((1,), (1,)), ((), ())), preferred_element_type=F32)
        xw = (xg * wdt_x[:, gsl]).astype(BF16)
        st = jnp.dot(bmf.T.astype(BF16), xw, preferred_element_type=F32)
        s_in = s_scr[g]
        y_off = jnp.dot(cm, s_in.astype(BF16), preferred_element_type=F32) * ea_x[:, gsl]
        s_scr[g] = s_in * dec_x[:, gsl] + st
        for r in range(SSD_HEADS // SSD_GROUPS):
            h = g * (SSD_HEADS // SSD_GROUPS) + r
            seg = acum[:, SM_DT + h:SM_DT + h + 1] - acumt[SM_DT + h:SM_DT + h + 1, :]
            lmat = jnp.exp(jnp.where(causal, seg, -jnp.inf))
            wmat = (cbm * lmat * dtt[SM_DT + h:SM_DT + h + 1, :]).astype(BF16)
            rsl = slice(r * SSD_HEAD_DIM, (r + 1) * SSD_HEAD_DIM)
            hsl = slice(h * SSD_HEAD_DIM, (h + 1) * SSD_HEAD_DIM)
            y_diag = jnp.dot(wmat, xgb[:, rsl], preferred_element_type=F32)
            y_scr[:, hsl] = y_diag + y_off[:, rsl] + xg[:, rsl] * dsk_ref[:, hsl]

    @pl.when(c == n_chunks - 1)
    def _():
        for g in range(SSD_GROUPS):
            snew_ref[g] = s_scr[g].T

    y = y_scr[...] * _silu(z_ref[...])
    y_ref[...] = _rms(y, nw_ref[...]).astype(BF16)


def _ssd(xbc, z, sm, conv_prev8, ssm_prev, p, t_valid):
    b, tp, _ = xbc.shape
    nc = tp // BLK
    gw = SSD_HEADS // SSD_GROUPS * SSD_HEAD_DIM
    seq = lambda width: pl.BlockSpec((None, BLK, width), lambda i, c: (i, c, 0))
    kern = functools.partial(_ssd_kernel, n_chunks=nc, t_valid=t_valid)
    return pl.pallas_call(
        kern,
        grid=(b, nc),
        in_specs=[seq(CONV_DIM), seq(SSD_INNER), seq(LANES),
                  pl.BlockSpec((None, HALO, CONV_DIM), lambda i, c: (i, 0, 0)),
                  pl.BlockSpec((None, SSD_GROUPS, gw, SSD_STATE), lambda i, c: (i, 0, 0, 0)),
                  _const_spec((CONV_WIDTH, CONV_DIM)), _const_spec((1, CONV_DIM)),
                  _const_spec((1, LANES)), _const_spec((LANES, 1)), _const_spec((1, LANES)), _const_spec((LANES, 1)),
                  _const_spec((1, SSD_INNER)), _const_spec((1, SSD_INNER)), _const_spec((LANES, SSD_INNER))],
        out_specs=[seq(SSD_INNER),
                   pl.BlockSpec((None, SSD_GROUPS, gw, SSD_STATE), lambda i, c: (i, 0, 0, 0)),
                   pl.BlockSpec((None, HALO, CONV_DIM), lambda i, c: (i, 0, 0))],
        out_shape=[jax.ShapeDtypeStruct((b, tp, SSD_INNER), BF16),
                   jax.ShapeDtypeStruct((b, SSD_GROUPS, gw, SSD_STATE), F32),
                   jax.ShapeDtypeStruct((b, HALO, CONV_DIM), F32)],
        scratch_shapes=[pltpu.VMEM((SSD_GROUPS, SSD_STATE, gw), F32),
                        pltpu.VMEM((BLK + 2 * HALO, CONV_DIM), F32),
                        pltpu.VMEM((BLK, CONV_DIM), F32),
                        pltpu.VMEM((BLK, SSD_INNER), F32)],
        compiler_params=_cparams(("parallel", "arbitrary")),
        name="ssd",
    )(xbc, z, sm, conv_prev8, ssm_prev, p["conv_w"], p["conv_b"], p["dtb"], p["dtb_t"], p["alog"], p["alog_t"],
      p["dskip_x"], p["ssd_norm_w"], p["expand"])


N_BIAS_TILES = 5


def _log_bucket_starts():
    nb = REL_BUCKETS // 2
    max_exact = nb // 2
    s = nb - max_exact
    starts = []
    for m in range(1, s):
        n = max_exact
        while n ** s * max_exact ** m < max_exact ** s * REL_MAX_DIST ** m:
            n += 1
        starts.append(n)
    return starts


def _bias_kernel(rb_ref, bt_ref):
    nb = REL_BUCKETS // 2
    max_exact = nb // 2
    qq = lax.broadcasted_iota(jnp.int32, (BLK, BLK), 0)
    kk = lax.broadcasted_iota(jnp.int32, (BLK, BLK), 1)
    for u in range(N_BIAS_TILES):
        rel = (u - 2) * BLK + kk - qq
        n = jnp.abs(rel)
        large = max_exact + sum(jnp.where(n >= start, 1, 0) for start in _log_bucket_starts())
        bucket = jnp.where(rel > 0, nb, 0) + jnp.where(n < max_exact, n, large)
        for h in range(N_HEADS):
            acc = jnp.zeros((BLK, BLK), F32)
            for bkt in range(REL_BUCKETS):
                acc = jnp.where(bucket == bkt, rb_ref[bkt, h], acc)
            bt_ref[h * N_BIAS_TILES + u] = acc


def _bias_tiles(rel_bias):
    return pl.pallas_call(
        _bias_kernel,
        in_specs=[pl.BlockSpec(memory_space=pltpu.SMEM)],
        out_specs=pl.BlockSpec(memory_space=pltpu.VMEM),
        out_shape=jax.ShapeDtypeStruct((N_HEADS * N_BIAS_TILES, BLK, BLK), F32),
        name="bias_tiles",
    )(rel_bias)


KB = 2 * BLK


def _attn_kernel(*refs, nkp_total, n_past, qblk0, chunk_off, l_valid, n_sel):
    if n_past:
        pki_ref, pk_ref, pv_ref, *refs = refs
    (qn_ref, qi_ref, sm_ref, ki_new_ref, k_new_ref, v_new_ref, bt_ref, o_ref,
     ki_scr, k_scr, v_scr, st_scr, m_scr, lg_scr, qis_scr, qs_scr, mrun_scr, lrun_scr, acc_scr) = refs
    i = pl.program_id(1)
    r = BLK

    @pl.when(i == 0)
    def _():
        if n_past:
            def load_past(c, carry):
                rows = pl.ds(pl.multiple_of(c * KB, KB), KB)
                ki_scr[rows, :] = pki_ref[rows, :].astype(BF16)
                for g in range(N_KV_HEADS):
                    src = pl.ds(pl.multiple_of(c * KB * N_KV_HEADS, KB) + g, KB, stride=N_KV_HEADS)
                    k_scr[rows, g * HEAD_DIM:(g + 1) * HEAD_DIM] = pk_ref[src, :].astype(BF16)
                    v_scr[rows, g * HEAD_DIM:(g + 1) * HEAD_DIM] = pv_ref[src, :].astype(BF16)
                return carry

            lax.fori_loop(0, n_past // KB, load_past, 0)
        n_new = k_new_ref.shape[0]
        ki_scr[n_past:n_past + n_new, :] = ki_new_ref[:, SM_KI:SM_KI + IDX_DIM]
        k_scr[n_past:n_past + n_new, :] = k_new_ref[...]
        v_scr[n_past:n_past + n_new, :] = v_new_ref[...]
        n_tail = nkp_total * KB - n_past - n_new
        if n_tail:
            for scr in (ki_scr, k_scr, v_scr):
                scr[n_past + n_new:, :] = jnp.zeros((n_tail, scr.shape[1]), BF16)

    qb = qblk0 + i
    q0 = qb * BLK
    k_end = CHUNK * ((q0 + BLK - 1 + chunk_off) // CHUNK + 1) - chunk_off
    nkp = jnp.minimum(nkp_total, (k_end + KB - 1) // KB)
    n_keys = nkp_total * KB

    nt = (((1,), (1,)), ((), ()))
    wit = sm_ref[...].T
    qpos = q0 + lax.broadcasted_iota(jnp.int32, (1, r), 1)
    n_adm = jnp.minimum(CHUNK * ((qpos + chunk_off) // CHUNK + 1) - chunk_off, l_valid)
    krow = lax.broadcasted_iota(jnp.int32, (KB, r), 0)

    fold_rows = 8 * SUBLANES

    def fold(x, op):
        return op(x.reshape(KB // fold_rows, fold_rows, r), axis=0)

    def for_key_steps(body, init):
        carry = lax.fori_loop(0, nkp // 2, lambda t, c: body(2 * t + 1, body(2 * t, c)), init)
        return lax.cond(nkp % 2 == 1, lambda c: body(nkp - 1, c), lambda c: c, carry)

    for h in range(IDX_HEADS):
        qis_scr[h * r:(h + 1) * r, :] = qi_ref[:, h * IDX_DIM:(h + 1) * IDX_DIM]

    def score_body(jp, carry):
        mn, mx = carry
        kij = ki_scr[pl.ds(pl.multiple_of(jp * KB, KB), KB), :]
        acc = jnp.zeros((KB, r), F32)
        for hp in range(IDX_HEADS // 2):
            sh = lax.dot_general(kij, qis_scr[2 * hp * r:2 * (hp + 1) * r, :], nt, preferred_element_type=F32)
            for e in range(2):
                h = 2 * hp + e
                acc = acc + wit[SM_WI + h:SM_WI + h + 1, :] * jnp.maximum(sh[:, e * r:(e + 1) * r], 0.0)
        adm = krow < n_adm - jp * KB
        s = jnp.where(adm, acc * (IDX_DIM ** -0.5), -jnp.inf)
        st_scr[jp] = s
        mn = jnp.minimum(mn, fold(jnp.where(adm, s, jnp.inf), jnp.min))
        mx = jnp.maximum(mx, fold(s, jnp.max))
        return mn, mx

    init = (jnp.full((fold_rows, r), jnp.inf, F32), jnp.full((fold_rows, r), -jnp.inf, F32))
    mn, mx = for_key_steps(score_body, init)
    lo0 = jnp.min(mn, axis=0, keepdims=True)
    hi0 = jnp.max(mx, axis=0, keepdims=True)
    kk = jnp.minimum(n_adm, n_sel).astype(F32)

    def count(pred):
        def body(jp, acc):
            return acc + fold(jnp.where(pred(st_scr[jp], jp), 1.0, 0.0), jnp.sum)
        acc = lax.fori_loop(0, nkp, body, jnp.zeros((fold_rows, r), F32))
        return jnp.sum(acc, axis=0, keepdims=True)

    def search_body(_, carry):
        lo, hi, c_lo, c_hi, ub, c_ub, hit = carry
        frac = jnp.clip((c_lo - kk) / (c_lo - c_hi), SEARCH_CLAMP, 1.0 - SEARCH_CLAMP)
        mid = jnp.where(hit > 0.0, lo, lo * (1.0 - frac) + hi * frac)
        cnt = count(lambda s, jp: s >= mid)
        ok = cnt >= kk
        now = cnt == kk
        lo = jnp.where(ok, mid, lo)
        hi = jnp.where(ok & ~now, hi, mid)
        c_lo = jnp.where(ok, cnt, c_lo)
        c_hi = jnp.where(now, kk - 1.0, jnp.where(ok, c_hi, cnt))
        ub = jnp.where(ok, ub, mid)
        c_ub = jnp.where(ok, c_ub, cnt)
        return lo, hi, c_lo, c_hi, ub, c_ub, jnp.where(now, 1.0, 0.0)

    all_adm = n_adm.astype(F32) == kk
    zeros = jnp.zeros((1, r), F32)
    init = (lo0, jnp.where(all_adm, lo0, hi0), n_adm.astype(F32), jnp.where(all_adm, kk - 1.0, zeros),
            jnp.full((1, r), jnp.inf, F32), zeros, jnp.where(all_adm, 1.0, 0.0))
    lo_f, _, _, _, ub, c_ub, hit_f = lax.fori_loop(0, SEARCH_STEPS, search_body, init)
    hit = hit_f > 0.0
    pending = jnp.sum(jnp.where(hit, 0, 1))
    take_all = jnp.full((1, r), float(n_keys), F32)

    def exact_path():
        def next_below(ub):
            def body(jp, acc):
                s = st_scr[jp]
                return jnp.maximum(acc, fold(jnp.where(s < ub, s, -jnp.inf), jnp.max))
            acc = lax.fori_loop(0, nkp, body, jnp.full((fold_rows, r), -jnp.inf, F32))
            return jnp.max(acc, axis=0, keepdims=True)

        def descend_cond(carry):
            *_, todo, it = carry
            return (todo > 0) & (it < n_keys)

        def descend_body(carry):
            ub, c_ub, _, _, _, it = carry
            t = next_below(ub)
            c_t = count(lambda s, jp: s >= t)
            done = hit | (c_t >= kk)
            return (jnp.where(done, ub, t), jnp.where(done, c_ub, c_t), t, c_t,
                    jnp.sum(jnp.where(done, 0, 1)), it + 1)

        _, c_gt, t, c_t, _, _ = lax.while_loop(descend_cond, descend_body,
                                               (ub, c_ub, lo0, zeros, jnp.int32(1), jnp.int32(0)))
        ties_wanted = jnp.where(hit, take_all, kk - c_gt)
        extra_ties = jnp.sum(jnp.where(~hit & (c_t - c_gt > ties_wanted), 1, 0))
        return jnp.where(hit, lo_f, t), ties_wanted, extra_ties

    thr, ties_wanted, extra_ties = lax.cond(pending > 0, exact_path, lambda: (lo_f, take_all, jnp.int32(0)))

    def mask_body(jp, carry):
        m_scr[jp] = jnp.where(st_scr[jp] >= thr, 0.0, NEG_BIG).T
        return carry

    def mask_with_ties():
        ii = lax.broadcasted_iota(jnp.int32, (KB, KB), 0)
        jj = lax.broadcasted_iota(jnp.int32, (KB, KB), 1)
        upto = (jj <= ii).astype(BF16)

        def body(jp, wanted):
            s = st_scr[jp]
            tied = s == thr
            rank = jnp.dot(upto, jnp.where(tied, 1.0, 0.0).astype(BF16), preferred_element_type=F32)
            m_scr[jp] = jnp.where((s > thr) | (tied & (rank <= wanted)), 0.0, NEG_BIG).T
            return wanted - rank[KB - 1:KB, :]

        lax.fori_loop(0, nkp, body, ties_wanted)
        return jnp.int32(0)

    lax.cond(extra_ties > 0, mask_with_ties, lambda: lax.fori_loop(0, nkp, mask_body, jnp.int32(0)))

    scale = HEAD_DIM ** -0.5
    rep = N_HEADS // N_KV_HEADS
    for h in range(N_HEADS):
        qs_scr[h // rep, (h % rep) * r:(h % rep + 1) * r, :] = qn_ref[:, h * HEAD_DIM:(h + 1) * HEAD_DIM]
    mrun_scr[...] = jnp.full(mrun_scr.shape, -jnp.inf, F32)
    lrun_scr[...] = jnp.zeros(lrun_scr.shape, F32)
    acc_scr[...] = jnp.zeros(acc_scr.shape, F32)

    def logit_body(jp, carry):
        u0 = jnp.clip(2 * jp - qb + 2, 0, N_BIAS_TILES - 1)
        u1 = jnp.clip(2 * jp + 1 - qb + 2, 0, N_BIAS_TILES - 1)
        madd = m_scr[jp]
        for g in range(N_KV_HEADS):
            lt = lax.dot_general(qs_scr[g], k_scr[pl.ds(pl.multiple_of(jp * KB, KB), KB),
                                                  g * HEAD_DIM:(g + 1) * HEAD_DIM], nt,
                                 preferred_element_type=F32)
            for e in range(rep):
                h = g * rep + e
                bias = jnp.concatenate([bt_ref[h * N_BIAS_TILES + u0], bt_ref[h * N_BIAS_TILES + u1]], axis=1)
                lg = lt[e * r:(e + 1) * r, :] * scale + bias + madd
                lg_scr[h, jp] = lg
                mrun_scr[h] = jnp.maximum(mrun_scr[h], jnp.maximum(lg[:, :BLK], lg[:, BLK:]))
        return carry

    for_key_steps(logit_body, 0)
    for h in range(N_HEADS):
        mrun_scr[h] = jnp.broadcast_to(jnp.max(mrun_scr[h], axis=1, keepdims=True), (r, BLK))

    def pv_body(jp, carry):
        for g in range(N_KV_HEADS):
            es = []
            for e in range(rep):
                h = g * rep + e
                mrow = mrun_scr[h]
                ex = jnp.exp(lg_scr[h, jp] - jnp.concatenate([mrow, mrow], axis=1))
                lrun_scr[h] = lrun_scr[h] + (ex[:, :BLK] + ex[:, BLK:])
                es.append(ex.astype(BF16))
            acc_scr[g] = acc_scr[g] + jnp.dot(jnp.concatenate(es, axis=0),
                                              v_scr[pl.ds(pl.multiple_of(jp * KB, KB), KB),
                                                    g * HEAD_DIM:(g + 1) * HEAD_DIM],
                                              preferred_element_type=F32)
        return carry

    for_key_steps(pv_body, 0)
    for h in range(N_HEADS):
        g, e = h // rep, h % rep
        den = jnp.sum(lrun_scr[h], axis=1, keepdims=True)
        o_ref[:, h * HEAD_DIM:(h + 1) * HEAD_DIM] = (acc_scr[g, e * r:(e + 1) * r, :] / den).astype(BF16)


def _attn(qn, qi, sm, ki_new, k_new, v_new, past, bias_tiles, *, qblk0, chunk_off, t_valid, n_sel):
    b, tq, _ = qn.shape
    n_past = past[0].shape[1] if past is not None else 0
    assert n_past % KB == 0
    n_keys = -(-(n_past + tq) // KB) * KB
    nkp_total = n_keys // KB
    rep = N_HEADS // N_KV_HEADS
    seq = lambda width: pl.BlockSpec((None, BLK, width), lambda bi, i: (bi, i, 0))
    rows = lambda n, width: pl.BlockSpec((None, n, width), lambda bi, i: (bi, 0, 0))
    kern = functools.partial(_attn_kernel, nkp_total=nkp_total, n_past=n_past, qblk0=qblk0, chunk_off=chunk_off,
                             l_valid=n_past + t_valid, n_sel=n_sel)
    once = lambda n, width: pl.BlockSpec((None, n, width), lambda bi, i: (bi, 0, 0), pipeline_mode=pl.Buffered(1))
    past_specs = [once(n_past, IDX_DIM), once(n_past * N_KV_HEADS, HEAD_DIM),
                  once(n_past * N_KV_HEADS, HEAD_DIM)] if n_past else []
    return pl.pallas_call(
        kern,
        grid=(b, tq // BLK),
        in_specs=past_specs + [seq(ATT_INNER), seq(IDX_HEADS * IDX_DIM), seq(LANES),
                               rows(tq, LANES), rows(tq, KV_DIM), rows(tq, KV_DIM),
                               _const_spec((N_HEADS * N_BIAS_TILES, BLK, BLK))],
        out_specs=seq(ATT_INNER),
        out_shape=jax.ShapeDtypeStruct((b, tq, ATT_INNER), BF16),
        scratch_shapes=[pltpu.VMEM((n_keys, IDX_DIM), BF16),
                        pltpu.VMEM((n_keys, KV_DIM), BF16),
                        pltpu.VMEM((n_keys, KV_DIM), BF16),
                        pltpu.VMEM((nkp_total, KB, BLK), F32),
                        pltpu.VMEM((nkp_total, BLK, KB), F32),
                        pltpu.VMEM((N_HEADS, nkp_total, BLK, KB), F32),
                        pltpu.VMEM((IDX_HEADS * BLK, IDX_DIM), BF16),
                        pltpu.VMEM((N_KV_HEADS, rep * BLK, HEAD_DIM), BF16),
                        pltpu.VMEM((N_HEADS, BLK, BLK), F32),
                        pltpu.VMEM((N_HEADS, BLK, BLK), F32),
                        pltpu.VMEM((N_KV_HEADS, rep * BLK, HEAD_DIM), F32)],
        compiler_params=_cparams(("parallel", "arbitrary")),
        name="attn",
    )(*(past or ()), qn, qi, sm, ki_new, k_new, v_new, bias_tiles)


def _out_ffn_kernel(x_ref, ys_ref, ya_ref, gs_ref, ga_ref, wbs_ref, wba_ref, wo_ref, n2_ref, wg_ref, wu_ref, wd_ref,
                    y_ref):
    dot = functools.partial(jnp.dot, preferred_element_type=F32)
    merged = (jax.nn.sigmoid(gs_ref[...]) * dot(ys_ref[...], wbs_ref[...])
              + jax.nn.sigmoid(ga_ref[...]) * dot(ya_ref[...], wba_ref[...]))
    h = x_ref[...] + dot(merged.astype(BF16), wo_ref[...])
    hn = _rms(h, n2_ref[...]).astype(BF16)
    act = (_silu(dot(hn, wg_ref[...])) * dot(hn, wu_ref[...])).astype(BF16)
    y_ref[...] = h + dot(act, wd_ref[...])


def _out_ffn(x2d, ys, ya, gs, ga, p):
    n = x2d.shape[0]
    tm = ROW_TILE
    d_ff = p["w_gate"].shape[1]
    row = lambda width: pl.BlockSpec((tm, width), lambda i: (i, 0))
    wspec = lambda shape: pl.BlockSpec(shape, lambda i: (0, 0), pipeline_mode=pl.Buffered(1))
    return pl.pallas_call(
        _out_ffn_kernel,
        grid=(n // tm,),
        in_specs=[row(D_MODEL), row(SSD_INNER), row(ATT_INNER), row(D_MODEL), row(D_MODEL),
                  wspec((SSD_INNER, D_MODEL)), wspec((ATT_INNER, D_MODEL)), wspec((D_MODEL, D_MODEL)),
                  _const_spec((1, D_MODEL)), wspec((D_MODEL, d_ff)), wspec((D_MODEL, d_ff)), wspec((d_ff, D_MODEL))],
        out_specs=row(D_MODEL),
        out_shape=jax.ShapeDtypeStruct((n, D_MODEL), F32),
        compiler_params=_cparams(("parallel",)),
        name="out_ffn",
    )(x2d, ys, ya, gs, ga, p["w_br_ssd"], p["w_br_att"], p["w_out"], p["norm2_w"], p["w_gate"], p["w_up"],
      p["w_down"])


def _layer(x_pad, t_valid, conv_prev, ssm_prev, past, p, bias_tiles, *, qblk0, chunk_off, n_sel):
    b, tp, _ = x_pad.shape
    x2d = x_pad.reshape(b * tp, D_MODEL)
    z, xbc, qn, k32, v32, kb, vb, qi, sm, smb, gs, ga = _in_proj(
        x2d, p["norm1_w"], p["w_in"], p["q_norm_w"], p["k_norm_w"], p["idx_k_norm_w"])
    seq = lambda a: a.reshape(b, tp, a.shape[-1])

    conv_prev8 = jnp.pad(conv_prev.astype(F32), ((0, 0), (HALO - (CONV_WIDTH - 1), 0), (0, 0)))
    gw = SSD_HEADS // SSD_GROUPS * SSD_HEAD_DIM
    y_ssd, ssm_new, conv_new8 = _ssd(seq(xbc), seq(z), seq(sm), conv_prev8,
                                     ssm_prev.astype(F32).reshape(b, SSD_GROUPS, gw, SSD_STATE), p, t_valid)

    if past is not None:
        pk, pv, pki = past
        n_past = pk.shape[1]
        past = (pki.astype(F32), pk.astype(F32).reshape(b, n_past * N_KV_HEADS, HEAD_DIM),
                pv.astype(F32).reshape(b, n_past * N_KV_HEADS, HEAD_DIM))
    y_att = _attn(seq(qn), seq(qi), seq(sm), seq(smb), seq(kb), seq(vb), past, bias_tiles,
                  qblk0=qblk0, chunk_off=chunk_off, t_valid=t_valid, n_sel=n_sel)

    y = _out_ffn(x2d, y_ssd.reshape(b * tp, SSD_INNER), y_att.reshape(b * tp, ATT_INNER), gs, ga, p)

    k_new = k32.reshape(b, tp, N_KV_HEADS, HEAD_DIM)[:, :t_valid]
    v_new = v32.reshape(b, tp, N_KV_HEADS, HEAD_DIM)[:, :t_valid]
    ki_new = seq(sm)[:, :t_valid, SM_KI:SM_KI + IDX_DIM]
    ssm_new = ssm_new.reshape(b, SSD_HEADS, SSD_HEAD_DIM, SSD_STATE)
    conv_new = conv_new8[:, HALO - (CONV_WIDTH - 1):]
    return y.reshape(b, tp, D_MODEL), k_new, v_new, ki_new, ssm_new, conv_new


def _prepare_params(l, norm1_w, w_in, conv_w, conv_b, dt_bias, a_log, d_skip, ssd_norm_w, q_norm_w, k_norm_w,
                    idx_k_norm_w, w_br_ssd, w_br_att, w_out, norm2_w, w_gate, w_up, w_down):
    offs = [0]
    for w in IN_WIDTHS:
        offs.append(offs[-1] + w)
    seg = lambda i: w_in[l][:, offs[i]:offs[i + 1]]
    i_z, i_xbc, i_dt, i_q, i_k, i_v, i_qi, i_ki, i_wi, i_gs, i_ga = range(11)
    pad = jnp.zeros((D_MODEL, LANES - IDX_DIM - SSD_HEADS - IDX_HEADS), w_in.dtype)
    w_perm = jnp.concatenate([seg(i_z), seg(i_xbc), seg(i_q), seg(i_k), seg(i_v), seg(i_qi), seg(i_gs), seg(i_ga),
                              seg(i_ki), seg(i_dt), seg(i_wi), pad], axis=1).astype(BF16)

    def lanes_at(vec, start):
        return jnp.zeros((1, LANES), F32).at[0, start:start + vec.shape[0]].set(vec.astype(F32))

    dtb = lanes_at(dt_bias[l], SM_DT)
    alog = lanes_at(a_log[l], SM_DT)
    head_of_channel = jnp.arange(SSD_INNER) // SSD_HEAD_DIM
    expand = (jnp.arange(LANES)[:, None] == head_of_channel[None, :] + SM_DT).astype(BF16)
    row = lambda v: v.astype(F32).reshape(1, -1)
    return dict(
        norm1_w=row(norm1_w[l]), w_in=w_perm, conv_w=conv_w[l].astype(F32), conv_b=row(conv_b[l]),
        dtb=dtb, dtb_t=dtb.reshape(LANES, 1), alog=alog, alog_t=alog.reshape(LANES, 1),
        dskip_x=row(jnp.repeat(d_skip[l], SSD_HEAD_DIM)), ssd_norm_w=row(ssd_norm_w[l]), expand=expand,
        q_norm_w=row(q_norm_w[l]), k_norm_w=row(k_norm_w[l]),
        idx_k_norm_w=jnp.ones((1, LANES), F32).at[0, SM_KI:SM_KI + IDX_DIM].set(idx_k_norm_w[l].astype(F32)),
        w_br_ssd=w_br_ssd[l].astype(BF16), w_br_att=w_br_att[l].astype(BF16), w_out=w_out[l].astype(BF16),
        norm2_w=row(norm2_w[l]), w_gate=w_gate[l].astype(BF16), w_up=w_up[l].astype(BF16),
        w_down=w_down[l].astype(BF16))


def _pad_rows(x, tp):
    return jnp.pad(x, ((0, 0), (0, tp - x.shape[1]), (0, 0)))


def kernel(x_prompt, x_sample, cache_k, cache_v, cache_kidx, state_ssm, state_conv, meta_tokens, rel_bias, norm1_w,
           w_in, conv_w, conv_b, dt_bias, a_log, d_skip, ssd_norm_w, q_norm_w, k_norm_w, idx_k_norm_w, w_br_ssd,
           w_br_att, w_out, norm2_w, w_gate, w_up, w_down):
    bp, sp, _ = x_prompt.shape
    bs, ts, _ = x_sample.shape
    depth = w_in.shape[0]
    past = cache_k.shape[2]
    assert past % BLK == 0 and BLK % CHUNK == 0 and N_META <= CHUNK

    tq_p = N_META + sp
    tp_p = -(-tq_p // BLK) * BLK
    tp_s = -(-ts // BLK) * BLK
    n_sel_p = min(TOPK_MAX, sp // 4)
    n_sel_s = min(TOPK_MAX, (past + ts) // 4)

    hp = jnp.concatenate([jnp.broadcast_to(meta_tokens.astype(x_prompt.dtype)[None], (bp, N_META, D_MODEL)),
                          x_prompt, jnp.zeros((bp, tp_p - tq_p, D_MODEL), x_prompt.dtype)], axis=1)
    hs = _pad_rows(x_sample, tp_s)
    conv0 = jnp.zeros((bp, CONV_WIDTH - 1, CONV_DIM), F32)
    ssm0 = jnp.zeros((bp, SSD_HEADS, SSD_HEAD_DIM, SSD_STATE), F32)
    bias_tiles = _bias_tiles(rel_bias.astype(F32))

    outs_p, outs_s = [], []
    for l in range(depth):
        p = _prepare_params(l, norm1_w, w_in, conv_w, conv_b, dt_bias, a_log, d_skip, ssd_norm_w, q_norm_w, k_norm_w,
                            idx_k_norm_w, w_br_ssd, w_br_att, w_out, norm2_w, w_gate, w_up, w_down)
        hp, *rest_p = _layer(hp, tq_p, conv0, ssm0, None, p, bias_tiles,
                             qblk0=0, chunk_off=CHUNK - N_META, n_sel=n_sel_p)
        hs, *rest_s = _layer(hs, ts, state_conv[l], state_ssm[l], (cache_k[l], cache_v[l], cache_kidx[l]), p,
                             bias_tiles, qblk0=past // BLK, chunk_off=0, n_sel=n_sel_s)
        outs_p.append(rest_p)
        outs_s.append(rest_s)

    stack = lambda outs, idx, dt: jnp.stack([o[idx] for o in outs], 0).astype(dt)
    y_prompt = hp[:, N_META:tq_p]
    y_sample = hs[:, :ts]
    return (y_prompt, y_sample,
            stack(outs_p, 0, x_prompt.dtype), stack(outs_p, 1, x_prompt.dtype), stack(outs_p, 2, x_prompt.dtype),
            stack(outs_p, 3, state_ssm.dtype), stack(outs_p, 4, x_prompt.dtype),
            stack(outs_s, 0, x_sample.dtype), stack(outs_s, 1, x_sample.dtype), stack(outs_s, 2, x_sample.dtype),
            stack(outs_s, 3, state_ssm.dtype), stack(outs_s, 4, x_sample.dtype))
```

```python
import functools

import jax
import jax.numpy as jnp
from jax import lax
from jax.experimental import pallas as pl
from jax.experimental.pallas import tpu as pltpu

F32 = jnp.float32
BF16 = jnp.bfloat16

D_MODEL = 1024
CHUNK = 64
N_META = 16
SSD_HEADS = 16
SSD_HEAD_DIM = 64
SSD_INNER = SSD_HEADS * SSD_HEAD_DIM
SSD_GROUPS = 4
SSD_STATE = 128
CONV_WIDTH = 4
CONV_DIM = SSD_INNER + 2 * SSD_GROUPS * SSD_STATE
N_HEADS = 8
N_KV_HEADS = 2
HEAD_DIM = 128
ATT_INNER = N_HEADS * HEAD_DIM
KV_DIM = N_KV_HEADS * HEAD_DIM
IDX_HEADS = 8
IDX_DIM = 64
TOPK_MAX = 256
REL_BUCKETS = 32
REL_MAX_DIST = 128
IN_WIDTHS = (SSD_INNER, CONV_DIM, SSD_HEADS, ATT_INNER, KV_DIM, KV_DIM, IDX_HEADS * IDX_DIM, IDX_DIM, IDX_HEADS,
             D_MODEL, D_MODEL)
EPS = 1e-6

LANES = 128
SUBLANES = 8
VMEM_LIMIT_BYTES = 56 * 1024 * 1024

BLK = LANES
ROW_TILE = 256
HALO = SUBLANES

C_Z = 0
C_XBC = C_Z + SSD_INNER
C_Q = C_XBC + CONV_DIM
C_K = C_Q + ATT_INNER
C_V = C_K + KV_DIM
C_QI = C_V + KV_DIM
C_GS = C_QI + IDX_HEADS * IDX_DIM
C_GA = C_GS + D_MODEL
C_SM = C_GA + D_MODEL
IN_PAD = C_SM + LANES
SM_KI = 0
SM_DT = SM_KI + IDX_DIM
SM_WI = SM_DT + SSD_HEADS

SEARCH_STEPS = 16
SEARCH_CLAMP = 1.0 / 16
NEG_BIG = -1e30


def _cparams(sem):
    return pltpu.CompilerParams(dimension_semantics=sem, vmem_limit_bytes=VMEM_LIMIT_BYTES)


def _const_spec(shape):
    nd = len(shape)
    return pl.BlockSpec(shape, lambda *_: (0,) * nd)


def _rms(x, w):
    return x * lax.rsqrt(jnp.mean(x * x, axis=-1, keepdims=True) + EPS) * w


def _silu(x):
    return x * jax.nn.sigmoid(x)


def _in_proj_kernel(x_ref, n1_ref, w_ref, qn_ref, kn_ref, kin_ref,
                    z_ref, xbc_ref, q_ref, k_ref, v_ref, kb_ref, vb_ref, qi_ref, sm_ref, smb_ref, gs_ref, ga_ref):
    hn = _rms(x_ref[...], n1_ref[...]).astype(BF16)
    tm = x_ref.shape[0]

    def mm(lo, hi):
        return jnp.dot(hn, w_ref[:, lo:hi], preferred_element_type=F32)

    z_ref[...] = mm(C_Z, C_XBC)
    xbc_ref[...] = mm(C_XBC, C_Q)
    q = mm(C_Q, C_K)
    for h in range(N_HEADS):
        sl = slice(h * HEAD_DIM, (h + 1) * HEAD_DIM)
        q_ref[:, sl] = _rms(q[:, sl], qn_ref[...]).astype(BF16)
    k = mm(C_K, C_V)
    for h in range(N_KV_HEADS):
        sl = slice(h * HEAD_DIM, (h + 1) * HEAD_DIM)
        kh = _rms(k[:, sl], kn_ref[...])
        k_ref[pl.ds(h, tm, stride=N_KV_HEADS), :] = kh
        kb_ref[:, sl] = kh.astype(BF16)
    v = mm(C_V, C_QI)
    for h in range(N_KV_HEADS):
        v_ref[pl.ds(h, tm, stride=N_KV_HEADS), :] = v[:, h * HEAD_DIM:(h + 1) * HEAD_DIM]
    vb_ref[...] = v.astype(BF16)
    qi_ref[...] = mm(C_QI, C_GS).astype(BF16)
    gs_ref[...] = mm(C_GS, C_GA)
    ga_ref[...] = mm(C_GA, C_SM)
    sm = mm(C_SM, IN_PAD)
    lane = lax.broadcasted_iota(jnp.int32, sm.shape, 1)
    is_ki = lane < SM_KI + IDX_DIM
    ms = jnp.sum(jnp.where(is_ki, sm * sm, 0.0), axis=-1, keepdims=True) * (1.0 / IDX_DIM)
    ki = sm * lax.rsqrt(ms + EPS) * kin_ref[...]
    is_wi = (lane >= SM_WI) & (lane < SM_WI + IDX_HEADS)
    out = jnp.where(is_ki, ki, jnp.where(is_wi, sm * (IDX_HEADS ** -0.5), sm))
    sm_ref[...] = out
    smb_ref[...] = out.astype(BF16)


def _row_tile(n):
    return ROW_TILE if n % ROW_TILE == 0 else BLK


def _in_proj(x2d, p):
    n = x2d.shape[0]
    tm = _row_tile(n)
    row = lambda width, per_token=1: pl.BlockSpec((tm * per_token, width), lambda i: (i, 0))
    outs = [
        (SSD_INNER, F32, 1), (CONV_DIM, F32, 1), (ATT_INNER, BF16, 1), (HEAD_DIM, F32, N_KV_HEADS),
        (HEAD_DIM, F32, N_KV_HEADS), (KV_DIM, BF16, 1), (KV_DIM, BF16, 1), (IDX_HEADS * IDX_DIM, BF16, 1),
        (LANES, F32, 1), (LANES, BF16, 1), (D_MODEL, F32, 1), (D_MODEL, F32, 1),
    ]
    return pl.pallas_call(
        _in_proj_kernel,
        grid=(n // tm,),
        in_specs=[row(D_MODEL), _const_spec((1, D_MODEL)),
                  pl.BlockSpec((D_MODEL, IN_PAD), lambda i: (0, 0), pipeline_mode=pl.Buffered(1)),
                  _const_spec((1, HEAD_DIM)), _const_spec((1, HEAD_DIM)), _const_spec((1, LANES))],
        out_specs=[row(w, per) for w, _, per in outs],
        out_shape=[jax.ShapeDtypeStruct((n * per, w), dt) for w, dt, per in outs],
        compiler_params=_cparams(("parallel",)),
        name="in_proj",
    )(x2d, p["norm1_w"], p["w_in"], p["q_norm_w"], p["k_norm_w"], p["idx_k_norm_w"])


def _softplus(x):
    return jnp.maximum(x, 0.0) + jnp.log1p(jnp.exp(-jnp.abs(x)))


def _split3(x):
    hi = x.astype(BF16)
    r1 = x - hi.astype(F32)
    mid = r1.astype(BF16)
    lo = (r1 - mid.astype(F32)).astype(BF16)
    return hi, mid, lo


def _ssd_kernel(*refs, n_chunks, n_lead, t_x):
    if n_lead:
        xbc_lead_ref, z_lead_ref, sm_lead_ref, *refs = refs
    (xbc_ref, z_ref, sm_ref, cprev_ref, sprev_ref, cw_ref, cb_ref, dtb_ref, dtbt_ref, alog_ref, alogt_ref,
     dsk_ref, nw_ref, exp_ref, y_ref, snew_ref, cnew_ref, s_scr, xpad_scr, xc_scr, y_scr) = refs
    c = pl.program_id(1)
    q = BLK
    gw = SSD_HEADS // SSD_GROUPS * SSD_HEAD_DIM
    lead_chunks = 1 if n_lead else 0
    n_valid = jnp.minimum(q, t_x - (c - lead_chunks) * q)
    if n_lead:
        n_valid = jnp.where(c == 0, n_lead, n_valid)

    @pl.when(c == 0)
    def _():
        for g in range(SSD_GROUPS):
            s_scr[g] = sprev_ref[g].T
        xpad_scr[0:HALO, :] = cprev_ref[...]

    if n_lead:
        @pl.when(c == 0)
        def _():
            xpad_scr[HALO:HALO + q, :] = xbc_lead_ref[...]

        @pl.when(c > 0)
        def _():
            xpad_scr[HALO:HALO + q, :] = xbc_ref[...]

        sm = jnp.where(c == 0, sm_lead_ref[...], sm_ref[...])
        z = jnp.where(c == 0, z_lead_ref[...], z_ref[...])
    else:
        xpad_scr[HALO:HALO + q, :] = xbc_ref[...]
        sm = sm_ref[...]
        z = z_ref[...]

    slab = 512
    for cc in range(CONV_DIM // slab):
        sl = slice(cc * slab, (cc + 1) * slab)
        acc = xpad_scr[HALO - 3:HALO - 3 + q, sl] * cw_ref[0:1, sl]
        for i in range(1, CONV_WIDTH):
            acc = acc + xpad_scr[HALO - 3 + i:HALO - 3 + i + q, sl] * cw_ref[i:i + 1, sl]
        xc_scr[:, sl] = _silu(cb_ref[:, sl] + acc)

    n_last = t_x - (n_chunks - lead_chunks - 1) * q

    @pl.when(c == n_chunks - 1)
    def _():
        cnew_ref[...] = xpad_scr[n_last:n_last + HALO, :]

    xpad_scr[0:HALO, :] = xpad_scr[pl.ds(pl.multiple_of(jnp.where(c == n_chunks - 1, q, n_valid), SUBLANES), HALO), :]

    smt = sm.T
    lane = lax.broadcasted_iota(jnp.int32, (q, LANES), 1)
    row = lax.broadcasted_iota(jnp.int32, (q, LANES), 0)
    is_dt = (lane >= SM_DT) & (lane < SM_DT + SSD_HEADS) & (row < n_valid)
    is_dt_t = (row >= SM_DT) & (row < SM_DT + SSD_HEADS) & (lane < n_valid)
    dt = jnp.where(is_dt, _softplus(sm + dtb_ref[...]), 0.0)
    dtt = jnp.where(is_dt_t, _softplus(smt + dtbt_ref[...]), 0.0)
    da = dt * (-jnp.exp(alog_ref[...]))
    dat = dtt * (-jnp.exp(alogt_ref[...]))
    ii = lax.broadcasted_iota(jnp.int32, (q, q), 0)
    jj = lax.broadcasted_iota(jnp.int32, (q, q), 1)
    causal = jj <= ii
    acum = sum(jnp.dot(causal.astype(BF16), p, preferred_element_type=F32) for p in _split3(da))
    acumt = sum(jnp.dot(p, (ii <= jj).astype(BF16), preferred_element_type=F32) for p in _split3(dat))
    a_last = acum[q - 1:q, :]
    expand = exp_ref[...]
    stacked = jnp.concatenate([jnp.exp(acum), jnp.exp(a_last - acum) * dt,
                               jnp.broadcast_to(jnp.exp(a_last), (SUBLANES, LANES))], axis=0)
    stacked_x = sum(jnp.dot(p, expand, preferred_element_type=F32) for p in _split3(stacked))
    ea_x = stacked_x[0:q]
    wdt_x = stacked_x[q:2 * q]
    dec_x = stacked_x[2 * q:2 * q + 1]

    for g in range(SSD_GROUPS):
        gsl = slice(g * gw, (g + 1) * gw)
        bsl = slice(SSD_INNER + g * SSD_STATE, SSD_INNER + (g + 1) * SSD_STATE)
        csl = slice(SSD_INNER + SSD_GROUPS * SSD_STATE + g * SSD_STATE,
                    SSD_INNER + SSD_GROUPS * SSD_STATE + (g + 1) * SSD_STATE)
        bmf = xc_scr[:, bsl]
        bm = bmf.astype(BF16)
        cm = xc_scr[:, csl].astype(BF16)
        xg = xc_scr[:, gsl]
        xgb = xg.astype(BF16)
        cbm = lax.dot_general(cm, bm, (((1,), (1,)), ((), ())), preferred_element_type=F32)
        xw = (xg * wdt_x[:, gsl]).astype(BF16)
        st = jnp.dot(bmf.T.astype(BF16), xw, preferred_element_type=F32)
        s_in = s_scr[g]
        y_off = jnp.dot(cm, s_in.astype(BF16), preferred_element_type=F32) * ea_x[:, gsl]
        s_scr[g] = s_in * dec_x[:, gsl] + st
        for r in range(SSD_HEADS // SSD_GROUPS):
            h = g * (SSD_HEADS // SSD_GROUPS) + r
            seg = acum[:, SM_DT + h:SM_DT + h + 1] - acumt[SM_DT + h:SM_DT + h + 1, :]
            lmat = jnp.exp(jnp.where(causal, seg, -jnp.inf))
            wmat = (cbm * lmat * dtt[SM_DT + h:SM_DT + h + 1, :]).astype(BF16)
            rsl = slice(r * SSD_HEAD_DIM, (r + 1) * SSD_HEAD_DIM)
            hsl = slice(h * SSD_HEAD_DIM, (h + 1) * SSD_HEAD_DIM)
            y_diag = jnp.dot(wmat, xgb[:, rsl], preferred_element_type=F32)
            y_scr[:, hsl] = y_diag + y_off[:, rsl] + xg[:, rsl] * dsk_ref[:, hsl]

    @pl.when(c == n_chunks - 1)
    def _():
        for g in range(SSD_GROUPS):
            snew_ref[g] = s_scr[g].T

    y = y_scr[...] * _silu(z)
    y_ref[...] = _rms(y, nw_ref[...]).astype(BF16)


def _ssd(lead, xbc, z, sm, conv_prev8, ssm_prev, p, *, n_lead, t_x):
    b, tp, _ = xbc.shape
    lead_chunks = 1 if n_lead else 0
    assert n_lead % SUBLANES == 0 and n_lead <= BLK
    nc = tp // BLK + lead_chunks
    gw = SSD_HEADS // SSD_GROUPS * SSD_HEAD_DIM
    seq = lambda width: pl.BlockSpec((None, BLK, width), lambda i, c: (i, jnp.maximum(c - lead_chunks, 0), 0))
    kern = functools.partial(_ssd_kernel, n_chunks=nc, n_lead=n_lead, t_x=t_x)
    lead_specs = [_const_spec((BLK, CONV_DIM)), _const_spec((BLK, SSD_INNER)), _const_spec((BLK, LANES))]
    return pl.pallas_call(
        kern,
        grid=(b, nc),
        in_specs=(lead_specs if n_lead else []) + [
                  seq(CONV_DIM), seq(SSD_INNER), seq(LANES),
                  pl.BlockSpec((None, HALO, CONV_DIM), lambda i, c: (i, 0, 0)),
                  pl.BlockSpec((None, SSD_GROUPS, gw, SSD_STATE), lambda i, c: (i, 0, 0, 0)),
                  _const_spec((CONV_WIDTH, CONV_DIM)), _const_spec((1, CONV_DIM)),
                  _const_spec((1, LANES)), _const_spec((LANES, 1)), _const_spec((1, LANES)), _const_spec((LANES, 1)),
                  _const_spec((1, SSD_INNER)), _const_spec((1, SSD_INNER)), _const_spec((LANES, SSD_INNER))],
        out_specs=[seq(SSD_INNER),
                   pl.BlockSpec((None, SSD_GROUPS, gw, SSD_STATE), lambda i, c: (i, 0, 0, 0)),
                   pl.BlockSpec((None, HALO, CONV_DIM), lambda i, c: (i, 0, 0))],
        out_shape=[jax.ShapeDtypeStruct((b, tp, SSD_INNER), BF16),
                   jax.ShapeDtypeStruct((b, SSD_GROUPS, gw, SSD_STATE), F32),
                   jax.ShapeDtypeStruct((b, HALO, CONV_DIM), F32)],
        scratch_shapes=[pltpu.VMEM((SSD_GROUPS, SSD_STATE, gw), F32),
                        pltpu.VMEM((BLK + 2 * HALO, CONV_DIM), F32),
                        pltpu.VMEM((BLK, CONV_DIM), F32),
                        pltpu.VMEM((BLK, SSD_INNER), F32)],
        compiler_params=_cparams(("parallel", "arbitrary")),
        name="ssd",
    )(*(lead if n_lead else ()), xbc, z, sm, conv_prev8, ssm_prev, p["conv_w"], p["conv_b"], p["dtb"], p["dtb_t"],
      p["alog"], p["alog_t"], p["dskip_x"], p["ssd_norm_w"], p["expand"])


N_SHIFT_TILES = 5
LEAD_TILE = N_SHIFT_TILES
N_BIAS_TILES = N_SHIFT_TILES + 1


def _log_bucket_starts():
    nb = REL_BUCKETS // 2
    max_exact = nb // 2
    s = nb - max_exact
    starts = []
    for m in range(1, s):
        n = max_exact
        while n ** s * max_exact ** m < max_exact ** s * REL_MAX_DIST ** m:
            n += 1
        starts.append(n)
    return starts


def _bias_kernel(rb_ref, bt_ref):
    nb = REL_BUCKETS // 2
    max_exact = nb // 2
    qq = lax.broadcasted_iota(jnp.int32, (BLK, BLK), 0)
    kk = lax.broadcasted_iota(jnp.int32, (BLK, BLK), 1)
    for u in range(N_BIAS_TILES):
        rel = kk - qq + ((u - 2) * BLK if u < N_SHIFT_TILES else -N_META)
        n = jnp.abs(rel)
        large = max_exact + sum(jnp.where(n >= start, 1, 0) for start in _log_bucket_starts())
        bucket = jnp.where(rel > 0, nb, 0) + jnp.where(n < max_exact, n, large)
        for h in range(N_HEADS):
            acc = jnp.zeros((BLK, BLK), F32)
            for bkt in range(REL_BUCKETS):
                acc = jnp.where(bucket == bkt, rb_ref[bkt, h], acc)
            bt_ref[h * N_BIAS_TILES + u] = acc


def _bias_tiles(rel_bias):
    return pl.pallas_call(
        _bias_kernel,
        in_specs=[pl.BlockSpec(memory_space=pltpu.SMEM)],
        out_specs=pl.BlockSpec(memory_space=pltpu.VMEM),
        out_shape=jax.ShapeDtypeStruct((N_HEADS * N_BIAS_TILES, BLK, BLK), F32),
        name="bias_tiles",
    )(rel_bias)


KB = 2 * BLK


def _attn_kernel(*refs, nkp_total, n_past, n_lead, chunk_off, t_x, n_sel):
    if n_past:
        pki_ref, pk_ref, pv_ref, *refs = refs
    if n_lead:
        lki_ref, lk_ref, lv_ref, *refs = refs
    (qn_ref, qi_ref, sm_ref, ki_new_ref, k_new_ref, v_new_ref, bt_ref, o_ref,
     ki_scr, k_scr, v_scr, st_scr, m_scr, lg_scr, qis_scr, qs_scr, mrun_scr, lrun_scr, acc_scr) = refs
    i = pl.program_id(1)
    r = BLK
    n_prefix = n_past + (BLK if n_lead else 0)
    gap = BLK - n_lead if n_lead else 0
    l_valid = n_prefix - gap + t_x

    @pl.when(i == 0)
    def _():
        if n_lead:
            ki_scr[0:BLK, :] = lki_ref[:, SM_KI:SM_KI + IDX_DIM]
            k_scr[0:BLK, :] = lk_ref[...]
            v_scr[0:BLK, :] = lv_ref[...]
        if n_past:
            def load_past(c, carry):
                rows = pl.ds(pl.multiple_of(c * KB, KB), KB)
                ki_scr[rows, :] = pki_ref[rows, :].astype(BF16)
                for g in range(N_KV_HEADS):
                    src = pl.ds(pl.multiple_of(c * KB * N_KV_HEADS, KB) + g, KB, stride=N_KV_HEADS)
                    k_scr[rows, g * HEAD_DIM:(g + 1) * HEAD_DIM] = pk_ref[src, :].astype(BF16)
                    v_scr[rows, g * HEAD_DIM:(g + 1) * HEAD_DIM] = pv_ref[src, :].astype(BF16)
                return carry

            lax.fori_loop(0, n_past // KB, load_past, 0)
        n_new = k_new_ref.shape[0]
        ki_scr[n_prefix:n_prefix + n_new, :] = ki_new_ref[:, SM_KI:SM_KI + IDX_DIM]
        k_scr[n_prefix:n_prefix + n_new, :] = k_new_ref[...]
        v_scr[n_prefix:n_prefix + n_new, :] = v_new_ref[...]
        n_tail = nkp_total * KB - n_prefix - n_new
        if n_tail:
            for scr in (ki_scr, k_scr, v_scr):
                scr[n_prefix + n_new:, :] = jnp.zeros((n_tail, scr.shape[1]), BF16)

    qb = n_prefix // BLK + i
    q0 = qb * BLK - gap
    n_keys = nkp_total * KB

    def chunk_end(pos):
        return jnp.minimum(CHUNK * ((pos + chunk_off) // CHUNK + 1) - chunk_off, l_valid)

    nkp = jnp.minimum(nkp_total, (chunk_end(q0 + BLK - 1) + gap + KB - 1) // KB)

    nt = (((1,), (1,)), ((), ()))
    wit = sm_ref[...].T
    n_adm = chunk_end(q0 + lax.broadcasted_iota(jnp.int32, (1, r), 1))
    krow = lax.broadcasted_iota(jnp.int32, (KB, r), 0)

    fold_rows = 8 * SUBLANES

    def fold(x, op):
        return op(x.reshape(KB // fold_rows, fold_rows, r), axis=0)

    def for_key_steps(body, init):
        carry = lax.fori_loop(0, nkp // 2, lambda t, c: body(2 * t + 1, body(2 * t, c)), init)
        return lax.cond(nkp % 2 == 1, lambda c: body(nkp - 1, c), lambda c: c, carry)

    for h in range(IDX_HEADS):
        qis_scr[h * r:(h + 1) * r, :] = qi_ref[:, h * IDX_DIM:(h + 1) * IDX_DIM]

    def score_body(jp, carry):
        mn, mx = carry
        kij = ki_scr[pl.ds(pl.multiple_of(jp * KB, KB), KB), :]
        acc = jnp.zeros((KB, r), F32)
        for hp in range(IDX_HEADS // 2):
            sh = lax.dot_general(kij, qis_scr[2 * hp * r:2 * (hp + 1) * r, :], nt, preferred_element_type=F32)
            for e in range(2):
                h = 2 * hp + e
                acc = acc + wit[SM_WI + h:SM_WI + h + 1, :] * jnp.maximum(sh[:, e * r:(e + 1) * r], 0.0)
        adm = krow < n_adm + gap - jp * KB
        if gap:
            adm = adm & ((jp > 0) | (krow < n_lead) | (krow >= BLK))
        s = jnp.where(adm, acc * (IDX_DIM ** -0.5), -jnp.inf)
        st_scr[jp] = s
        mn = jnp.minimum(mn, fold(jnp.where(adm, s, jnp.inf), jnp.min))
        mx = jnp.maximum(mx, fold(s, jnp.max))
        return mn, mx

    init = (jnp.full((fold_rows, r), jnp.inf, F32), jnp.full((fold_rows, r), -jnp.inf, F32))
    mn, mx = for_key_steps(score_body, init)
    lo0 = jnp.min(mn, axis=0, keepdims=True)
    hi0 = jnp.max(mx, axis=0, keepdims=True)
    kk = jnp.minimum(n_adm, n_sel).astype(F32)

    def count(pred):
        def body(jp, acc):
            return acc + fold(jnp.where(pred(st_scr[jp], jp), 1.0, 0.0), jnp.sum)
        acc = lax.fori_loop(0, nkp, body, jnp.zeros((fold_rows, r), F32))
        return jnp.sum(acc, axis=0, keepdims=True)

    def search_body(_, carry):
        lo, hi, c_lo, c_hi, ub, c_ub, hit = carry
        frac = jnp.clip((c_lo - kk) / (c_lo - c_hi), SEARCH_CLAMP, 1.0 - SEARCH_CLAMP)
        mid = jnp.where(hit > 0.0, lo, lo * (1.0 - frac) + hi * frac)
        cnt = count(lambda s, jp: s >= mid)
        ok = cnt >= kk
        now = cnt == kk
        lo = jnp.where(ok, mid, lo)
        hi = jnp.where(ok & ~now, hi, mid)
        c_lo = jnp.where(ok, cnt, c_lo)
        c_hi = jnp.where(now, kk - 1.0, jnp.where(ok, c_hi, cnt))
        ub = jnp.where(ok, ub, mid)
        c_ub = jnp.where(ok, c_ub, cnt)
        return lo, hi, c_lo, c_hi, ub, c_ub, jnp.where(now, 1.0, 0.0)

    all_adm = n_adm.astype(F32) == kk
    zeros = jnp.zeros((1, r), F32)
    init = (lo0, jnp.where(all_adm, lo0, hi0), n_adm.astype(F32), jnp.where(all_adm, kk - 1.0, zeros),
            jnp.full((1, r), jnp.inf, F32), zeros, jnp.where(all_adm, 1.0, 0.0))
    lo_f, _, _, _, ub, c_ub, hit_f = lax.fori_loop(0, SEARCH_STEPS, search_body, init)
    hit = hit_f > 0.0
    pending = jnp.sum(jnp.where(hit, 0, 1))
    take_all = jnp.full((1, r), float(n_keys), F32)

    def exact_path():
        def next_below(ub):
            def body(jp, acc):
                s = st_scr[jp]
                return jnp.maximum(acc, fold(jnp.where(s < ub, s, -jnp.inf), jnp.max))
            acc = lax.fori_loop(0, nkp, body, jnp.full((fold_rows, r), -jnp.inf, F32))
            return jnp.max(acc, axis=0, keepdims=True)

        def descend_cond(carry):
            *_, todo, it = carry
            return (todo > 0) & (it < n_keys)

        def descend_body(carry):
            ub, c_ub, _, _, _, it = carry
            t = next_below(ub)
            c_t = count(lambda s, jp: s >= t)
            done = hit | (c_t >= kk)
            return (jnp.where(done, ub, t), jnp.where(done, c_ub, c_t), t, c_t,
                    jnp.sum(jnp.where(done, 0, 1)), it + 1)

        _, c_gt, t, c_t, _, _ = lax.while_loop(descend_cond, descend_body,
                                               (ub, c_ub, lo0, zeros, jnp.int32(1), jnp.int32(0)))
        ties_wanted = jnp.where(hit, take_all, kk - c_gt)
        extra_ties = jnp.sum(jnp.where(~hit & (c_t - c_gt > ties_wanted), 1, 0))
        return jnp.where(hit, lo_f, t), ties_wanted, extra_ties

    thr, ties_wanted, extra_ties = lax.cond(pending > 0, exact_path, lambda: (lo_f, take_all, jnp.int32(0)))

    def mask_body(jp, carry):
        m_scr[jp] = jnp.where(st_scr[jp] >= thr, 0.0, NEG_BIG).T
        return carry

    def mask_with_ties():
        ii = lax.broadcasted_iota(jnp.int32, (KB, KB), 0)
        jj = lax.broadcasted_iota(jnp.int32, (KB, KB), 1)
        upto = (jj <= ii).astype(BF16)

        def body(jp, wanted):
            s = st_scr[jp]
            tied = s == thr
            rank = jnp.dot(upto, jnp.where(tied, 1.0, 0.0).astype(BF16), preferred_element_type=F32)
            m_scr[jp] = jnp.where((s > thr) | (tied & (rank <= wanted)), 0.0, NEG_BIG).T
            return wanted - rank[KB - 1:KB, :]

        lax.fori_loop(0, nkp, body, ties_wanted)
        return jnp.int32(0)

    lax.cond(extra_ties > 0, mask_with_ties, lambda: lax.fori_loop(0, nkp, mask_body, jnp.int32(0)))

    scale = HEAD_DIM ** -0.5
    rep = N_HEADS // N_KV_HEADS
    for h in range(N_HEADS):
        qs_scr[h // rep, (h % rep) * r:(h % rep + 1) * r, :] = qn_ref[:, h * HEAD_DIM:(h + 1) * HEAD_DIM]
    mrun_scr[...] = jnp.full(mrun_scr.shape, -jnp.inf, F32)
    lrun_scr[...] = jnp.zeros(lrun_scr.shape, F32)
    acc_scr[...] = jnp.zeros(acc_scr.shape, F32)

    def logit_body(jp, carry):
        u0 = jnp.clip(2 * jp - qb + 2, 0, N_SHIFT_TILES - 1)
        u1 = jnp.clip(2 * jp + 1 - qb + 2, 0, N_SHIFT_TILES - 1)
        if gap:
            u0 = jnp.where((jp == 0) & (qb == 1), LEAD_TILE, u0)
        madd = m_scr[jp]
        for g in range(N_KV_HEADS):
            lt = lax.dot_general(qs_scr[g], k_scr[pl.ds(pl.multiple_of(jp * KB, KB), KB),
                                                  g * HEAD_DIM:(g + 1) * HEAD_DIM], nt,
                                 preferred_element_type=F32)
            for e in range(rep):
                h = g * rep + e
                bias = jnp.concatenate([bt_ref[h * N_BIAS_TILES + u0], bt_ref[h * N_BIAS_TILES + u1]], axis=1)
                lg = lt[e * r:(e + 1) * r, :] * scale + bias + madd
                lg_scr[h, jp] = lg
                mrun_scr[h] = jnp.maximum(mrun_scr[h], jnp.maximum(lg[:, :BLK], lg[:, BLK:]))
        return carry

    for_key_steps(logit_body, 0)
    for h in range(N_HEADS):
        mrun_scr[h] = jnp.broadcast_to(jnp.max(mrun_scr[h], axis=1, keepdims=True), (r, BLK))

    def pv_body(jp, carry):
        for g in range(N_KV_HEADS):
            es = []
            for e in range(rep):
                h = g * rep + e
                mrow = mrun_scr[h]
                ex = jnp.exp(lg_scr[h, jp] - jnp.concatenate([mrow, mrow], axis=1))
                lrun_scr[h] = lrun_scr[h] + (ex[:, :BLK] + ex[:, BLK:])
                es.append(ex.astype(BF16))
            acc_scr[g] = acc_scr[g] + jnp.dot(jnp.concatenate(es, axis=0),
                                              v_scr[pl.ds(pl.multiple_of(jp * KB, KB), KB),
                                                    g * HEAD_DIM:(g + 1) * HEAD_DIM],
                                              preferred_element_type=F32)
        return carry

    for_key_steps(pv_body, 0)
    for h in range(N_HEADS):
        g, e = h // rep, h % rep
        den = jnp.sum(lrun_scr[h], axis=1, keepdims=True)
        o_ref[:, h * HEAD_DIM:(h + 1) * HEAD_DIM] = (acc_scr[g, e * r:(e + 1) * r, :] / den).astype(BF16)


def _attn(qn, qi, sm, ki_new, k_new, v_new, past, lead, bias_tiles, *, n_lead, chunk_off, t_x, n_sel):
    b, tq, _ = qn.shape
    n_past = past[0].shape[1] if past is not None else 0
    assert n_past % KB == 0 and not (n_past and n_lead)
    n_prefix = n_past + (BLK if n_lead else 0)
    n_keys = -(-(n_prefix + tq) // KB) * KB
    nkp_total = n_keys // KB
    rep = N_HEADS // N_KV_HEADS
    seq = lambda width: pl.BlockSpec((None, BLK, width), lambda bi, i: (bi, i, 0))
    rows = lambda n, width: pl.BlockSpec((None, n, width), lambda bi, i: (bi, 0, 0))
    kern = functools.partial(_attn_kernel, nkp_total=nkp_total, n_past=n_past, n_lead=n_lead, chunk_off=chunk_off,
                             t_x=t_x, n_sel=n_sel)
    once = lambda n, width: pl.BlockSpec((None, n, width), lambda bi, i: (bi, 0, 0), pipeline_mode=pl.Buffered(1))
    past_specs = [once(n_past, IDX_DIM), once(n_past * N_KV_HEADS, HEAD_DIM),
                  once(n_past * N_KV_HEADS, HEAD_DIM)] if n_past else []
    lead_specs = [_const_spec((BLK, LANES)), _const_spec((BLK, KV_DIM)), _const_spec((BLK, KV_DIM))] if n_lead else []
    return pl.pallas_call(
        kern,
        grid=(b, tq // BLK),
        in_specs=past_specs + lead_specs + [seq(ATT_INNER), seq(IDX_HEADS * IDX_DIM), seq(LANES),
                                            rows(tq, LANES), rows(tq, KV_DIM), rows(tq, KV_DIM),
                                            _const_spec((N_HEADS * N_BIAS_TILES, BLK, BLK))],
        out_specs=seq(ATT_INNER),
        out_shape=jax.ShapeDtypeStruct((b, tq, ATT_INNER), BF16),
        scratch_shapes=[pltpu.VMEM((n_keys, IDX_DIM), BF16),
                        pltpu.VMEM((n_keys, KV_DIM), BF16),
                        pltpu.VMEM((n_keys, KV_DIM), BF16),
                        pltpu.VMEM((nkp_total, KB, BLK), F32),
                        pltpu.VMEM((nkp_total, BLK, KB), F32),
                        pltpu.VMEM((N_HEADS, nkp_total, BLK, KB), F32),
                        pltpu.VMEM((IDX_HEADS * BLK, IDX_DIM), BF16),
                        pltpu.VMEM((N_KV_HEADS, rep * BLK, HEAD_DIM), BF16),
                        pltpu.VMEM((N_HEADS, BLK, BLK), F32),
                        pltpu.VMEM((N_HEADS, BLK, BLK), F32),
                        pltpu.VMEM((N_KV_HEADS, rep * BLK, HEAD_DIM), F32)],
        compiler_params=_cparams(("parallel", "arbitrary")),
        name="attn",
    )(*(past or ()), *(lead or ()), qn, qi, sm, ki_new, k_new, v_new, bias_tiles)


def _out_ffn_kernel(x_ref, ys_ref, ya_ref, gs_ref, ga_ref, wbs_ref, wba_ref, wo_ref, n2_ref, wg_ref, wu_ref, wd_ref,
                    y_ref):
    dot = functools.partial(jnp.dot, preferred_element_type=F32)
    merged = (jax.nn.sigmoid(gs_ref[...]) * dot(ys_ref[...], wbs_ref[...])
              + jax.nn.sigmoid(ga_ref[...]) * dot(ya_ref[...], wba_ref[...]))
    h = x_ref[...] + dot(merged.astype(BF16), wo_ref[...])
    hn = _rms(h, n2_ref[...]).astype(BF16)
    act = (_silu(dot(hn, wg_ref[...])) * dot(hn, wu_ref[...])).astype(BF16)
    y_ref[...] = h + dot(act, wd_ref[...])


def _out_ffn(x2d, ys, ya, gs, ga, p):
    n = x2d.shape[0]
    tm = _row_tile(n)
    d_ff = p["w_gate"].shape[1]
    row = lambda width: pl.BlockSpec((tm, width), lambda i: (i, 0))
    wspec = lambda shape: pl.BlockSpec(shape, lambda i: (0, 0), pipeline_mode=pl.Buffered(1))
    return pl.pallas_call(
        _out_ffn_kernel,
        grid=(n // tm,),
        in_specs=[row(D_MODEL), row(SSD_INNER), row(ATT_INNER), row(D_MODEL), row(D_MODEL),
                  wspec((SSD_INNER, D_MODEL)), wspec((ATT_INNER, D_MODEL)), wspec((D_MODEL, D_MODEL)),
                  _const_spec((1, D_MODEL)), wspec((D_MODEL, d_ff)), wspec((D_MODEL, d_ff)), wspec((d_ff, D_MODEL))],
        out_specs=row(D_MODEL),
        out_shape=jax.ShapeDtypeStruct((n, D_MODEL), F32),
        compiler_params=_cparams(("parallel",)),
        name="out_ffn",
    )(x2d, ys, ya, gs, ga, p["w_br_ssd"], p["w_br_att"], p["w_out"], p["norm2_w"], p["w_gate"], p["w_up"],
      p["w_down"])


def _layer(x, lead_rows, conv_prev, ssm_prev, past, p, bias_tiles, *, chunk_off, n_sel):
    b, t, _ = x.shape
    n_lead = lead_rows.shape[0]
    tp = -(-t // BLK) * BLK
    x2d = jnp.pad(x, ((0, 0), (0, tp - t), (0, 0))).reshape(b * tp, D_MODEL)
    z, xbc, qn, k32, v32, kb, vb, qi, sm, smb, gs, ga = _in_proj(x2d, p)
    seq = lambda a: a.reshape(b, tp, a.shape[-1])
    ssd_lead = attn_lead = None
    if n_lead:
        lead = _in_proj(jnp.pad(lead_rows.astype(x.dtype), ((0, BLK - n_lead), (0, 0))), p)
        lz, lxbc, _, lk32, lv32, lkb, lvb, _, lsm, lsmb, _, _ = lead
        ssd_lead, attn_lead = (lxbc, lz, lsm), (lsmb, lkb, lvb)

    conv_prev8 = jnp.pad(conv_prev.astype(F32), ((0, 0), (HALO - (CONV_WIDTH - 1), 0), (0, 0)))
    gw = SSD_HEADS // SSD_GROUPS * SSD_HEAD_DIM
    y_ssd, ssm_new, conv_new8 = _ssd(ssd_lead, seq(xbc), seq(z), seq(sm), conv_prev8,
                                     ssm_prev.astype(F32).reshape(b, SSD_GROUPS, gw, SSD_STATE), p,
                                     n_lead=n_lead, t_x=t)

    if past is not None:
        pk, pv, pki = past
        n_past = pk.shape[1]
        past = (pki.astype(F32), pk.astype(F32).reshape(b, n_past * N_KV_HEADS, HEAD_DIM),
                pv.astype(F32).reshape(b, n_past * N_KV_HEADS, HEAD_DIM))
    y_att = _attn(seq(qn), seq(qi), seq(sm), seq(smb), seq(kb), seq(vb), past, attn_lead, bias_tiles,
                  n_lead=n_lead, chunk_off=chunk_off, t_x=t, n_sel=n_sel)

    y = _out_ffn(x2d, y_ssd.reshape(b * tp, SSD_INNER), y_att.reshape(b * tp, ATT_INNER), gs, ga, p)

    def with_lead(new, lead_part):
        if not n_lead:
            return new
        return jnp.concatenate([jnp.broadcast_to(lead_part[None, :n_lead], (b, n_lead) + new.shape[2:]), new], axis=1)

    heads = lambda a, rows: a.reshape(-1, rows, N_KV_HEADS, HEAD_DIM)
    k_new = with_lead(heads(k32, tp)[:, :t], heads(lk32, BLK)[0] if n_lead else None)
    v_new = with_lead(heads(v32, tp)[:, :t], heads(lv32, BLK)[0] if n_lead else None)
    ki_new = with_lead(seq(sm)[:, :t, SM_KI:SM_KI + IDX_DIM], lsm[:, SM_KI:SM_KI + IDX_DIM] if n_lead else None)
    ssm_new = ssm_new.reshape(b, SSD_HEADS, SSD_HEAD_DIM, SSD_STATE)
    conv_new = conv_new8[:, HALO - (CONV_WIDTH - 1):]
    return y.reshape(b, tp, D_MODEL)[:, :t], k_new, v_new, ki_new, ssm_new, conv_new


def _prepare_params(l, norm1_w, w_in, conv_w, conv_b, dt_bias, a_log, d_skip, ssd_norm_w, q_norm_w, k_norm_w,
                    idx_k_norm_w, w_br_ssd, w_br_att, w_out, norm2_w, w_gate, w_up, w_down):
    offs = [0]
    for w in IN_WIDTHS:
        offs.append(offs[-1] + w)
    seg = lambda i: w_in[l][:, offs[i]:offs[i + 1]]
    i_z, i_xbc, i_dt, i_q, i_k, i_v, i_qi, i_ki, i_wi, i_gs, i_ga = range(11)
    pad = jnp.zeros((D_MODEL, LANES - IDX_DIM - SSD_HEADS - IDX_HEADS), w_in.dtype)
    w_perm = jnp.concatenate([seg(i_z), seg(i_xbc), seg(i_q), seg(i_k), seg(i_v), seg(i_qi), seg(i_gs), seg(i_ga),
                              seg(i_ki), seg(i_dt), seg(i_wi), pad], axis=1).astype(BF16)

    def lanes_at(vec, start):
        return jnp.zeros((1, LANES), F32).at[0, start:start + vec.shape[0]].set(vec.astype(F32))

    dtb = lanes_at(dt_bias[l], SM_DT)
    alog = lanes_at(a_log[l], SM_DT)
    head_of_channel = jnp.arange(SSD_INNER) // SSD_HEAD_DIM
    expand = (jnp.arange(LANES)[:, None] == head_of_channel[None, :] + SM_DT).astype(BF16)
    row = lambda v: v.astype(F32).reshape(1, -1)
    return dict(
        norm1_w=row(norm1_w[l]), w_in=w_perm, conv_w=conv_w[l].astype(F32), conv_b=row(conv_b[l]),
        dtb=dtb, dtb_t=dtb.reshape(LANES, 1), alog=alog, alog_t=alog.reshape(LANES, 1),
        dskip_x=row(jnp.repeat(d_skip[l], SSD_HEAD_DIM)), ssd_norm_w=row(ssd_norm_w[l]), expand=expand,
        q_norm_w=row(q_norm_w[l]), k_norm_w=row(k_norm_w[l]),
        idx_k_norm_w=jnp.ones((1, LANES), F32).at[0, SM_KI:SM_KI + IDX_DIM].set(idx_k_norm_w[l].astype(F32)),
        w_br_ssd=w_br_ssd[l].astype(BF16), w_br_att=w_br_att[l].astype(BF16), w_out=w_out[l].astype(BF16),
        norm2_w=row(norm2_w[l]), w_gate=w_gate[l].astype(BF16), w_up=w_up[l].astype(BF16),
        w_down=w_down[l].astype(BF16))


def kernel(x_prompt, x_sample, cache_k, cache_v, cache_kidx, state_ssm, state_conv, meta_tokens, rel_bias, norm1_w,
           w_in, conv_w, conv_b, dt_bias, a_log, d_skip, ssd_norm_w, q_norm_w, k_norm_w, idx_k_norm_w, w_br_ssd,
           w_br_att, w_out, norm2_w, w_gate, w_up, w_down):
    bp, sp, _ = x_prompt.shape
    bs, ts, _ = x_sample.shape
    past = cache_k.shape[2]
    assert w_in.shape[0] == 1
    assert past % BLK == 0 and BLK % CHUNK == 0 and N_META <= CHUNK
    l = 0

    n_sel_p = min(TOPK_MAX, sp // 4)
    n_sel_s = min(TOPK_MAX, (past + ts) // 4)
    conv0 = jnp.zeros((bp, CONV_WIDTH - 1, CONV_DIM), F32)
    ssm0 = jnp.zeros((bp, SSD_HEADS, SSD_HEAD_DIM, SSD_STATE), F32)
    bias_tiles = _bias_tiles(rel_bias.astype(F32))
    p = _prepare_params(l, norm1_w, w_in, conv_w, conv_b, dt_bias, a_log, d_skip, ssd_norm_w, q_norm_w, k_norm_w,
                        idx_k_norm_w, w_br_ssd, w_br_att, w_out, norm2_w, w_gate, w_up, w_down)
    y_prompt, *rest_p = _layer(x_prompt, meta_tokens, conv0, ssm0, None, p, bias_tiles,
                               chunk_off=CHUNK - N_META, n_sel=n_sel_p)
    y_sample, *rest_s = _layer(x_sample, meta_tokens[:0], state_conv[l], state_ssm[l],
                               (cache_k[l], cache_v[l], cache_kidx[l]), p, bias_tiles,
                               chunk_off=0, n_sel=n_sel_s)

    dtypes = (x_prompt.dtype, x_prompt.dtype, x_prompt.dtype, state_ssm.dtype, x_prompt.dtype)
    return (y_prompt, y_sample, *(o[None].astype(dt) for o, dt in zip(rest_p, dtypes)),
            *(o[None].astype(dt) for o, dt in zip(rest_s, dtypes)))
```

```python
import functools

import jax
import jax.numpy as jnp
from jax import lax
from jax.experimental import pallas as pl
from jax.experimental.pallas import tpu as pltpu

F32 = jnp.float32
BF16 = jnp.bfloat16

D_MODEL = 1024
CHUNK = 64
N_META = 16
SSD_HEADS = 16
SSD_HEAD_DIM = 64
SSD_INNER = SSD_HEADS * SSD_HEAD_DIM
SSD_GROUPS = 4
SSD_STATE = 128
CONV_WIDTH = 4
CONV_DIM = SSD_INNER + 2 * SSD_GROUPS * SSD_STATE
N_HEADS = 8
N_KV_HEADS = 2
HEAD_DIM = 128
ATT_INNER = N_HEADS * HEAD_DIM
KV_DIM = N_KV_HEADS * HEAD_DIM
IDX_HEADS = 8
IDX_DIM = 64
TOPK_MAX = 256
REL_BUCKETS = 32
REL_MAX_DIST = 128
IN_WIDTHS = (SSD_INNER, CONV_DIM, SSD_HEADS, ATT_INNER, KV_DIM, KV_DIM, IDX_HEADS * IDX_DIM, IDX_DIM, IDX_HEADS,
             D_MODEL, D_MODEL)
EPS = 1e-6

LANES = 128
SUBLANES = 8
VMEM_LIMIT_BYTES = 56 * 1024 * 1024

BLK = LANES
ROW_TILE = 256
HALO = SUBLANES

C_Z = 0
C_XBC = C_Z + SSD_INNER
C_Q = C_XBC + CONV_DIM
C_K = C_Q + ATT_INNER
C_V = C_K + KV_DIM
C_QI = C_V + KV_DIM
C_GS = C_QI + IDX_HEADS * IDX_DIM
C_GA = C_GS + D_MODEL
C_SM = C_GA + D_MODEL
IN_PAD = C_SM + LANES
SM_KI = 0
SM_DT = SM_KI + IDX_DIM
SM_WI = SM_DT + SSD_HEADS

SEARCH_STEPS = 16
SEARCH_CLAMP = 1.0 / 16
NEG_BIG = -1e30


def _cparams(sem):
    return pltpu.CompilerParams(dimension_semantics=sem, vmem_limit_bytes=VMEM_LIMIT_BYTES)


def _const_spec(shape):
    nd = len(shape)
    return pl.BlockSpec(shape, lambda *_: (0,) * nd)


def _rms(x, w):
    return x * lax.rsqrt(jnp.mean(x * x, axis=-1, keepdims=True) + EPS) * w


def _silu(x):
    return x * jax.nn.sigmoid(x)


def _in_proj_kernel(x_ref, hist_ref, n1_ref, w_ref, qn_ref, kn_ref, kin_ref, cw_ref, cb_ref,
                    z_ref, xc_ref, q_ref, k_ref, v_ref, kb_ref, vb_ref, qi_ref, sm_ref, smb_ref, gs_ref, ga_ref, tail_ref,
                    xpad_scr, *, tiles_per_seq, t_x):
    j = pl.program_id(1)
    hn = _rms(x_ref[...], n1_ref[...]).astype(BF16)
    tm = x_ref.shape[0]

    def mm(lo, hi):
        return jnp.dot(hn, w_ref[:, lo:hi], preferred_element_type=F32)

    @pl.when(j == 0)
    def _():
        xpad_scr[0:HALO, :] = hist_ref[...]

    step = 2 * LANES
    for c0 in range(0, CONV_DIM, step):
        xpad_scr[HALO:HALO + tm, c0:c0 + step] = mm(C_XBC + c0, C_XBC + c0 + step)

    conv_slabs = iter(range(0, CONV_DIM, LANES))

    def conv_next():
        c0 = next(conv_slabs, None)
        if c0 is None:
            return
        sl = slice(c0, c0 + LANES)
        acc = xpad_scr[HALO - 3:HALO - 3 + tm, sl] * cw_ref[0:1, sl]
        for i in range(1, CONV_WIDTH):
            acc = acc + xpad_scr[HALO - 3 + i:HALO - 3 + i + tm, sl] * cw_ref[i:i + 1, sl]
        xc_ref[:, sl] = _silu(cb_ref[:, sl] + acc)

    def project(lo, hi, out_ref, dtype=F32):
        for c0 in range(0, hi - lo, step):
            out_ref[:, c0:c0 + step] = mm(lo + c0, lo + c0 + step).astype(dtype)
            conv_next()

    project(C_Z, C_XBC, z_ref)
    for c0 in range(0, ATT_INNER, step):
        q = mm(C_Q + c0, C_Q + c0 + step)
        conv_next()
        for h in range(step // HEAD_DIM):
            sl = slice(h * HEAD_DIM, (h + 1) * HEAD_DIM)
            q_ref[:, c0 + h * HEAD_DIM:c0 + (h + 1) * HEAD_DIM] = _rms(q[:, sl], qn_ref[...]).astype(BF16)
    k = mm(C_K, C_V)
    conv_next()
    v = mm(C_V, C_QI)
    conv_next()
    for h in range(N_KV_HEADS):
        sl = slice(h * HEAD_DIM, (h + 1) * HEAD_DIM)
        kh = _rms(k[:, sl], kn_ref[...])
        k_ref[pl.ds(h, tm, stride=N_KV_HEADS), :] = kh
        kb_ref[:, sl] = kh.astype(BF16)
        v_ref[pl.ds(h, tm, stride=N_KV_HEADS), :] = v[:, sl]
    vb_ref[...] = v.astype(BF16)
    project(C_QI, C_GS, qi_ref, BF16)
    project(C_GS, C_GA, gs_ref)
    project(C_GA, C_SM, ga_ref)
    sm = mm(C_SM, IN_PAD)
    assert next(conv_slabs, None) is None
    lane = lax.broadcasted_iota(jnp.int32, sm.shape, 1)
    is_ki = lane < SM_KI + IDX_DIM
    ms = jnp.sum(jnp.where(is_ki, sm * sm, 0.0), axis=-1, keepdims=True) * (1.0 / IDX_DIM)
    ki = sm * lax.rsqrt(ms + EPS) * kin_ref[...]
    is_wi = (lane >= SM_WI) & (lane < SM_WI + IDX_HEADS)
    out = jnp.where(is_ki, ki, jnp.where(is_wi, sm * (IDX_HEADS ** -0.5), sm))
    sm_ref[...] = out
    smb_ref[...] = out.astype(BF16)

    n_last = t_x - (tiles_per_seq - 1) * tm

    @pl.when(j == tiles_per_seq - 1)
    def _():
        tail_ref[...] = xpad_scr[n_last:n_last + HALO, :]

    xpad_scr[0:HALO, :] = xpad_scr[tm:tm + HALO, :]


def _row_tile(n):
    return ROW_TILE if n % ROW_TILE == 0 else BLK


def _in_proj(x2d, hist, p, *, n_seq, t_x):
    n = x2d.shape[0]
    tp = n // n_seq
    tm = _row_tile(tp)
    tiles = tp // tm
    shared_hist = hist.shape[0] == 1
    row = lambda width, per_token=1: pl.BlockSpec((tm * per_token, width), lambda s, j: (s * tiles + j, 0))
    per_seq = pl.BlockSpec((None, HALO, CONV_DIM), lambda s, j: (0 if shared_hist else s, 0, 0))
    outs = [
        (SSD_INNER, F32, 1), (CONV_DIM, F32, 1), (ATT_INNER, BF16, 1), (HEAD_DIM, F32, N_KV_HEADS),
        (HEAD_DIM, F32, N_KV_HEADS), (KV_DIM, BF16, 1), (KV_DIM, BF16, 1), (IDX_HEADS * IDX_DIM, BF16, 1),
        (LANES, F32, 1), (LANES, BF16, 1), (D_MODEL, F32, 1), (D_MODEL, F32, 1),
    ]
    kern = functools.partial(_in_proj_kernel, tiles_per_seq=tiles, t_x=t_x)
    return pl.pallas_call(
        kern,
        grid=(n_seq, tiles),
        in_specs=[row(D_MODEL), per_seq, _const_spec((1, D_MODEL)),
                  pl.BlockSpec((D_MODEL, IN_PAD), lambda s, j: (0, 0), pipeline_mode=pl.Buffered(1)),
                  _const_spec((1, HEAD_DIM)), _const_spec((1, HEAD_DIM)), _const_spec((1, LANES)),
                  _const_spec((CONV_WIDTH, CONV_DIM)), _const_spec((1, CONV_DIM))],
        out_specs=[row(w, per) for w, _, per in outs]
        + [pl.BlockSpec((None, HALO, CONV_DIM), lambda s, j: (s, 0, 0))],
        out_shape=[jax.ShapeDtypeStruct((n * per, w), dt) for w, dt, per in outs]
        + [jax.ShapeDtypeStruct((n_seq, HALO, CONV_DIM), F32)],
        scratch_shapes=[pltpu.VMEM((tm + 2 * HALO, CONV_DIM), F32)],
        compiler_params=_cparams(("parallel", "arbitrary")),
        name="in_proj",
    )(x2d, hist, p["norm1_w"], p["w_in"], p["q_norm_w"], p["k_norm_w"], p["idx_k_norm_w"], p["conv_w"], p["conv_b"])


def _softplus(x):
    return jnp.maximum(x, 0.0) + jnp.log1p(jnp.exp(-jnp.abs(x)))


def _split3(x):
    hi = x.astype(BF16)
    r1 = x - hi.astype(F32)
    mid = r1.astype(BF16)
    lo = (r1 - mid.astype(F32)).astype(BF16)
    return hi, mid, lo


def _ssd_kernel(*refs, n_chunks, n_lead, t_x):
    if n_lead:
        xbc_lead_ref, z_lead_ref, sm_lead_ref, *refs = refs
    (xbc_ref, z_ref, sm_ref, sprev_ref, dtb_ref, dtbt_ref, alog_ref, alogt_ref,
     dsk_ref, nw_ref, exp_ref, y_ref, snew_ref, s_scr, xc_scr, y_scr) = refs
    c = pl.program_id(1)
    q = BLK
    gw = SSD_HEADS // SSD_GROUPS * SSD_HEAD_DIM
    lead_chunks = 1 if n_lead else 0
    n_valid = jnp.minimum(q, t_x - (c - lead_chunks) * q)
    if n_lead:
        n_valid = jnp.where(c == 0, n_lead, n_valid)

    @pl.when(c == 0)
    def _():
        for g in range(SSD_GROUPS):
            s_scr[g] = sprev_ref[g].T

    if n_lead:
        @pl.when(c == 0)
        def _():
            xc_scr[...] = xbc_lead_ref[...]

        @pl.when(c > 0)
        def _():
            xc_scr[...] = xbc_ref[...]

        sm = jnp.where(c == 0, sm_lead_ref[...], sm_ref[...])
        z = jnp.where(c == 0, z_lead_ref[...], z_ref[...])
    else:
        xc_scr[...] = xbc_ref[...]
        sm = sm_ref[...]
        z = z_ref[...]

    smt = sm.T
    lane = lax.broadcasted_iota(jnp.int32, (q, LANES), 1)
    row = lax.broadcasted_iota(jnp.int32, (q, LANES), 0)
    is_dt = (lane >= SM_DT) & (lane < SM_DT + SSD_HEADS) & (row < n_valid)
    is_dt_t = (row >= SM_DT) & (row < SM_DT + SSD_HEADS) & (lane < n_valid)
    dt = jnp.where(is_dt, _softplus(sm + dtb_ref[...]), 0.0)
    dtt = jnp.where(is_dt_t, _softplus(smt + dtbt_ref[...]), 0.0)
    da = dt * (-jnp.exp(alog_ref[...]))
    dat = dtt * (-jnp.exp(alogt_ref[...]))
    ii = lax.broadcasted_iota(jnp.int32, (q, q), 0)
    jj = lax.broadcasted_iota(jnp.int32, (q, q), 1)
    causal = jj <= ii
    acum = sum(jnp.dot(causal.astype(BF16), p, preferred_element_type=F32) for p in _split3(da))
    acumt = sum(jnp.dot(p, (ii <= jj).astype(BF16), preferred_element_type=F32) for p in _split3(dat))
    a_last = acum[q - 1:q, :]
    expand = exp_ref[...]
    stacked = jnp.concatenate([jnp.exp(acum), jnp.exp(a_last - acum) * dt,
                               jnp.broadcast_to(jnp.exp(a_last), (SUBLANES, LANES))], axis=0)
    stacked_x = sum(jnp.dot(p, expand, preferred_element_type=F32) for p in _split3(stacked))
    ea_x = stacked_x[0:q]
    wdt_x = stacked_x[q:2 * q]
    dec_x = stacked_x[2 * q:2 * q + 1]

    for g in range(SSD_GROUPS):
        gsl = slice(g * gw, (g + 1) * gw)
        bsl = slice(SSD_INNER + g * SSD_STATE, SSD_INNER + (g + 1) * SSD_STATE)
        csl = slice(SSD_INNER + SSD_GROUPS * SSD_STATE + g * SSD_STATE,
                    SSD_INNER + SSD_GROUPS * SSD_STATE + (g + 1) * SSD_STATE)
        bmf = xc_scr[:, bsl]
        bm = bmf.astype(BF16)
        cm = xc_scr[:, csl].astype(BF16)
        xg = xc_scr[:, gsl]
        xgb = xg.astype(BF16)
        cbm = lax.dot_general(cm, bm, (((1,), (1,)), ((), ())), preferred_element_type=F32)
        xw = (xg * wdt_x[:, gsl]).astype(BF16)
        st = jnp.dot(bmf.T.astype(BF16), xw, preferred_element_type=F32)
        s_in = s_scr[g]
        y_off = jnp.dot(cm, s_in.astype(BF16), preferred_element_type=F32) * ea_x[:, gsl]
        s_scr[g] = s_in * dec_x[:, gsl] + st
        for r in range(SSD_HEADS // SSD_GROUPS):
            h = g * (SSD_HEADS // SSD_GROUPS) + r
            seg = acum[:, SM_DT + h:SM_DT + h + 1] - acumt[SM_DT + h:SM_DT + h + 1, :]
            lmat = jnp.exp(jnp.where(causal, seg, -jnp.inf))
            wmat = (cbm * lmat * dtt[SM_DT + h:SM_DT + h + 1, :]).astype(BF16)
            rsl = slice(r * SSD_HEAD_DIM, (r + 1) * SSD_HEAD_DIM)
            hsl = slice(h * SSD_HEAD_DIM, (h + 1) * SSD_HEAD_DIM)
            y_diag = jnp.dot(wmat, xgb[:, rsl], preferred_element_type=F32)
            y_scr[:, hsl] = y_diag + y_off[:, rsl] + xg[:, rsl] * dsk_ref[:, hsl]

    @pl.when(c == n_chunks - 1)
    def _():
        for g in range(SSD_GROUPS):
            snew_ref[g] = s_scr[g].T

    y = y_scr[...] * _silu(z)
    y_ref[...] = _rms(y, nw_ref[...]).astype(BF16)


def _ssd(lead, xbc, z, sm, ssm_prev, p, *, n_lead, t_x):
    b, tp, _ = xbc.shape
    lead_chunks = 1 if n_lead else 0
    assert n_lead % SUBLANES == 0 and n_lead <= BLK
    nc = tp // BLK + lead_chunks
    gw = SSD_HEADS // SSD_GROUPS * SSD_HEAD_DIM
    seq = lambda width: pl.BlockSpec((None, BLK, width), lambda i, c: (i, jnp.maximum(c - lead_chunks, 0), 0))
    kern = functools.partial(_ssd_kernel, n_chunks=nc, n_lead=n_lead, t_x=t_x)
    lead_specs = [_const_spec((BLK, CONV_DIM)), _const_spec((BLK, SSD_INNER)), _const_spec((BLK, LANES))]
    return pl.pallas_call(
        kern,
        grid=(b, nc),
        in_specs=(lead_specs if n_lead else []) + [
                  seq(CONV_DIM), seq(SSD_INNER), seq(LANES),
                  pl.BlockSpec((None, SSD_GROUPS, gw, SSD_STATE), lambda i, c: (i, 0, 0, 0)),
                  _const_spec((1, LANES)), _const_spec((LANES, 1)), _const_spec((1, LANES)), _const_spec((LANES, 1)),
                  _const_spec((1, SSD_INNER)), _const_spec((1, SSD_INNER)), _const_spec((LANES, SSD_INNER))],
        out_specs=[seq(SSD_INNER),
                   pl.BlockSpec((None, SSD_GROUPS, gw, SSD_STATE), lambda i, c: (i, 0, 0, 0))],
        out_shape=[jax.ShapeDtypeStruct((b, tp, SSD_INNER), BF16),
                   jax.ShapeDtypeStruct((b, SSD_GROUPS, gw, SSD_STATE), F32)],
        scratch_shapes=[pltpu.VMEM((SSD_GROUPS, SSD_STATE, gw), F32),
                        pltpu.VMEM((BLK, CONV_DIM), F32),
                        pltpu.VMEM((BLK, SSD_INNER), F32)],
        compiler_params=_cparams(("parallel", "arbitrary")),
        name="ssd",
    )(*(lead if n_lead else ()), xbc, z, sm, ssm_prev, p["dtb"], p["dtb_t"],
      p["alog"], p["alog_t"], p["dskip_x"], p["ssd_norm_w"], p["expand"])


N_SHIFT_TILES = 5
LEAD_TILE = N_SHIFT_TILES
N_BIAS_TILES = N_SHIFT_TILES + 1


def _log_bucket_starts():
    nb = REL_BUCKETS // 2
    max_exact = nb // 2
    s = nb - max_exact
    starts = []
    for m in range(1, s):
        n = max_exact
        while n ** s * max_exact ** m < max_exact ** s * REL_MAX_DIST ** m:
            n += 1
        starts.append(n)
    return starts


def _bias_kernel(rb_ref, bt_ref):
    nb = REL_BUCKETS // 2
    max_exact = nb // 2
    qq = lax.broadcasted_iota(jnp.int32, (BLK, BLK), 0)
    kk = lax.broadcasted_iota(jnp.int32, (BLK, BLK), 1)
    for u in range(N_BIAS_TILES):
        rel = kk - qq + ((u - 2) * BLK if u < N_SHIFT_TILES else -N_META)
        n = jnp.abs(rel)
        large = max_exact + sum(jnp.where(n >= start, 1, 0) for start in _log_bucket_starts())
        bucket = jnp.where(rel > 0, nb, 0) + jnp.where(n < max_exact, n, large)
        for h in range(N_HEADS):
            acc = jnp.zeros((BLK, BLK), F32)
            for bkt in range(REL_BUCKETS):
                acc = jnp.where(bucket == bkt, rb_ref[bkt, h], acc)
            bt_ref[h * N_BIAS_TILES + u] = acc


def _bias_tiles(rel_bias):
    return pl.pallas_call(
        _bias_kernel,
        in_specs=[pl.BlockSpec(memory_space=pltpu.SMEM)],
        out_specs=pl.BlockSpec(memory_space=pltpu.VMEM),
        out_shape=jax.ShapeDtypeStruct((N_HEADS * N_BIAS_TILES, BLK, BLK), F32),
        name="bias_tiles",
    )(rel_bias)


KB = 2 * BLK


def _attn_kernel(*refs, nkp_total, n_past, n_lead, chunk_off, t_x, n_sel):
    if n_past:
        pki_ref, pk_ref, pv_ref, *refs = refs
    if n_lead:
        lki_ref, lk_ref, lv_ref, *refs = refs
    (qn_ref, qi_ref, sm_ref, ki_new_ref, k_new_ref, v_new_ref, bt_ref, o_ref,
     ki_scr, k_scr, v_scr, st_scr, m_scr, lg_scr, qis_scr, qs_scr, mrun_scr, lrun_scr, acc_scr) = refs
    i = pl.program_id(1)
    r = BLK
    n_prefix = n_past + (BLK if n_lead else 0)
    gap = BLK - n_lead if n_lead else 0
    l_valid = n_prefix - gap + t_x

    @pl.when(i == 0)
    def _():
        if n_lead:
            ki_scr[0:BLK, :] = lki_ref[:, SM_KI:SM_KI + IDX_DIM]
            k_scr[0:BLK, :] = lk_ref[...]
            v_scr[0:BLK, :] = lv_ref[...]
        if n_past:
            def load_past(c, carry):
                rows = pl.ds(pl.multiple_of(c * KB, KB), KB)
                ki_scr[rows, :] = pki_ref[rows, :].astype(BF16)
                for g in range(N_KV_HEADS):
                    src = pl.ds(pl.multiple_of(c * KB * N_KV_HEADS, KB) + g, KB, stride=N_KV_HEADS)
                    k_scr[rows, g * HEAD_DIM:(g + 1) * HEAD_DIM] = pk_ref[src, :].astype(BF16)
                    v_scr[rows, g * HEAD_DIM:(g + 1) * HEAD_DIM] = pv_ref[src, :].astype(BF16)
                return carry

            lax.fori_loop(0, n_past // KB, load_past, 0)
        n_new = k_new_ref.shape[0]
        ki_scr[n_prefix:n_prefix + n_new, :] = ki_new_ref[:, SM_KI:SM_KI + IDX_DIM]
        k_scr[n_prefix:n_prefix + n_new, :] = k_new_ref[...]
        v_scr[n_prefix:n_prefix + n_new, :] = v_new_ref[...]
        n_tail = nkp_total * KB - n_prefix - n_new
        if n_tail:
            for scr in (ki_scr, k_scr, v_scr):
                scr[n_prefix + n_new:, :] = jnp.zeros((n_tail, scr.shape[1]), BF16)

    qb = n_prefix // BLK + i
    q0 = qb * BLK - gap
    n_keys = nkp_total * KB

    def chunk_end(pos):
        return jnp.minimum(CHUNK * ((pos + chunk_off) // CHUNK + 1) - chunk_off, l_valid)

    nkp = jnp.minimum(nkp_total, (chunk_end(q0 + BLK - 1) + gap + KB - 1) // KB)

    nt = (((1,), (1,)), ((), ()))
    wit = sm_ref[...].T
    n_adm = chunk_end(q0 + lax.broadcasted_iota(jnp.int32, (1, r), 1))
    krow = lax.broadcasted_iota(jnp.int32, (KB, r), 0)

    fold_rows = 8 * SUBLANES

    def fold(x, op):
        return op(x.reshape(KB // fold_rows, fold_rows, r), axis=0)

    def for_key_steps(body, init):
        carry = lax.fori_loop(0, nkp // 2, lambda t, c: body(2 * t + 1, body(2 * t, c)), init)
        return lax.cond(nkp % 2 == 1, lambda c: body(nkp - 1, c), lambda c: c, carry)

    for h in range(IDX_HEADS):
        qis_scr[h * r:(h + 1) * r, :] = qi_ref[:, h * IDX_DIM:(h + 1) * IDX_DIM]

    def score_body(jp, carry):
        mn, mx = carry
        kij = ki_scr[pl.ds(pl.multiple_of(jp * KB, KB), KB), :]
        acc = jnp.zeros((KB, r), F32)
        for hp in range(IDX_HEADS // 2):
            sh = lax.dot_general(kij, qis_scr[2 * hp * r:2 * (hp + 1) * r, :], nt, preferred_element_type=F32)
            for e in range(2):
                h = 2 * hp + e
                acc = acc + wit[SM_WI + h:SM_WI + h + 1, :] * jnp.maximum(sh[:, e * r:(e + 1) * r], 0.0)
        adm = krow < n_adm + gap - jp * KB
        if gap:
            adm = adm & ((jp > 0) | (krow < n_lead) | (krow >= BLK))
        s = jnp.where(adm, acc * (IDX_DIM ** -0.5), -jnp.inf)
        st_scr[jp] = s
        mn = jnp.minimum(mn, fold(jnp.where(adm, s, jnp.inf), jnp.min))
        mx = jnp.maximum(mx, fold(s, jnp.max))
        return mn, mx

    init = (jnp.full((fold_rows, r), jnp.inf, F32), jnp.full((fold_rows, r), -jnp.inf, F32))
    mn, mx = for_key_steps(score_body, init)
    lo0 = jnp.min(mn, axis=0, keepdims=True)
    hi0 = jnp.max(mx, axis=0, keepdims=True)
    kk = jnp.minimum(n_adm, n_sel).astype(F32)

    def count(pred):
        def body(jp, acc):
            return acc + fold(jnp.where(pred(st_scr[jp], jp), 1.0, 0.0), jnp.sum)
        acc = lax.fori_loop(0, nkp, body, jnp.zeros((fold_rows, r), F32))
        return jnp.sum(acc, axis=0, keepdims=True)

    def search_body(_, carry):
        lo, hi, c_lo, c_hi, ub, c_ub, hit = carry
        frac = jnp.clip((c_lo - kk) / (c_lo - c_hi), SEARCH_CLAMP, 1.0 - SEARCH_CLAMP)
        mid = jnp.where(hit > 0.0, lo, lo * (1.0 - frac) + hi * frac)
        cnt = count(lambda s, jp: s >= mid)
        ok = cnt >= kk
        now = cnt == kk
        lo = jnp.where(ok, mid, lo)
        hi = jnp.where(ok & ~now, hi, mid)
        c_lo = jnp.where(ok, cnt, c_lo)
        c_hi = jnp.where(now, kk - 1.0, jnp.where(ok, c_hi, cnt))
        ub = jnp.where(ok, ub, mid)
        c_ub = jnp.where(ok, c_ub, cnt)
        return lo, hi, c_lo, c_hi, ub, c_ub, jnp.where(now, 1.0, 0.0)

    all_adm = n_adm.astype(F32) == kk
    zeros = jnp.zeros((1, r), F32)
    init = (lo0, jnp.where(all_adm, lo0, hi0), n_adm.astype(F32), jnp.where(all_adm, kk - 1.0, zeros),
            jnp.full((1, r), jnp.inf, F32), zeros, jnp.where(all_adm, 1.0, 0.0))
    lo_f, _, _, _, ub, c_ub, hit_f = lax.fori_loop(0, SEARCH_STEPS, search_body, init)
    hit = hit_f > 0.0
    pending = jnp.sum(jnp.where(hit, 0, 1))
    take_all = jnp.full((1, r), float(n_keys), F32)

    def exact_path():
        def next_below(ub):
            def body(jp, acc):
                s = st_scr[jp]
                return jnp.maximum(acc, fold(jnp.where(s < ub, s, -jnp.inf), jnp.max))
            acc = lax.fori_loop(0, nkp, body, jnp.full((fold_rows, r), -jnp.inf, F32))
            return jnp.max(acc, axis=0, keepdims=True)

        def descend_cond(carry):
            *_, todo, it = carry
            return (todo > 0) & (it < n_keys)

        def descend_body(carry):
            ub, c_ub, _, _, _, it = carry
            t = next_below(ub)
            c_t = count(lambda s, jp: s >= t)
            done = hit | (c_t >= kk)
            return (jnp.where(done, ub, t), jnp.where(done, c_ub, c_t), t, c_t,
                    jnp.sum(jnp.where(done, 0, 1)), it + 1)

        _, c_gt, t, c_t, _, _ = lax.while_loop(descend_cond, descend_body,
                                               (ub, c_ub, lo0, zeros, jnp.int32(1), jnp.int32(0)))
        ties_wanted = jnp.where(hit, take_all, kk - c_gt)
        extra_ties = jnp.sum(jnp.where(~hit & (c_t - c_gt > ties_wanted), 1, 0))
        return jnp.where(hit, lo_f, t), ties_wanted, extra_ties

    thr, ties_wanted, extra_ties = lax.cond(pending > 0, exact_path, lambda: (lo_f, take_all, jnp.int32(0)))

    def mask_body(jp, carry):
        m_scr[jp] = jnp.where(st_scr[jp] >= thr, 0.0, NEG_BIG).T
        return carry

    def mask_with_ties():
        ii = lax.broadcasted_iota(jnp.int32, (KB, KB), 0)
        jj = lax.broadcasted_iota(jnp.int32, (KB, KB), 1)
        upto = (jj <= ii).astype(BF16)

        def body(jp, wanted):
            s = st_scr[jp]
            tied = s == thr
            rank = jnp.dot(upto, jnp.where(tied, 1.0, 0.0).astype(BF16), preferred_element_type=F32)
            m_scr[jp] = jnp.where((s > thr) | (tied & (rank <= wanted)), 0.0, NEG_BIG).T
            return wanted - rank[KB - 1:KB, :]

        lax.fori_loop(0, nkp, body, ties_wanted)
        return jnp.int32(0)

    lax.cond(extra_ties > 0, mask_with_ties, lambda: lax.fori_loop(0, nkp, mask_body, jnp.int32(0)))

    scale = HEAD_DIM ** -0.5
    rep = N_HEADS // N_KV_HEADS
    for h in range(N_HEADS):
        qs_scr[h // rep, (h % rep) * r:(h % rep + 1) * r, :] = qn_ref[:, h * HEAD_DIM:(h + 1) * HEAD_DIM]
    mrun_scr[...] = jnp.full(mrun_scr.shape, -jnp.inf, F32)
    lrun_scr[...] = jnp.zeros(lrun_scr.shape, F32)
    acc_scr[...] = jnp.zeros(acc_scr.shape, F32)

    def logit_body(jp, carry):
        u0 = jnp.clip(2 * jp - qb + 2, 0, N_SHIFT_TILES - 1)
        u1 = jnp.clip(2 * jp + 1 - qb + 2, 0, N_SHIFT_TILES - 1)
        if gap:
            u0 = jnp.where((jp == 0) & (qb == 1), LEAD_TILE, u0)
        madd = m_scr[jp]
        for g in range(N_KV_HEADS):
            lt = lax.dot_general(qs_scr[g], k_scr[pl.ds(pl.multiple_of(jp * KB, KB), KB),
                                                  g * HEAD_DIM:(g + 1) * HEAD_DIM], nt,
                                 preferred_element_type=F32)
            for e in range(rep):
                h = g * rep + e
                bias = jnp.concatenate([bt_ref[h * N_BIAS_TILES + u0], bt_ref[h * N_BIAS_TILES + u1]], axis=1)
                lg = lt[e * r:(e + 1) * r, :] * scale + bias + madd
                lg_scr[h, jp] = lg
                mrun_scr[h] = jnp.maximum(mrun_scr[h], jnp.maximum(lg[:, :BLK], lg[:, BLK:]))
        return carry

    for_key_steps(logit_body, 0)
    for h in range(N_HEADS):
        mrun_scr[h] = jnp.broadcast_to(jnp.max(mrun_scr[h], axis=1, keepdims=True), (r, BLK))

    def pv_body(jp, carry):
        for g in range(N_KV_HEADS):
            es = []
            for e in range(rep):
                h = g * rep + e
                mrow = mrun_scr[h]
                ex = jnp.exp(lg_scr[h, jp] - jnp.concatenate([mrow, mrow], axis=1))
                lrun_scr[h] = lrun_scr[h] + (ex[:, :BLK] + ex[:, BLK:])
                es.append(ex.astype(BF16))
            acc_scr[g] = acc_scr[g] + jnp.dot(jnp.concatenate(es, axis=0),
                                              v_scr[pl.ds(pl.multiple_of(jp * KB, KB), KB),
                                                    g * HEAD_DIM:(g + 1) * HEAD_DIM],
                                              preferred_element_type=F32)
        return carry

    for_key_steps(pv_body, 0)
    for h in range(N_HEADS):
        g, e = h // rep, h % rep
        den = jnp.sum(lrun_scr[h], axis=1, keepdims=True)
        o_ref[:, h * HEAD_DIM:(h + 1) * HEAD_DIM] = (acc_scr[g, e * r:(e + 1) * r, :] / den).astype(BF16)


def _attn(qn, qi, sm, ki_new, k_new, v_new, past, lead, bias_tiles, *, n_lead, chunk_off, t_x, n_sel):
    b, tq, _ = qn.shape
    n_past = past[0].shape[1] if past is not None else 0
    assert n_past % KB == 0 and not (n_past and n_lead)
    n_prefix = n_past + (BLK if n_lead else 0)
    n_keys = -(-(n_prefix + tq) // KB) * KB
    nkp_total = n_keys // KB
    rep = N_HEADS // N_KV_HEADS
    seq = lambda width: pl.BlockSpec((None, BLK, width), lambda bi, i: (bi, i, 0))
    rows = lambda n, width: pl.BlockSpec((None, n, width), lambda bi, i: (bi, 0, 0))
    kern = functools.partial(_attn_kernel, nkp_total=nkp_total, n_past=n_past, n_lead=n_lead, chunk_off=chunk_off,
                             t_x=t_x, n_sel=n_sel)
    past_specs = [rows(n_past, IDX_DIM), rows(n_past * N_KV_HEADS, HEAD_DIM),
                  rows(n_past * N_KV_HEADS, HEAD_DIM)] if n_past else []
    lead_specs = [_const_spec((BLK, LANES)), _const_spec((BLK, KV_DIM)), _const_spec((BLK, KV_DIM))] if n_lead else []
    return pl.pallas_call(
        kern,
        grid=(b, tq // BLK),
        in_specs=past_specs + lead_specs + [seq(ATT_INNER), seq(IDX_HEADS * IDX_DIM), seq(LANES),
                                            rows(tq, LANES), rows(tq, KV_DIM), rows(tq, KV_DIM),
                                            _const_spec((N_HEADS * N_BIAS_TILES, BLK, BLK))],
        out_specs=seq(ATT_INNER),
        out_shape=jax.ShapeDtypeStruct((b, tq, ATT_INNER), BF16),
        scratch_shapes=[pltpu.VMEM((n_keys, IDX_DIM), BF16),
                        pltpu.VMEM((n_keys, KV_DIM), BF16),
                        pltpu.VMEM((n_keys, KV_DIM), BF16),
                        pltpu.VMEM((nkp_total, KB, BLK), F32),
                        pltpu.VMEM((nkp_total, BLK, KB), F32),
                        pltpu.VMEM((N_HEADS, nkp_total, BLK, KB), F32),
                        pltpu.VMEM((IDX_HEADS * BLK, IDX_DIM), BF16),
                        pltpu.VMEM((N_KV_HEADS, rep * BLK, HEAD_DIM), BF16),
                        pltpu.VMEM((N_HEADS, BLK, BLK), F32),
                        pltpu.VMEM((N_HEADS, BLK, BLK), F32),
                        pltpu.VMEM((N_KV_HEADS, rep * BLK, HEAD_DIM), F32)],
        compiler_params=_cparams(("parallel", "arbitrary")),
        name="attn",
    )(*(past or ()), *(lead or ()), qn, qi, sm, ki_new, k_new, v_new, bias_tiles)


def _out_ffn_kernel(x_ref, ys_ref, ya_ref, gs_ref, ga_ref, wbs_ref, wba_ref, wo_ref, n2_ref, wg_ref, wu_ref, wd_ref,
                    y_ref):
    dot = functools.partial(jnp.dot, preferred_element_type=F32)
    merged = (jax.nn.sigmoid(gs_ref[...]) * dot(ys_ref[...], wbs_ref[...])
              + jax.nn.sigmoid(ga_ref[...]) * dot(ya_ref[...], wba_ref[...]))
    h = x_ref[...] + dot(merged.astype(BF16), wo_ref[...])
    hn = _rms(h, n2_ref[...]).astype(BF16)
    act = (_silu(dot(hn, wg_ref[...])) * dot(hn, wu_ref[...])).astype(BF16)
    y_ref[...] = h + dot(act, wd_ref[...])


def _out_ffn(x2d, ys, ya, gs, ga, p):
    n = x2d.shape[0]
    tm = _row_tile(n)
    d_ff = p["w_gate"].shape[1]
    row = lambda width: pl.BlockSpec((tm, width), lambda i: (i, 0))
    wspec = lambda shape: pl.BlockSpec(shape, lambda i: (0, 0), pipeline_mode=pl.Buffered(1))
    return pl.pallas_call(
        _out_ffn_kernel,
        grid=(n // tm,),
        in_specs=[row(D_MODEL), row(SSD_INNER), row(ATT_INNER), row(D_MODEL), row(D_MODEL),
                  wspec((SSD_INNER, D_MODEL)), wspec((ATT_INNER, D_MODEL)), wspec((D_MODEL, D_MODEL)),
                  _const_spec((1, D_MODEL)), wspec((D_MODEL, d_ff)), wspec((D_MODEL, d_ff)), wspec((d_ff, D_MODEL))],
        out_specs=row(D_MODEL),
        out_shape=jax.ShapeDtypeStruct((n, D_MODEL), F32),
        compiler_params=_cparams(("parallel",)),
        name="out_ffn",
    )(x2d, ys, ya, gs, ga, p["w_br_ssd"], p["w_br_att"], p["w_out"], p["norm2_w"], p["w_gate"], p["w_up"],
      p["w_down"])


def _layer(x, lead_rows, conv_prev, ssm_prev, past, p, bias_tiles, *, chunk_off, n_sel):
    b, t, _ = x.shape
    n_lead = lead_rows.shape[0]
    tp = -(-t // BLK) * BLK
    x2d = jnp.pad(x, ((0, 0), (0, tp - t), (0, 0))).reshape(b * tp, D_MODEL)
    hist = jnp.pad(conv_prev.astype(F32), ((0, 0), (HALO - (CONV_WIDTH - 1), 0), (0, 0)))
    seq = lambda a: a.reshape(b, tp, a.shape[-1])
    ssd_lead = attn_lead = None
    if n_lead:
        lead = _in_proj(jnp.pad(lead_rows.astype(x.dtype), ((0, BLK - n_lead), (0, 0))), hist[:1], p,
                        n_seq=1, t_x=n_lead)
        lz, lxbc, _, lk32, lv32, lkb, lvb, _, lsm, lsmb, _, _, hist = lead
        ssd_lead, attn_lead = (lxbc, lz, lsm), (lsmb, lkb, lvb)
    z, xbc, qn, k32, v32, kb, vb, qi, sm, smb, gs, ga, conv_new8 = _in_proj(x2d, hist, p, n_seq=b, t_x=t)

    gw = SSD_HEADS // SSD_GROUPS * SSD_HEAD_DIM
    y_ssd, ssm_new = _ssd(ssd_lead, seq(xbc), seq(z), seq(sm),
                          ssm_prev.astype(F32).reshape(b, SSD_GROUPS, gw, SSD_STATE), p, n_lead=n_lead, t_x=t)

    if past is not None:
        pk, pv, pki = past
        n_past = pk.shape[1]
        past = (pki.astype(F32), pk.astype(F32).reshape(b, n_past * N_KV_HEADS, HEAD_DIM),
                pv.astype(F32).reshape(b, n_past * N_KV_HEADS, HEAD_DIM))
    y_att = _attn(seq(qn), seq(qi), seq(sm), seq(smb), seq(kb), seq(vb), past, attn_lead, bias_tiles,
                  n_lead=n_lead, chunk_off=chunk_off, t_x=t, n_sel=n_sel)

    y = _out_ffn(x2d, y_ssd.reshape(b * tp, SSD_INNER), y_att.reshape(b * tp, ATT_INNER), gs, ga, p)

    def with_lead(new, lead_part):
        if not n_lead:
            return new
        return jnp.concatenate([jnp.broadcast_to(lead_part[None, :n_lead], (b, n_lead) + new.shape[2:]), new], axis=1)

    heads = lambda a, rows: a.reshape(-1, rows, N_KV_HEADS, HEAD_DIM)
    k_new = with_lead(heads(k32, tp)[:, :t], heads(lk32, BLK)[0] if n_lead else None)
    v_new = with_lead(heads(v32, tp)[:, :t], heads(lv32, BLK)[0] if n_lead else None)
    ki_new = with_lead(seq(sm)[:, :t, SM_KI:SM_KI + IDX_DIM], lsm[:, SM_KI:SM_KI + IDX_DIM] if n_lead else None)
    ssm_new = ssm_new.reshape(b, SSD_HEADS, SSD_HEAD_DIM, SSD_STATE)
    conv_new = conv_new8[:, HALO - (CONV_WIDTH - 1):]
    return y.reshape(b, tp, D_MODEL)[:, :t], k_new, v_new, ki_new, ssm_new, conv_new


def _prepare_params(l, norm1_w, w_in, conv_w, conv_b, dt_bias, a_log, d_skip, ssd_norm_w, q_norm_w, k_norm_w,
                    idx_k_norm_w, w_br_ssd, w_br_att, w_out, norm2_w, w_gate, w_up, w_down):
    offs = [0]
    for w in IN_WIDTHS:
        offs.append(offs[-1] + w)
    seg = lambda i: w_in[l][:, offs[i]:offs[i + 1]]
    i_z, i_xbc, i_dt, i_q, i_k, i_v, i_qi, i_ki, i_wi, i_gs, i_ga = range(11)
    pad = jnp.zeros((D_MODEL, LANES - IDX_DIM - SSD_HEADS - IDX_HEADS), w_in.dtype)
    w_perm = jnp.concatenate([seg(i_z), seg(i_xbc), seg(i_q), seg(i_k), seg(i_v), seg(i_qi), seg(i_gs), seg(i_ga),
                              seg(i_ki), seg(i_dt), seg(i_wi), pad], axis=1).astype(BF16)

    def lanes_at(vec, start):
        return jnp.zeros((1, LANES), F32).at[0, start:start + vec.shape[0]].set(vec.astype(F32))

    dtb = lanes_at(dt_bias[l], SM_DT)
    alog = lanes_at(a_log[l], SM_DT)
    head_of_channel = jnp.arange(SSD_INNER) // SSD_HEAD_DIM
    expand = (jnp.arange(LANES)[:, None] == head_of_channel[None, :] + SM_DT).astype(BF16)
    row = lambda v: v.astype(F32).reshape(1, -1)
    return dict(
        norm1_w=row(norm1_w[l]), w_in=w_perm, conv_w=conv_w[l].astype(F32), conv_b=row(conv_b[l]),
        dtb=dtb, dtb_t=dtb.reshape(LANES, 1), alog=alog, alog_t=alog.reshape(LANES, 1),
        dskip_x=row(jnp.repeat(d_skip[l], SSD_HEAD_DIM)), ssd_norm_w=row(ssd_norm_w[l]), expand=expand,
        q_norm_w=row(q_norm_w[l]), k_norm_w=row(k_norm_w[l]),
        idx_k_norm_w=jnp.ones((1, LANES), F32).at[0, SM_KI:SM_KI + IDX_DIM].set(idx_k_norm_w[l].astype(F32)),
        w_br_ssd=w_br_ssd[l].astype(BF16), w_br_att=w_br_att[l].astype(BF16), w_out=w_out[l].astype(BF16),
        norm2_w=row(norm2_w[l]), w_gate=w_gate[l].astype(BF16), w_up=w_up[l].astype(BF16),
        w_down=w_down[l].astype(BF16))


def kernel(x_prompt, x_sample, cache_k, cache_v, cache_kidx, state_ssm, state_conv, meta_tokens, rel_bias, norm1_w,
           w_in, conv_w, conv_b, dt_bias, a_log, d_skip, ssd_norm_w, q_norm_w, k_norm_w, idx_k_norm_w, w_br_ssd,
           w_br_att, w_out, norm2_w, w_gate, w_up, w_down):
    bp, sp, _ = x_prompt.shape
    bs, ts, _ = x_sample.shape
    past = cache_k.shape[2]
    assert w_in.shape[0] == 1
    assert past % BLK == 0 and BLK % CHUNK == 0 and N_META <= CHUNK
    l = 0

    n_sel_p = min(TOPK_MAX, sp // 4)
    n_sel_s = min(TOPK_MAX, (past + ts) // 4)
    conv0 = jnp.zeros((bp, CONV_WIDTH - 1, CONV_DIM), F32)
    ssm0 = jnp.zeros((bp, SSD_HEADS, SSD_HEAD_DIM, SSD_STATE), F32)
    bias_tiles = _bias_tiles(rel_bias.astype(F32))
    p = _prepare_params(l, norm1_w, w_in, conv_w, conv_b, dt_bias, a_log, d_skip, ssd_norm_w, q_norm_w, k_norm_w,
                        idx_k_norm_w, w_br_ssd, w_br_att, w_out, norm2_w, w_gate, w_up, w_down)
    y_prompt, *rest_p = _layer(x_prompt, meta_tokens, conv0, ssm0, None, p, bias_tiles,
                               chunk_off=CHUNK - N_META, n_sel=n_sel_p)
    y_sample, *rest_s = _layer(x_sample, meta_tokens[:0], state_conv[l], state_ssm[l],
                               (cache_k[l], cache_v[l], cache_kidx[l]), p, bias_tiles,
                               chunk_off=0, n_sel=n_sel_s)

    dtypes = (x_prompt.dtype, x_prompt.dtype, x_prompt.dtype, state_ssm.dtype, x_prompt.dtype)
    return (y_prompt, y_sample, *(o[None].astype(dt) for o, dt in zip(rest_p, dtypes)),
            *(o[None].astype(dt) for o, dt in zip(rest_s, dtypes)))
```

```python
import functools

import jax
import jax.numpy as jnp
from jax import lax
from jax.experimental import pallas as pl
from jax.experimental.pallas import tpu as pltpu

F32 = jnp.float32
BF16 = jnp.bfloat16

D_MODEL = 1024
CHUNK = 64
N_META = 16
SSD_HEADS = 16
SSD_HEAD_DIM = 64
SSD_INNER = SSD_HEADS * SSD_HEAD_DIM
SSD_GROUPS = 4
SSD_STATE = 128
CONV_WIDTH = 4
CONV_DIM = SSD_INNER + 2 * SSD_GROUPS * SSD_STATE
N_HEADS = 8
N_KV_HEADS = 2
HEAD_DIM = 128
ATT_INNER = N_HEADS * HEAD_DIM
KV_DIM = N_KV_HEADS * HEAD_DIM
IDX_HEADS = 8
IDX_DIM = 64
TOPK_MAX = 256
REL_BUCKETS = 32
REL_MAX_DIST = 128
IN_WIDTHS = (SSD_INNER, CONV_DIM, SSD_HEADS, ATT_INNER, KV_DIM, KV_DIM, IDX_HEADS * IDX_DIM, IDX_DIM, IDX_HEADS,
             D_MODEL, D_MODEL)
EPS = 1e-6

LANES = 128
SUBLANES = 8
VMEM_LIMIT_BYTES = 56 * 1024 * 1024

BLK = LANES
ROW_TILE = 256
HALO = SUBLANES

C_Z = 0
C_XBC = C_Z + SSD_INNER
C_Q = C_XBC + CONV_DIM
C_K = C_Q + ATT_INNER
C_V = C_K + KV_DIM
C_QI = C_V + KV_DIM
C_GS = C_QI + IDX_HEADS * IDX_DIM
C_GA = C_GS + D_MODEL
C_SM = C_GA + D_MODEL
IN_PAD = C_SM + LANES
SM_KI = 0
SM_DT = SM_KI + IDX_DIM
SM_WI = SM_DT + SSD_HEADS

SEARCH_STEPS = 16
SEARCH_CLAMP = 1.0 / 16
NEG_BIG = -1e30


def _cparams(sem):
    return pltpu.CompilerParams(dimension_semantics=sem, vmem_limit_bytes=VMEM_LIMIT_BYTES)


def _const_spec(shape):
    nd = len(shape)
    return pl.BlockSpec(shape, lambda *_: (0,) * nd)


def _rms(x, w):
    return x * lax.rsqrt(jnp.mean(x * x, axis=-1, keepdims=True) + EPS) * w


def _silu(x):
    return x * jax.nn.sigmoid(x)


def _in_proj_kernel(x_ref, hist_ref, n1_ref, w_ref, qn_ref, kn_ref, kin_ref, cw_ref, cb_ref,
                    z_ref, xc_ref, q_ref, k_ref, v_ref, kb_ref, vb_ref, qi_ref, sm_ref, smb_ref, gs_ref, ga_ref, tail_ref,
                    xpad_scr, *, tiles_per_seq, t_x):
    j = pl.program_id(1)
    hn = _rms(x_ref[...], n1_ref[...]).astype(BF16)
    tm = x_ref.shape[0]

    def mm(lo, hi):
        return jnp.dot(hn, w_ref[:, lo:hi], preferred_element_type=F32)

    @pl.when(j == 0)
    def _():
        xpad_scr[0:HALO, :] = hist_ref[...]

    step = 2 * LANES
    conv_slabs = iter(range(0, CONV_DIM, LANES))

    def conv_next():
        c0 = next(conv_slabs, None)
        if c0 is None:
            return
        if c0 % step == 0:
            xpad_scr[HALO:HALO + tm, c0:c0 + step] = mm(C_XBC + c0, C_XBC + c0 + step)
        sl = slice(c0, c0 + LANES)
        acc = xpad_scr[HALO - 3:HALO - 3 + tm, sl] * cw_ref[0:1, sl]
        for i in range(1, CONV_WIDTH):
            acc = acc + xpad_scr[HALO - 3 + i:HALO - 3 + i + tm, sl] * cw_ref[i:i + 1, sl]
        xc_ref[:, sl] = _silu(cb_ref[:, sl] + acc)

    def project(lo, hi, out_ref, dtype=F32):
        for c0 in range(0, hi - lo, step):
            out_ref[:, c0:c0 + step] = mm(lo + c0, lo + c0 + step).astype(dtype)
            conv_next()

    project(C_Z, C_XBC, z_ref)
    for c0 in range(0, ATT_INNER, step):
        q = mm(C_Q + c0, C_Q + c0 + step)
        conv_next()
        for h in range(step // HEAD_DIM):
            sl = slice(h * HEAD_DIM, (h + 1) * HEAD_DIM)
            q_ref[:, c0 + h * HEAD_DIM:c0 + (h + 1) * HEAD_DIM] = _rms(q[:, sl], qn_ref[...]).astype(BF16)
    k = mm(C_K, C_V)
    conv_next()
    v = mm(C_V, C_QI)
    conv_next()
    for h in range(N_KV_HEADS):
        sl = slice(h * HEAD_DIM, (h + 1) * HEAD_DIM)
        kh = _rms(k[:, sl], kn_ref[...])
        k_ref[pl.ds(h, tm, stride=N_KV_HEADS), :] = kh
        kb_ref[:, sl] = kh.astype(BF16)
        v_ref[pl.ds(h, tm, stride=N_KV_HEADS), :] = v[:, sl]
    vb_ref[...] = v.astype(BF16)
    project(C_QI, C_GS, qi_ref, BF16)
    project(C_GS, C_GA, gs_ref)
    project(C_GA, C_SM, ga_ref)
    sm = mm(C_SM, IN_PAD)
    assert next(conv_slabs, None) is None
    lane = lax.broadcasted_iota(jnp.int32, sm.shape, 1)
    is_ki = lane < SM_KI + IDX_DIM
    ms = jnp.sum(jnp.where(is_ki, sm * sm, 0.0), axis=-1, keepdims=True) * (1.0 / IDX_DIM)
    ki = sm * lax.rsqrt(ms + EPS) * kin_ref[...]
    is_wi = (lane >= SM_WI) & (lane < SM_WI + IDX_HEADS)
    out = jnp.where(is_ki, ki, jnp.where(is_wi, sm * (IDX_HEADS ** -0.5), sm))
    sm_ref[...] = out
    smb_ref[...] = out.astype(BF16)

    n_last = t_x - (tiles_per_seq - 1) * tm

    @pl.when(j == tiles_per_seq - 1)
    def _():
        tail_ref[...] = xpad_scr[n_last:n_last + HALO, :]

    xpad_scr[0:HALO, :] = xpad_scr[tm:tm + HALO, :]


def _row_tile(n):
    return ROW_TILE if n % ROW_TILE == 0 else BLK


def _in_proj(x2d, hist, p, *, n_seq, t_x):
    n = x2d.shape[0]
    tp = n // n_seq
    tm = _row_tile(tp)
    tiles = tp // tm
    shared_hist = hist.shape[0] == 1
    row = lambda width, per_token=1: pl.BlockSpec((tm * per_token, width), lambda s, j: (s * tiles + j, 0))
    per_seq = pl.BlockSpec((None, HALO, CONV_DIM), lambda s, j: (0 if shared_hist else s, 0, 0))
    outs = [
        (SSD_INNER, F32, 1), (CONV_DIM, F32, 1), (ATT_INNER, BF16, 1), (HEAD_DIM, F32, N_KV_HEADS),
        (HEAD_DIM, F32, N_KV_HEADS), (KV_DIM, BF16, 1), (KV_DIM, BF16, 1), (IDX_HEADS * IDX_DIM, BF16, 1),
        (LANES, F32, 1), (LANES, BF16, 1), (D_MODEL, F32, 1), (D_MODEL, F32, 1),
    ]
    kern = functools.partial(_in_proj_kernel, tiles_per_seq=tiles, t_x=t_x)
    return pl.pallas_call(
        kern,
        grid=(n_seq, tiles),
        in_specs=[row(D_MODEL), per_seq, _const_spec((1, D_MODEL)),
                  pl.BlockSpec((D_MODEL, IN_PAD), lambda s, j: (0, 0), pipeline_mode=pl.Buffered(1)),
                  _const_spec((1, HEAD_DIM)), _const_spec((1, HEAD_DIM)), _const_spec((1, LANES)),
                  _const_spec((CONV_WIDTH, CONV_DIM)), _const_spec((1, CONV_DIM))],
        out_specs=[row(w, per) for w, _, per in outs]
        + [pl.BlockSpec((None, HALO, CONV_DIM), lambda s, j: (s, 0, 0))],
        out_shape=[jax.ShapeDtypeStruct((n * per, w), dt) for w, dt, per in outs]
        + [jax.ShapeDtypeStruct((n_seq, HALO, CONV_DIM), F32)],
        scratch_shapes=[pltpu.VMEM((tm + 2 * HALO, CONV_DIM), F32)],
        compiler_params=_cparams(("parallel", "arbitrary")),
        name="in_proj",
    )(x2d, hist, p["norm1_w"], p["w_in"], p["q_norm_w"], p["k_norm_w"], p["idx_k_norm_w"], p["conv_w"], p["conv_b"])


def _softplus(x):
    return jnp.maximum(x, 0.0) + jnp.log1p(jnp.exp(-jnp.abs(x)))


def _split3(x):
    hi = x.astype(BF16)
    r1 = x - hi.astype(F32)
    mid = r1.astype(BF16)
    lo = (r1 - mid.astype(F32)).astype(BF16)
    return hi, mid, lo


def _ssd_kernel(*refs, n_chunks, n_lead, t_x):
    if n_lead:
        xbc_lead_ref, z_lead_ref, sm_lead_ref, *refs = refs
    (xbc_ref, z_ref, sm_ref, sprev_ref, dtb_ref, dtbt_ref, alog_ref, alogt_ref,
     dsk_ref, nw_ref, exp_ref, y_ref, snew_ref, s_scr, y_scr) = refs
    c = pl.program_id(1)
    lead_chunks = 1 if n_lead else 0

    @pl.when(c == 0)
    def _():
        for g in range(SSD_GROUPS):
            s_scr[g] = sprev_ref[g].T

    def chunk(xc_ref, z_src_ref, sm_src_ref, n_valid):
        _ssd_chunk(xc_ref, z_src_ref, sm_src_ref, n_valid, dtb_ref, dtbt_ref, alog_ref, alogt_ref, dsk_ref, nw_ref,
                   exp_ref, y_ref, s_scr, y_scr)

    if n_lead:
        pl.when(c == 0)(lambda: chunk(xbc_lead_ref, z_lead_ref, sm_lead_ref, n_lead))
        pl.when(c > 0)(lambda: chunk(xbc_ref, z_ref, sm_ref, jnp.minimum(BLK, t_x - (c - lead_chunks) * BLK)))
    else:
        chunk(xbc_ref, z_ref, sm_ref, jnp.minimum(BLK, t_x - c * BLK))

    @pl.when(c == n_chunks - 1)
    def _():
        for g in range(SSD_GROUPS):
            snew_ref[g] = s_scr[g].T


def _ssd_chunk(xc_ref, z_ref, sm_ref, n_valid, dtb_ref, dtbt_ref, alog_ref, alogt_ref, dsk_ref, nw_ref, exp_ref,
               y_ref, s_scr, y_scr):
    q = BLK
    gw = SSD_HEADS // SSD_GROUPS * SSD_HEAD_DIM
    sm = sm_ref[...]

    smt = sm.T
    lane = lax.broadcasted_iota(jnp.int32, (q, LANES), 1)
    row = lax.broadcasted_iota(jnp.int32, (q, LANES), 0)
    is_dt = (lane >= SM_DT) & (lane < SM_DT + SSD_HEADS) & (row < n_valid)
    is_dt_t = (row >= SM_DT) & (row < SM_DT + SSD_HEADS) & (lane < n_valid)
    dt = jnp.where(is_dt, _softplus(sm + dtb_ref[...]), 0.0)
    dtt = jnp.where(is_dt_t, _softplus(smt + dtbt_ref[...]), 0.0)
    da = dt * (-jnp.exp(alog_ref[...]))
    dat = dtt * (-jnp.exp(alogt_ref[...]))
    ii = lax.broadcasted_iota(jnp.int32, (q, q), 0)
    jj = lax.broadcasted_iota(jnp.int32, (q, q), 1)
    causal = jj <= ii
    acum = sum(jnp.dot(causal.astype(BF16), p, preferred_element_type=F32) for p in _split3(da))
    acumt = sum(jnp.dot(p, (ii <= jj).astype(BF16), preferred_element_type=F32) for p in _split3(dat))
    a_last = acum[q - 1:q, :]
    expand = exp_ref[...]
    stacked = jnp.concatenate([jnp.exp(acum), jnp.exp(a_last - acum) * dt,
                               jnp.broadcast_to(jnp.exp(a_last), (SUBLANES, LANES))], axis=0)
    stacked_x = sum(jnp.dot(p, expand, preferred_element_type=F32) for p in _split3(stacked))
    ea_x = stacked_x[0:q]
    wdt_x = stacked_x[q:2 * q]
    dec_x = stacked_x[2 * q:2 * q + 1]

    sq = None
    for g in range(SSD_GROUPS):
        gsl = slice(g * gw, (g + 1) * gw)
        bsl = slice(SSD_INNER + g * SSD_STATE, SSD_INNER + (g + 1) * SSD_STATE)
        csl = slice(SSD_INNER + SSD_GROUPS * SSD_STATE + g * SSD_STATE,
                    SSD_INNER + SSD_GROUPS * SSD_STATE + (g + 1) * SSD_STATE)
        bmf = xc_ref[:, bsl]
        bm = bmf.astype(BF16)
        cm = xc_ref[:, csl].astype(BF16)
        xg = xc_ref[:, gsl]
        xgb = xg.astype(BF16)
        cbm = lax.dot_general(cm, bm, (((1,), (1,)), ((), ())), preferred_element_type=F32)
        xw = (xg * wdt_x[:, gsl]).astype(BF16)
        st = jnp.dot(bmf.T.astype(BF16), xw, preferred_element_type=F32)
        s_in = s_scr[g]
        y_off = jnp.dot(cm, s_in.astype(BF16), preferred_element_type=F32) * ea_x[:, gsl]
        s_scr[g] = s_in * dec_x[:, gsl] + st
        for r in range(SSD_HEADS // SSD_GROUPS):
            h = g * (SSD_HEADS // SSD_GROUPS) + r
            seg = acum[:, SM_DT + h:SM_DT + h + 1] - acumt[SM_DT + h:SM_DT + h + 1, :]
            lmat = jnp.exp(jnp.where(causal, seg, -jnp.inf))
            wmat = (cbm * lmat * dtt[SM_DT + h:SM_DT + h + 1, :]).astype(BF16)
            rsl = slice(r * SSD_HEAD_DIM, (r + 1) * SSD_HEAD_DIM)
            hsl = slice(h * SSD_HEAD_DIM, (h + 1) * SSD_HEAD_DIM)
            y_diag = jnp.dot(wmat, xgb[:, rsl], preferred_element_type=F32)
            y_scr[:, hsl] = y_diag + y_off[:, rsl] + xg[:, rsl] * dsk_ref[:, hsl]
        yg = y_scr[:, gsl] * _silu(z_ref[:, gsl])
        y_scr[:, gsl] = yg
        sq = yg * yg if sq is None else sq + yg * yg

    ms = jnp.sum(sq, axis=-1, keepdims=True) * (1.0 / SSD_INNER)
    y_ref[...] = (y_scr[...] * lax.rsqrt(ms + EPS) * nw_ref[...]).astype(BF16)


def _ssd(lead, xbc, z, sm, ssm_prev, p, *, n_lead, t_x):
    b, tp, _ = xbc.shape
    lead_chunks = 1 if n_lead else 0
    assert n_lead % SUBLANES == 0 and n_lead <= BLK
    nc = tp // BLK + lead_chunks
    gw = SSD_HEADS // SSD_GROUPS * SSD_HEAD_DIM
    seq = lambda width: pl.BlockSpec((None, BLK, width), lambda i, c: (i, jnp.maximum(c - lead_chunks, 0), 0))
    kern = functools.partial(_ssd_kernel, n_chunks=nc, n_lead=n_lead, t_x=t_x)
    lead_specs = [_const_spec((BLK, CONV_DIM)), _const_spec((BLK, SSD_INNER)), _const_spec((BLK, LANES))]
    return pl.pallas_call(
        kern,
        grid=(b, nc),
        in_specs=(lead_specs if n_lead else []) + [
                  seq(CONV_DIM), seq(SSD_INNER), seq(LANES),
                  pl.BlockSpec((None, SSD_GROUPS, gw, SSD_STATE), lambda i, c: (i, 0, 0, 0)),
                  _const_spec((1, LANES)), _const_spec((LANES, 1)), _const_spec((1, LANES)), _const_spec((LANES, 1)),
                  _const_spec((1, SSD_INNER)), _const_spec((1, SSD_INNER)), _const_spec((LANES, SSD_INNER))],
        out_specs=[seq(SSD_INNER),
                   pl.BlockSpec((None, SSD_GROUPS, gw, SSD_STATE), lambda i, c: (i, 0, 0, 0))],
        out_shape=[jax.ShapeDtypeStruct((b, tp, SSD_INNER), BF16),
                   jax.ShapeDtypeStruct((b, SSD_GROUPS, gw, SSD_STATE), F32)],
        scratch_shapes=[pltpu.VMEM((SSD_GROUPS, SSD_STATE, gw), F32),
                        pltpu.VMEM((BLK, SSD_INNER), F32)],
        compiler_params=_cparams(("parallel", "arbitrary")),
        name="ssd",
    )(*(lead if n_lead else ()), xbc, z, sm, ssm_prev, p["dtb"], p["dtb_t"],
      p["alog"], p["alog_t"], p["dskip_x"], p["ssd_norm_w"], p["expand"])


N_SHIFT_TILES = 5
LEAD_TILE = N_SHIFT_TILES
N_BIAS_TILES = N_SHIFT_TILES + 1


def _log_bucket_starts():
    nb = REL_BUCKETS // 2
    max_exact = nb // 2
    s = nb - max_exact
    starts = []
    for m in range(1, s):
        n = max_exact
        while n ** s * max_exact ** m < max_exact ** s * REL_MAX_DIST ** m:
            n += 1
        starts.append(n)
    return starts


def _bias_kernel(rb_ref, bt_ref):
    nb = REL_BUCKETS // 2
    max_exact = nb // 2
    qq = lax.broadcasted_iota(jnp.int32, (BLK, BLK), 0)
    kk = lax.broadcasted_iota(jnp.int32, (BLK, BLK), 1)
    for u in range(N_BIAS_TILES):
        rel = kk - qq + ((u - 2) * BLK if u < N_SHIFT_TILES else -N_META)
        n = jnp.abs(rel)
        large = max_exact + sum(jnp.where(n >= start, 1, 0) for start in _log_bucket_starts())
        bucket = jnp.where(rel > 0, nb, 0) + jnp.where(n < max_exact, n, large)
        for h in range(N_HEADS):
            acc = jnp.zeros((BLK, BLK), F32)
            for bkt in range(REL_BUCKETS):
                acc = jnp.where(bucket == bkt, rb_ref[bkt, h], acc)
            bt_ref[h * N_BIAS_TILES + u] = acc


def _bias_tiles(rel_bias):
    return pl.pallas_call(
        _bias_kernel,
        in_specs=[pl.BlockSpec(memory_space=pltpu.SMEM)],
        out_specs=pl.BlockSpec(memory_space=pltpu.VMEM),
        out_shape=jax.ShapeDtypeStruct((N_HEADS * N_BIAS_TILES, BLK, BLK), F32),
        name="bias_tiles",
    )(rel_bias)


KB = 2 * BLK


def _attn_kernel(*refs, nkp_total, n_past, n_lead, chunk_off, t_x, n_sel):
    if n_past:
        pki_ref, pk_ref, pv_ref, *refs = refs
    if n_lead:
        lki_ref, lk_ref, lv_ref, *refs = refs
    (qn_ref, qi_ref, sm_ref, ki_new_ref, k_new_ref, v_new_ref, bt_ref, o_ref,
     ki_scr, k_scr, v_scr, st_scr, m_scr, lg_scr, qis_scr, qs_scr, mrun_scr, lrun_scr, acc_scr) = refs
    i = pl.program_id(1)
    r = BLK
    n_prefix = n_past + (BLK if n_lead else 0)
    gap = BLK - n_lead if n_lead else 0
    l_valid = n_prefix - gap + t_x

    @pl.when(i == 0)
    def _():
        if n_lead:
            ki_scr[0:BLK, :] = lki_ref[:, SM_KI:SM_KI + IDX_DIM]
            k_scr[0:BLK, :] = lk_ref[...]
            v_scr[0:BLK, :] = lv_ref[...]
        if n_past:
            def load_past(c, carry):
                rows = pl.ds(pl.multiple_of(c * KB, KB), KB)
                ki_scr[rows, :] = pki_ref[rows, :].astype(BF16)
                for g in range(N_KV_HEADS):
                    src = pl.ds(pl.multiple_of(c * KB * N_KV_HEADS, KB) + g, KB, stride=N_KV_HEADS)
                    k_scr[rows, g * HEAD_DIM:(g + 1) * HEAD_DIM] = pk_ref[src, :].astype(BF16)
                    v_scr[rows, g * HEAD_DIM:(g + 1) * HEAD_DIM] = pv_ref[src, :].astype(BF16)
                return carry

            lax.fori_loop(0, n_past // KB, load_past, 0)
        n_new = k_new_ref.shape[0]
        ki_scr[n_prefix:n_prefix + n_new, :] = ki_new_ref[:, SM_KI:SM_KI + IDX_DIM]
        k_scr[n_prefix:n_prefix + n_new, :] = k_new_ref[...]
        v_scr[n_prefix:n_prefix + n_new, :] = v_new_ref[...]
        n_tail = nkp_total * KB - n_prefix - n_new
        if n_tail:
            for scr in (ki_scr, k_scr, v_scr):
                scr[n_prefix + n_new:, :] = jnp.zeros((n_tail, scr.shape[1]), BF16)

    qb = n_prefix // BLK + i
    q0 = qb * BLK - gap
    n_keys = nkp_total * KB

    def chunk_end(pos):
        return jnp.minimum(CHUNK * ((pos + chunk_off) // CHUNK + 1) - chunk_off, l_valid)

    nkp = jnp.minimum(nkp_total, (chunk_end(q0 + BLK - 1) + gap + KB - 1) // KB)

    nt = (((1,), (1,)), ((), ()))
    wit = sm_ref[...].T
    n_adm = chunk_end(q0 + lax.broadcasted_iota(jnp.int32, (1, r), 1))
    krow = lax.broadcasted_iota(jnp.int32, (KB, r), 0)

    fold_rows = 8 * SUBLANES

    def fold(x, op):
        return op(x.reshape(KB // fold_rows, fold_rows, r), axis=0)

    def for_key_steps(body, init):
        carry = lax.fori_loop(0, nkp // 2, lambda t, c: body(2 * t + 1, body(2 * t, c)), init)
        return lax.cond(nkp % 2 == 1, lambda c: body(nkp - 1, c), lambda c: c, carry)

    for h in range(IDX_HEADS):
        qis_scr[h * r:(h + 1) * r, :] = qi_ref[:, h * IDX_DIM:(h + 1) * IDX_DIM]

    def score_body(jp, carry):
        mn, mx = carry
        kij = ki_scr[pl.ds(pl.multiple_of(jp * KB, KB), KB), :]
        acc = jnp.zeros((KB, r), F32)
        for hp in range(IDX_HEADS // 2):
            sh = lax.dot_general(kij, qis_scr[2 * hp * r:2 * (hp + 1) * r, :], nt, preferred_element_type=F32)
            for e in range(2):
                h = 2 * hp + e
                acc = acc + wit[SM_WI + h:SM_WI + h + 1, :] * jnp.maximum(sh[:, e * r:(e + 1) * r], 0.0)
        adm = krow < n_adm + gap - jp * KB
        if gap:
            adm = adm & ((jp > 0) | (krow < n_lead) | (krow >= BLK))
        s = jnp.where(adm, acc * (IDX_DIM ** -0.5), -jnp.inf)
        st_scr[jp] = s
        mn = jnp.minimum(mn, fold(jnp.where(adm, s, jnp.inf), jnp.min))
        mx = jnp.maximum(mx, fold(s, jnp.max))
        return mn, mx

    init = (jnp.full((fold_rows, r), jnp.inf, F32), jnp.full((fold_rows, r), -jnp.inf, F32))
    mn, mx = for_key_steps(score_body, init)
    lo0 = jnp.min(mn, axis=0, keepdims=True)
    hi0 = jnp.max(mx, axis=0, keepdims=True)
    kk = jnp.minimum(n_adm, n_sel).astype(F32)

    def count(pred):
        def body(jp, acc):
            return acc + fold(jnp.where(pred(st_scr[jp], jp), 1.0, 0.0), jnp.sum)
        acc = lax.fori_loop(0, nkp, body, jnp.zeros((fold_rows, r), F32))
        return jnp.sum(acc, axis=0, keepdims=True)

    def search_body(_, carry):
        lo, hi, c_lo, c_hi, ub, c_ub, hit = carry
        frac = jnp.clip((c_lo - kk) / (c_lo - c_hi), SEARCH_CLAMP, 1.0 - SEARCH_CLAMP)
        mid = jnp.where(hit > 0.0, lo, lo * (1.0 - frac) + hi * frac)
        cnt = count(lambda s, jp: s >= mid)
        ok = cnt >= kk
        now = cnt == kk
        lo = jnp.where(ok, mid, lo)
        hi = jnp.where(ok & ~now, hi, mid)
        c_lo = jnp.where(ok, cnt, c_lo)
        c_hi = jnp.where(now, kk - 1.0, jnp.where(ok, c_hi, cnt))
        ub = jnp.where(ok, ub, mid)
        c_ub = jnp.where(ok, c_ub, cnt)
        return lo, hi, c_lo, c_hi, ub, c_ub, jnp.where(now, 1.0, 0.0)

    all_adm = n_adm.astype(F32) == kk
    zeros = jnp.zeros((1, r), F32)
    init = (lo0, jnp.where(all_adm, lo0, hi0), n_adm.astype(F32), jnp.where(all_adm, kk - 1.0, zeros),
            jnp.full((1, r), jnp.inf, F32), zeros, jnp.where(all_adm, 1.0, 0.0))
    lo_f, _, _, _, ub, c_ub, hit_f = lax.fori_loop(0, SEARCH_STEPS, search_body, init)
    hit = hit_f > 0.0
    pending = jnp.sum(jnp.where(hit, 0, 1))
    take_all = jnp.full((1, r), float(n_keys), F32)

    def exact_path():
        def next_below(ub):
            def body(jp, acc):
                s = st_scr[jp]
                return jnp.maximum(acc, fold(jnp.where(s < ub, s, -jnp.inf), jnp.max))
            acc = lax.fori_loop(0, nkp, body, jnp.full((fold_rows, r), -jnp.inf, F32))
            return jnp.max(acc, axis=0, keepdims=True)

        def descend_cond(carry):
            *_, todo, it = carry
            return (todo > 0) & (it < n_keys)

        def descend_body(carry):
            ub, c_ub, _, _, _, it = carry
            t = next_below(ub)
            c_t = count(lambda s, jp: s >= t)
            done = hit | (c_t >= kk)
            return (jnp.where(done, ub, t), jnp.where(done, c_ub, c_t), t, c_t,
                    jnp.sum(jnp.where(done, 0, 1)), it + 1)

        _, c_gt, t, c_t, _, _ = lax.while_loop(descend_cond, descend_body,
                                               (ub, c_ub, lo0, zeros, jnp.int32(1), jnp.int32(0)))
        ties_wanted = jnp.where(hit, take_all, kk - c_gt)
        extra_ties = jnp.sum(jnp.where(~hit & (c_t - c_gt > ties_wanted), 1, 0))
        return jnp.where(hit, lo_f, t), ties_wanted, extra_ties

    thr, ties_wanted, extra_ties = lax.cond(pending > 0, exact_path, lambda: (lo_f, take_all, jnp.int32(0)))

    def mask_body(jp, carry):
        m_scr[jp] = jnp.where(st_scr[jp] >= thr, 0.0, NEG_BIG).T
        return carry

    def mask_with_ties():
        ii = lax.broadcasted_iota(jnp.int32, (KB, KB), 0)
        jj = lax.broadcasted_iota(jnp.int32, (KB, KB), 1)
        upto = (jj <= ii).astype(BF16)

        def body(jp, wanted):
            s = st_scr[jp]
            tied = s == thr
            rank = jnp.dot(upto, jnp.where(tied, 1.0, 0.0).astype(BF16), preferred_element_type=F32)
            m_scr[jp] = jnp.where((s > thr) | (tied & (rank <= wanted)), 0.0, NEG_BIG).T
            return wanted - rank[KB - 1:KB, :]

        lax.fori_loop(0, nkp, body, ties_wanted)
        return jnp.int32(0)

    lax.cond(extra_ties > 0, mask_with_ties, lambda: lax.fori_loop(0, nkp, mask_body, jnp.int32(0)))

    scale = HEAD_DIM ** -0.5
    rep = N_HEADS // N_KV_HEADS
    for h in range(N_HEADS):
        qs_scr[h // rep, (h % rep) * r:(h % rep + 1) * r, :] = qn_ref[:, h * HEAD_DIM:(h + 1) * HEAD_DIM]
    mrun_scr[...] = jnp.full(mrun_scr.shape, -jnp.inf, F32)
    lrun_scr[...] = jnp.zeros(lrun_scr.shape, F32)
    acc_scr[...] = jnp.zeros(acc_scr.shape, F32)

    def logit_body(jp, carry):
        u0 = jnp.clip(2 * jp - qb + 2, 0, N_SHIFT_TILES - 1)
        u1 = jnp.clip(2 * jp + 1 - qb + 2, 0, N_SHIFT_TILES - 1)
        if gap:
            u0 = jnp.where((jp == 0) & (qb == 1), LEAD_TILE, u0)
        madd = m_scr[jp]
        for g in range(N_KV_HEADS):
            lt = lax.dot_general(qs_scr[g], k_scr[pl.ds(pl.multiple_of(jp * KB, KB), KB),
                                                  g * HEAD_DIM:(g + 1) * HEAD_DIM], nt,
                                 preferred_element_type=F32)
            for e in range(rep):
                h = g * rep + e
                bias = jnp.concatenate([bt_ref[h * N_BIAS_TILES + u0], bt_ref[h * N_BIAS_TILES + u1]], axis=1)
                lg = lt[e * r:(e + 1) * r, :] * scale + bias + madd
                lg_scr[h, jp] = lg
                mrun_scr[h] = jnp.maximum(mrun_scr[h], jnp.maximum(lg[:, :BLK], lg[:, BLK:]))
        return carry

    for_key_steps(logit_body, 0)
    for h in range(N_HEADS):
        mrun_scr[h] = jnp.broadcast_to(jnp.max(mrun_scr[h], axis=1, keepdims=True), (r, BLK))

    def pv_body(jp, carry):
        for g in range(N_KV_HEADS):
            es = []
            for e in range(rep):
                h = g * rep + e
                mrow = mrun_scr[h]
                ex = jnp.exp(lg_scr[h, jp] - jnp.concatenate([mrow, mrow], axis=1))
                lrun_scr[h] = lrun_scr[h] + (ex[:, :BLK] + ex[:, BLK:])
                es.append(ex.astype(BF16))
            acc_scr[g] = acc_scr[g] + jnp.dot(jnp.concatenate(es, axis=0),
                                              v_scr[pl.ds(pl.multiple_of(jp * KB, KB), KB),
                                                    g * HEAD_DIM:(g + 1) * HEAD_DIM],
                                              preferred_element_type=F32)
        return carry

    for_key_steps(pv_body, 0)
    for h in range(N_HEADS):
        g, e = h // rep, h % rep
        den = jnp.sum(lrun_scr[h], axis=1, keepdims=True)
        o_ref[:, h * HEAD_DIM:(h + 1) * HEAD_DIM] = (acc_scr[g, e * r:(e + 1) * r, :] / den).astype(BF16)


def _attn(qn, qi, sm, ki_new, k_new, v_new, past, lead, bias_tiles, *, n_lead, chunk_off, t_x, n_sel):
    b, tq, _ = qn.shape
    n_past = past[0].shape[1] if past is not None else 0
    assert n_past % KB == 0 and not (n_past and n_lead)
    n_prefix = n_past + (BLK if n_lead else 0)
    n_keys = -(-(n_prefix + tq) // KB) * KB
    nkp_total = n_keys // KB
    rep = N_HEADS // N_KV_HEADS
    seq = lambda width: pl.BlockSpec((None, BLK, width), lambda bi, i: (bi, i, 0))
    rows = lambda n, width: pl.BlockSpec((None, n, width), lambda bi, i: (bi, 0, 0))
    kern = functools.partial(_attn_kernel, nkp_total=nkp_total, n_past=n_past, n_lead=n_lead, chunk_off=chunk_off,
                             t_x=t_x, n_sel=n_sel)
    past_specs = [rows(n_past, IDX_DIM), rows(n_past * N_KV_HEADS, HEAD_DIM),
                  rows(n_past * N_KV_HEADS, HEAD_DIM)] if n_past else []
    lead_specs = [_const_spec((BLK, LANES)), _const_spec((BLK, KV_DIM)), _const_spec((BLK, KV_DIM))] if n_lead else []
    return pl.pallas_call(
        kern,
        grid=(b, tq // BLK),
        in_specs=past_specs + lead_specs + [seq(ATT_INNER), seq(IDX_HEADS * IDX_DIM), seq(LANES),
                                            rows(tq, LANES), rows(tq, KV_DIM), rows(tq, KV_DIM),
                                            _const_spec((N_HEADS * N_BIAS_TILES, BLK, BLK))],
        out_specs=seq(ATT_INNER),
        out_shape=jax.ShapeDtypeStruct((b, tq, ATT_INNER), BF16),
        scratch_shapes=[pltpu.VMEM((n_keys, IDX_DIM), BF16),
                        pltpu.VMEM((n_keys, KV_DIM), BF16),
                        pltpu.VMEM((n_keys, KV_DIM), BF16),
                        pltpu.VMEM((nkp_total, KB, BLK), F32),
                        pltpu.VMEM((nkp_total, BLK, KB), F32),
                        pltpu.VMEM((N_HEADS, nkp_total, BLK, KB), F32),
                        pltpu.VMEM((IDX_HEADS * BLK, IDX_DIM), BF16),
                        pltpu.VMEM((N_KV_HEADS, rep * BLK, HEAD_DIM), BF16),
                        pltpu.VMEM((N_HEADS, BLK, BLK), F32),
                        pltpu.VMEM((N_HEADS, BLK, BLK), F32),
                        pltpu.VMEM((N_KV_HEADS, rep * BLK, HEAD_DIM), F32)],
        compiler_params=_cparams(("parallel", "arbitrary")),
        name="attn",
    )(*(past or ()), *(lead or ()), qn, qi, sm, ki_new, k_new, v_new, bias_tiles)


def _out_ffn_kernel(x_ref, ys_ref, ya_ref, gs_ref, ga_ref, wbs_ref, wba_ref, wo_ref, n2_ref, wg_ref, wu_ref, wd_ref,
                    y_ref):
    dot = functools.partial(jnp.dot, preferred_element_type=F32)
    merged = (jax.nn.sigmoid(gs_ref[...]) * dot(ys_ref[...], wbs_ref[...])
              + jax.nn.sigmoid(ga_ref[...]) * dot(ya_ref[...], wba_ref[...]))
    h = x_ref[...] + dot(merged.astype(BF16), wo_ref[...])
    hn = _rms(h, n2_ref[...]).astype(BF16)
    act = (_silu(dot(hn, wg_ref[...])) * dot(hn, wu_ref[...])).astype(BF16)
    y_ref[...] = h + dot(act, wd_ref[...])


def _out_ffn(x2d, ys, ya, gs, ga, p):
    n = x2d.shape[0]
    tm = _row_tile(n)
    d_ff = p["w_gate"].shape[1]
    row = lambda width: pl.BlockSpec((tm, width), lambda i: (i, 0))
    wspec = lambda shape: pl.BlockSpec(shape, lambda i: (0, 0), pipeline_mode=pl.Buffered(1))
    return pl.pallas_call(
        _out_ffn_kernel,
        grid=(n // tm,),
        in_specs=[row(D_MODEL), row(SSD_INNER), row(ATT_INNER), row(D_MODEL), row(D_MODEL),
                  wspec((SSD_INNER, D_MODEL)), wspec((ATT_INNER, D_MODEL)), wspec((D_MODEL, D_MODEL)),
                  _const_spec((1, D_MODEL)), wspec((D_MODEL, d_ff)), wspec((D_MODEL, d_ff)), wspec((d_ff, D_MODEL))],
        out_specs=row(D_MODEL),
        out_shape=jax.ShapeDtypeStruct((n, D_MODEL), F32),
        compiler_params=_cparams(("parallel",)),
        name="out_ffn",
    )(x2d, ys, ya, gs, ga, p["w_br_ssd"], p["w_br_att"], p["w_out"], p["norm2_w"], p["w_gate"], p["w_up"],
      p["w_down"])


def _layer(x, lead_rows, conv_prev, ssm_prev, past, p, bias_tiles, *, chunk_off, n_sel):
    b, t, _ = x.shape
    n_lead = lead_rows.shape[0]
    tp = -(-t // BLK) * BLK
    x2d = jnp.pad(x, ((0, 0), (0, tp - t), (0, 0))).reshape(b * tp, D_MODEL)
    hist = jnp.pad(conv_prev.astype(F32), ((0, 0), (HALO - (CONV_WIDTH - 1), 0), (0, 0)))
    seq = lambda a: a.reshape(b, tp, a.shape[-1])
    ssd_lead = attn_lead = None
    if n_lead:
        lead = _in_proj(jnp.pad(lead_rows.astype(x.dtype), ((0, BLK - n_lead), (0, 0))), hist[:1], p,
                        n_seq=1, t_x=n_lead)
        lz, lxbc, _, lk32, lv32, lkb, lvb, _, lsm, lsmb, _, _, hist = lead
        ssd_lead, attn_lead = (lxbc, lz, lsm), (lsmb, lkb, lvb)
    z, xbc, qn, k32, v32, kb, vb, qi, sm, smb, gs, ga, conv_new8 = _in_proj(x2d, hist, p, n_seq=b, t_x=t)

    gw = SSD_HEADS // SSD_GROUPS * SSD_HEAD_DIM
    y_ssd, ssm_new = _ssd(ssd_lead, seq(xbc), seq(z), seq(sm),
                          ssm_prev.astype(F32).reshape(b, SSD_GROUPS, gw, SSD_STATE), p, n_lead=n_lead, t_x=t)

    if past is not None:
        pk, pv, pki = past
        n_past = pk.shape[1]
        past = (pki.astype(F32), pk.astype(F32).reshape(b, n_past * N_KV_HEADS, HEAD_DIM),
                pv.astype(F32).reshape(b, n_past * N_KV_HEADS, HEAD_DIM))
    y_att = _attn(seq(qn), seq(qi), seq(sm), seq(smb), seq(kb), seq(vb), past, attn_lead, bias_tiles,
                  n_lead=n_lead, chunk_off=chunk_off, t_x=t, n_sel=n_sel)

    y = _out_ffn(x2d, y_ssd.reshape(b * tp, SSD_INNER), y_att.reshape(b * tp, ATT_INNER), gs, ga, p)

    def with_lead(new, lead_part):
        if not n_lead:
            return new
        return jnp.concatenate([jnp.broadcast_to(lead_part[None, :n_lead], (b, n_lead) + new.shape[2:]), new], axis=1)

    heads = lambda a, rows: a.reshape(-1, rows, N_KV_HEADS, HEAD_DIM)
    k_new = with_lead(heads(k32, tp)[:, :t], heads(lk32, BLK)[0] if n_lead else None)
    v_new = with_lead(heads(v32, tp)[:, :t], heads(lv32, BLK)[0] if n_lead else None)
    ki_new = with_lead(seq(sm)[:, :t, SM_KI:SM_KI + IDX_DIM], lsm[:, SM_KI:SM_KI + IDX_DIM] if n_lead else None)
    ssm_new = ssm_new.reshape(b, SSD_HEADS, SSD_HEAD_DIM, SSD_STATE)
    conv_new = conv_new8[:, HALO - (CONV_WIDTH - 1):]
    return y.reshape(b, tp, D_MODEL)[:, :t], k_new, v_new, ki_new, ssm_new, conv_new


def _prepare_params(l, norm1_w, w_in, conv_w, conv_b, dt_bias, a_log, d_skip, ssd_norm_w, q_norm_w, k_norm_w,
                    idx_k_norm_w, w_br_ssd, w_br_att, w_out, norm2_w, w_gate, w_up, w_down):
    offs = [0]
    for w in IN_WIDTHS:
        offs.append(offs[-1] + w)
    seg = lambda i: w_in[l][:, offs[i]:offs[i + 1]]
    i_z, i_xbc, i_dt, i_q, i_k, i_v, i_qi, i_ki, i_wi, i_gs, i_ga = range(11)
    pad = jnp.zeros((D_MODEL, LANES - IDX_DIM - SSD_HEADS - IDX_HEADS), w_in.dtype)
    w_perm = jnp.concatenate([seg(i_z), seg(i_xbc), seg(i_q), seg(i_k), seg(i_v), seg(i_qi), seg(i_gs), seg(i_ga),
                              seg(i_ki), seg(i_dt), seg(i_wi), pad], axis=1).astype(BF16)

    def lanes_at(vec, start):
        return jnp.zeros((1, LANES), F32).at[0, start:start + vec.shape[0]].set(vec.astype(F32))

    dtb = lanes_at(dt_bias[l], SM_DT)
    alog = lanes_at(a_log[l], SM_DT)
    head_of_channel = jnp.arange(SSD_INNER) // SSD_HEAD_DIM
    expand = (jnp.arange(LANES)[:, None] == head_of_channel[None, :] + SM_DT).astype(BF16)
    row = lambda v: v.astype(F32).reshape(1, -1)
    return dict(
        norm1_w=row(norm1_w[l]), w_in=w_perm, conv_w=conv_w[l].astype(F32), conv_b=row(conv_b[l]),
        dtb=dtb, dtb_t=dtb.reshape(LANES, 1), alog=alog, alog_t=alog.reshape(LANES, 1),
        dskip_x=row(jnp.repeat(d_skip[l], SSD_HEAD_DIM)), ssd_norm_w=row(ssd_norm_w[l]), expand=expand,
        q_norm_w=row(q_norm_w[l]), k_norm_w=row(k_norm_w[l]),
        idx_k_norm_w=jnp.ones((1, LANES), F32).at[0, SM_KI:SM_KI + IDX_DIM].set(idx_k_norm_w[l].astype(F32)),
        w_br_ssd=w_br_ssd[l].astype(BF16), w_br_att=w_br_att[l].astype(BF16), w_out=w_out[l].astype(BF16),
        norm2_w=row(norm2_w[l]), w_gate=w_gate[l].astype(BF16), w_up=w_up[l].astype(BF16),
        w_down=w_down[l].astype(BF16))


def kernel(x_prompt, x_sample, cache_k, cache_v, cache_kidx, state_ssm, state_conv, meta_tokens, rel_bias, norm1_w,
           w_in, conv_w, conv_b, dt_bias, a_log, d_skip, ssd_norm_w, q_norm_w, k_norm_w, idx_k_norm_w, w_br_ssd,
           w_br_att, w_out, norm2_w, w_gate, w_up, w_down):
    bp, sp, _ = x_prompt.shape
    bs, ts, _ = x_sample.shape
    past = cache_k.shape[2]
    assert w_in.shape[0] == 1
    assert past % BLK == 0 and BLK % CHUNK == 0 and N_META <= CHUNK
    l = 0

    n_sel_p = min(TOPK_MAX, sp // 4)
    n_sel_s = min(TOPK_MAX, (past + ts) // 4)
    conv0 = jnp.zeros((bp, CONV_WIDTH - 1, CONV_DIM), F32)
    ssm0 = jnp.zeros((bp, SSD_HEADS, SSD_HEAD_DIM, SSD_STATE), F32)
    bias_tiles = _bias_tiles(rel_bias.astype(F32))
    p = _prepare_params(l, norm1_w, w_in, conv_w, conv_b, dt_bias, a_log, d_skip, ssd_norm_w, q_norm_w, k_norm_w,
                        idx_k_norm_w, w_br_ssd, w_br_att, w_out, norm2_w, w_gate, w_up, w_down)
    y_prompt, *rest_p = _layer(x_prompt, meta_tokens, conv0, ssm0, None, p, bias_tiles,
                               chunk_off=CHUNK - N_META, n_sel=n_sel_p)
    y_sample, *rest_s = _layer(x_sample, meta_tokens[:0], state_conv[l], state_ssm[l],
                               (cache_k[l], cache_v[l], cache_kidx[l]), p, bias_tiles,
                               chunk_off=0, n_sel=n_sel_s)

    dtypes = (x_prompt.dtype, x_prompt.dtype, x_prompt.dtype, state_ssm.dtype, x_prompt.dtype)
    return (y_prompt, y_sample, *(o[None].astype(dt) for o, dt in zip(rest_p, dtypes)),
            *(o[None].astype(dt) for o, dt in zip(rest_s, dtypes)))
```

```python
import functools

import jax
import jax.numpy as jnp
from jax import lax
from jax.experimental import pallas as pl
from jax.experimental.pallas import tpu as pltpu

F32 = jnp.float32
BF16 = jnp.bfloat16

D_MODEL = 1024
CHUNK = 64
N_META = 16
SSD_HEADS = 16
SSD_HEAD_DIM = 64
SSD_INNER = SSD_HEADS * SSD_HEAD_DIM
SSD_GROUPS = 4
SSD_STATE = 128
CONV_WIDTH = 4
CONV_DIM = SSD_INNER + 2 * SSD_GROUPS * SSD_STATE
N_HEADS = 8
N_KV_HEADS = 2
HEAD_DIM = 128
ATT_INNER = N_HEADS * HEAD_DIM
KV_DIM = N_KV_HEADS * HEAD_DIM
IDX_HEADS = 8
IDX_DIM = 64
TOPK_MAX = 256
REL_BUCKETS = 32
REL_MAX_DIST = 128
IN_WIDTHS = (SSD_INNER, CONV_DIM, SSD_HEADS, ATT_INNER, KV_DIM, KV_DIM, IDX_HEADS * IDX_DIM, IDX_DIM, IDX_HEADS,
             D_MODEL, D_MODEL)
EPS = 1e-6

LANES = 128
SUBLANES = 8
VMEM_LIMIT_BYTES = 56 * 1024 * 1024

BLK = LANES
ROW_TILE = 256
HALO = SUBLANES

C_Z = 0
C_XBC = C_Z + SSD_INNER
C_Q = C_XBC + CONV_DIM
C_K = C_Q + ATT_INNER
C_V = C_K + KV_DIM
C_QI = C_V + KV_DIM
C_GS = C_QI + IDX_HEADS * IDX_DIM
C_GA = C_GS + D_MODEL
C_SM = C_GA + D_MODEL
IN_PAD = C_SM + LANES
SM_KI = 0
SM_DT = SM_KI + IDX_DIM
SM_WI = SM_DT + SSD_HEADS

SEARCH_STEPS = 16
SEARCH_CLAMP = 1.0 / 16
NEG_BIG = -1e30


def _cparams(sem):
    return pltpu.CompilerParams(dimension_semantics=sem, vmem_limit_bytes=VMEM_LIMIT_BYTES)


def _const_spec(shape):
    nd = len(shape)
    return pl.BlockSpec(shape, lambda *_: (0,) * nd)


def _rms(x, w):
    return x * lax.rsqrt(jnp.mean(x * x, axis=-1, keepdims=True) + EPS) * w


def _silu(x):
    return x * jax.nn.sigmoid(x)


def _in_proj_kernel(x_ref, hist_ref, n1_ref, w_ref, qn_ref, kn_ref, kin_ref, cw_ref, cb_ref,
                    z_ref, xc_ref, q_ref, k_ref, v_ref, kb_ref, vb_ref, qi_ref, sm_ref, smb_ref, gs_ref, ga_ref, tail_ref,
                    xpad_scr, *, tiles_per_seq, t_x):
    j = pl.program_id(1)
    hn = _rms(x_ref[...], n1_ref[...]).astype(BF16)
    tm = x_ref.shape[0]

    def mm(lo, hi):
        return jnp.dot(hn, w_ref[:, lo:hi], preferred_element_type=F32)

    @pl.when(j == 0)
    def _():
        xpad_scr[0:HALO, :] = hist_ref[...]

    step = 2 * LANES
    conv_slabs = iter(range(0, CONV_DIM, LANES))

    def conv_next():
        c0 = next(conv_slabs, None)
        if c0 is None:
            return
        if c0 % step == 0:
            xpad_scr[HALO:HALO + tm, c0:c0 + step] = mm(C_XBC + c0, C_XBC + c0 + step)
        sl = slice(c0, c0 + LANES)
        acc = xpad_scr[HALO - 3:HALO - 3 + tm, sl] * cw_ref[0:1, sl]
        for i in range(1, CONV_WIDTH):
            acc = acc + xpad_scr[HALO - 3 + i:HALO - 3 + i + tm, sl] * cw_ref[i:i + 1, sl]
        xc_ref[:, sl] = _silu(cb_ref[:, sl] + acc)

    def project(lo, hi, out_ref, dtype=F32):
        for c0 in range(0, hi - lo, step):
            out_ref[:, c0:c0 + step] = mm(lo + c0, lo + c0 + step).astype(dtype)
            conv_next()

    project(C_Z, C_XBC, z_ref)
    for c0 in range(0, ATT_INNER, step):
        q = mm(C_Q + c0, C_Q + c0 + step)
        conv_next()
        for h in range(step // HEAD_DIM):
            sl = slice(h * HEAD_DIM, (h + 1) * HEAD_DIM)
            q_ref[:, c0 + h * HEAD_DIM:c0 + (h + 1) * HEAD_DIM] = _rms(q[:, sl], qn_ref[...]).astype(BF16)
    k = mm(C_K, C_V)
    conv_next()
    v = mm(C_V, C_QI)
    conv_next()
    for h in range(N_KV_HEADS):
        sl = slice(h * HEAD_DIM, (h + 1) * HEAD_DIM)
        kh = _rms(k[:, sl], kn_ref[...])
        k_ref[pl.ds(h, tm, stride=N_KV_HEADS), :] = kh
        kb_ref[:, sl] = kh.astype(BF16)
        v_ref[pl.ds(h, tm, stride=N_KV_HEADS), :] = v[:, sl]
    vb_ref[...] = v.astype(BF16)
    project(C_QI, C_GS, qi_ref, BF16)
    project(C_GS, C_GA, gs_ref)
    project(C_GA, C_SM, ga_ref)
    sm = mm(C_SM, IN_PAD)
    assert next(conv_slabs, None) is None
    lane = lax.broadcasted_iota(jnp.int32, sm.shape, 1)
    is_ki = lane < SM_KI + IDX_DIM
    ms = jnp.sum(jnp.where(is_ki, sm * sm, 0.0), axis=-1, keepdims=True) * (1.0 / IDX_DIM)
    ki = sm * lax.rsqrt(ms + EPS) * kin_ref[...]
    is_wi = (lane >= SM_WI) & (lane < SM_WI + IDX_HEADS)
    out = jnp.where(is_ki, ki, jnp.where(is_wi, sm * (IDX_HEADS ** -0.5), sm))
    sm_ref[...] = out
    smb_ref[...] = out.astype(BF16)

    n_last = t_x - (tiles_per_seq - 1) * tm

    @pl.when(j == tiles_per_seq - 1)
    def _():
        tail_ref[...] = xpad_scr[n_last:n_last + HALO, :]

    xpad_scr[0:HALO, :] = xpad_scr[tm:tm + HALO, :]


def _row_tile(n):
    return ROW_TILE if n % ROW_TILE == 0 else BLK


def _in_proj(x2d, hist, p, *, n_seq, t_x):
    n = x2d.shape[0]
    tp = n // n_seq
    tm = _row_tile(tp)
    tiles = tp // tm
    shared_hist = hist.shape[0] == 1
    row = lambda width, per_token=1: pl.BlockSpec((tm * per_token, width), lambda s, j: (s * tiles + j, 0))
    per_seq = pl.BlockSpec((None, HALO, CONV_DIM), lambda s, j: (0 if shared_hist else s, 0, 0))
    outs = [
        (SSD_INNER, F32, 1), (CONV_DIM, F32, 1), (ATT_INNER, BF16, 1), (HEAD_DIM, F32, N_KV_HEADS),
        (HEAD_DIM, F32, N_KV_HEADS), (KV_DIM, BF16, 1), (KV_DIM, BF16, 1), (IDX_HEADS * IDX_DIM, BF16, 1),
        (LANES, F32, 1), (LANES, BF16, 1), (D_MODEL, F32, 1), (D_MODEL, F32, 1),
    ]
    kern = functools.partial(_in_proj_kernel, tiles_per_seq=tiles, t_x=t_x)
    return pl.pallas_call(
        kern,
        grid=(n_seq, tiles),
        in_specs=[row(D_MODEL), per_seq, _const_spec((1, D_MODEL)),
                  pl.BlockSpec((D_MODEL, IN_PAD), lambda s, j: (0, 0), pipeline_mode=pl.Buffered(1)),
                  _const_spec((1, HEAD_DIM)), _const_spec((1, HEAD_DIM)), _const_spec((1, LANES)),
                  _const_spec((CONV_WIDTH, CONV_DIM)), _const_spec((1, CONV_DIM))],
        out_specs=[row(w, per) for w, _, per in outs]
        + [pl.BlockSpec((None, HALO, CONV_DIM), lambda s, j: (s, 0, 0))],
        out_shape=[jax.ShapeDtypeStruct((n * per, w), dt) for w, dt, per in outs]
        + [jax.ShapeDtypeStruct((n_seq, HALO, CONV_DIM), F32)],
        scratch_shapes=[pltpu.VMEM((tm + 2 * HALO, CONV_DIM), F32)],
        compiler_params=_cparams(("parallel", "arbitrary")),
        name="in_proj",
    )(x2d, hist, p["norm1_w"], p["w_in"], p["q_norm_w"], p["k_norm_w"], p["idx_k_norm_w"], p["conv_w"], p["conv_b"])


def _softplus(x):
    return jnp.maximum(x, 0.0) + jnp.log1p(jnp.exp(-jnp.abs(x)))


def _split3(x):
    hi = x.astype(BF16)
    r1 = x - hi.astype(F32)
    mid = r1.astype(BF16)
    lo = (r1 - mid.astype(F32)).astype(BF16)
    return hi, mid, lo


def _ssd_kernel(*refs, n_chunks, n_lead, t_x):
    if n_lead:
        xbc_lead_ref, z_lead_ref, sm_lead_ref, *refs = refs
    (xbc_ref, z_ref, sm_ref, sprev_ref, dtb_ref, dtbt_ref, alog_ref, alogt_ref,
     dsk_ref, nw_ref, exp_ref, y_ref, snew_ref, s_scr, y_scr) = refs
    c = pl.program_id(1)
    lead_chunks = 1 if n_lead else 0

    @pl.when(c == 0)
    def _():
        for g in range(SSD_GROUPS):
            s_scr[g] = sprev_ref[g].T

    def chunk(xc_ref, z_src_ref, sm_src_ref, n_valid):
        _ssd_chunk(xc_ref, z_src_ref, sm_src_ref, n_valid, dtb_ref, dtbt_ref, alog_ref, alogt_ref, dsk_ref, nw_ref,
                   exp_ref, y_ref, s_scr, y_scr)

    if n_lead:
        pl.when(c == 0)(lambda: chunk(xbc_lead_ref, z_lead_ref, sm_lead_ref, n_lead))
        pl.when(c > 0)(lambda: chunk(xbc_ref, z_ref, sm_ref, jnp.minimum(BLK, t_x - (c - lead_chunks) * BLK)))
    else:
        chunk(xbc_ref, z_ref, sm_ref, jnp.minimum(BLK, t_x - c * BLK))

    @pl.when(c == n_chunks - 1)
    def _():
        for g in range(SSD_GROUPS):
            snew_ref[g] = s_scr[g].T


def _ssd_chunk(xc_ref, z_ref, sm_ref, n_valid, dtb_ref, dtbt_ref, alog_ref, alogt_ref, dsk_ref, nw_ref, exp_ref,
               y_ref, s_scr, y_scr):
    q = BLK
    gw = SSD_HEADS // SSD_GROUPS * SSD_HEAD_DIM
    sm = sm_ref[...]

    smt = sm.T
    lane = lax.broadcasted_iota(jnp.int32, (q, LANES), 1)
    row = lax.broadcasted_iota(jnp.int32, (q, LANES), 0)
    is_dt = (lane >= SM_DT) & (lane < SM_DT + SSD_HEADS) & (row < n_valid)
    is_dt_t = (row >= SM_DT) & (row < SM_DT + SSD_HEADS) & (lane < n_valid)
    dt = jnp.where(is_dt, _softplus(sm + dtb_ref[...]), 0.0)
    dtt = jnp.where(is_dt_t, _softplus(smt + dtbt_ref[...]), 0.0)
    da = dt * (-jnp.exp(alog_ref[...]))
    dat = dtt * (-jnp.exp(alogt_ref[...]))
    ii = lax.broadcasted_iota(jnp.int32, (q, q), 0)
    jj = lax.broadcasted_iota(jnp.int32, (q, q), 1)
    causal = jj <= ii
    acum = sum(jnp.dot(causal.astype(BF16), p, preferred_element_type=F32) for p in _split3(da))
    acumt = sum(jnp.dot(p, (ii <= jj).astype(BF16), preferred_element_type=F32) for p in _split3(dat))
    a_last = acum[q - 1:q, :]
    expand = exp_ref[...]
    stacked = jnp.concatenate([jnp.exp(acum), jnp.exp(a_last - acum) * dt,
                               jnp.broadcast_to(jnp.exp(a_last), (SUBLANES, LANES))], axis=0)
    stacked_x = sum(jnp.dot(p, expand, preferred_element_type=F32) for p in _split3(stacked))
    ea_x = stacked_x[0:q]
    wdt_x = stacked_x[q:2 * q]
    dec_x = stacked_x[2 * q:2 * q + 1]

    sq = None
    for g in range(SSD_GROUPS):
        gsl = slice(g * gw, (g + 1) * gw)
        bsl = slice(SSD_INNER + g * SSD_STATE, SSD_INNER + (g + 1) * SSD_STATE)
        csl = slice(SSD_INNER + SSD_GROUPS * SSD_STATE + g * SSD_STATE,
                    SSD_INNER + SSD_GROUPS * SSD_STATE + (g + 1) * SSD_STATE)
        bmf = xc_ref[:, bsl]
        bm = bmf.astype(BF16)
        cm = xc_ref[:, csl].astype(BF16)
        xg = xc_ref[:, gsl]
        xgb = xg.astype(BF16)
        cbm = lax.dot_general(cm, bm, (((1,), (1,)), ((), ())), preferred_element_type=F32)
        xw = (xg * wdt_x[:, gsl]).astype(BF16)
        st = jnp.dot(bmf.T.astype(BF16), xw, preferred_element_type=F32)
        s_in = s_scr[g]
        y_off = jnp.dot(cm, s_in.astype(BF16), preferred_element_type=F32) * ea_x[:, gsl]
        s_scr[g] = s_in * dec_x[:, gsl] + st
        for r in range(SSD_HEADS // SSD_GROUPS):
            h = g * (SSD_HEADS // SSD_GROUPS) + r
            seg = acum[:, SM_DT + h:SM_DT + h + 1] - acumt[SM_DT + h:SM_DT + h + 1, :]
            lmat = jnp.exp(jnp.where(causal, seg, -jnp.inf))
            wmat = (cbm * lmat * dtt[SM_DT + h:SM_DT + h + 1, :]).astype(BF16)
            rsl = slice(r * SSD_HEAD_DIM, (r + 1) * SSD_HEAD_DIM)
            hsl = slice(h * SSD_HEAD_DIM, (h + 1) * SSD_HEAD_DIM)
            y_diag = jnp.dot(wmat, xgb[:, rsl], preferred_element_type=F32)
            y_scr[:, hsl] = y_diag + y_off[:, rsl] + xg[:, rsl] * dsk_ref[:, hsl]
        yg = y_scr[:, gsl] * _silu(z_ref[:, gsl])
        y_scr[:, gsl] = yg
        sq = yg * yg if sq is None else sq + yg * yg

    ms = jnp.sum(sq, axis=-1, keepdims=True) * (1.0 / SSD_INNER)
    y_ref[...] = (y_scr[...] * lax.rsqrt(ms + EPS) * nw_ref[...]).astype(BF16)


def _ssd(lead, xbc, z, sm, ssm_prev, p, *, n_lead, t_x):
    b, tp, _ = xbc.shape
    lead_chunks = 1 if n_lead else 0
    assert n_lead % SUBLANES == 0 and n_lead <= BLK
    nc = tp // BLK + lead_chunks
    gw = SSD_HEADS // SSD_GROUPS * SSD_HEAD_DIM
    seq = lambda width: pl.BlockSpec((None, BLK, width), lambda i, c: (i, jnp.maximum(c - lead_chunks, 0), 0))
    kern = functools.partial(_ssd_kernel, n_chunks=nc, n_lead=n_lead, t_x=t_x)
    lead_specs = [_const_spec((BLK, CONV_DIM)), _const_spec((BLK, SSD_INNER)), _const_spec((BLK, LANES))]
    return pl.pallas_call(
        kern,
        grid=(b, nc),
        in_specs=(lead_specs if n_lead else []) + [
                  seq(CONV_DIM), seq(SSD_INNER), seq(LANES),
                  pl.BlockSpec((None, SSD_GROUPS, gw, SSD_STATE), lambda i, c: (i, 0, 0, 0)),
                  _const_spec((1, LANES)), _const_spec((LANES, 1)), _const_spec((1, LANES)), _const_spec((LANES, 1)),
                  _const_spec((1, SSD_INNER)), _const_spec((1, SSD_INNER)), _const_spec((LANES, SSD_INNER))],
        out_specs=[seq(SSD_INNER),
                   pl.BlockSpec((None, SSD_GROUPS, gw, SSD_STATE), lambda i, c: (i, 0, 0, 0))],
        out_shape=[jax.ShapeDtypeStruct((b, tp, SSD_INNER), BF16),
                   jax.ShapeDtypeStruct((b, SSD_GROUPS, gw, SSD_STATE), F32)],
        scratch_shapes=[pltpu.VMEM((SSD_GROUPS, SSD_STATE, gw), F32),
                        pltpu.VMEM((BLK, SSD_INNER), F32)],
        compiler_params=_cparams(("parallel", "arbitrary")),
        name="ssd",
    )(*(lead if n_lead else ()), xbc, z, sm, ssm_prev, p["dtb"], p["dtb_t"],
      p["alog"], p["alog_t"], p["dskip_x"], p["ssd_norm_w"], p["expand"])


N_SHIFT_TILES = 5
LEAD_TILE = N_SHIFT_TILES
N_BIAS_TILES = N_SHIFT_TILES + 1


def _log_bucket_starts():
    nb = REL_BUCKETS // 2
    max_exact = nb // 2
    s = nb - max_exact
    starts = []
    for m in range(1, s):
        n = max_exact
        while n ** s * max_exact ** m < max_exact ** s * REL_MAX_DIST ** m:
            n += 1
        starts.append(n)
    return starts


def _bias_kernel(rb_ref, bt_ref):
    nb = REL_BUCKETS // 2
    max_exact = nb // 2
    qq = lax.broadcasted_iota(jnp.int32, (BLK, BLK), 0)
    kk = lax.broadcasted_iota(jnp.int32, (BLK, BLK), 1)
    for u in range(N_BIAS_TILES):
        rel = kk - qq + ((u - 2) * BLK if u < N_SHIFT_TILES else -N_META)
        n = jnp.abs(rel)
        large = max_exact + sum(jnp.where(n >= start, 1, 0) for start in _log_bucket_starts())
        bucket = jnp.where(rel > 0, nb, 0) + jnp.where(n < max_exact, n, large)
        for h in range(N_HEADS):
            acc = jnp.zeros((BLK, BLK), F32)
            for bkt in range(REL_BUCKETS):
                acc = jnp.where(bucket == bkt, rb_ref[bkt, h], acc)
            bt_ref[h * N_BIAS_TILES + u] = acc


def _bias_tiles(rel_bias):
    return pl.pallas_call(
        _bias_kernel,
        in_specs=[pl.BlockSpec(memory_space=pltpu.SMEM)],
        out_specs=pl.BlockSpec(memory_space=pltpu.VMEM),
        out_shape=jax.ShapeDtypeStruct((N_HEADS * N_BIAS_TILES, BLK, BLK), F32),
        name="bias_tiles",
    )(rel_bias)


KB = 2 * BLK


def _attn_kernel(*refs, nkp_total, n_past, n_lead, chunk_off, t_x, n_sel):
    if n_past:
        pki_ref, pk_ref, pv_ref, *refs = refs
    if n_lead:
        lki_ref, lk_ref, lv_ref, *refs = refs
    (qn_ref, qi_ref, sm_ref, ki_new_ref, k_new_ref, v_new_ref, bt_ref, o_ref,
     ki_scr, k_scr, v_scr, st_scr, m_scr, lg_scr, qis_scr, qs_scr, mrun_scr, lrun_scr, acc_scr) = refs
    i = pl.program_id(1)
    r = BLK
    n_prefix = n_past + (BLK if n_lead else 0)
    gap = BLK - n_lead if n_lead else 0
    l_valid = n_prefix - gap + t_x

    @pl.when(i == 0)
    def _():
        if n_lead:
            ki_scr[0:BLK, :] = lki_ref[:, SM_KI:SM_KI + IDX_DIM]
            k_scr[0:BLK, :] = lk_ref[...]
            v_scr[0:BLK, :] = lv_ref[...]
        if n_past:
            def load_past(c, carry):
                rows = pl.ds(pl.multiple_of(c * KB, KB), KB)
                ki_scr[rows, :] = pki_ref[rows, :].astype(BF16)
                for g in range(N_KV_HEADS):
                    src = pl.ds(pl.multiple_of(c * KB * N_KV_HEADS, KB) + g, KB, stride=N_KV_HEADS)
                    k_scr[rows, g * HEAD_DIM:(g + 1) * HEAD_DIM] = pk_ref[src, :].astype(BF16)
                    v_scr[rows, g * HEAD_DIM:(g + 1) * HEAD_DIM] = pv_ref[src, :].astype(BF16)
                return carry

            lax.fori_loop(0, n_past // KB, load_past, 0)
        n_new = k_new_ref.shape[0]
        ki_scr[n_prefix:n_prefix + n_new, :] = ki_new_ref[:, SM_KI:SM_KI + IDX_DIM]
        k_scr[n_prefix:n_prefix + n_new, :] = k_new_ref[...]
        v_scr[n_prefix:n_prefix + n_new, :] = v_new_ref[...]
        n_tail = nkp_total * KB - n_prefix - n_new
        if n_tail:
            for scr in (ki_scr, k_scr, v_scr):
                scr[n_prefix + n_new:, :] = jnp.zeros((n_tail, scr.shape[1]), BF16)

    qb = n_prefix // BLK + i
    q0 = qb * BLK - gap
    n_keys = nkp_total * KB

    def chunk_end(pos):
        return jnp.minimum(CHUNK * ((pos + chunk_off) // CHUNK + 1) - chunk_off, l_valid)

    nkp = jnp.minimum(nkp_total, (chunk_end(q0 + BLK - 1) + gap + KB - 1) // KB)

    nt = (((1,), (1,)), ((), ()))
    wit = sm_ref[...].T
    n_adm = chunk_end(q0 + lax.broadcasted_iota(jnp.int32, (1, r), 1))
    krow = lax.broadcasted_iota(jnp.int32, (KB, r), 0)

    fold_rows = 8 * SUBLANES

    def fold(x, op):
        return op(x.reshape(KB // fold_rows, fold_rows, r), axis=0)

    def for_key_steps(body, init):
        carry = lax.fori_loop(0, nkp // 2, lambda t, c: body(2 * t + 1, body(2 * t, c)), init)
        return lax.cond(nkp % 2 == 1, lambda c: body(nkp - 1, c), lambda c: c, carry)

    for h in range(IDX_HEADS):
        qis_scr[h * r:(h + 1) * r, :] = qi_ref[:, h * IDX_DIM:(h + 1) * IDX_DIM]

    def score_body(jp, carry):
        mn, mx = carry
        kij = ki_scr[pl.ds(pl.multiple_of(jp * KB, KB), KB), :]
        acc = jnp.zeros((KB, r), F32)
        for hp in range(IDX_HEADS // 2):
            sh = lax.dot_general(kij, qis_scr[2 * hp * r:2 * (hp + 1) * r, :], nt, preferred_element_type=F32)
            for e in range(2):
                h = 2 * hp + e
                acc = acc + wit[SM_WI + h:SM_WI + h + 1, :] * jnp.maximum(sh[:, e * r:(e + 1) * r], 0.0)
        adm = krow < n_adm + gap - jp * KB
        if gap:
            adm = adm & ((jp > 0) | (krow < n_lead) | (krow >= BLK))
        s = jnp.where(adm, acc * (IDX_DIM ** -0.5), -jnp.inf)
        st_scr[jp] = s
        mn = jnp.minimum(mn, fold(jnp.where(adm, s, jnp.inf), jnp.min))
        mx = jnp.maximum(mx, fold(s, jnp.max))
        return mn, mx

    init = (jnp.full((fold_rows, r), jnp.inf, F32), jnp.full((fold_rows, r), -jnp.inf, F32))
    mn, mx = for_key_steps(score_body, init)
    lo0 = jnp.min(mn, axis=0, keepdims=True)
    hi0 = jnp.max(mx, axis=0, keepdims=True)
    kk = jnp.minimum(n_adm, n_sel).astype(F32)

    def count(pred):
        def body(jp, acc):
            return acc + fold(jnp.where(pred(st_scr[jp], jp), 1.0, 0.0), jnp.sum)
        acc = lax.fori_loop(0, nkp, body, jnp.zeros((fold_rows, r), F32))
        return jnp.sum(acc, axis=0, keepdims=True)

    def search_body(_, carry):
        lo, hi, c_lo, c_hi, ub, c_ub, hit = carry
        frac = jnp.clip((c_lo - kk) / (c_lo - c_hi), SEARCH_CLAMP, 1.0 - SEARCH_CLAMP)
        mid = jnp.where(hit > 0.0, lo, lo * (1.0 - frac) + hi * frac)
        cnt = count(lambda s, jp: s >= mid)
        ok = cnt >= kk
        now = cnt == kk
        lo = jnp.where(ok, mid, lo)
        hi = jnp.where(ok & ~now, hi, mid)
        c_lo = jnp.where(ok, cnt, c_lo)
        c_hi = jnp.where(now, kk - 1.0, jnp.where(ok, c_hi, cnt))
        ub = jnp.where(ok, ub, mid)
        c_ub = jnp.where(ok, c_ub, cnt)
        return lo, hi, c_lo, c_hi, ub, c_ub, jnp.where(now, 1.0, 0.0)

    all_adm = n_adm.astype(F32) == kk
    zeros = jnp.zeros((1, r), F32)
    init = (lo0, jnp.where(all_adm, lo0, hi0), n_adm.astype(F32), jnp.where(all_adm, kk - 1.0, zeros),
            jnp.full((1, r), jnp.inf, F32), zeros, jnp.where(all_adm, 1.0, 0.0))
    lo_f, _, _, _, ub, c_ub, hit_f = lax.fori_loop(0, SEARCH_STEPS, search_body, init)
    hit = hit_f > 0.0
    pending = jnp.sum(jnp.where(hit, 0, 1))
    take_all = jnp.full((1, r), float(n_keys), F32)

    def exact_path():
        def next_below(ub):
            def body(jp, acc):
                s = st_scr[jp]
                return jnp.maximum(acc, fold(jnp.where(s < ub, s, -jnp.inf), jnp.max))
            acc = lax.fori_loop(0, nkp, body, jnp.full((fold_rows, r), -jnp.inf, F32))
            return jnp.max(acc, axis=0, keepdims=True)

        def descend_cond(carry):
            *_, todo, it = carry
            return (todo > 0) & (it < n_keys)

        def descend_body(carry):
            ub, c_ub, _, _, _, it = carry
            t = next_below(ub)
            c_t = count(lambda s, jp: s >= t)
            done = hit | (c_t >= kk)
            return (jnp.where(done, ub, t), jnp.where(done, c_ub, c_t), t, c_t,
                    jnp.sum(jnp.where(done, 0, 1)), it + 1)

        _, c_gt, t, c_t, _, _ = lax.while_loop(descend_cond, descend_body,
                                               (ub, c_ub, lo0, zeros, jnp.int32(1), jnp.int32(0)))
        ties_wanted = jnp.where(hit, take_all, kk - c_gt)
        extra_ties = jnp.sum(jnp.where(~hit & (c_t - c_gt > ties_wanted), 1, 0))
        return jnp.where(hit, lo_f, t), ties_wanted, extra_ties

    thr, ties_wanted, extra_ties = lax.cond(pending > 0, exact_path, lambda: (lo_f, take_all, jnp.int32(0)))

    def mask_body(jp, carry):
        m_scr[jp] = jnp.where(st_scr[jp] >= thr, 0.0, NEG_BIG).T
        return carry

    def mask_with_ties():
        ii = lax.broadcasted_iota(jnp.int32, (KB, KB), 0)
        jj = lax.broadcasted_iota(jnp.int32, (KB, KB), 1)
        upto = (jj <= ii).astype(BF16)

        def body(jp, wanted):
            s = st_scr[jp]
            tied = s == thr
            rank = jnp.dot(upto, jnp.where(tied, 1.0, 0.0).astype(BF16), preferred_element_type=F32)
            m_scr[jp] = jnp.where((s > thr) | (tied & (rank <= wanted)), 0.0, NEG_BIG).T
            return wanted - rank[KB - 1:KB, :]

        lax.fori_loop(0, nkp, body, ties_wanted)
        return jnp.int32(0)

    lax.cond(extra_ties > 0, mask_with_ties, lambda: lax.fori_loop(0, nkp, mask_body, jnp.int32(0)))

    scale = HEAD_DIM ** -0.5
    rep = N_HEADS // N_KV_HEADS
    rq = mrun_scr.shape[1]
    for h in range(N_HEADS):
        qs_scr[h // rep, (h % rep) * rq:(h % rep + 1) * rq, :] = qn_ref[0:rq, h * HEAD_DIM:(h + 1) * HEAD_DIM]
    mrun_scr[...] = jnp.full(mrun_scr.shape, -jnp.inf, F32)
    lrun_scr[...] = jnp.zeros(lrun_scr.shape, F32)
    acc_scr[...] = jnp.zeros(acc_scr.shape, F32)

    def logit_body(jp, carry):
        u0 = jnp.clip(2 * jp - qb + 2, 0, N_SHIFT_TILES - 1)
        u1 = jnp.clip(2 * jp + 1 - qb + 2, 0, N_SHIFT_TILES - 1)
        if gap:
            u0 = jnp.where((jp == 0) & (qb == 1), LEAD_TILE, u0)
        madd = m_scr[jp][0:rq, :]
        for g in range(N_KV_HEADS):
            lt = lax.dot_general(qs_scr[g], k_scr[pl.ds(pl.multiple_of(jp * KB, KB), KB),
                                                  g * HEAD_DIM:(g + 1) * HEAD_DIM], nt,
                                 preferred_element_type=F32)
            for e in range(rep):
                h = g * rep + e
                bias = jnp.concatenate([bt_ref[h * N_BIAS_TILES + u0][0:rq, :],
                                        bt_ref[h * N_BIAS_TILES + u1][0:rq, :]], axis=1)
                lg = lt[e * rq:(e + 1) * rq, :] * scale + bias + madd
                lg_scr[h, jp] = lg
                mrun_scr[h] = jnp.maximum(mrun_scr[h], jnp.maximum(lg[:, :BLK], lg[:, BLK:]))
        return carry

    for_key_steps(logit_body, 0)
    for h in range(N_HEADS):
        mrun_scr[h] = jnp.broadcast_to(jnp.max(mrun_scr[h], axis=1, keepdims=True), (rq, BLK))

    def pv_body(jp, carry):
        for g in range(N_KV_HEADS):
            es = []
            for e in range(rep):
                h = g * rep + e
                mrow = mrun_scr[h]
                ex = jnp.exp(lg_scr[h, jp] - jnp.concatenate([mrow, mrow], axis=1))
                lrun_scr[h] = lrun_scr[h] + (ex[:, :BLK] + ex[:, BLK:])
                es.append(ex.astype(BF16))
            acc_scr[g] = acc_scr[g] + jnp.dot(jnp.concatenate(es, axis=0),
                                              v_scr[pl.ds(pl.multiple_of(jp * KB, KB), KB),
                                                    g * HEAD_DIM:(g + 1) * HEAD_DIM],
                                              preferred_element_type=F32)
        return carry

    for_key_steps(pv_body, 0)
    for h in range(N_HEADS):
        g, e = h // rep, h % rep
        den = jnp.sum(lrun_scr[h], axis=1, keepdims=True)
        o_ref[0:rq, h * HEAD_DIM:(h + 1) * HEAD_DIM] = (acc_scr[g, e * rq:(e + 1) * rq, :] / den).astype(BF16)
    if rq < r:
        o_ref[rq:, :] = jnp.zeros((r - rq, ATT_INNER), BF16)


def _attn(qn, qi, sm, ki_new, k_new, v_new, past, lead, bias_tiles, *, n_lead, chunk_off, t_x, n_sel):
    b, tq, _ = qn.shape
    n_past = past[0].shape[1] if past is not None else 0
    assert n_past % KB == 0 and not (n_past and n_lead)
    n_prefix = n_past + (BLK if n_lead else 0)
    n_keys = -(-(n_prefix + tq) // KB) * KB
    nkp_total = n_keys // KB
    rep = N_HEADS // N_KV_HEADS
    rq = BLK if tq > BLK else min(BLK, -(-t_x // (2 * SUBLANES)) * 2 * SUBLANES)
    seq = lambda width: pl.BlockSpec((None, BLK, width), lambda bi, i: (bi, i, 0))
    rows = lambda n, width: pl.BlockSpec((None, n, width), lambda bi, i: (bi, 0, 0))
    kern = functools.partial(_attn_kernel, nkp_total=nkp_total, n_past=n_past, n_lead=n_lead, chunk_off=chunk_off,
                             t_x=t_x, n_sel=n_sel)
    past_specs = [rows(n_past, IDX_DIM), rows(n_past * N_KV_HEADS, HEAD_DIM),
                  rows(n_past * N_KV_HEADS, HEAD_DIM)] if n_past else []
    lead_specs = [_const_spec((BLK, LANES)), _const_spec((BLK, KV_DIM)), _const_spec((BLK, KV_DIM))] if n_lead else []
    return pl.pallas_call(
        kern,
        grid=(b, tq // BLK),
        in_specs=past_specs + lead_specs + [seq(ATT_INNER), seq(IDX_HEADS * IDX_DIM), seq(LANES),
                                            rows(tq, LANES), rows(tq, KV_DIM), rows(tq, KV_DIM),
                                            _const_spec((N_HEADS * N_BIAS_TILES, BLK, BLK))],
        out_specs=seq(ATT_INNER),
        out_shape=jax.ShapeDtypeStruct((b, tq, ATT_INNER), BF16),
        scratch_shapes=[pltpu.VMEM((n_keys, IDX_DIM), BF16),
                        pltpu.VMEM((n_keys, KV_DIM), BF16),
                        pltpu.VMEM((n_keys, KV_DIM), BF16),
                        pltpu.VMEM((nkp_total, KB, BLK), F32),
                        pltpu.VMEM((nkp_total, BLK, KB), F32),
                        pltpu.VMEM((N_HEADS, nkp_total, rq, KB), F32),
                        pltpu.VMEM((IDX_HEADS * BLK, IDX_DIM), BF16),
                        pltpu.VMEM((N_KV_HEADS, rep * rq, HEAD_DIM), BF16),
                        pltpu.VMEM((N_HEADS, rq, BLK), F32),
                        pltpu.VMEM((N_HEADS, rq, BLK), F32),
                        pltpu.VMEM((N_KV_HEADS, rep * rq, HEAD_DIM), F32)],
        compiler_params=_cparams(("parallel", "arbitrary")),
        name="attn",
    )(*(past or ()), *(lead or ()), qn, qi, sm, ki_new, k_new, v_new, bias_tiles)


def _out_ffn_kernel(x_ref, ys_ref, ya_ref, gs_ref, ga_ref, wbs_ref, wba_ref, wo_ref, n2_ref, wg_ref, wu_ref, wd_ref,
                    y_ref):
    dot = functools.partial(jnp.dot, preferred_element_type=F32)
    merged = (jax.nn.sigmoid(gs_ref[...]) * dot(ys_ref[...], wbs_ref[...])
              + jax.nn.sigmoid(ga_ref[...]) * dot(ya_ref[...], wba_ref[...]))
    h = x_ref[...] + dot(merged.astype(BF16), wo_ref[...])
    hn = _rms(h, n2_ref[...]).astype(BF16)
    act = (_silu(dot(hn, wg_ref[...])) * dot(hn, wu_ref[...])).astype(BF16)
    y_ref[...] = h + dot(act, wd_ref[...])


def _out_ffn(x2d, ys, ya, gs, ga, p):
    n = x2d.shape[0]
    tm = _row_tile(n)
    d_ff = p["w_gate"].shape[1]
    row = lambda width: pl.BlockSpec((tm, width), lambda i: (i, 0))
    wspec = lambda shape: pl.BlockSpec(shape, lambda i: (0, 0), pipeline_mode=pl.Buffered(1))
    return pl.pallas_call(
        _out_ffn_kernel,
        grid=(n // tm,),
        in_specs=[row(D_MODEL), row(SSD_INNER), row(ATT_INNER), row(D_MODEL), row(D_MODEL),
                  wspec((SSD_INNER, D_MODEL)), wspec((ATT_INNER, D_MODEL)), wspec((D_MODEL, D_MODEL)),
                  _const_spec((1, D_MODEL)), wspec((D_MODEL, d_ff)), wspec((D_MODEL, d_ff)), wspec((d_ff, D_MODEL))],
        out_specs=row(D_MODEL),
        out_shape=jax.ShapeDtypeStruct((n, D_MODEL), F32),
        compiler_params=_cparams(("parallel",)),
        name="out_ffn",
    )(x2d, ys, ya, gs, ga, p["w_br_ssd"], p["w_br_att"], p["w_out"], p["norm2_w"], p["w_gate"], p["w_up"],
      p["w_down"])


def _layer(x, lead_rows, conv_prev, ssm_prev, past, p, bias_tiles, *, chunk_off, n_sel):
    b, t, _ = x.shape
    n_lead = lead_rows.shape[0]
    tp = -(-t // BLK) * BLK
    x2d = jnp.pad(x, ((0, 0), (0, tp - t), (0, 0))).reshape(b * tp, D_MODEL)
    hist = jnp.pad(conv_prev.astype(F32), ((0, 0), (HALO - (CONV_WIDTH - 1), 0), (0, 0)))
    seq = lambda a: a.reshape(b, tp, a.shape[-1])
    ssd_lead = attn_lead = None
    if n_lead:
        lead = _in_proj(jnp.pad(lead_rows.astype(x.dtype), ((0, BLK - n_lead), (0, 0))), hist[:1], p,
                        n_seq=1, t_x=n_lead)
        lz, lxbc, _, lk32, lv32, lkb, lvb, _, lsm, lsmb, _, _, hist = lead
        ssd_lead, attn_lead = (lxbc, lz, lsm), (lsmb, lkb, lvb)
    z, xbc, qn, k32, v32, kb, vb, qi, sm, smb, gs, ga, conv_new8 = _in_proj(x2d, hist, p, n_seq=b, t_x=t)

    gw = SSD_HEADS // SSD_GROUPS * SSD_HEAD_DIM
    y_ssd, ssm_new = _ssd(ssd_lead, seq(xbc), seq(z), seq(sm),
                          ssm_prev.astype(F32).reshape(b, SSD_GROUPS, gw, SSD_STATE), p, n_lead=n_lead, t_x=t)

    if past is not None:
        pk, pv, pki = past
        n_past = pk.shape[1]
        past = (pki.astype(F32), pk.astype(F32).reshape(b, n_past * N_KV_HEADS, HEAD_DIM),
                pv.astype(F32).reshape(b, n_past * N_KV_HEADS, HEAD_DIM))
    y_att = _attn(seq(qn), seq(qi), seq(sm), seq(smb), seq(kb), seq(vb), past, attn_lead, bias_tiles,
                  n_lead=n_lead, chunk_off=chunk_off, t_x=t, n_sel=n_sel)

    y = _out_ffn(x2d, y_ssd.reshape(b * tp, SSD_INNER), y_att.reshape(b * tp, ATT_INNER), gs, ga, p)

    def with_lead(new, lead_part):
        if not n_lead:
            return new
        return jnp.concatenate([jnp.broadcast_to(lead_part[None, :n_lead], (b, n_lead) + new.shape[2:]), new], axis=1)

    heads = lambda a, rows: a.reshape(-1, rows, N_KV_HEADS, HEAD_DIM)
    k_new = with_lead(heads(k32, tp)[:, :t], heads(lk32, BLK)[0] if n_lead else None)
    v_new = with_lead(heads(v32, tp)[:, :t], heads(lv32, BLK)[0] if n_lead else None)
    ki_new = with_lead(seq(sm)[:, :t, SM_KI:SM_KI + IDX_DIM], lsm[:, SM_KI:SM_KI + IDX_DIM] if n_lead else None)
    ssm_new = ssm_new.reshape(b, SSD_HEADS, SSD_HEAD_DIM, SSD_STATE)
    conv_new = conv_new8[:, HALO - (CONV_WIDTH - 1):]
    return y.reshape(b, tp, D_MODEL)[:, :t], k_new, v_new, ki_new, ssm_new, conv_new


def _prepare_params(l, norm1_w, w_in, conv_w, conv_b, dt_bias, a_log, d_skip, ssd_norm_w, q_norm_w, k_norm_w,
                    idx_k_norm_w, w_br_ssd, w_br_att, w_out, norm2_w, w_gate, w_up, w_down):
    offs = [0]
    for w in IN_WIDTHS:
        offs.append(offs[-1] + w)
    seg = lambda i: w_in[l][:, offs[i]:offs[i + 1]]
    i_z, i_xbc, i_dt, i_q, i_k, i_v, i_qi, i_ki, i_wi, i_gs, i_ga = range(11)
    pad = jnp.zeros((D_MODEL, LANES - IDX_DIM - SSD_HEADS - IDX_HEADS), w_in.dtype)
    w_perm = jnp.concatenate([seg(i_z), seg(i_xbc), seg(i_q), seg(i_k), seg(i_v), seg(i_qi), seg(i_gs), seg(i_ga),
                              seg(i_ki), seg(i_dt), seg(i_wi), pad], axis=1).astype(BF16)

    def lanes_at(vec, start):
        return jnp.zeros((1, LANES), F32).at[0, start:start + vec.shape[0]].set(vec.astype(F32))

    dtb = lanes_at(dt_bias[l], SM_DT)
    alog = lanes_at(a_log[l], SM_DT)
    head_of_channel = jnp.arange(SSD_INNER) // SSD_HEAD_DIM
    expand = (jnp.arange(LANES)[:, None] == head_of_channel[None, :] + SM_DT).astype(BF16)
    row = lambda v: v.astype(F32).reshape(1, -1)
    return dict(
        norm1_w=row(norm1_w[l]), w_in=w_perm, conv_w=conv_w[l].astype(F32), conv_b=row(conv_b[l]),
        dtb=dtb, dtb_t=dtb.reshape(LANES, 1), alog=alog, alog_t=alog.reshape(LANES, 1),
        dskip_x=row(jnp.repeat(d_skip[l], SSD_HEAD_DIM)), ssd_norm_w=row(ssd_norm_w[l]), expand=expand,
        q_norm_w=row(q_norm_w[l]), k_norm_w=row(k_norm_w[l]),
        idx_k_norm_w=jnp.ones((1, LANES), F32).at[0, SM_KI:SM_KI + IDX_DIM].set(idx_k_norm_w[l].astype(F32)),
        w_br_ssd=w_br_ssd[l].astype(BF16), w_br_att=w_br_att[l].astype(BF16), w_out=w_out[l].astype(BF16),
        norm2_w=row(norm2_w[l]), w_gate=w_gate[l].astype(BF16), w_up=w_up[l].astype(BF16),
        w_down=w_down[l].astype(BF16))


def kernel(x_prompt, x_sample, cache_k, cache_v, cache_kidx, state_ssm, state_conv, meta_tokens, rel_bias, norm1_w,
           w_in, conv_w, conv_b, dt_bias, a_log, d_skip, ssd_norm_w, q_norm_w, k_norm_w, idx_k_norm_w, w_br_ssd,
           w_br_att, w_out, norm2_w, w_gate, w_up, w_down):
    bp, sp, _ = x_prompt.shape
    bs, ts, _ = x_sample.shape
    past = cache_k.shape[2]
    assert w_in.shape[0] == 1
    assert past % BLK == 0 and BLK % CHUNK == 0 and N_META <= CHUNK
    l = 0

    n_sel_p = min(TOPK_MAX, sp // 4)
    n_sel_s = min(TOPK_MAX, (past + ts) // 4)
    conv0 = jnp.zeros((bp, CONV_WIDTH - 1, CONV_DIM), F32)
    ssm0 = jnp.zeros((bp, SSD_HEADS, SSD_HEAD_DIM, SSD_STATE), F32)
    bias_tiles = _bias_tiles(rel_bias.astype(F32))
    p = _prepare_params(l, norm1_w, w_in, conv_w, conv_b, dt_bias, a_log, d_skip, ssd_norm_w, q_norm_w, k_norm_w,
                        idx_k_norm_w, w_br_ssd, w_br_att, w_out, norm2_w, w_gate, w_up, w_down)
    y_prompt, *rest_p = _layer(x_prompt, meta_tokens, conv0, ssm0, None, p, bias_tiles,
                               chunk_off=CHUNK - N_META, n_sel=n_sel_p)
    y_sample, *rest_s = _layer(x_sample, meta_tokens[:0], state_conv[l], state_ssm[l],
                               (cache_k[l], cache_v[l], cache_kidx[l]), p, bias_tiles,
                               chunk_off=0, n_sel=n_sel_s)

    dtypes = (x_prompt.dtype, x_prompt.dtype, x_prompt.dtype, state_ssm.dtype, x_prompt.dtype)
    return (y_prompt, y_sample, *(o[None].astype(dt) for o, dt in zip(rest_p, dtypes)),
            *(o[None].astype(dt) for o, dt in zip(rest_s, dtypes)))
```

```python
import functools

import jax
import jax.numpy as jnp
from jax import lax
from jax.experimental import pallas as pl
from jax.experimental.pallas import tpu as pltpu

F32 = jnp.float32
BF16 = jnp.bfloat16

D_MODEL = 1024
CHUNK = 64
N_META = 16
SSD_HEADS = 16
SSD_HEAD_DIM = 64
SSD_INNER = SSD_HEADS * SSD_HEAD_DIM
SSD_GROUPS = 4
SSD_STATE = 128
CONV_WIDTH = 4
CONV_DIM = SSD_INNER + 2 * SSD_GROUPS * SSD_STATE
N_HEADS = 8
N_KV_HEADS = 2
HEAD_DIM = 128
ATT_INNER = N_HEADS * HEAD_DIM
KV_DIM = N_KV_HEADS * HEAD_DIM
IDX_HEADS = 8
IDX_DIM = 64
TOPK_MAX = 256
REL_BUCKETS = 32
REL_MAX_DIST = 128
IN_WIDTHS = (SSD_INNER, CONV_DIM, SSD_HEADS, ATT_INNER, KV_DIM, KV_DIM, IDX_HEADS * IDX_DIM, IDX_DIM, IDX_HEADS,
             D_MODEL, D_MODEL)
EPS = 1e-6

LANES = 128
SUBLANES = 8
VMEM_LIMIT_BYTES = 56 * 1024 * 1024

BLK = LANES
ROW_TILE = 256
HALO = SUBLANES

C_Z = 0
C_XBC = C_Z + SSD_INNER
C_Q = C_XBC + CONV_DIM
C_K = C_Q + ATT_INNER
C_V = C_K + KV_DIM
C_QI = C_V + KV_DIM
C_GS = C_QI + IDX_HEADS * IDX_DIM
C_GA = C_GS + D_MODEL
C_SM = C_GA + D_MODEL
IN_PAD = C_SM + LANES
SM_KI = 0
SM_DT = SM_KI + IDX_DIM
SM_WI = SM_DT + SSD_HEADS

SEARCH_STEPS = 16
SEARCH_CLAMP = 1.0 / 16
NEG_BIG = -1e30


def _cparams(sem):
    return pltpu.CompilerParams(dimension_semantics=sem, vmem_limit_bytes=VMEM_LIMIT_BYTES)


def _const_spec(shape):
    nd = len(shape)
    return pl.BlockSpec(shape, lambda *_: (0,) * nd)


def _rms(x, w):
    return x * lax.rsqrt(jnp.mean(x * x, axis=-1, keepdims=True) + EPS) * w


def _silu(x):
    return x * jax.nn.sigmoid(x)


def _in_proj_kernel(x_ref, hist_ref, n1_ref, w_ref, qn_ref, kn_ref, kin_ref, cw_ref, cb_ref,
                    z_ref, xc_ref, q_ref, k_ref, v_ref, kb_ref, vb_ref, qi_ref, sm_ref, smb_ref, gs_ref, ga_ref, tail_ref,
                    xpad_scr, *, tiles_per_seq, t_x):
    j = pl.program_id(1)
    hn = _rms(x_ref[...], n1_ref[...]).astype(BF16)
    tm = x_ref.shape[0]

    def mm(lo, hi):
        return jnp.dot(hn, w_ref[:, lo:hi], preferred_element_type=F32)

    @pl.when(j == 0)
    def _():
        xpad_scr[0:HALO, :] = hist_ref[...]

    step = 2 * LANES
    conv_slabs = iter(range(0, CONV_DIM, LANES))

    def conv_next():
        c0 = next(conv_slabs, None)
        if c0 is None:
            return
        if c0 % step == 0:
            xpad_scr[HALO:HALO + tm, c0:c0 + step] = mm(C_XBC + c0, C_XBC + c0 + step)
        sl = slice(c0, c0 + LANES)
        acc = xpad_scr[HALO - 3:HALO - 3 + tm, sl] * cw_ref[0:1, sl]
        for i in range(1, CONV_WIDTH):
            acc = acc + xpad_scr[HALO - 3 + i:HALO - 3 + i + tm, sl] * cw_ref[i:i + 1, sl]
        xc_ref[:, sl] = _silu(cb_ref[:, sl] + acc)

    def project(lo, hi, out_ref, dtype=F32):
        for c0 in range(0, hi - lo, step):
            out_ref[:, c0:c0 + step] = mm(lo + c0, lo + c0 + step).astype(dtype)
            conv_next()

    project(C_Z, C_XBC, z_ref)
    for c0 in range(0, ATT_INNER, step):
        q = mm(C_Q + c0, C_Q + c0 + step)
        conv_next()
        for h in range(step // HEAD_DIM):
            sl = slice(h * HEAD_DIM, (h + 1) * HEAD_DIM)
            q_ref[:, c0 + h * HEAD_DIM:c0 + (h + 1) * HEAD_DIM] = _rms(q[:, sl], qn_ref[...]).astype(BF16)
    k = mm(C_K, C_V)
    conv_next()
    v = mm(C_V, C_QI)
    conv_next()
    for h in range(N_KV_HEADS):
        sl = slice(h * HEAD_DIM, (h + 1) * HEAD_DIM)
        kh = _rms(k[:, sl], kn_ref[...])
        k_ref[pl.ds(h, tm, stride=N_KV_HEADS), :] = kh
        kb_ref[:, sl] = kh.astype(BF16)
        v_ref[pl.ds(h, tm, stride=N_KV_HEADS), :] = v[:, sl]
    vb_ref[...] = v.astype(BF16)
    project(C_QI, C_GS, qi_ref, BF16)
    project(C_GS, C_GA, gs_ref)
    project(C_GA, C_SM, ga_ref)
    sm = mm(C_SM, IN_PAD)
    assert next(conv_slabs, None) is None
    lane = lax.broadcasted_iota(jnp.int32, sm.shape, 1)
    is_ki = lane < SM_KI + IDX_DIM
    ms = jnp.sum(jnp.where(is_ki, sm * sm, 0.0), axis=-1, keepdims=True) * (1.0 / IDX_DIM)
    ki = sm * lax.rsqrt(ms + EPS) * kin_ref[...]
    is_wi = (lane >= SM_WI) & (lane < SM_WI + IDX_HEADS)
    out = jnp.where(is_ki, ki, jnp.where(is_wi, sm * (IDX_HEADS ** -0.5), sm))
    sm_ref[...] = out
    smb_ref[...] = out.astype(BF16)

    n_last = t_x - (tiles_per_seq - 1) * tm

    @pl.when(j == tiles_per_seq - 1)
    def _():
        tail_ref[...] = xpad_scr[n_last:n_last + HALO, :]

    xpad_scr[0:HALO, :] = xpad_scr[tm:tm + HALO, :]


def _row_tile(n):
    return ROW_TILE if n % ROW_TILE == 0 else BLK


def _in_proj(x2d, hist, p, *, n_seq, t_x):
    n = x2d.shape[0]
    tp = n // n_seq
    tm = _row_tile(tp)
    tiles = tp // tm
    shared_hist = hist.shape[0] == 1
    row = lambda width, per_token=1: pl.BlockSpec((tm * per_token, width), lambda s, j: (s * tiles + j, 0))
    per_seq = pl.BlockSpec((None, HALO, CONV_DIM), lambda s, j: (0 if shared_hist else s, 0, 0))
    outs = [
        (SSD_INNER, F32, 1), (CONV_DIM, F32, 1), (ATT_INNER, BF16, 1), (HEAD_DIM, F32, N_KV_HEADS),
        (HEAD_DIM, F32, N_KV_HEADS), (KV_DIM, BF16, 1), (KV_DIM, BF16, 1), (IDX_HEADS * IDX_DIM, BF16, 1),
        (LANES, F32, 1), (LANES, BF16, 1), (D_MODEL, F32, 1), (D_MODEL, F32, 1),
    ]
    kern = functools.partial(_in_proj_kernel, tiles_per_seq=tiles, t_x=t_x)
    return pl.pallas_call(
        kern,
        grid=(n_seq, tiles),
        in_specs=[row(D_MODEL), per_seq, _const_spec((1, D_MODEL)),
                  pl.BlockSpec((D_MODEL, IN_PAD), lambda s, j: (0, 0), pipeline_mode=pl.Buffered(1)),
                  _const_spec((1, HEAD_DIM)), _const_spec((1, HEAD_DIM)), _const_spec((1, LANES)),
                  _const_spec((CONV_WIDTH, CONV_DIM)), _const_spec((1, CONV_DIM))],
        out_specs=[row(w, per) for w, _, per in outs]
        + [pl.BlockSpec((None, HALO, CONV_DIM), lambda s, j: (s, 0, 0))],
        out_shape=[jax.ShapeDtypeStruct((n * per, w), dt) for w, dt, per in outs]
        + [jax.ShapeDtypeStruct((n_seq, HALO, CONV_DIM), F32)],
        scratch_shapes=[pltpu.VMEM((tm + 2 * HALO, CONV_DIM), F32)],
        compiler_params=_cparams(("parallel", "arbitrary")),
        name="in_proj",
    )(x2d, hist, p["norm1_w"], p["w_in"], p["q_norm_w"], p["k_norm_w"], p["idx_k_norm_w"], p["conv_w"], p["conv_b"])


def _softplus(x):
    return jnp.maximum(x, 0.0) + jnp.log1p(jnp.exp(-jnp.abs(x)))


def _split3(x):
    hi = x.astype(BF16)
    r1 = x - hi.astype(F32)
    mid = r1.astype(BF16)
    lo = (r1 - mid.astype(F32)).astype(BF16)
    return hi, mid, lo


def _ssd_kernel(*refs, n_chunks, n_lead, t_x):
    if n_lead:
        xbc_lead_ref, z_lead_ref, sm_lead_ref, *refs = refs
    (xbc_ref, z_ref, sm_ref, sprev_ref, dtb_ref, dtbt_ref, alog_ref, alogt_ref,
     dsk_ref, nw_ref, exp_ref, y_ref, snew_ref, s_scr, y_scr) = refs
    c = pl.program_id(1)
    lead_chunks = 1 if n_lead else 0

    @pl.when(c == 0)
    def _():
        for g in range(SSD_GROUPS):
            s_scr[g] = sprev_ref[g].T

    def chunk(xc_ref, z_src_ref, sm_src_ref, n_valid):
        _ssd_chunk(xc_ref, z_src_ref, sm_src_ref, n_valid, dtb_ref, dtbt_ref, alog_ref, alogt_ref, dsk_ref, nw_ref,
                   exp_ref, y_ref, s_scr, y_scr)

    if n_lead:
        pl.when(c == 0)(lambda: chunk(xbc_lead_ref, z_lead_ref, sm_lead_ref, n_lead))
        pl.when(c > 0)(lambda: chunk(xbc_ref, z_ref, sm_ref, jnp.minimum(BLK, t_x - (c - lead_chunks) * BLK)))
    else:
        chunk(xbc_ref, z_ref, sm_ref, jnp.minimum(BLK, t_x - c * BLK))

    @pl.when(c == n_chunks - 1)
    def _():
        for g in range(SSD_GROUPS):
            snew_ref[g] = s_scr[g].T


def _ssd_chunk(xc_ref, z_ref, sm_ref, n_valid, dtb_ref, dtbt_ref, alog_ref, alogt_ref, dsk_ref, nw_ref, exp_ref,
               y_ref, s_scr, y_scr):
    q = BLK
    gw = SSD_HEADS // SSD_GROUPS * SSD_HEAD_DIM
    sm = sm_ref[...]

    smt = sm.T
    lane = lax.broadcasted_iota(jnp.int32, (q, LANES), 1)
    row = lax.broadcasted_iota(jnp.int32, (q, LANES), 0)
    is_dt = (lane >= SM_DT) & (lane < SM_DT + SSD_HEADS) & (row < n_valid)
    is_dt_t = (row >= SM_DT) & (row < SM_DT + SSD_HEADS) & (lane < n_valid)
    dt = jnp.where(is_dt, _softplus(sm + dtb_ref[...]), 0.0)
    dtt = jnp.where(is_dt_t, _softplus(smt + dtbt_ref[...]), 0.0)
    da = dt * (-jnp.exp(alog_ref[...]))
    dat = dtt * (-jnp.exp(alogt_ref[...]))
    ii = lax.broadcasted_iota(jnp.int32, (q, q), 0)
    jj = lax.broadcasted_iota(jnp.int32, (q, q), 1)
    causal = jj <= ii
    acum = sum(jnp.dot(causal.astype(BF16), p, preferred_element_type=F32) for p in _split3(da))
    acumt = sum(jnp.dot(p, (ii <= jj).astype(BF16), preferred_element_type=F32) for p in _split3(dat))
    a_last = acum[q - 1:q, :]
    expand = exp_ref[...]
    stacked = jnp.concatenate([jnp.exp(acum), jnp.exp(a_last - acum) * dt,
                               jnp.broadcast_to(jnp.exp(a_last), (SUBLANES, LANES))], axis=0)
    stacked_x = sum(jnp.dot(p, expand, preferred_element_type=F32) for p in _split3(stacked))
    ea_x = stacked_x[0:q]
    wdt_x = stacked_x[q:2 * q]
    dec_x = stacked_x[2 * q:2 * q + 1]

    sq = None
    for g in range(SSD_GROUPS):
        gsl = slice(g * gw, (g + 1) * gw)
        bsl = slice(SSD_INNER + g * SSD_STATE, SSD_INNER + (g + 1) * SSD_STATE)
        csl = slice(SSD_INNER + SSD_GROUPS * SSD_STATE + g * SSD_STATE,
                    SSD_INNER + SSD_GROUPS * SSD_STATE + (g + 1) * SSD_STATE)
        bmf = xc_ref[:, bsl]
        bm = bmf.astype(BF16)
        cm = xc_ref[:, csl].astype(BF16)
        xg = xc_ref[:, gsl]
        xgb = xg.astype(BF16)
        cbm = lax.dot_general(cm, bm, (((1,), (1,)), ((), ())), preferred_element_type=F32)
        xw = (xg * wdt_x[:, gsl]).astype(BF16)
        st = jnp.dot(bmf.T.astype(BF16), xw, preferred_element_type=F32)
        s_in = s_scr[g]
        y_off = jnp.dot(cm, s_in.astype(BF16), preferred_element_type=F32) * ea_x[:, gsl]
        s_scr[g] = s_in * dec_x[:, gsl] + st
        for r in range(SSD_HEADS // SSD_GROUPS):
            h = g * (SSD_HEADS // SSD_GROUPS) + r
            seg = acum[:, SM_DT + h:SM_DT + h + 1] - acumt[SM_DT + h:SM_DT + h + 1, :]
            lmat = jnp.exp(jnp.where(causal, seg, -jnp.inf))
            wmat = (cbm * lmat * dtt[SM_DT + h:SM_DT + h + 1, :]).astype(BF16)
            rsl = slice(r * SSD_HEAD_DIM, (r + 1) * SSD_HEAD_DIM)
            hsl = slice(h * SSD_HEAD_DIM, (h + 1) * SSD_HEAD_DIM)
            y_diag = jnp.dot(wmat, xgb[:, rsl], preferred_element_type=F32)
            y_scr[:, hsl] = y_diag + y_off[:, rsl] + xg[:, rsl] * dsk_ref[:, hsl]
        yg = y_scr[:, gsl] * _silu(z_ref[:, gsl])
        y_scr[:, gsl] = yg
        sq = yg * yg if sq is None else sq + yg * yg

    ms = jnp.sum(sq, axis=-1, keepdims=True) * (1.0 / SSD_INNER)
    y_ref[...] = (y_scr[...] * lax.rsqrt(ms + EPS) * nw_ref[...]).astype(BF16)


def _ssd(lead, xbc, z, sm, ssm_prev, p, *, n_lead, t_x):
    b, tp, _ = xbc.shape
    lead_chunks = 1 if n_lead else 0
    assert n_lead % SUBLANES == 0 and n_lead <= BLK
    nc = tp // BLK + lead_chunks
    gw = SSD_HEADS // SSD_GROUPS * SSD_HEAD_DIM
    seq = lambda width: pl.BlockSpec((None, BLK, width), lambda i, c: (i, jnp.maximum(c - lead_chunks, 0), 0))
    kern = functools.partial(_ssd_kernel, n_chunks=nc, n_lead=n_lead, t_x=t_x)
    lead_specs = [_const_spec((BLK, CONV_DIM)), _const_spec((BLK, SSD_INNER)), _const_spec((BLK, LANES))]
    return pl.pallas_call(
        kern,
        grid=(b, nc),
        in_specs=(lead_specs if n_lead else []) + [
                  seq(CONV_DIM), seq(SSD_INNER), seq(LANES),
                  pl.BlockSpec((None, SSD_GROUPS, gw, SSD_STATE), lambda i, c: (i, 0, 0, 0)),
                  _const_spec((1, LANES)), _const_spec((LANES, 1)), _const_spec((1, LANES)), _const_spec((LANES, 1)),
                  _const_spec((1, SSD_INNER)), _const_spec((1, SSD_INNER)), _const_spec((LANES, SSD_INNER))],
        out_specs=[seq(SSD_INNER),
                   pl.BlockSpec((None, SSD_GROUPS, gw, SSD_STATE), lambda i, c: (i, 0, 0, 0))],
        out_shape=[jax.ShapeDtypeStruct((b, tp, SSD_INNER), BF16),
                   jax.ShapeDtypeStruct((b, SSD_GROUPS, gw, SSD_STATE), F32)],
        scratch_shapes=[pltpu.VMEM((SSD_GROUPS, SSD_STATE, gw), F32),
                        pltpu.VMEM((BLK, SSD_INNER), F32)],
        compiler_params=_cparams(("parallel", "arbitrary")),
        name="ssd",
    )(*(lead if n_lead else ()), xbc, z, sm, ssm_prev, p["dtb"], p["dtb_t"],
      p["alog"], p["alog_t"], p["dskip_x"], p["ssd_norm_w"], p["expand"])


N_SHIFT_TILES = 5
LEAD_TILE = N_SHIFT_TILES
N_BIAS_TILES = N_SHIFT_TILES + 1


def _log_bucket_starts():
    nb = REL_BUCKETS // 2
    max_exact = nb // 2
    s = nb - max_exact
    starts = []
    for m in range(1, s):
        n = max_exact
        while n ** s * max_exact ** m < max_exact ** s * REL_MAX_DIST ** m:
            n += 1
        starts.append(n)
    return starts


def _bias_kernel(rb_ref, bt_ref):
    nb = REL_BUCKETS // 2
    max_exact = nb // 2
    qq = lax.broadcasted_iota(jnp.int32, (BLK, BLK), 0)
    kk = lax.broadcasted_iota(jnp.int32, (BLK, BLK), 1)
    for u in range(N_BIAS_TILES):
        rel = kk - qq + ((u - 2) * BLK if u < N_SHIFT_TILES else -N_META)
        n = jnp.abs(rel)
        large = max_exact + sum(jnp.where(n >= start, 1, 0) for start in _log_bucket_starts())
        bucket = jnp.where(rel > 0, nb, 0) + jnp.where(n < max_exact, n, large)
        for h in range(N_HEADS):
            acc = jnp.zeros((BLK, BLK), F32)
            for bkt in range(REL_BUCKETS):
                acc = jnp.where(bucket == bkt, rb_ref[bkt, h], acc)
            bt_ref[h * N_BIAS_TILES + u] = acc


def _bias_tiles(rel_bias):
    return pl.pallas_call(
        _bias_kernel,
        in_specs=[pl.BlockSpec(memory_space=pltpu.SMEM)],
        out_specs=pl.BlockSpec(memory_space=pltpu.VMEM),
        out_shape=jax.ShapeDtypeStruct((N_HEADS * N_BIAS_TILES, BLK, BLK), F32),
        name="bias_tiles",
    )(rel_bias)


KB = 2 * BLK


def _attn_kernel(*refs, nkp_total, n_past, n_lead, chunk_off, t_x, n_sel):
    if n_past:
        pki_ref, pk_ref, pv_ref, *refs = refs
    if n_lead:
        lki_ref, lk_ref, lv_ref, *refs = refs
    (qn_ref, qi_ref, sm_ref, ki_new_ref, k_new_ref, v_new_ref, bt_ref, o_ref,
     ki_scr, k_scr, v_scr, st_scr, m_scr, lg_scr, qis_scr, qs_scr, mrun_scr, lrun_scr, acc_scr) = refs
    i = pl.program_id(1)
    r = BLK
    n_prefix = n_past + (BLK if n_lead else 0)
    gap = BLK - n_lead if n_lead else 0
    l_valid = n_prefix - gap + t_x

    @pl.when(i == 0)
    def _():
        if n_lead:
            ki_scr[0:BLK, :] = lki_ref[:, SM_KI:SM_KI + IDX_DIM]
            k_scr[0:BLK, :] = lk_ref[...]
            v_scr[0:BLK, :] = lv_ref[...]
        if n_past:
            def load_past(c, carry):
                rows = pl.ds(pl.multiple_of(c * KB, KB), KB)
                ki_scr[rows, :] = pki_ref[rows, :].astype(BF16)
                for g in range(N_KV_HEADS):
                    src = pl.ds(pl.multiple_of(c * KB * N_KV_HEADS, KB) + g, KB, stride=N_KV_HEADS)
                    k_scr[rows, g * HEAD_DIM:(g + 1) * HEAD_DIM] = pk_ref[src, :].astype(BF16)
                    v_scr[rows, g * HEAD_DIM:(g + 1) * HEAD_DIM] = pv_ref[src, :].astype(BF16)
                return carry

            lax.fori_loop(0, n_past // KB, load_past, 0)
        n_new = k_new_ref.shape[0]
        ki_scr[n_prefix:n_prefix + n_new, :] = ki_new_ref[:, SM_KI:SM_KI + IDX_DIM]
        k_scr[n_prefix:n_prefix + n_new, :] = k_new_ref[...]
        v_scr[n_prefix:n_prefix + n_new, :] = v_new_ref[...]
        n_tail = nkp_total * KB - n_prefix - n_new
        if n_tail:
            for scr in (ki_scr, k_scr, v_scr):
                scr[n_prefix + n_new:, :] = jnp.zeros((n_tail, scr.shape[1]), BF16)

    qb = n_prefix // BLK + i
    q0 = qb * BLK - gap
    n_keys = nkp_total * KB

    def chunk_end(pos):
        return jnp.minimum(CHUNK * ((pos + chunk_off) // CHUNK + 1) - chunk_off, l_valid)

    nkp = jnp.minimum(nkp_total, (chunk_end(q0 + BLK - 1) + gap + KB - 1) // KB)

    nt = (((1,), (1,)), ((), ()))
    wit = sm_ref[...].T
    n_adm = chunk_end(q0 + lax.broadcasted_iota(jnp.int32, (1, r), 1))
    krow = lax.broadcasted_iota(jnp.int32, (KB, r), 0)

    fold_rows = 8 * SUBLANES

    def fold(x, op):
        return op(x.reshape(KB // fold_rows, fold_rows, r), axis=0)

    def for_key_steps(body, init):
        carry = lax.fori_loop(0, nkp // 2, lambda t, c: body(2 * t + 1, body(2 * t, c)), init)
        return lax.cond(nkp % 2 == 1, lambda c: body(nkp - 1, c), lambda c: c, carry)

    for h in range(IDX_HEADS):
        qis_scr[h * r:(h + 1) * r, :] = qi_ref[:, h * IDX_DIM:(h + 1) * IDX_DIM]

    def score_body(jp, carry):
        mn, mx = carry
        kij = ki_scr[pl.ds(pl.multiple_of(jp * KB, KB), KB), :]
        acc = jnp.zeros((KB, r), F32)
        for hp in range(IDX_HEADS // 2):
            sh = lax.dot_general(kij, qis_scr[2 * hp * r:2 * (hp + 1) * r, :], nt, preferred_element_type=F32)
            for e in range(2):
                h = 2 * hp + e
                acc = acc + wit[SM_WI + h:SM_WI + h + 1, :] * jnp.maximum(sh[:, e * r:(e + 1) * r], 0.0)
        adm = krow < n_adm + gap - jp * KB
        if gap:
            adm = adm & ((jp > 0) | (krow < n_lead) | (krow >= BLK))
        s = jnp.where(adm, acc * (IDX_DIM ** -0.5), -jnp.inf)
        st_scr[jp] = s
        mn = jnp.minimum(mn, fold(jnp.where(adm, s, jnp.inf), jnp.min))
        mx = jnp.maximum(mx, fold(s, jnp.max))
        return mn, mx

    init = (jnp.full((fold_rows, r), jnp.inf, F32), jnp.full((fold_rows, r), -jnp.inf, F32))
    mn, mx = for_key_steps(score_body, init)
    lo0 = jnp.min(mn, axis=0, keepdims=True)
    hi0 = jnp.max(mx, axis=0, keepdims=True)
    kk = jnp.minimum(n_adm, n_sel).astype(F32)

    def count(pred):
        def body(jp, acc):
            return acc + fold(jnp.where(pred(st_scr[jp], jp), 1.0, 0.0), jnp.sum)
        acc = lax.fori_loop(0, nkp, body, jnp.zeros((fold_rows, r), F32))
        return jnp.sum(acc, axis=0, keepdims=True)

    def search_body(_, carry):
        lo, hi, c_lo, c_hi, ub, c_ub, hit = carry
        frac = jnp.clip((c_lo - kk) / (c_lo - c_hi), SEARCH_CLAMP, 1.0 - SEARCH_CLAMP)
        mid = jnp.where(hit > 0.0, lo, lo * (1.0 - frac) + hi * frac)
        cnt = count(lambda s, jp: s >= mid)
        ok = cnt >= kk
        now = cnt == kk
        lo = jnp.where(ok, mid, lo)
        hi = jnp.where(ok & ~now, hi, mid)
        c_lo = jnp.where(ok, cnt, c_lo)
        c_hi = jnp.where(now, kk - 1.0, jnp.where(ok, c_hi, cnt))
        ub = jnp.where(ok, ub, mid)
        c_ub = jnp.where(ok, c_ub, cnt)
        return lo, hi, c_lo, c_hi, ub, c_ub, jnp.where(now, 1.0, 0.0)

    all_adm = n_adm.astype(F32) == kk
    zeros = jnp.zeros((1, r), F32)
    init = (lo0, jnp.where(all_adm, lo0, hi0), n_adm.astype(F32), jnp.where(all_adm, kk - 1.0, zeros),
            jnp.full((1, r), jnp.inf, F32), zeros, jnp.where(all_adm, 1.0, 0.0))
    lo_f, _, _, _, ub, c_ub, hit_f = lax.fori_loop(0, SEARCH_STEPS, search_body, init)
    hit = hit_f > 0.0
    pending = jnp.sum(jnp.where(hit, 0, 1))
    take_all = jnp.full((1, r), float(n_keys), F32)

    def exact_path():
        def next_below(ub):
            def body(jp, acc):
                s = st_scr[jp]
                return jnp.maximum(acc, fold(jnp.where(s < ub, s, -jnp.inf), jnp.max))
            acc = lax.fori_loop(0, nkp, body, jnp.full((fold_rows, r), -jnp.inf, F32))
            return jnp.max(acc, axis=0, keepdims=True)

        def descend_cond(carry):
            *_, todo, it = carry
            return (todo > 0) & (it < n_keys)

        def descend_body(carry):
            ub, c_ub, _, _, _, it = carry
            t = next_below(ub)
            c_t = count(lambda s, jp: s >= t)
            done = hit | (c_t >= kk)
            return (jnp.where(done, ub, t), jnp.where(done, c_ub, c_t), t, c_t,
                    jnp.sum(jnp.where(done, 0, 1)), it + 1)

        _, c_gt, t, c_t, _, _ = lax.while_loop(descend_cond, descend_body,
                                               (ub, c_ub, lo0, zeros, jnp.int32(1), jnp.int32(0)))
        ties_wanted = jnp.where(hit, take_all, kk - c_gt)
        extra_ties = jnp.sum(jnp.where(~hit & (c_t - c_gt > ties_wanted), 1, 0))
        return jnp.where(hit, lo_f, t), ties_wanted, extra_ties

    thr, ties_wanted, extra_ties = lax.cond(pending > 0, exact_path, lambda: (lo_f, take_all, jnp.int32(0)))

    def mask_with_ties():
        ii = lax.broadcasted_iota(jnp.int32, (KB, KB), 0)
        jj = lax.broadcasted_iota(jnp.int32, (KB, KB), 1)
        upto = (jj <= ii).astype(BF16)

        def body(jp, wanted):
            s = st_scr[jp]
            tied = s == thr
            rank = jnp.dot(upto, jnp.where(tied, 1.0, 0.0).astype(BF16), preferred_element_type=F32)
            m_scr[jp] = jnp.where((s > thr) | (tied & (rank <= wanted)), 0.0, NEG_BIG).T
            return wanted - rank[KB - 1:KB, :]

        lax.fori_loop(0, nkp, body, ties_wanted)

    scale = HEAD_DIM ** -0.5
    rep = N_HEADS // N_KV_HEADS
    rq = mrun_scr.shape[1]
    for h in range(N_HEADS):
        qs_scr[h // rep, (h % rep) * rq:(h % rep + 1) * rq, :] = qn_ref[0:rq, h * HEAD_DIM:(h + 1) * HEAD_DIM]
    mrun_scr[...] = jnp.full(mrun_scr.shape, -jnp.inf, F32)
    lrun_scr[...] = jnp.zeros(lrun_scr.shape, F32)
    acc_scr[...] = jnp.zeros(acc_scr.shape, F32)

    def logit_body(mask_of, jp, carry):
        u0 = jnp.clip(2 * jp - qb + 2, 0, N_SHIFT_TILES - 1)
        u1 = jnp.clip(2 * jp + 1 - qb + 2, 0, N_SHIFT_TILES - 1)
        if gap:
            u0 = jnp.where((jp == 0) & (qb == 1), LEAD_TILE, u0)
        madd = mask_of(jp)[0:rq, :]
        for g in range(N_KV_HEADS):
            lt = lax.dot_general(qs_scr[g], k_scr[pl.ds(pl.multiple_of(jp * KB, KB), KB),
                                                  g * HEAD_DIM:(g + 1) * HEAD_DIM], nt,
                                 preferred_element_type=F32)
            for e in range(rep):
                h = g * rep + e
                bias = jnp.concatenate([bt_ref[h * N_BIAS_TILES + u0][0:rq, :],
                                        bt_ref[h * N_BIAS_TILES + u1][0:rq, :]], axis=1)
                lg = lt[e * rq:(e + 1) * rq, :] * scale + bias + madd
                lg_scr[h, jp] = lg
                mrun_scr[h] = jnp.maximum(mrun_scr[h], jnp.maximum(lg[:, :BLK], lg[:, BLK:]))
        return carry

    def logits_with_ties():
        mask_with_ties()
        return for_key_steps(functools.partial(logit_body, lambda jp: m_scr[jp]), jnp.int32(0))

    def logits_plain():
        inline_mask = lambda jp: jnp.where(st_scr[jp] >= thr, 0.0, NEG_BIG).T
        return for_key_steps(functools.partial(logit_body, inline_mask), jnp.int32(0))

    lax.cond(extra_ties > 0, logits_with_ties, logits_plain)
    for h in range(N_HEADS):
        mrun_scr[h] = jnp.broadcast_to(jnp.max(mrun_scr[h], axis=1, keepdims=True), (rq, BLK))

    def pv_body(jp, carry):
        for g in range(N_KV_HEADS):
            es = []
            for e in range(rep):
                h = g * rep + e
                mrow = mrun_scr[h]
                ex = jnp.exp(lg_scr[h, jp] - jnp.concatenate([mrow, mrow], axis=1))
                lrun_scr[h] = lrun_scr[h] + (ex[:, :BLK] + ex[:, BLK:])
                es.append(ex.astype(BF16))
            acc_scr[g] = acc_scr[g] + jnp.dot(jnp.concatenate(es, axis=0),
                                              v_scr[pl.ds(pl.multiple_of(jp * KB, KB), KB),
                                                    g * HEAD_DIM:(g + 1) * HEAD_DIM],
                                              preferred_element_type=F32)
        return carry

    for_key_steps(pv_body, 0)
    for h in range(N_HEADS):
        g, e = h // rep, h % rep
        den = jnp.sum(lrun_scr[h], axis=1, keepdims=True)
        o_ref[0:rq, h * HEAD_DIM:(h + 1) * HEAD_DIM] = (acc_scr[g, e * rq:(e + 1) * rq, :] / den).astype(BF16)
    if rq < r:
        o_ref[rq:, :] = jnp.zeros((r - rq, ATT_INNER), BF16)


def _attn(qn, qi, sm, ki_new, k_new, v_new, past, lead, bias_tiles, *, n_lead, chunk_off, t_x, n_sel):
    b, tq, _ = qn.shape
    n_past = past[0].shape[1] if past is not None else 0
    assert n_past % KB == 0 and not (n_past and n_lead)
    n_prefix = n_past + (BLK if n_lead else 0)
    n_keys = -(-(n_prefix + tq) // KB) * KB
    nkp_total = n_keys // KB
    rep = N_HEADS // N_KV_HEADS
    rq = BLK if tq > BLK else min(BLK, -(-t_x // (2 * SUBLANES)) * 2 * SUBLANES)
    seq = lambda width: pl.BlockSpec((None, BLK, width), lambda bi, i: (bi, i, 0))
    rows = lambda n, width: pl.BlockSpec((None, n, width), lambda bi, i: (bi, 0, 0))
    kern = functools.partial(_attn_kernel, nkp_total=nkp_total, n_past=n_past, n_lead=n_lead, chunk_off=chunk_off,
                             t_x=t_x, n_sel=n_sel)
    past_specs = [rows(n_past, IDX_DIM), rows(n_past * N_KV_HEADS, HEAD_DIM),
                  rows(n_past * N_KV_HEADS, HEAD_DIM)] if n_past else []
    lead_specs = [_const_spec((BLK, LANES)), _const_spec((BLK, KV_DIM)), _const_spec((BLK, KV_DIM))] if n_lead else []
    return pl.pallas_call(
        kern,
        grid=(b, tq // BLK),
        in_specs=past_specs + lead_specs + [seq(ATT_INNER), seq(IDX_HEADS * IDX_DIM), seq(LANES),
                                            rows(tq, LANES), rows(tq, KV_DIM), rows(tq, KV_DIM),
                                            _const_spec((N_HEADS * N_BIAS_TILES, BLK, BLK))],
        out_specs=seq(ATT_INNER),
        out_shape=jax.ShapeDtypeStruct((b, tq, ATT_INNER), BF16),
        scratch_shapes=[pltpu.VMEM((n_keys, IDX_DIM), BF16),
                        pltpu.VMEM((n_keys, KV_DIM), BF16),
                        pltpu.VMEM((n_keys, KV_DIM), BF16),
                        pltpu.VMEM((nkp_total, KB, BLK), F32),
                        pltpu.VMEM((nkp_total, BLK, KB), F32),
                        pltpu.VMEM((N_HEADS, nkp_total, rq, KB), F32),
                        pltpu.VMEM((IDX_HEADS * BLK, IDX_DIM), BF16),
                        pltpu.VMEM((N_KV_HEADS, rep * rq, HEAD_DIM), BF16),
                        pltpu.VMEM((N_HEADS, rq, BLK), F32),
                        pltpu.VMEM((N_HEADS, rq, BLK), F32),
                        pltpu.VMEM((N_KV_HEADS, rep * rq, HEAD_DIM), F32)],
        compiler_params=_cparams(("parallel", "arbitrary")),
        name="attn",
    )(*(past or ()), *(lead or ()), qn, qi, sm, ki_new, k_new, v_new, bias_tiles)


def _out_ffn_kernel(x_ref, ys_ref, ya_ref, gs_ref, ga_ref, wbs_ref, wba_ref, wo_ref, n2_ref, wg_ref, wu_ref, wd_ref,
                    y_ref):
    dot = functools.partial(jnp.dot, preferred_element_type=F32)
    merged = (jax.nn.sigmoid(gs_ref[...]) * dot(ys_ref[...], wbs_ref[...])
              + jax.nn.sigmoid(ga_ref[...]) * dot(ya_ref[...], wba_ref[...]))
    h = x_ref[...] + dot(merged.astype(BF16), wo_ref[...])
    hn = _rms(h, n2_ref[...]).astype(BF16)
    act = (_silu(dot(hn, wg_ref[...])) * dot(hn, wu_ref[...])).astype(BF16)
    y_ref[...] = h + dot(act, wd_ref[...])


def _out_ffn(x2d, ys, ya, gs, ga, p):
    n = x2d.shape[0]
    tm = _row_tile(n)
    d_ff = p["w_gate"].shape[1]
    row = lambda width: pl.BlockSpec((tm, width), lambda i: (i, 0))
    wspec = lambda shape: pl.BlockSpec(shape, lambda i: (0, 0), pipeline_mode=pl.Buffered(1))
    return pl.pallas_call(
        _out_ffn_kernel,
        grid=(n // tm,),
        in_specs=[row(D_MODEL), row(SSD_INNER), row(ATT_INNER), row(D_MODEL), row(D_MODEL),
                  wspec((SSD_INNER, D_MODEL)), wspec((ATT_INNER, D_MODEL)), wspec((D_MODEL, D_MODEL)),
                  _const_spec((1, D_MODEL)), wspec((D_MODEL, d_ff)), wspec((D_MODEL, d_ff)), wspec((d_ff, D_MODEL))],
        out_specs=row(D_MODEL),
        out_shape=jax.ShapeDtypeStruct((n, D_MODEL), F32),
        compiler_params=_cparams(("parallel",)),
        name="out_ffn",
    )(x2d, ys, ya, gs, ga, p["w_br_ssd"], p["w_br_att"], p["w_out"], p["norm2_w"], p["w_gate"], p["w_up"],
      p["w_down"])


def _layer(x, lead_rows, conv_prev, ssm_prev, past, p, bias_tiles, *, chunk_off, n_sel):
    b, t, _ = x.shape
    n_lead = lead_rows.shape[0]
    tp = -(-t // BLK) * BLK
    x2d = jnp.pad(x, ((0, 0), (0, tp - t), (0, 0))).reshape(b * tp, D_MODEL)
    hist = jnp.pad(conv_prev.astype(F32), ((0, 0), (HALO - (CONV_WIDTH - 1), 0), (0, 0)))
    seq = lambda a: a.reshape(b, tp, a.shape[-1])
    ssd_lead = attn_lead = None
    if n_lead:
        lead = _in_proj(jnp.pad(lead_rows.astype(x.dtype), ((0, BLK - n_lead), (0, 0))), hist[:1], p,
                        n_seq=1, t_x=n_lead)
        lz, lxbc, _, lk32, lv32, lkb, lvb, _, lsm, lsmb, _, _, hist = lead
        ssd_lead, attn_lead = (lxbc, lz, lsm), (lsmb, lkb, lvb)
    z, xbc, qn, k32, v32, kb, vb, qi, sm, smb, gs, ga, conv_new8 = _in_proj(x2d, hist, p, n_seq=b, t_x=t)

    gw = SSD_HEADS // SSD_GROUPS * SSD_HEAD_DIM
    y_ssd, ssm_new = _ssd(ssd_lead, seq(xbc), seq(z), seq(sm),
                          ssm_prev.astype(F32).reshape(b, SSD_GROUPS, gw, SSD_STATE), p, n_lead=n_lead, t_x=t)

    if past is not None:
        pk, pv, pki = past
        n_past = pk.shape[1]
        past = (pki.astype(F32), pk.astype(F32).reshape(b, n_past * N_KV_HEADS, HEAD_DIM),
                pv.astype(F32).reshape(b, n_past * N_KV_HEADS, HEAD_DIM))
    y_att = _attn(seq(qn), seq(qi), seq(sm), seq(smb), seq(kb), seq(vb), past, attn_lead, bias_tiles,
                  n_lead=n_lead, chunk_off=chunk_off, t_x=t, n_sel=n_sel)

    y = _out_ffn(x2d, y_ssd.reshape(b * tp, SSD_INNER), y_att.reshape(b * tp, ATT_INNER), gs, ga, p)

    def with_lead(new, lead_part):
        if not n_lead:
            return new
        return jnp.concatenate([jnp.broadcast_to(lead_part[None, :n_lead], (b, n_lead) + new.shape[2:]), new], axis=1)

    heads = lambda a, rows: a.reshape(-1, rows, N_KV_HEADS, HEAD_DIM)
    k_new = with_lead(heads(k32, tp)[:, :t], heads(lk32, BLK)[0] if n_lead else None)
    v_new = with_lead(heads(v32, tp)[:, :t], heads(lv32, BLK)[0] if n_lead else None)
    ki_new = with_lead(seq(sm)[:, :t, SM_KI:SM_KI + IDX_DIM], lsm[:, SM_KI:SM_KI + IDX_DIM] if n_lead else None)
    ssm_new = ssm_new.reshape(b, SSD_HEADS, SSD_HEAD_DIM, SSD_STATE)
    conv_new = conv_new8[:, HALO - (CONV_WIDTH - 1):]
    return y.reshape(b, tp, D_MODEL)[:, :t], k_new, v_new, ki_new, ssm_new, conv_new


def _prepare_params(l, norm1_w, w_in, conv_w, conv_b, dt_bias, a_log, d_skip, ssd_norm_w, q_norm_w, k_norm_w,
                    idx_k_norm_w, w_br_ssd, w_br_att, w_out, norm2_w, w_gate, w_up, w_down):
    offs = [0]
    for w in IN_WIDTHS:
        offs.append(offs[-1] + w)
    i_z, i_xbc, i_dt, i_q, i_k, i_v, i_qi, i_ki, i_wi, i_gs, i_ga = range(11)
    run = lambda first, last: w_in[l][:, offs[first]:offs[last + 1]].astype(BF16)
    pad = jnp.zeros((D_MODEL, LANES - IDX_DIM - SSD_HEADS - IDX_HEADS), BF16)
    w_perm = jnp.concatenate([run(i_z, i_xbc), run(i_q, i_qi), run(i_gs, i_ga),
                              run(i_ki, i_ki), run(i_dt, i_dt), run(i_wi, i_wi), pad], axis=1)

    def lanes_at(vec, start):
        return jnp.zeros((1, LANES), F32).at[0, start:start + vec.shape[0]].set(vec.astype(F32))

    dtb = lanes_at(dt_bias[l], SM_DT)
    alog = lanes_at(a_log[l], SM_DT)
    head_of_channel = jnp.arange(SSD_INNER) // SSD_HEAD_DIM
    expand = (jnp.arange(LANES)[:, None] == head_of_channel[None, :] + SM_DT).astype(BF16)
    row = lambda v: v.astype(F32).reshape(1, -1)
    return dict(
        norm1_w=row(norm1_w[l]), w_in=w_perm, conv_w=conv_w[l].astype(F32), conv_b=row(conv_b[l]),
        dtb=dtb, dtb_t=dtb.reshape(LANES, 1), alog=alog, alog_t=alog.reshape(LANES, 1),
        dskip_x=row(jnp.repeat(d_skip[l], SSD_HEAD_DIM)), ssd_norm_w=row(ssd_norm_w[l]), expand=expand,
        q_norm_w=row(q_norm_w[l]), k_norm_w=row(k_norm_w[l]),
        idx_k_norm_w=jnp.ones((1, LANES), F32).at[0, SM_KI:SM_KI + IDX_DIM].set(idx_k_norm_w[l].astype(F32)),
        w_br_ssd=w_br_ssd[l].astype(BF16), w_br_att=w_br_att[l].astype(BF16), w_out=w_out[l].astype(BF16),
        norm2_w=row(norm2_w[l]), w_gate=w_gate[l].astype(BF16), w_up=w_up[l].astype(BF16),
        w_down=w_down[l].astype(BF16))


def kernel(x_prompt, x_sample, cache_k, cache_v, cache_kidx, state_ssm, state_conv, meta_tokens, rel_bias, norm1_w,
           w_in, conv_w, conv_b, dt_bias, a_log, d_skip, ssd_norm_w, q_norm_w, k_norm_w, idx_k_norm_w, w_br_ssd,
           w_br_att, w_out, norm2_w, w_gate, w_up, w_down):
    bp, sp, _ = x_prompt.shape
    bs, ts, _ = x_sample.shape
    past = cache_k.shape[2]
    assert w_in.shape[0] == 1
    assert past % BLK == 0 and BLK % CHUNK == 0 and N_META <= CHUNK
    l = 0

    n_sel_p = min(TOPK_MAX, sp // 4)
    n_sel_s = min(TOPK_MAX, (past + ts) // 4)
    conv0 = jnp.zeros((bp, CONV_WIDTH - 1, CONV_DIM), F32)
    ssm0 = jnp.zeros((bp, SSD_HEADS, SSD_HEAD_DIM, SSD_STATE), F32)
    bias_tiles = _bias_tiles(rel_bias.astype(F32))
    p = _prepare_params(l, norm1_w, w_in, conv_w, conv_b, dt_bias, a_log, d_skip, ssd_norm_w, q_norm_w, k_norm_w,
                        idx_k_norm_w, w_br_ssd, w_br_att, w_out, norm2_w, w_gate, w_up, w_down)
    y_prompt, *rest_p = _layer(x_prompt, meta_tokens, conv0, ssm0, None, p, bias_tiles,
                               chunk_off=CHUNK - N_META, n_sel=n_sel_p)
    y_sample, *rest_s = _layer(x_sample, meta_tokens[:0], state_conv[l], state_ssm[l],
                               (cache_k[l], cache_v[l], cache_kidx[l]), p, bias_tiles,
                               chunk_off=0, n_sel=n_sel_s)

    dtypes = (x_prompt.dtype, x_prompt.dtype, x_prompt.dtype, state_ssm.dtype, x_prompt.dtype)
    return (y_prompt, y_sample, *(o[None].astype(dt) for o, dt in zip(rest_p, dtypes)),
            *(o[None].astype(dt) for o, dt in zip(rest_s, dtypes)))
```

```python
import functools

import jax
import jax.numpy as jnp
from jax import lax
from jax.experimental import pallas as pl
from jax.experimental.pallas import tpu as pltpu

F32 = jnp.float32
BF16 = jnp.bfloat16

D_MODEL = 1024
CHUNK = 64
N_META = 16
SSD_HEADS = 16
SSD_HEAD_DIM = 64
SSD_INNER = SSD_HEADS * SSD_HEAD_DIM
SSD_GROUPS = 4
SSD_STATE = 128
CONV_WIDTH = 4
CONV_DIM = SSD_INNER + 2 * SSD_GROUPS * SSD_STATE
N_HEADS = 8
N_KV_HEADS = 2
HEAD_DIM = 128
ATT_INNER = N_HEADS * HEAD_DIM
KV_DIM = N_KV_HEADS * HEAD_DIM
IDX_HEADS = 8
IDX_DIM = 64
TOPK_MAX = 256
REL_BUCKETS = 32
REL_MAX_DIST = 128
IN_WIDTHS = (SSD_INNER, CONV_DIM, SSD_HEADS, ATT_INNER, KV_DIM, KV_DIM, IDX_HEADS * IDX_DIM, IDX_DIM, IDX_HEADS,
             D_MODEL, D_MODEL)
EPS = 1e-6

LANES = 128
SUBLANES = 8
VMEM_LIMIT_BYTES = 56 * 1024 * 1024

BLK = LANES
ROW_TILE = 256
HALO = SUBLANES

C_Z = 0
C_XBC = C_Z + SSD_INNER
C_Q = C_XBC + CONV_DIM
C_K = C_Q + ATT_INNER
C_V = C_K + KV_DIM
C_QI = C_V + KV_DIM
C_GS = C_QI + IDX_HEADS * IDX_DIM
C_GA = C_GS + D_MODEL
C_SM = C_GA + D_MODEL
IN_PAD = C_SM + LANES
SM_KI = 0
SM_DT = SM_KI + IDX_DIM
SM_WI = SM_DT + SSD_HEADS

SEARCH_STEPS = 16
SEARCH_CLAMP = 1.0 / 16
NEG_BIG = -1e30


def _cparams(sem):
    return pltpu.CompilerParams(dimension_semantics=sem, vmem_limit_bytes=VMEM_LIMIT_BYTES)


def _const_spec(shape):
    nd = len(shape)
    return pl.BlockSpec(shape, lambda *_: (0,) * nd)


def _rms(x, w):
    return x * lax.rsqrt(jnp.mean(x * x, axis=-1, keepdims=True) + EPS) * w


def _silu(x):
    return x * jax.nn.sigmoid(x)


def _in_proj_kernel(x_ref, hist_ref, n1_ref, w_ref, qn_ref, kn_ref, kin_ref, cw_ref, cb_ref,
                    z_ref, xc_ref, q_ref, k_ref, v_ref, kb_ref, vb_ref, qi_ref, sm_ref, smb_ref, gs_ref, ga_ref, tail_ref,
                    xpad_scr, *, tiles_per_seq, t_x):
    j = pl.program_id(1)
    hn = _rms(x_ref[...], n1_ref[...]).astype(BF16)
    tm = x_ref.shape[0]

    def mm(lo, hi):
        return jnp.dot(hn, w_ref[:, lo:hi], preferred_element_type=F32)

    @pl.when(j == 0)
    def _():
        xpad_scr[0:HALO, :] = hist_ref[...]

    step = 2 * LANES
    conv_slabs = iter(range(0, CONV_DIM, LANES))

    def conv_next():
        c0 = next(conv_slabs, None)
        if c0 is None:
            return
        if c0 % step == 0:
            xpad_scr[HALO:HALO + tm, c0:c0 + step] = mm(C_XBC + c0, C_XBC + c0 + step)
        sl = slice(c0, c0 + LANES)
        first = HALO - (CONV_WIDTH - 1)
        acc = xpad_scr[first:first + tm, sl] * cw_ref[0:1, sl]
        for i in range(1, CONV_WIDTH):
            acc = acc + xpad_scr[first + i:first + i + tm, sl] * cw_ref[i:i + 1, sl]
        xc_ref[:, sl] = _silu(cb_ref[:, sl] + acc)

    def project(lo, hi, out_ref, dtype=F32):
        for c0 in range(0, hi - lo, step):
            out_ref[:, c0:c0 + step] = mm(lo + c0, lo + c0 + step).astype(dtype)
            conv_next()

    project(C_Z, C_XBC, z_ref)
    for c0 in range(0, ATT_INNER, step):
        q = mm(C_Q + c0, C_Q + c0 + step)
        conv_next()
        for h in range(step // HEAD_DIM):
            sl = slice(h * HEAD_DIM, (h + 1) * HEAD_DIM)
            q_ref[:, c0 + h * HEAD_DIM:c0 + (h + 1) * HEAD_DIM] = _rms(q[:, sl], qn_ref[...]).astype(BF16)
    k = mm(C_K, C_V)
    conv_next()
    v = mm(C_V, C_QI)
    conv_next()
    for h in range(N_KV_HEADS):
        sl = slice(h * HEAD_DIM, (h + 1) * HEAD_DIM)
        kh = _rms(k[:, sl], kn_ref[...])
        k_ref[pl.ds(h, tm, stride=N_KV_HEADS), :] = kh
        kb_ref[:, sl] = kh.astype(BF16)
        v_ref[pl.ds(h, tm, stride=N_KV_HEADS), :] = v[:, sl]
    vb_ref[...] = v.astype(BF16)
    project(C_QI, C_GS, qi_ref, BF16)
    project(C_GS, C_GA, gs_ref)
    project(C_GA, C_SM, ga_ref)
    sm = mm(C_SM, IN_PAD)
    assert next(conv_slabs, None) is None
    lane = lax.broadcasted_iota(jnp.int32, sm.shape, 1)
    is_ki = lane < SM_KI + IDX_DIM
    ms = jnp.sum(jnp.where(is_ki, sm * sm, 0.0), axis=-1, keepdims=True) * (1.0 / IDX_DIM)
    ki = sm * lax.rsqrt(ms + EPS) * kin_ref[...]
    is_wi = (lane >= SM_WI) & (lane < SM_WI + IDX_HEADS)
    out = jnp.where(is_ki, ki, jnp.where(is_wi, sm * (IDX_HEADS ** -0.5), sm))
    sm_ref[...] = out
    smb_ref[...] = out.astype(BF16)

    n_last = t_x - (tiles_per_seq - 1) * tm

    @pl.when(j == tiles_per_seq - 1)
    def _():
        tail_ref[...] = xpad_scr[n_last:n_last + HALO, :]

    xpad_scr[0:HALO, :] = xpad_scr[tm:tm + HALO, :]


def _row_tile(n):
    return ROW_TILE if n % ROW_TILE == 0 else BLK


def _in_proj(x2d, hist, p, *, n_seq, t_x):
    n = x2d.shape[0]
    tp = n // n_seq
    tm = _row_tile(tp)
    tiles = tp // tm
    shared_hist = hist.shape[0] == 1
    row = lambda width, per_token=1: pl.BlockSpec((tm * per_token, width), lambda s, j: (s * tiles + j, 0))
    per_seq = pl.BlockSpec((None, HALO, CONV_DIM), lambda s, j: (0 if shared_hist else s, 0, 0))
    outs = [
        (SSD_INNER, F32, 1), (CONV_DIM, F32, 1), (ATT_INNER, BF16, 1), (HEAD_DIM, F32, N_KV_HEADS),
        (HEAD_DIM, F32, N_KV_HEADS), (KV_DIM, BF16, 1), (KV_DIM, BF16, 1), (IDX_HEADS * IDX_DIM, BF16, 1),
        (LANES, F32, 1), (LANES, BF16, 1), (D_MODEL, F32, 1), (D_MODEL, F32, 1),
    ]
    kern = functools.partial(_in_proj_kernel, tiles_per_seq=tiles, t_x=t_x)
    return pl.pallas_call(
        kern,
        grid=(n_seq, tiles),
        in_specs=[row(D_MODEL), per_seq, _const_spec((1, D_MODEL)),
                  pl.BlockSpec((D_MODEL, IN_PAD), lambda s, j: (0, 0), pipeline_mode=pl.Buffered(1)),
                  _const_spec((1, HEAD_DIM)), _const_spec((1, HEAD_DIM)), _const_spec((1, LANES)),
                  _const_spec((CONV_WIDTH, CONV_DIM)), _const_spec((1, CONV_DIM))],
        out_specs=[row(w, per) for w, _, per in outs]
        + [pl.BlockSpec((None, HALO, CONV_DIM), lambda s, j: (s, 0, 0))],
        out_shape=[jax.ShapeDtypeStruct((n * per, w), dt) for w, dt, per in outs]
        + [jax.ShapeDtypeStruct((n_seq, HALO, CONV_DIM), F32)],
        scratch_shapes=[pltpu.VMEM((tm + 2 * HALO, CONV_DIM), F32)],
        compiler_params=_cparams(("parallel", "arbitrary")),
        name="in_proj",
    )(x2d, hist, p["norm1_w"], p["w_in"], p["q_norm_w"], p["k_norm_w"], p["idx_k_norm_w"], p["conv_w"], p["conv_b"])


def _softplus(x):
    return jnp.maximum(x, 0.0) + jnp.log1p(jnp.exp(-jnp.abs(x)))


def _split3(x):
    hi = x.astype(BF16)
    r1 = x - hi.astype(F32)
    mid = r1.astype(BF16)
    lo = (r1 - mid.astype(F32)).astype(BF16)
    return hi, mid, lo


def _ssd_kernel(*refs, n_chunks, n_lead, t_x):
    if n_lead:
        xbc_lead_ref, z_lead_ref, sm_lead_ref, *refs = refs
    (xbc_ref, z_ref, sm_ref, sprev_ref, dtb_ref, dtbt_ref, alog_ref, alogt_ref,
     dsk_ref, nw_ref, exp_ref, y_ref, snew_ref, s_scr, y_scr) = refs
    c = pl.program_id(1)
    lead_chunks = 1 if n_lead else 0

    @pl.when(c == 0)
    def _():
        for g in range(SSD_GROUPS):
            s_scr[g] = sprev_ref[g].T

    def chunk(xc_ref, z_src_ref, sm_src_ref, n_valid):
        _ssd_chunk(xc_ref, z_src_ref, sm_src_ref, n_valid, dtb_ref, dtbt_ref, alog_ref, alogt_ref, dsk_ref, nw_ref,
                   exp_ref, y_ref, s_scr, y_scr)

    if n_lead:
        pl.when(c == 0)(lambda: chunk(xbc_lead_ref, z_lead_ref, sm_lead_ref, n_lead))
        pl.when(c > 0)(lambda: chunk(xbc_ref, z_ref, sm_ref, jnp.minimum(BLK, t_x - (c - lead_chunks) * BLK)))
    else:
        chunk(xbc_ref, z_ref, sm_ref, jnp.minimum(BLK, t_x - c * BLK))

    @pl.when(c == n_chunks - 1)
    def _():
        for g in range(SSD_GROUPS):
            snew_ref[g] = s_scr[g].T


def _ssd_chunk(xc_ref, z_ref, sm_ref, n_valid, dtb_ref, dtbt_ref, alog_ref, alogt_ref, dsk_ref, nw_ref, exp_ref,
               y_ref, s_scr, y_scr):
    q = BLK
    gw = SSD_HEADS // SSD_GROUPS * SSD_HEAD_DIM
    sm = sm_ref[...]

    heads = slice(SM_DT, SM_DT + SSD_HEADS)
    smt = sm.T[heads, :]
    lane = lax.broadcasted_iota(jnp.int32, (q, LANES), 1)
    row = lax.broadcasted_iota(jnp.int32, (q, LANES), 0)
    is_dt = (lane >= SM_DT) & (lane < SM_DT + SSD_HEADS) & (row < n_valid)
    time_t = lax.broadcasted_iota(jnp.int32, (SSD_HEADS, q), 1)
    dt = jnp.where(is_dt, _softplus(sm + dtb_ref[...]), 0.0)
    dtt = jnp.where(time_t < n_valid, _softplus(smt + dtbt_ref[heads, :]), 0.0)
    da = dt * (-jnp.exp(alog_ref[...]))
    dat = dtt * (-jnp.exp(alogt_ref[heads, :]))
    ii = lax.broadcasted_iota(jnp.int32, (q, q), 0)
    jj = lax.broadcasted_iota(jnp.int32, (q, q), 1)
    causal = jj <= ii
    acum = sum(jnp.dot(causal.astype(BF16), p, preferred_element_type=F32) for p in _split3(da))
    acumt = sum(jnp.dot(p, (ii <= jj).astype(BF16), preferred_element_type=F32) for p in _split3(dat))
    a_last = acum[q - 1:q, :]
    expand = exp_ref[...]
    stacked = jnp.concatenate([jnp.exp(acum), jnp.exp(a_last - acum) * dt,
                               jnp.broadcast_to(jnp.exp(a_last), (SUBLANES, LANES))], axis=0)
    stacked_x = sum(jnp.dot(p, expand, preferred_element_type=F32) for p in _split3(stacked))
    ea_x = stacked_x[0:q]
    wdt_x = stacked_x[q:2 * q]
    dec_x = stacked_x[2 * q:2 * q + 1]

    sq = None
    for g in range(SSD_GROUPS):
        gsl = slice(g * gw, (g + 1) * gw)
        bsl = slice(SSD_INNER + g * SSD_STATE, SSD_INNER + (g + 1) * SSD_STATE)
        csl = slice(SSD_INNER + SSD_GROUPS * SSD_STATE + g * SSD_STATE,
                    SSD_INNER + SSD_GROUPS * SSD_STATE + (g + 1) * SSD_STATE)
        bmf = xc_ref[:, bsl]
        bm = bmf.astype(BF16)
        cm = xc_ref[:, csl].astype(BF16)
        xg = xc_ref[:, gsl]
        xgb = xg.astype(BF16)
        cbm = lax.dot_general(cm, bm, (((1,), (1,)), ((), ())), preferred_element_type=F32)
        xw = (xg * wdt_x[:, gsl]).astype(BF16)
        st = jnp.dot(bmf.T.astype(BF16), xw, preferred_element_type=F32)
        s_in = s_scr[g]
        y_off = jnp.dot(cm, s_in.astype(BF16), preferred_element_type=F32) * ea_x[:, gsl]
        s_scr[g] = s_in * dec_x[:, gsl] + st
        first_of_pair = lax.broadcasted_iota(jnp.int32, (q, LANES), 1) < SSD_HEAD_DIM
        for rp in range(gw // LANES):
            psl = slice(rp * LANES, (rp + 1) * LANES)
            osl = slice(g * gw + rp * LANES, g * gw + (rp + 1) * LANES)
            pair = []
            for h in (g * (SSD_HEADS // SSD_GROUPS) + 2 * rp, g * (SSD_HEADS // SSD_GROUPS) + 2 * rp + 1):
                seg = acum[:, SM_DT + h:SM_DT + h + 1] - acumt[h:h + 1, :]
                lmat = jnp.exp(jnp.where(causal, seg, -jnp.inf))
                wmat = (cbm * lmat * dtt[h:h + 1, :]).astype(BF16)
                pair.append(jnp.dot(wmat, xgb[:, psl], preferred_element_type=F32))
            y_diag = jnp.where(first_of_pair, pair[0], pair[1])
            yp = (y_diag + y_off[:, psl] + xg[:, psl] * dsk_ref[:, osl]) * _silu(z_ref[:, osl])
            y_scr[:, osl] = yp
            sq = yp * yp if sq is None else sq + yp * yp

    ms = jnp.sum(sq, axis=-1, keepdims=True) * (1.0 / SSD_INNER)
    y_ref[...] = (y_scr[...] * lax.rsqrt(ms + EPS) * nw_ref[...]).astype(BF16)


def _ssd(lead, xbc, z, sm, ssm_prev, p, *, n_lead, t_x):
    b, tp, _ = xbc.shape
    lead_chunks = 1 if n_lead else 0
    assert n_lead % SUBLANES == 0 and n_lead <= BLK and 2 * SSD_HEAD_DIM == LANES
    nc = tp // BLK + lead_chunks
    gw = SSD_HEADS // SSD_GROUPS * SSD_HEAD_DIM
    seq = lambda width: pl.BlockSpec((None, BLK, width), lambda i, c: (i, jnp.maximum(c - lead_chunks, 0), 0))
    kern = functools.partial(_ssd_kernel, n_chunks=nc, n_lead=n_lead, t_x=t_x)
    lead_specs = [_const_spec((BLK, CONV_DIM)), _const_spec((BLK, SSD_INNER)), _const_spec((BLK, LANES))]
    return pl.pallas_call(
        kern,
        grid=(b, nc),
        in_specs=(lead_specs if n_lead else []) + [
                  seq(CONV_DIM), seq(SSD_INNER), seq(LANES),
                  pl.BlockSpec((None, SSD_GROUPS, gw, SSD_STATE), lambda i, c: (i, 0, 0, 0)),
                  _const_spec((1, LANES)), _const_spec((LANES, 1)), _const_spec((1, LANES)), _const_spec((LANES, 1)),
                  _const_spec((1, SSD_INNER)), _const_spec((1, SSD_INNER)), _const_spec((LANES, SSD_INNER))],
        out_specs=[seq(SSD_INNER),
                   pl.BlockSpec((None, SSD_GROUPS, gw, SSD_STATE), lambda i, c: (i, 0, 0, 0))],
        out_shape=[jax.ShapeDtypeStruct((b, tp, SSD_INNER), BF16),
                   jax.ShapeDtypeStruct((b, SSD_GROUPS, gw, SSD_STATE), F32)],
        scratch_shapes=[pltpu.VMEM((SSD_GROUPS, SSD_STATE, gw), F32),
                        pltpu.VMEM((BLK, SSD_INNER), F32)],
        compiler_params=_cparams(("parallel", "arbitrary")),
        name="ssd",
    )(*(lead if n_lead else ()), xbc, z, sm, ssm_prev, p["dtb"], p["dtb_t"],
      p["alog"], p["alog_t"], p["dskip_x"], p["ssd_norm_w"], p["expand"])


N_SHIFT_TILES = 5
LEAD_TILE = N_SHIFT_TILES
N_BIAS_TILES = N_SHIFT_TILES + 1


def _log_bucket_starts():
    nb = REL_BUCKETS // 2
    max_exact = nb // 2
    s = nb - max_exact
    starts = []
    for m in range(1, s):
        n = max_exact
        while n ** s * max_exact ** m < max_exact ** s * REL_MAX_DIST ** m:
            n += 1
        starts.append(n)
    return starts


def _bias_kernel(rb_ref, bt_ref):
    nb = REL_BUCKETS // 2
    max_exact = nb // 2
    qq = lax.broadcasted_iota(jnp.int32, (BLK, BLK), 0)
    kk = lax.broadcasted_iota(jnp.int32, (BLK, BLK), 1)
    for u in range(N_BIAS_TILES):
        rel = kk - qq + ((u - 2) * BLK if u < N_SHIFT_TILES else -N_META)
        n = jnp.abs(rel)
        large = max_exact + sum(jnp.where(n >= start, 1, 0) for start in _log_bucket_starts())
        bucket = jnp.where(rel > 0, nb, 0) + jnp.where(n < max_exact, n, large)
        for h in range(N_HEADS):
            acc = jnp.zeros((BLK, BLK), F32)
            for bkt in range(REL_BUCKETS):
                acc = jnp.where(bucket == bkt, rb_ref[bkt, h], acc)
            bt_ref[h * N_BIAS_TILES + u] = acc


def _bias_tiles(rel_bias):
    return pl.pallas_call(
        _bias_kernel,
        in_specs=[pl.BlockSpec(memory_space=pltpu.SMEM)],
        out_specs=pl.BlockSpec(memory_space=pltpu.VMEM),
        out_shape=jax.ShapeDtypeStruct((N_HEADS * N_BIAS_TILES, BLK, BLK), F32),
        name="bias_tiles",
    )(rel_bias)


KB = 2 * BLK


def _attn_kernel(*refs, nkp_total, n_past, n_lead, chunk_off, t_x, n_sel):
    if n_past:
        pki_ref, pk_ref, pv_ref, *refs = refs
    if n_lead:
        lki_ref, lk_ref, lv_ref, *refs = refs
    (qn_ref, qi_ref, sm_ref, ki_new_ref, k_new_ref, v_new_ref, bt_ref, o_ref,
     ki_scr, k_scr, v_scr, st_scr, m_scr, lg_scr, qis_scr, qs_scr, mrun_scr, lrun_scr, acc_scr) = refs
    i = pl.program_id(1)
    r = BLK
    n_prefix = n_past + (BLK if n_lead else 0)
    gap = BLK - n_lead if n_lead else 0
    l_valid = n_prefix - gap + t_x

    @pl.when(i == 0)
    def _():
        if n_lead:
            ki_scr[0:BLK, :] = lki_ref[:, SM_KI:SM_KI + IDX_DIM]
            k_scr[0:BLK, :] = lk_ref[...]
            v_scr[0:BLK, :] = lv_ref[...]
        if n_past:
            def load_past(c, carry):
                rows = pl.ds(pl.multiple_of(c * KB, KB), KB)
                ki_scr[rows, :] = pki_ref[rows, :].astype(BF16)
                for g in range(N_KV_HEADS):
                    src = pl.ds(pl.multiple_of(c * KB * N_KV_HEADS, KB) + g, KB, stride=N_KV_HEADS)
                    k_scr[rows, g * HEAD_DIM:(g + 1) * HEAD_DIM] = pk_ref[src, :].astype(BF16)
                    v_scr[rows, g * HEAD_DIM:(g + 1) * HEAD_DIM] = pv_ref[src, :].astype(BF16)
                return carry

            lax.fori_loop(0, n_past // KB, load_past, 0)
        n_new = k_new_ref.shape[0]
        ki_scr[n_prefix:n_prefix + n_new, :] = ki_new_ref[:, SM_KI:SM_KI + IDX_DIM]
        k_scr[n_prefix:n_prefix + n_new, :] = k_new_ref[...]
        v_scr[n_prefix:n_prefix + n_new, :] = v_new_ref[...]
        n_tail = nkp_total * KB - n_prefix - n_new
        if n_tail:
            for scr in (ki_scr, k_scr, v_scr):
                scr[n_prefix + n_new:, :] = jnp.zeros((n_tail, scr.shape[1]), BF16)

    qb = n_prefix // BLK + i
    q0 = qb * BLK - gap
    n_keys = nkp_total * KB

    def chunk_end(pos):
        return jnp.minimum(CHUNK * ((pos + chunk_off) // CHUNK + 1) - chunk_off, l_valid)

    nkp = jnp.minimum(nkp_total, (chunk_end(q0 + BLK - 1) + gap + KB - 1) // KB)

    nt = (((1,), (1,)), ((), ()))
    wit = sm_ref[...].T
    n_adm = chunk_end(q0 + lax.broadcasted_iota(jnp.int32, (1, r), 1))
    krow = lax.broadcasted_iota(jnp.int32, (KB, r), 0)

    fold_rows = 8 * SUBLANES

    def fold(x, op):
        return op(x.reshape(KB // fold_rows, fold_rows, r), axis=0)

    def for_key_steps(body, init):
        carry = lax.fori_loop(0, nkp // 2, lambda t, c: body(2 * t + 1, body(2 * t, c)), init)
        return lax.cond(nkp % 2 == 1, lambda c: body(nkp - 1, c), lambda c: c, carry)

    for h in range(IDX_HEADS):
        qis_scr[h * r:(h + 1) * r, :] = qi_ref[:, h * IDX_DIM:(h + 1) * IDX_DIM]

    def score_body(jp, carry):
        mn, mx = carry
        kij = ki_scr[pl.ds(pl.multiple_of(jp * KB, KB), KB), :]
        acc = jnp.zeros((KB, r), F32)
        for hp in range(IDX_HEADS // 2):
            sh = lax.dot_general(kij, qis_scr[2 * hp * r:2 * (hp + 1) * r, :], nt, preferred_element_type=F32)
            for e in range(2):
                h = 2 * hp + e
                acc = acc + wit[SM_WI + h:SM_WI + h + 1, :] * jnp.maximum(sh[:, e * r:(e + 1) * r], 0.0)
        adm = krow < n_adm + gap - jp * KB
        if gap:
            adm = adm & ((jp > 0) | (krow < n_lead) | (krow >= BLK))
        s = jnp.where(adm, acc * (IDX_DIM ** -0.5), -jnp.inf)
        st_scr[jp] = s
        mn = jnp.minimum(mn, fold(jnp.where(adm, s, jnp.inf), jnp.min))
        mx = jnp.maximum(mx, fold(s, jnp.max))
        return mn, mx

    init = (jnp.full((fold_rows, r), jnp.inf, F32), jnp.full((fold_rows, r), -jnp.inf, F32))
    mn, mx = for_key_steps(score_body, init)
    lo0 = jnp.min(mn, axis=0, keepdims=True)
    hi0 = jnp.max(mx, axis=0, keepdims=True)
    kk = jnp.minimum(n_adm, n_sel).astype(F32)

    def count(pred):
        def body(jp, acc):
            return acc + fold(jnp.where(pred(st_scr[jp], jp), 1.0, 0.0), jnp.sum)
        acc = lax.fori_loop(0, nkp, body, jnp.zeros((fold_rows, r), F32))
        return jnp.sum(acc, axis=0, keepdims=True)

    def search_body(_, carry):
        lo, hi, c_lo, c_hi, ub, c_ub, hit = carry
        frac = jnp.clip((c_lo - kk) / (c_lo - c_hi), SEARCH_CLAMP, 1.0 - SEARCH_CLAMP)
        mid = jnp.where(hit > 0.0, lo, lo * (1.0 - frac) + hi * frac)
        cnt = count(lambda s, jp: s >= mid)
        ok = cnt >= kk
        now = cnt == kk
        lo = jnp.where(ok, mid, lo)
        hi = jnp.where(ok & ~now, hi, mid)
        c_lo = jnp.where(ok, cnt, c_lo)
        c_hi = jnp.where(now, kk - 1.0, jnp.where(ok, c_hi, cnt))
        ub = jnp.where(ok, ub, mid)
        c_ub = jnp.where(ok, c_ub, cnt)
        return lo, hi, c_lo, c_hi, ub, c_ub, jnp.where(now, 1.0, 0.0)

    all_adm = n_adm.astype(F32) == kk
    zeros = jnp.zeros((1, r), F32)
    init = (lo0, jnp.where(all_adm, lo0, hi0), n_adm.astype(F32), jnp.where(all_adm, kk - 1.0, zeros),
            jnp.full((1, r), jnp.inf, F32), zeros, jnp.where(all_adm, 1.0, 0.0))
    lo_f, _, _, _, ub, c_ub, hit_f = lax.fori_loop(0, SEARCH_STEPS, search_body, init)
    hit = hit_f > 0.0
    pending = jnp.sum(jnp.where(hit, 0, 1))
    take_all = jnp.full((1, r), float(n_keys), F32)

    def exact_path():
        def next_below(ub):
            def body(jp, acc):
                s = st_scr[jp]
                return jnp.maximum(acc, fold(jnp.where(s < ub, s, -jnp.inf), jnp.max))
            acc = lax.fori_loop(0, nkp, body, jnp.full((fold_rows, r), -jnp.inf, F32))
            return jnp.max(acc, axis=0, keepdims=True)

        def descend_cond(carry):
            *_, todo, it = carry
            return (todo > 0) & (it < n_keys)

        def descend_body(carry):
            ub, c_ub, _, _, _, it = carry
            t = next_below(ub)
            c_t = count(lambda s, jp: s >= t)
            done = hit | (c_t >= kk)
            return (jnp.where(done, ub, t), jnp.where(done, c_ub, c_t), t, c_t,
                    jnp.sum(jnp.where(done, 0, 1)), it + 1)

        _, c_gt, t, c_t, _, _ = lax.while_loop(descend_cond, descend_body,
                                               (ub, c_ub, lo0, zeros, jnp.int32(1), jnp.int32(0)))
        ties_wanted = jnp.where(hit, take_all, kk - c_gt)
        extra_ties = jnp.sum(jnp.where(~hit & (c_t - c_gt > ties_wanted), 1, 0))
        return jnp.where(hit, lo_f, t), ties_wanted, extra_ties

    thr, ties_wanted, extra_ties = lax.cond(pending > 0, exact_path, lambda: (lo_f, take_all, jnp.int32(0)))

    def mask_with_ties():
        ii = lax.broadcasted_iota(jnp.int32, (KB, KB), 0)
        jj = lax.broadcasted_iota(jnp.int32, (KB, KB), 1)
        upto = (jj <= ii).astype(BF16)

        def body(jp, wanted):
            s = st_scr[jp]
            tied = s == thr
            rank = jnp.dot(upto, jnp.where(tied, 1.0, 0.0).astype(BF16), preferred_element_type=F32)
            m_scr[jp] = jnp.where((s > thr) | (tied & (rank <= wanted)), 0.0, NEG_BIG).T
            return wanted - rank[KB - 1:KB, :]

        lax.fori_loop(0, nkp, body, ties_wanted)

    scale = HEAD_DIM ** -0.5
    rep = N_HEADS // N_KV_HEADS
    rq = mrun_scr.shape[1]
    for h in range(N_HEADS):
        qs_scr[h // rep, (h % rep) * rq:(h % rep + 1) * rq, :] = qn_ref[0:rq, h * HEAD_DIM:(h + 1) * HEAD_DIM]
    mrun_scr[...] = jnp.full(mrun_scr.shape, -jnp.inf, F32)
    lrun_scr[...] = jnp.zeros(lrun_scr.shape, F32)
    acc_scr[...] = jnp.zeros(acc_scr.shape, F32)

    def logit_body(mask_of, jp, carry):
        u0 = jnp.clip(2 * jp - qb + 2, 0, N_SHIFT_TILES - 1)
        u1 = jnp.clip(2 * jp + 1 - qb + 2, 0, N_SHIFT_TILES - 1)
        if gap:
            u0 = jnp.where((jp == 0) & (qb == 1), LEAD_TILE, u0)
        madd = mask_of(jp)[0:rq, :]
        for g in range(N_KV_HEADS):
            lt = lax.dot_general(qs_scr[g], k_scr[pl.ds(pl.multiple_of(jp * KB, KB), KB),
                                                  g * HEAD_DIM:(g + 1) * HEAD_DIM], nt,
                                 preferred_element_type=F32)
            for e in range(rep):
                h = g * rep + e
                bias = jnp.concatenate([bt_ref[h * N_BIAS_TILES + u0][0:rq, :],
                                        bt_ref[h * N_BIAS_TILES + u1][0:rq, :]], axis=1)
                lg = lt[e * rq:(e + 1) * rq, :] * scale + bias + madd
                lg_scr[h, jp] = lg
                mrun_scr[h] = jnp.maximum(mrun_scr[h], jnp.maximum(lg[:, :BLK], lg[:, BLK:]))
        return carry

    def logits_with_ties():
        mask_with_ties()
        return for_key_steps(functools.partial(logit_body, lambda jp: m_scr[jp]), jnp.int32(0))

    def logits_plain():
        inline_mask = lambda jp: jnp.where(st_scr[jp] >= thr, 0.0, NEG_BIG).T
        return for_key_steps(functools.partial(logit_body, inline_mask), jnp.int32(0))

    lax.cond(extra_ties > 0, logits_with_ties, logits_plain)
    for h in range(N_HEADS):
        mrun_scr[h] = jnp.broadcast_to(jnp.max(mrun_scr[h], axis=1, keepdims=True), (rq, BLK))

    def pv_body(jp, carry):
        for g in range(N_KV_HEADS):
            es = []
            for e in range(rep):
                h = g * rep + e
                mrow = mrun_scr[h]
                ex = jnp.exp(lg_scr[h, jp] - jnp.concatenate([mrow, mrow], axis=1))
                lrun_scr[h] = lrun_scr[h] + (ex[:, :BLK] + ex[:, BLK:])
                es.append(ex.astype(BF16))
            acc_scr[g] = acc_scr[g] + jnp.dot(jnp.concatenate(es, axis=0),
                                              v_scr[pl.ds(pl.multiple_of(jp * KB, KB), KB),
                                                    g * HEAD_DIM:(g + 1) * HEAD_DIM],
                                              preferred_element_type=F32)
        return carry

    for_key_steps(pv_body, 0)
    for h in range(N_HEADS):
        g, e = h // rep, h % rep
        den = jnp.sum(lrun_scr[h], axis=1, keepdims=True)
        o_ref[0:rq, h * HEAD_DIM:(h + 1) * HEAD_DIM] = (acc_scr[g, e * rq:(e + 1) * rq, :] / den).astype(BF16)
    if rq < r:
        o_ref[rq:, :] = jnp.zeros((r - rq, ATT_INNER), BF16)


def _attn(qn, qi, sm, ki_new, k_new, v_new, past, lead, bias_tiles, *, n_lead, chunk_off, t_x, n_sel):
    b, tq, _ = qn.shape
    n_past = past[0].shape[1] if past is not None else 0
    assert n_past % KB == 0 and not (n_past and n_lead)
    n_prefix = n_past + (BLK if n_lead else 0)
    n_keys = -(-(n_prefix + tq) // KB) * KB
    nkp_total = n_keys // KB
    rep = N_HEADS // N_KV_HEADS
    rq = BLK if tq > BLK else min(BLK, -(-t_x // (2 * SUBLANES)) * 2 * SUBLANES)
    seq = lambda width: pl.BlockSpec((None, BLK, width), lambda bi, i: (bi, i, 0))
    rows = lambda n, width: pl.BlockSpec((None, n, width), lambda bi, i: (bi, 0, 0))
    kern = functools.partial(_attn_kernel, nkp_total=nkp_total, n_past=n_past, n_lead=n_lead, chunk_off=chunk_off,
                             t_x=t_x, n_sel=n_sel)
    past_specs = [rows(n_past, IDX_DIM), rows(n_past * N_KV_HEADS, HEAD_DIM),
                  rows(n_past * N_KV_HEADS, HEAD_DIM)] if n_past else []
    lead_specs = [_const_spec((BLK, LANES)), _const_spec((BLK, KV_DIM)), _const_spec((BLK, KV_DIM))] if n_lead else []
    return pl.pallas_call(
        kern,
        grid=(b, tq // BLK),
        in_specs=past_specs + lead_specs + [seq(ATT_INNER), seq(IDX_HEADS * IDX_DIM), seq(LANES),
                                            rows(tq, LANES), rows(tq, KV_DIM), rows(tq, KV_DIM),
                                            _const_spec((N_HEADS * N_BIAS_TILES, BLK, BLK))],
        out_specs=seq(ATT_INNER),
        out_shape=jax.ShapeDtypeStruct((b, tq, ATT_INNER), BF16),
        scratch_shapes=[pltpu.VMEM((n_keys, IDX_DIM), BF16),
                        pltpu.VMEM((n_keys, KV_DIM), BF16),
                        pltpu.VMEM((n_keys, KV_DIM), BF16),
                        pltpu.VMEM((nkp_total, KB, BLK), F32),
                        pltpu.VMEM((nkp_total, BLK, KB), F32),
                        pltpu.VMEM((N_HEADS, nkp_total, rq, KB), F32),
                        pltpu.VMEM((IDX_HEADS * BLK, IDX_DIM), BF16),
                        pltpu.VMEM((N_KV_HEADS, rep * rq, HEAD_DIM), BF16),
                        pltpu.VMEM((N_HEADS, rq, BLK), F32),
                        pltpu.VMEM((N_HEADS, rq, BLK), F32),
                        pltpu.VMEM((N_KV_HEADS, rep * rq, HEAD_DIM), F32)],
        compiler_params=_cparams(("parallel", "arbitrary")),
        name="attn",
    )(*(past or ()), *(lead or ()), qn, qi, sm, ki_new, k_new, v_new, bias_tiles)


def _out_ffn_kernel(x_ref, ys_ref, ya_ref, gs_ref, ga_ref, wbs_ref, wba_ref, wo_ref, n2_ref, wg_ref, wu_ref, wd_ref,
                    y_ref):
    dot = functools.partial(jnp.dot, preferred_element_type=F32)
    merged = (jax.nn.sigmoid(gs_ref[...]) * dot(ys_ref[...], wbs_ref[...])
              + jax.nn.sigmoid(ga_ref[...]) * dot(ya_ref[...], wba_ref[...]))
    h = x_ref[...] + dot(merged.astype(BF16), wo_ref[...])
    hn = _rms(h, n2_ref[...]).astype(BF16)
    act = (_silu(dot(hn, wg_ref[...])) * dot(hn, wu_ref[...])).astype(BF16)
    y_ref[...] = h + dot(act, wd_ref[...])


def _out_ffn(x2d, ys, ya, gs, ga, p):
    n = x2d.shape[0]
    tm = _row_tile(n)
    d_ff = p["w_gate"].shape[1]
    row = lambda width: pl.BlockSpec((tm, width), lambda i: (i, 0))
    wspec = lambda shape: pl.BlockSpec(shape, lambda i: (0, 0), pipeline_mode=pl.Buffered(1))
    return pl.pallas_call(
        _out_ffn_kernel,
        grid=(n // tm,),
        in_specs=[row(D_MODEL), row(SSD_INNER), row(ATT_INNER), row(D_MODEL), row(D_MODEL),
                  wspec((SSD_INNER, D_MODEL)), wspec((ATT_INNER, D_MODEL)), wspec((D_MODEL, D_MODEL)),
                  _const_spec((1, D_MODEL)), wspec((D_MODEL, d_ff)), wspec((D_MODEL, d_ff)), wspec((d_ff, D_MODEL))],
        out_specs=row(D_MODEL),
        out_shape=jax.ShapeDtypeStruct((n, D_MODEL), F32),
        compiler_params=_cparams(("parallel",)),
        name="out_ffn",
    )(x2d, ys, ya, gs, ga, p["w_br_ssd"], p["w_br_att"], p["w_out"], p["norm2_w"], p["w_gate"], p["w_up"],
      p["w_down"])


def _layer(x, lead_rows, conv_prev, ssm_prev, past, p, bias_tiles, *, chunk_off, n_sel):
    b, t, _ = x.shape
    n_lead = lead_rows.shape[0]
    tp = -(-t // BLK) * BLK
    x2d = jnp.pad(x, ((0, 0), (0, tp - t), (0, 0))).reshape(b * tp, D_MODEL)
    hist = jnp.pad(conv_prev.astype(F32), ((0, 0), (HALO - (CONV_WIDTH - 1), 0), (0, 0)))
    seq = lambda a: a.reshape(b, tp, a.shape[-1])
    ssd_lead = attn_lead = None
    if n_lead:
        lead = _in_proj(jnp.pad(lead_rows.astype(x.dtype), ((0, BLK - n_lead), (0, 0))), hist[:1], p,
                        n_seq=1, t_x=n_lead)
        lz, lxbc, _, lk32, lv32, lkb, lvb, _, lsm, lsmb, _, _, hist = lead
        ssd_lead, attn_lead = (lxbc, lz, lsm), (lsmb, lkb, lvb)
    z, xbc, qn, k32, v32, kb, vb, qi, sm, smb, gs, ga, conv_new8 = _in_proj(x2d, hist, p, n_seq=b, t_x=t)

    gw = SSD_HEADS // SSD_GROUPS * SSD_HEAD_DIM
    y_ssd, ssm_new = _ssd(ssd_lead, seq(xbc), seq(z), seq(sm),
                          ssm_prev.astype(F32).reshape(b, SSD_GROUPS, gw, SSD_STATE), p, n_lead=n_lead, t_x=t)

    if past is not None:
        pk, pv, pki = past
        n_past = pk.shape[1]
        past = (pki.astype(F32), pk.astype(F32).reshape(b, n_past * N_KV_HEADS, HEAD_DIM),
                pv.astype(F32).reshape(b, n_past * N_KV_HEADS, HEAD_DIM))
    y_att = _attn(seq(qn), seq(qi), seq(sm), seq(smb), seq(kb), seq(vb), past, attn_lead, bias_tiles,
                  n_lead=n_lead, chunk_off=chunk_off, t_x=t, n_sel=n_sel)

    y = _out_ffn(x2d, y_ssd.reshape(b * tp, SSD_INNER), y_att.reshape(b * tp, ATT_INNER), gs, ga, p)

    def with_lead(new, lead_part):
        if not n_lead:
            return new
        return jnp.concatenate([jnp.broadcast_to(lead_part[None, :n_lead], (b, n_lead) + new.shape[2:]), new], axis=1)

    heads = lambda a, rows: a.reshape(-1, rows, N_KV_HEADS, HEAD_DIM)
    k_new = with_lead(heads(k32, tp)[:, :t], heads(lk32, BLK)[0] if n_lead else None)
    v_new = with_lead(heads(v32, tp)[:, :t], heads(lv32, BLK)[0] if n_lead else None)
    ki_new = with_lead(seq(sm)[:, :t, SM_KI:SM_KI + IDX_DIM], lsm[:, SM_KI:SM_KI + IDX_DIM] if n_lead else None)
    ssm_new = ssm_new.reshape(b, SSD_HEADS, SSD_HEAD_DIM, SSD_STATE)
    conv_new = conv_new8[:, HALO - (CONV_WIDTH - 1):]
    return y.reshape(b, tp, D_MODEL)[:, :t], k_new, v_new, ki_new, ssm_new, conv_new


def _prepare_params(l, norm1_w, w_in, conv_w, conv_b, dt_bias, a_log, d_skip, ssd_norm_w, q_norm_w, k_norm_w,
                    idx_k_norm_w, w_br_ssd, w_br_att, w_out, norm2_w, w_gate, w_up, w_down):
    offs = [0]
    for w in IN_WIDTHS:
        offs.append(offs[-1] + w)
    i_z, i_xbc, i_dt, i_q, i_k, i_v, i_qi, i_ki, i_wi, i_gs, i_ga = range(11)
    run = lambda first, last: w_in[l][:, offs[first]:offs[last + 1]].astype(BF16)
    pad = jnp.zeros((D_MODEL, LANES - IDX_DIM - SSD_HEADS - IDX_HEADS), BF16)
    w_perm = jnp.concatenate([run(i_z, i_xbc), run(i_q, i_qi), run(i_gs, i_ga),
                              run(i_ki, i_ki), run(i_dt, i_dt), run(i_wi, i_wi), pad], axis=1)

    def lanes_at(vec, start):
        return jnp.zeros((1, LANES), F32).at[0, start:start + vec.shape[0]].set(vec.astype(F32))

    dtb = lanes_at(dt_bias[l], SM_DT)
    alog = lanes_at(a_log[l], SM_DT)
    head_of_channel = jnp.arange(SSD_INNER) // SSD_HEAD_DIM
    expand = (jnp.arange(LANES)[:, None] == head_of_channel[None, :] + SM_DT).astype(BF16)
    row = lambda v: v.astype(F32).reshape(1, -1)
    return dict(
        norm1_w=row(norm1_w[l]), w_in=w_perm, conv_w=conv_w[l].astype(F32), conv_b=row(conv_b[l]),
        dtb=dtb, dtb_t=dtb.reshape(LANES, 1), alog=alog, alog_t=alog.reshape(LANES, 1),
        dskip_x=row(jnp.repeat(d_skip[l], SSD_HEAD_DIM)), ssd_norm_w=row(ssd_norm_w[l]), expand=expand,
        q_norm_w=row(q_norm_w[l]), k_norm_w=row(k_norm_w[l]),
        idx_k_norm_w=jnp.ones((1, LANES), F32).at[0, SM_KI:SM_KI + IDX_DIM].set(idx_k_norm_w[l].astype(F32)),
        w_br_ssd=w_br_ssd[l].astype(BF16), w_br_att=w_br_att[l].astype(BF16), w_out=w_out[l].astype(BF16),
        norm2_w=row(norm2_w[l]), w_gate=w_gate[l].astype(BF16), w_up=w_up[l].astype(BF16),
        w_down=w_down[l].astype(BF16))


def kernel(x_prompt, x_sample, cache_k, cache_v, cache_kidx, state_ssm, state_conv, meta_tokens, rel_bias, norm1_w,
           w_in, conv_w, conv_b, dt_bias, a_log, d_skip, ssd_norm_w, q_norm_w, k_norm_w, idx_k_norm_w, w_br_ssd,
           w_br_att, w_out, norm2_w, w_gate, w_up, w_down):
    bp, sp, _ = x_prompt.shape
    bs, ts, _ = x_sample.shape
    past = cache_k.shape[2]
    assert w_in.shape[0] == 1
    assert past % BLK == 0 and BLK % CHUNK == 0 and N_META <= CHUNK
    l = 0

    n_sel_p = min(TOPK_MAX, sp // 4)
    n_sel_s = min(TOPK_MAX, (past + ts) // 4)
    conv0 = jnp.zeros((bp, CONV_WIDTH - 1, CONV_DIM), F32)
    ssm0 = jnp.zeros((bp, SSD_HEADS, SSD_HEAD_DIM, SSD_STATE), F32)
    bias_tiles = _bias_tiles(rel_bias.astype(F32))
    p = _prepare_params(l, norm1_w, w_in, conv_w, conv_b, dt_bias, a_log, d_skip, ssd_norm_w, q_norm_w, k_norm_w,
                        idx_k_norm_w, w_br_ssd, w_br_att, w_out, norm2_w, w_gate, w_up, w_down)
    y_prompt, *rest_p = _layer(x_prompt, meta_tokens, conv0, ssm0, None, p, bias_tiles,
                               chunk_off=CHUNK - N_META, n_sel=n_sel_p)
    y_sample, *rest_s = _layer(x_sample, meta_tokens[:0], state_conv[l], state_ssm[l],
                               (cache_k[l], cache_v[l], cache_kidx[l]), p, bias_tiles,
                               chunk_off=0, n_sel=n_sel_s)

    dtypes = (x_prompt.dtype, x_prompt.dtype, x_prompt.dtype, state_ssm.dtype, x_prompt.dtype)
    return (y_prompt, y_sample, *(o[None].astype(dt) for o, dt in zip(rest_p, dtypes)),
            *(o[None].astype(dt) for o, dt in zip(rest_s, dtypes)))
```

```python
import functools

import jax
import jax.numpy as jnp
from jax import lax
from jax.experimental import pallas as pl
from jax.experimental.pallas import tpu as pltpu

F32 = jnp.float32
BF16 = jnp.bfloat16

D_MODEL = 1024
CHUNK = 64
N_META = 16
SSD_HEADS = 16
SSD_HEAD_DIM = 64
SSD_INNER = SSD_HEADS * SSD_HEAD_DIM
SSD_GROUPS = 4
SSD_STATE = 128
CONV_WIDTH = 4
CONV_DIM = SSD_INNER + 2 * SSD_GROUPS * SSD_STATE
N_HEADS = 8
N_KV_HEADS = 2
HEAD_DIM = 128
ATT_INNER = N_HEADS * HEAD_DIM
KV_DIM = N_KV_HEADS * HEAD_DIM
IDX_HEADS = 8
IDX_DIM = 64
TOPK_MAX = 256
REL_BUCKETS = 32
REL_MAX_DIST = 128
IN_WIDTHS = (SSD_INNER, CONV_DIM, SSD_HEADS, ATT_INNER, KV_DIM, KV_DIM, IDX_HEADS * IDX_DIM, IDX_DIM, IDX_HEADS,
             D_MODEL, D_MODEL)
EPS = 1e-6

LANES = 128
SUBLANES = 8
VMEM_LIMIT_BYTES = 56 * 1024 * 1024

BLK = LANES
ROW_TILE = 256
HALO = SUBLANES
SSD_SEQS_PER_STEP = 4

C_Z = 0
C_XBC = C_Z + SSD_INNER
C_Q = C_XBC + CONV_DIM
C_K = C_Q + ATT_INNER
C_V = C_K + KV_DIM
C_QI = C_V + KV_DIM
C_GS = C_QI + IDX_HEADS * IDX_DIM
C_GA = C_GS + D_MODEL
C_SM = C_GA + D_MODEL
IN_PAD = C_SM + LANES
SM_KI = 0
SM_DT = SM_KI + IDX_DIM
SM_WI = SM_DT + SSD_HEADS

SEARCH_STEPS = 16
SEARCH_CLAMP = 1.0 / 16
NEG_BIG = -1e30


def _cparams(sem):
    return pltpu.CompilerParams(dimension_semantics=sem, vmem_limit_bytes=VMEM_LIMIT_BYTES)


def _const_spec(shape):
    nd = len(shape)
    return pl.BlockSpec(shape, lambda *_: (0,) * nd)


def _rms(x, w):
    return x * lax.rsqrt(jnp.mean(x * x, axis=-1, keepdims=True) + EPS) * w


def _silu(x):
    return x * jax.nn.sigmoid(x)


def _in_proj_kernel(x_ref, hist_ref, n1_ref, w_ref, qn_ref, kn_ref, kin_ref, cw_ref, cb_ref,
                    z_ref, xc_ref, q_ref, k_ref, v_ref, kb_ref, vb_ref, qi_ref, sm_ref, smb_ref, gs_ref, ga_ref, tail_ref,
                    xpad_scr, *, tiles_per_seq, t_x):
    j = pl.program_id(1)
    hn = _rms(x_ref[...], n1_ref[...]).astype(BF16)
    tm = x_ref.shape[0]

    def mm(lo, hi):
        return jnp.dot(hn, w_ref[:, lo:hi], preferred_element_type=F32)

    @pl.when(j == 0)
    def _():
        xpad_scr[0:HALO, :] = hist_ref[...]

    step = 2 * LANES
    conv_slabs = iter(range(0, CONV_DIM, LANES))

    def conv_next():
        c0 = next(conv_slabs, None)
        if c0 is None:
            return
        if c0 % step == 0:
            xpad_scr[HALO:HALO + tm, c0:c0 + step] = mm(C_XBC + c0, C_XBC + c0 + step)
        sl = slice(c0, c0 + LANES)
        first = HALO - (CONV_WIDTH - 1)
        acc = xpad_scr[first:first + tm, sl] * cw_ref[0:1, sl]
        for i in range(1, CONV_WIDTH):
            acc = acc + xpad_scr[first + i:first + i + tm, sl] * cw_ref[i:i + 1, sl]
        xc_ref[:, sl] = _silu(cb_ref[:, sl] + acc)

    def project(lo, hi, out_ref, dtype=F32):
        for c0 in range(0, hi - lo, step):
            out_ref[:, c0:c0 + step] = mm(lo + c0, lo + c0 + step).astype(dtype)
            conv_next()

    project(C_Z, C_XBC, z_ref)
    for c0 in range(0, ATT_INNER, step):
        q = mm(C_Q + c0, C_Q + c0 + step)
        conv_next()
        for h in range(step // HEAD_DIM):
            sl = slice(h * HEAD_DIM, (h + 1) * HEAD_DIM)
            q_ref[:, c0 + h * HEAD_DIM:c0 + (h + 1) * HEAD_DIM] = _rms(q[:, sl], qn_ref[...]).astype(BF16)
    k = mm(C_K, C_V)
    conv_next()
    v = mm(C_V, C_QI)
    conv_next()
    for h in range(N_KV_HEADS):
        sl = slice(h * HEAD_DIM, (h + 1) * HEAD_DIM)
        kh = _rms(k[:, sl], kn_ref[...])
        k_ref[pl.ds(h, tm, stride=N_KV_HEADS), :] = kh
        kb_ref[:, sl] = kh.astype(BF16)
        v_ref[pl.ds(h, tm, stride=N_KV_HEADS), :] = v[:, sl]
    vb_ref[...] = v.astype(BF16)
    project(C_QI, C_GS, qi_ref, BF16)
    project(C_GS, C_GA, gs_ref)
    project(C_GA, C_SM, ga_ref)
    sm = mm(C_SM, IN_PAD)
    assert next(conv_slabs, None) is None
    lane = lax.broadcasted_iota(jnp.int32, sm.shape, 1)
    is_ki = lane < SM_KI + IDX_DIM
    ms = jnp.sum(jnp.where(is_ki, sm * sm, 0.0), axis=-1, keepdims=True) * (1.0 / IDX_DIM)
    ki = sm * lax.rsqrt(ms + EPS) * kin_ref[...]
    is_wi = (lane >= SM_WI) & (lane < SM_WI + IDX_HEADS)
    out = jnp.where(is_ki, ki, jnp.where(is_wi, sm * (IDX_HEADS ** -0.5), sm))
    sm_ref[...] = out
    smb_ref[...] = out.astype(BF16)

    n_last = t_x - (tiles_per_seq - 1) * tm

    @pl.when(j == tiles_per_seq - 1)
    def _():
        tail_ref[...] = xpad_scr[n_last:n_last + HALO, :]

    xpad_scr[0:HALO, :] = xpad_scr[tm:tm + HALO, :]


def _row_tile(n):
    return ROW_TILE if n % ROW_TILE == 0 else BLK


def _in_proj(x2d, hist, p, *, n_seq, t_x):
    n = x2d.shape[0]
    tp = n // n_seq
    tm = _row_tile(tp)
    tiles = tp // tm
    shared_hist = hist.shape[0] == 1
    row = lambda width, per_token=1: pl.BlockSpec((tm * per_token, width), lambda s, j: (s * tiles + j, 0))
    per_seq = pl.BlockSpec((None, HALO, CONV_DIM), lambda s, j: (0 if shared_hist else s, 0, 0))
    outs = [
        (SSD_INNER, F32, 1), (CONV_DIM, F32, 1), (ATT_INNER, BF16, 1), (HEAD_DIM, F32, N_KV_HEADS),
        (HEAD_DIM, F32, N_KV_HEADS), (KV_DIM, BF16, 1), (KV_DIM, BF16, 1), (IDX_HEADS * IDX_DIM, BF16, 1),
        (LANES, F32, 1), (LANES, BF16, 1), (D_MODEL, F32, 1), (D_MODEL, F32, 1),
    ]
    kern = functools.partial(_in_proj_kernel, tiles_per_seq=tiles, t_x=t_x)
    return pl.pallas_call(
        kern,
        grid=(n_seq, tiles),
        in_specs=[row(D_MODEL), per_seq, _const_spec((1, D_MODEL)),
                  pl.BlockSpec((D_MODEL, IN_PAD), lambda s, j: (0, 0), pipeline_mode=pl.Buffered(1)),
                  _const_spec((1, HEAD_DIM)), _const_spec((1, HEAD_DIM)), _const_spec((1, LANES)),
                  _const_spec((CONV_WIDTH, CONV_DIM)), _const_spec((1, CONV_DIM))],
        out_specs=[row(w, per) for w, _, per in outs]
        + [pl.BlockSpec((None, HALO, CONV_DIM), lambda s, j: (s, 0, 0))],
        out_shape=[jax.ShapeDtypeStruct((n * per, w), dt) for w, dt, per in outs]
        + [jax.ShapeDtypeStruct((n_seq, HALO, CONV_DIM), F32)],
        scratch_shapes=[pltpu.VMEM((tm + 2 * HALO, CONV_DIM), F32)],
        compiler_params=_cparams(("parallel", "arbitrary")),
        name="in_proj",
    )(x2d, hist, p["norm1_w"], p["w_in"], p["q_norm_w"], p["k_norm_w"], p["idx_k_norm_w"], p["conv_w"], p["conv_b"])


def _softplus(x):
    return jnp.maximum(x, 0.0) + jnp.log1p(jnp.exp(-jnp.abs(x)))


def _split3(x):
    hi = x.astype(BF16)
    r1 = x - hi.astype(F32)
    mid = r1.astype(BF16)
    lo = (r1 - mid.astype(F32)).astype(BF16)
    return hi, mid, lo


def _ssd_kernel(*refs, n_chunks, n_lead, t_x):
    if n_lead:
        xbc_lead_ref, z_lead_ref, sm_lead_ref, *refs = refs
    (xbc_ref, z_ref, sm_ref, sprev_ref, dtb_ref, dtbt_ref, alog_ref, alogt_ref,
     dsk_ref, nw_ref, exp_ref, y_ref, snew_ref, s_scr, y_scr) = refs
    c = pl.program_id(1)
    lead_chunks = 1 if n_lead else 0
    seqs = range(xbc_ref.shape[0])

    @pl.when(c == 0)
    def _():
        for k in seqs:
            for g in range(SSD_GROUPS):
                s_scr[k, g] = sprev_ref[k, g].T

    def chunk(k, xc_ref, z_src_ref, sm_src_ref, n_valid):
        _ssd_chunk(xc_ref, z_src_ref, sm_src_ref, n_valid, dtb_ref, dtbt_ref, alog_ref, alogt_ref, dsk_ref, nw_ref,
                   exp_ref, y_ref.at[k], s_scr.at[k], y_scr.at[k])

    def x_chunks():
        for k in seqs:
            chunk(k, xbc_ref.at[k], z_ref.at[k], sm_ref.at[k], jnp.minimum(BLK, t_x - (c - lead_chunks) * BLK))

    if n_lead:
        @pl.when(c == 0)
        def _():
            for k in seqs:
                chunk(k, xbc_lead_ref, z_lead_ref, sm_lead_ref, n_lead)

        pl.when(c > 0)(x_chunks)
    else:
        x_chunks()

    @pl.when(c == n_chunks - 1)
    def _():
        for k in seqs:
            for g in range(SSD_GROUPS):
                snew_ref[k, g] = s_scr[k, g].T


def _ssd_chunk(xc_ref, z_ref, sm_ref, n_valid, dtb_ref, dtbt_ref, alog_ref, alogt_ref, dsk_ref, nw_ref, exp_ref,
               y_ref, s_scr, y_scr):
    q = BLK
    gw = SSD_HEADS // SSD_GROUPS * SSD_HEAD_DIM
    sm = sm_ref[...]

    heads = slice(SM_DT, SM_DT + SSD_HEADS)
    smt = sm.T[heads, :]
    lane = lax.broadcasted_iota(jnp.int32, (q, LANES), 1)
    row = lax.broadcasted_iota(jnp.int32, (q, LANES), 0)
    is_dt = (lane >= SM_DT) & (lane < SM_DT + SSD_HEADS) & (row < n_valid)
    time_t = lax.broadcasted_iota(jnp.int32, (SSD_HEADS, q), 1)
    dt = jnp.where(is_dt, _softplus(sm + dtb_ref[...]), 0.0)
    dtt = jnp.where(time_t < n_valid, _softplus(smt + dtbt_ref[heads, :]), 0.0)
    da = dt * (-jnp.exp(alog_ref[...]))
    dat = dtt * (-jnp.exp(alogt_ref[heads, :]))
    ii = lax.broadcasted_iota(jnp.int32, (q, q), 0)
    jj = lax.broadcasted_iota(jnp.int32, (q, q), 1)
    causal = jj <= ii
    acum = sum(jnp.dot(causal.astype(BF16), p, preferred_element_type=F32) for p in _split3(da))
    acumt = sum(jnp.dot(p, (ii <= jj).astype(BF16), preferred_element_type=F32) for p in _split3(dat))
    a_last = acum[q - 1:q, :]
    expand = exp_ref[...]
    stacked = jnp.concatenate([jnp.exp(acum), jnp.exp(a_last - acum) * dt,
                               jnp.broadcast_to(jnp.exp(a_last), (SUBLANES, LANES))], axis=0)
    stacked_x = sum(jnp.dot(p, expand, preferred_element_type=F32) for p in _split3(stacked))
    ea_x = stacked_x[0:q]
    wdt_x = stacked_x[q:2 * q]
    dec_x = stacked_x[2 * q:2 * q + 1]

    sq = None
    for g in range(SSD_GROUPS):
        gsl = slice(g * gw, (g + 1) * gw)
        bsl = slice(SSD_INNER + g * SSD_STATE, SSD_INNER + (g + 1) * SSD_STATE)
        csl = slice(SSD_INNER + SSD_GROUPS * SSD_STATE + g * SSD_STATE,
                    SSD_INNER + SSD_GROUPS * SSD_STATE + (g + 1) * SSD_STATE)
        bmf = xc_ref[:, bsl]
        bm = bmf.astype(BF16)
        cm = xc_ref[:, csl].astype(BF16)
        xg = xc_ref[:, gsl]
        xgb = xg.astype(BF16)
        cbm = lax.dot_general(cm, bm, (((1,), (1,)), ((), ())), preferred_element_type=F32)
        xw = (xg * wdt_x[:, gsl]).astype(BF16)
        st = jnp.dot(bmf.T.astype(BF16), xw, preferred_element_type=F32)
        s_in = s_scr[g]
        y_off = jnp.dot(cm, s_in.astype(BF16), preferred_element_type=F32) * ea_x[:, gsl]
        s_scr[g] = s_in * dec_x[:, gsl] + st
        first_of_pair = lax.broadcasted_iota(jnp.int32, (q, LANES), 1) < SSD_HEAD_DIM
        for rp in range(gw // LANES):
            psl = slice(rp * LANES, (rp + 1) * LANES)
            osl = slice(g * gw + rp * LANES, g * gw + (rp + 1) * LANES)
            pair = []
            for h in (g * (SSD_HEADS // SSD_GROUPS) + 2 * rp, g * (SSD_HEADS // SSD_GROUPS) + 2 * rp + 1):
                seg = acum[:, SM_DT + h:SM_DT + h + 1] - acumt[h:h + 1, :]
                lmat = jnp.exp(jnp.where(causal, seg, -jnp.inf))
                wmat = (cbm * lmat * dtt[h:h + 1, :]).astype(BF16)
                pair.append(jnp.dot(wmat, xgb[:, psl], preferred_element_type=F32))
            y_diag = jnp.where(first_of_pair, pair[0], pair[1])
            yp = (y_diag + y_off[:, psl] + xg[:, psl] * dsk_ref[:, osl]) * _silu(z_ref[:, osl])
            y_scr[:, osl] = yp
            sq = yp * yp if sq is None else sq + yp * yp

    ms = jnp.sum(sq, axis=-1, keepdims=True) * (1.0 / SSD_INNER)
    y_ref[...] = (y_scr[...] * lax.rsqrt(ms + EPS) * nw_ref[...]).astype(BF16)


def _ssd(lead, xbc, z, sm, ssm_prev, p, *, n_lead, t_x):
    b, tp, _ = xbc.shape
    lead_chunks = 1 if n_lead else 0
    assert n_lead % SUBLANES == 0 and n_lead <= BLK and 2 * SSD_HEAD_DIM == LANES
    nc = tp // BLK + lead_chunks
    gw = SSD_HEADS // SSD_GROUPS * SSD_HEAD_DIM
    bb = SSD_SEQS_PER_STEP if b % SSD_SEQS_PER_STEP == 0 else 1
    seq = lambda width: pl.BlockSpec((bb, BLK, width), lambda i, c: (i, jnp.maximum(c - lead_chunks, 0), 0))
    state = pl.BlockSpec((bb, SSD_GROUPS, gw, SSD_STATE), lambda i, c: (i, 0, 0, 0))
    kern = functools.partial(_ssd_kernel, n_chunks=nc, n_lead=n_lead, t_x=t_x)
    lead_specs = [_const_spec((BLK, CONV_DIM)), _const_spec((BLK, SSD_INNER)), _const_spec((BLK, LANES))]
    return pl.pallas_call(
        kern,
        grid=(b // bb, nc),
        in_specs=(lead_specs if n_lead else []) + [
                  seq(CONV_DIM), seq(SSD_INNER), seq(LANES), state,
                  _const_spec((1, LANES)), _const_spec((LANES, 1)), _const_spec((1, LANES)), _const_spec((LANES, 1)),
                  _const_spec((1, SSD_INNER)), _const_spec((1, SSD_INNER)), _const_spec((LANES, SSD_INNER))],
        out_specs=[seq(SSD_INNER), state],
        out_shape=[jax.ShapeDtypeStruct((b, tp, SSD_INNER), BF16),
                   jax.ShapeDtypeStruct((b, SSD_GROUPS, gw, SSD_STATE), F32)],
        scratch_shapes=[pltpu.VMEM((bb, SSD_GROUPS, SSD_STATE, gw), F32),
                        pltpu.VMEM((bb, BLK, SSD_INNER), F32)],
        compiler_params=_cparams(("parallel", "arbitrary")),
        name="ssd",
    )(*(lead if n_lead else ()), xbc, z, sm, ssm_prev, p["dtb"], p["dtb_t"],
      p["alog"], p["alog_t"], p["dskip_x"], p["ssd_norm_w"], p["expand"])


N_SHIFT_TILES = 5
LEAD_TILE = N_SHIFT_TILES
N_BIAS_TILES = N_SHIFT_TILES + 1


def _log_bucket_starts():
    nb = REL_BUCKETS // 2
    max_exact = nb // 2
    s = nb - max_exact
    starts = []
    for m in range(1, s):
        n = max_exact
        while n ** s * max_exact ** m < max_exact ** s * REL_MAX_DIST ** m:
            n += 1
        starts.append(n)
    return starts


def _bias_kernel(rb_ref, bt_ref):
    nb = REL_BUCKETS // 2
    max_exact = nb // 2
    qq = lax.broadcasted_iota(jnp.int32, (BLK, BLK), 0)
    kk = lax.broadcasted_iota(jnp.int32, (BLK, BLK), 1)
    for u in range(N_BIAS_TILES):
        rel = kk - qq + ((u - 2) * BLK if u < N_SHIFT_TILES else -N_META)
        n = jnp.abs(rel)
        large = max_exact + sum(jnp.where(n >= start, 1, 0) for start in _log_bucket_starts())
        bucket = jnp.where(rel > 0, nb, 0) + jnp.where(n < max_exact, n, large)
        for h in range(N_HEADS):
            acc = jnp.zeros((BLK, BLK), F32)
            for bkt in range(REL_BUCKETS):
                acc = jnp.where(bucket == bkt, rb_ref[bkt, h], acc)
            bt_ref[h * N_BIAS_TILES + u] = acc


def _bias_tiles(rel_bias):
    return pl.pallas_call(
        _bias_kernel,
        in_specs=[pl.BlockSpec(memory_space=pltpu.SMEM)],
        out_specs=pl.BlockSpec(memory_space=pltpu.VMEM),
        out_shape=jax.ShapeDtypeStruct((N_HEADS * N_BIAS_TILES, BLK, BLK), F32),
        name="bias_tiles",
    )(rel_bias)


KB = 2 * BLK


def _attn_kernel(*refs, nkp_total, n_past, n_lead, chunk_off, t_x, n_sel):
    if n_past:
        pki_ref, pk_ref, pv_ref, *refs = refs
    if n_lead:
        lki_ref, lk_ref, lv_ref, *refs = refs
    (qn_ref, qi_ref, sm_ref, ki_new_ref, k_new_ref, v_new_ref, bt_ref, o_ref,
     ki_scr, k_scr, v_scr, st_scr, m_scr, lg_scr, qis_scr, qs_scr, mrun_scr, lrun_scr, acc_scr) = refs
    i = pl.program_id(1)
    r = BLK
    n_prefix = n_past + (BLK if n_lead else 0)
    gap = BLK - n_lead if n_lead else 0
    l_valid = n_prefix - gap + t_x

    @pl.when(i == 0)
    def _():
        if n_lead:
            ki_scr[0:BLK, :] = lki_ref[:, SM_KI:SM_KI + IDX_DIM]
            k_scr[0:BLK, :] = lk_ref[...]
            v_scr[0:BLK, :] = lv_ref[...]
        if n_past:
            def load_past(c, carry):
                rows = pl.ds(pl.multiple_of(c * KB, KB), KB)
                ki_scr[rows, :] = pki_ref[rows, :].astype(BF16)
                for g in range(N_KV_HEADS):
                    src = pl.ds(pl.multiple_of(c * KB * N_KV_HEADS, KB) + g, KB, stride=N_KV_HEADS)
                    k_scr[rows, g * HEAD_DIM:(g + 1) * HEAD_DIM] = pk_ref[src, :].astype(BF16)
                    v_scr[rows, g * HEAD_DIM:(g + 1) * HEAD_DIM] = pv_ref[src, :].astype(BF16)
                return carry

            lax.fori_loop(0, n_past // KB, load_past, 0)
        n_new = k_new_ref.shape[0]
        ki_scr[n_prefix:n_prefix + n_new, :] = ki_new_ref[:, SM_KI:SM_KI + IDX_DIM]
        k_scr[n_prefix:n_prefix + n_new, :] = k_new_ref[...]
        v_scr[n_prefix:n_prefix + n_new, :] = v_new_ref[...]
        n_tail = nkp_total * KB - n_prefix - n_new
        if n_tail:
            for scr in (ki_scr, k_scr, v_scr):
                scr[n_prefix + n_new:, :] = jnp.zeros((n_tail, scr.shape[1]), BF16)

    qb = n_prefix // BLK + i
    q0 = qb * BLK - gap
    n_keys = nkp_total * KB

    def chunk_end(pos):
        return jnp.minimum(CHUNK * ((pos + chunk_off) // CHUNK + 1) - chunk_off, l_valid)

    nkp = jnp.minimum(nkp_total, (chunk_end(q0 + BLK - 1) + gap + KB - 1) // KB)

    nt = (((1,), (1,)), ((), ()))
    wit = sm_ref[...].T
    n_adm = chunk_end(q0 + lax.broadcasted_iota(jnp.int32, (1, r), 1))
    krow = lax.broadcasted_iota(jnp.int32, (KB, r), 0)

    fold_rows = 8 * SUBLANES

    def fold(x, op):
        return op(x.reshape(KB // fold_rows, fold_rows, r), axis=0)

    def for_key_steps(body, init):
        carry = lax.fori_loop(0, nkp // 2, lambda t, c: body(2 * t + 1, body(2 * t, c)), init)
        return lax.cond(nkp % 2 == 1, lambda c: body(nkp - 1, c), lambda c: c, carry)

    for h in range(IDX_HEADS):
        qis_scr[h * r:(h + 1) * r, :] = qi_ref[:, h * IDX_DIM:(h + 1) * IDX_DIM]

    def score_body(jp, carry):
        mn, mx = carry
        kij = ki_scr[pl.ds(pl.multiple_of(jp * KB, KB), KB), :]
        acc = jnp.zeros((KB, r), F32)
        for hp in range(IDX_HEADS // 2):
            sh = lax.dot_general(kij, qis_scr[2 * hp * r:2 * (hp + 1) * r, :], nt, preferred_element_type=F32)
            for e in range(2):
                h = 2 * hp + e
                acc = acc + wit[SM_WI + h:SM_WI + h + 1, :] * jnp.maximum(sh[:, e * r:(e + 1) * r], 0.0)
        adm = krow < n_adm + gap - jp * KB
        if gap:
            adm = adm & ((jp > 0) | (krow < n_lead) | (krow >= BLK))
        s = jnp.where(adm, acc * (IDX_DIM ** -0.5), -jnp.inf)
        st_scr[jp] = s
        mn = jnp.minimum(mn, fold(jnp.where(adm, s, jnp.inf), jnp.min))
        mx = jnp.maximum(mx, fold(s, jnp.max))
        return mn, mx

    init = (jnp.full((fold_rows, r), jnp.inf, F32), jnp.full((fold_rows, r), -jnp.inf, F32))
    mn, mx = for_key_steps(score_body, init)
    lo0 = jnp.min(mn, axis=0, keepdims=True)
    hi0 = jnp.max(mx, axis=0, keepdims=True)
    kk = jnp.minimum(n_adm, n_sel).astype(F32)

    def count(pred):
        def body(jp, acc):
            return acc + fold(jnp.where(pred(st_scr[jp], jp), 1.0, 0.0), jnp.sum)
        acc = lax.fori_loop(0, nkp, body, jnp.zeros((fold_rows, r), F32))
        return jnp.sum(acc, axis=0, keepdims=True)

    def search_body(_, carry):
        lo, hi, c_lo, c_hi, ub, c_ub, hit = carry
        frac = jnp.clip((c_lo - kk) / (c_lo - c_hi), SEARCH_CLAMP, 1.0 - SEARCH_CLAMP)
        mid = jnp.where(hit > 0.0, lo, lo * (1.0 - frac) + hi * frac)
        cnt = count(lambda s, jp: s >= mid)
        ok = cnt >= kk
        now = cnt == kk
        lo = jnp.where(ok, mid, lo)
        hi = jnp.where(ok & ~now, hi, mid)
        c_lo = jnp.where(ok, cnt, c_lo)
        c_hi = jnp.where(now, kk - 1.0, jnp.where(ok, c_hi, cnt))
        ub = jnp.where(ok, ub, mid)
        c_ub = jnp.where(ok, c_ub, cnt)
        return lo, hi, c_lo, c_hi, ub, c_ub, jnp.where(now, 1.0, 0.0)

    all_adm = n_adm.astype(F32) == kk
    zeros = jnp.zeros((1, r), F32)
    init = (lo0, jnp.where(all_adm, lo0, hi0), n_adm.astype(F32), jnp.where(all_adm, kk - 1.0, zeros),
            jnp.full((1, r), jnp.inf, F32), zeros, jnp.where(all_adm, 1.0, 0.0))
    lo_f, _, _, _, ub, c_ub, hit_f = lax.fori_loop(0, SEARCH_STEPS, search_body, init)
    hit = hit_f > 0.0
    pending = jnp.sum(jnp.where(hit, 0, 1))
    take_all = jnp.full((1, r), float(n_keys), F32)

    def exact_path():
        def next_below(ub):
            def body(jp, acc):
                s = st_scr[jp]
                return jnp.maximum(acc, fold(jnp.where(s < ub, s, -jnp.inf), jnp.max))
            acc = lax.fori_loop(0, nkp, body, jnp.full((fold_rows, r), -jnp.inf, F32))
            return jnp.max(acc, axis=0, keepdims=True)

        def descend_cond(carry):
            *_, todo, it = carry
            return (todo > 0) & (it < n_keys)

        def descend_body(carry):
            ub, c_ub, _, _, _, it = carry
            t = next_below(ub)
            c_t = count(lambda s, jp: s >= t)
            done = hit | (c_t >= kk)
            return (jnp.where(done, ub, t), jnp.where(done, c_ub, c_t), t, c_t,
                    jnp.sum(jnp.where(done, 0, 1)), it + 1)

        _, c_gt, t, c_t, _, _ = lax.while_loop(descend_cond, descend_body,
                                               (ub, c_ub, lo0, zeros, jnp.int32(1), jnp.int32(0)))
        ties_wanted = jnp.where(hit, take_all, kk - c_gt)
        extra_ties = jnp.sum(jnp.where(~hit & (c_t - c_gt > ties_wanted), 1, 0))
        return jnp.where(hit, lo_f, t), ties_wanted, extra_ties

    thr, ties_wanted, extra_ties = lax.cond(pending > 0, exact_path, lambda: (lo_f, take_all, jnp.int32(0)))

    def mask_with_ties():
        ii = lax.broadcasted_iota(jnp.int32, (KB, KB), 0)
        jj = lax.broadcasted_iota(jnp.int32, (KB, KB), 1)
        upto = (jj <= ii).astype(BF16)

        def body(jp, wanted):
            s = st_scr[jp]
            tied = s == thr
            rank = jnp.dot(upto, jnp.where(tied, 1.0, 0.0).astype(BF16), preferred_element_type=F32)
            m_scr[jp] = jnp.where((s > thr) | (tied & (rank <= wanted)), 0.0, NEG_BIG).T
            return wanted - rank[KB - 1:KB, :]

        lax.fori_loop(0, nkp, body, ties_wanted)

    scale = HEAD_DIM ** -0.5
    rep = N_HEADS // N_KV_HEADS
    rq = mrun_scr.shape[1]
    for h in range(N_HEADS):
        qs_scr[h // rep, (h % rep) * rq:(h % rep + 1) * rq, :] = qn_ref[0:rq, h * HEAD_DIM:(h + 1) * HEAD_DIM]
    mrun_scr[...] = jnp.full(mrun_scr.shape, -jnp.inf, F32)
    lrun_scr[...] = jnp.zeros(lrun_scr.shape, F32)
    acc_scr[...] = jnp.zeros(acc_scr.shape, F32)

    def logit_body(mask_of, jp, carry):
        u0 = jnp.clip(2 * jp - qb + 2, 0, N_SHIFT_TILES - 1)
        u1 = jnp.clip(2 * jp + 1 - qb + 2, 0, N_SHIFT_TILES - 1)
        if gap:
            u0 = jnp.where((jp == 0) & (qb == 1), LEAD_TILE, u0)
        madd = mask_of(jp)[0:rq, :]
        for g in range(N_KV_HEADS):
            lt = lax.dot_general(qs_scr[g], k_scr[pl.ds(pl.multiple_of(jp * KB, KB), KB),
                                                  g * HEAD_DIM:(g + 1) * HEAD_DIM], nt,
                                 preferred_element_type=F32)
            for e in range(rep):
                h = g * rep + e
                bias = jnp.concatenate([bt_ref[h * N_BIAS_TILES + u0][0:rq, :],
                                        bt_ref[h * N_BIAS_TILES + u1][0:rq, :]], axis=1)
                lg = lt[e * rq:(e + 1) * rq, :] * scale + bias + madd
                lg_scr[h, jp] = lg
                mrun_scr[h] = jnp.maximum(mrun_scr[h], jnp.maximum(lg[:, :BLK], lg[:, BLK:]))
        return carry

    def logits_with_ties():
        mask_with_ties()
        return for_key_steps(functools.partial(logit_body, lambda jp: m_scr[jp]), jnp.int32(0))

    def logits_plain():
        inline_mask = lambda jp: jnp.where(st_scr[jp] >= thr, 0.0, NEG_BIG).T
        return for_key_steps(functools.partial(logit_body, inline_mask), jnp.int32(0))

    lax.cond(extra_ties > 0, logits_with_ties, logits_plain)
    for h in range(N_HEADS):
        mrun_scr[h] = jnp.broadcast_to(jnp.max(mrun_scr[h], axis=1, keepdims=True), (rq, BLK))

    def pv_body(jp, carry):
        for g in range(N_KV_HEADS):
            es = []
            for e in range(rep):
                h = g * rep + e
                mrow = mrun_scr[h]
                ex = jnp.exp(lg_scr[h, jp] - jnp.concatenate([mrow, mrow], axis=1))
                lrun_scr[h] = lrun_scr[h] + (ex[:, :BLK] + ex[:, BLK:])
                es.append(ex.astype(BF16))
            acc_scr[g] = acc_scr[g] + jnp.dot(jnp.concatenate(es, axis=0),
                                              v_scr[pl.ds(pl.multiple_of(jp * KB, KB), KB),
                                                    g * HEAD_DIM:(g + 1) * HEAD_DIM],
                                              preferred_element_type=F32)
        return carry

    for_key_steps(pv_body, 0)
    for h in range(N_HEADS):
        g, e = h // rep, h % rep
        den = jnp.sum(lrun_scr[h], axis=1, keepdims=True)
        o_ref[0:rq, h * HEAD_DIM:(h + 1) * HEAD_DIM] = (acc_scr[g, e * rq:(e + 1) * rq, :] / den).astype(BF16)
    if rq < r:
        o_ref[rq:, :] = jnp.zeros((r - rq, ATT_INNER), BF16)


def _attn(qn, qi, sm, ki_new, k_new, v_new, past, lead, bias_tiles, *, n_lead, chunk_off, t_x, n_sel):
    b, tq, _ = qn.shape
    n_past = past[0].shape[1] if past is not None else 0
    assert n_past % KB == 0 and not (n_past and n_lead)
    n_prefix = n_past + (BLK if n_lead else 0)
    n_keys = -(-(n_prefix + tq) // KB) * KB
    nkp_total = n_keys // KB
    rep = N_HEADS // N_KV_HEADS
    rq = BLK if tq > BLK else min(BLK, -(-t_x // (2 * SUBLANES)) * 2 * SUBLANES)
    seq = lambda width: pl.BlockSpec((None, BLK, width), lambda bi, i: (bi, i, 0))
    rows = lambda n, width: pl.BlockSpec((None, n, width), lambda bi, i: (bi, 0, 0))
    kern = functools.partial(_attn_kernel, nkp_total=nkp_total, n_past=n_past, n_lead=n_lead, chunk_off=chunk_off,
                             t_x=t_x, n_sel=n_sel)
    past_specs = [rows(n_past, IDX_DIM), rows(n_past * N_KV_HEADS, HEAD_DIM),
                  rows(n_past * N_KV_HEADS, HEAD_DIM)] if n_past else []
    lead_specs = [_const_spec((BLK, LANES)), _const_spec((BLK, KV_DIM)), _const_spec((BLK, KV_DIM))] if n_lead else []
    return pl.pallas_call(
        kern,
        grid=(b, tq // BLK),
        in_specs=past_specs + lead_specs + [seq(ATT_INNER), seq(IDX_HEADS * IDX_DIM), seq(LANES),
                                            rows(tq, LANES), rows(tq, KV_DIM), rows(tq, KV_DIM),
                                            _const_spec((N_HEADS * N_BIAS_TILES, BLK, BLK))],
        out_specs=seq(ATT_INNER),
        out_shape=jax.ShapeDtypeStruct((b, tq, ATT_INNER), BF16),
        scratch_shapes=[pltpu.VMEM((n_keys, IDX_DIM), BF16),
                        pltpu.VMEM((n_keys, KV_DIM), BF16),
                        pltpu.VMEM((n_keys, KV_DIM), BF16),
                        pltpu.VMEM((nkp_total, KB, BLK), F32),
                        pltpu.VMEM((nkp_total, BLK, KB), F32),
                        pltpu.VMEM((N_HEADS, nkp_total, rq, KB), F32),
                        pltpu.VMEM((IDX_HEADS * BLK, IDX_DIM), BF16),
                        pltpu.VMEM((N_KV_HEADS, rep * rq, HEAD_DIM), BF16),
                        pltpu.VMEM((N_HEADS, rq, BLK), F32),
                        pltpu.VMEM((N_HEADS, rq, BLK), F32),
                        pltpu.VMEM((N_KV_HEADS, rep * rq, HEAD_DIM), F32)],
        compiler_params=_cparams(("parallel", "arbitrary")),
        name="attn",
    )(*(past or ()), *(lead or ()), qn, qi, sm, ki_new, k_new, v_new, bias_tiles)


def _out_ffn_kernel(x_ref, ys_ref, ya_ref, gs_ref, ga_ref, wbs_ref, wba_ref, wo_ref, n2_ref, wg_ref, wu_ref, wd_ref,
                    y_ref):
    dot = functools.partial(jnp.dot, preferred_element_type=F32)
    merged = (jax.nn.sigmoid(gs_ref[...]) * dot(ys_ref[...], wbs_ref[...])
              + jax.nn.sigmoid(ga_ref[...]) * dot(ya_ref[...], wba_ref[...]))
    h = x_ref[...] + dot(merged.astype(BF16), wo_ref[...])
    hn = _rms(h, n2_ref[...]).astype(BF16)
    act = (_silu(dot(hn, wg_ref[...])) * dot(hn, wu_ref[...])).astype(BF16)
    y_ref[...] = h + dot(act, wd_ref[...])


def _out_ffn(x2d, ys, ya, gs, ga, p):
    n = x2d.shape[0]
    tm = _row_tile(n)
    d_ff = p["w_gate"].shape[1]
    row = lambda width: pl.BlockSpec((tm, width), lambda i: (i, 0))
    wspec = lambda shape: pl.BlockSpec(shape, lambda i: (0, 0), pipeline_mode=pl.Buffered(1))
    return pl.pallas_call(
        _out_ffn_kernel,
        grid=(n // tm,),
        in_specs=[row(D_MODEL), row(SSD_INNER), row(ATT_INNER), row(D_MODEL), row(D_MODEL),
                  wspec((SSD_INNER, D_MODEL)), wspec((ATT_INNER, D_MODEL)), wspec((D_MODEL, D_MODEL)),
                  _const_spec((1, D_MODEL)), wspec((D_MODEL, d_ff)), wspec((D_MODEL, d_ff)), wspec((d_ff, D_MODEL))],
        out_specs=row(D_MODEL),
        out_shape=jax.ShapeDtypeStruct((n, D_MODEL), F32),
        compiler_params=_cparams(("parallel",)),
        name="out_ffn",
    )(x2d, ys, ya, gs, ga, p["w_br_ssd"], p["w_br_att"], p["w_out"], p["norm2_w"], p["w_gate"], p["w_up"],
      p["w_down"])


def _layer(x, lead_rows, conv_prev, ssm_prev, past, p, bias_tiles, *, chunk_off, n_sel):
    b, t, _ = x.shape
    n_lead = lead_rows.shape[0]
    tp = -(-t // BLK) * BLK
    x2d = jnp.pad(x, ((0, 0), (0, tp - t), (0, 0))).reshape(b * tp, D_MODEL)
    hist = jnp.pad(conv_prev.astype(F32), ((0, 0), (HALO - (CONV_WIDTH - 1), 0), (0, 0)))
    seq = lambda a: a.reshape(b, tp, a.shape[-1])
    ssd_lead = attn_lead = None
    if n_lead:
        lead = _in_proj(jnp.pad(lead_rows.astype(x.dtype), ((0, BLK - n_lead), (0, 0))), hist[:1], p,
                        n_seq=1, t_x=n_lead)
        lz, lxbc, _, lk32, lv32, lkb, lvb, _, lsm, lsmb, _, _, hist = lead
        ssd_lead, attn_lead = (lxbc, lz, lsm), (lsmb, lkb, lvb)
    z, xbc, qn, k32, v32, kb, vb, qi, sm, smb, gs, ga, conv_new8 = _in_proj(x2d, hist, p, n_seq=b, t_x=t)

    gw = SSD_HEADS // SSD_GROUPS * SSD_HEAD_DIM
    y_ssd, ssm_new = _ssd(ssd_lead, seq(xbc), seq(z), seq(sm),
                          ssm_prev.astype(F32).reshape(b, SSD_GROUPS, gw, SSD_STATE), p, n_lead=n_lead, t_x=t)

    if past is not None:
        pk, pv, pki = past
        n_past = pk.shape[1]
        past = (pki.astype(F32), pk.astype(F32).reshape(b, n_past * N_KV_HEADS, HEAD_DIM),
                pv.astype(F32).reshape(b, n_past * N_KV_HEADS, HEAD_DIM))
    y_att = _attn(seq(qn), seq(qi), seq(sm), seq(smb), seq(kb), seq(vb), past, attn_lead, bias_tiles,
                  n_lead=n_lead, chunk_off=chunk_off, t_x=t, n_sel=n_sel)

    y = _out_ffn(x2d, y_ssd.reshape(b * tp, SSD_INNER), y_att.reshape(b * tp, ATT_INNER), gs, ga, p)

    def with_lead(new, lead_part):
        if not n_lead:
            return new
        return jnp.concatenate([jnp.broadcast_to(lead_part[None, :n_lead], (b, n_lead) + new.shape[2:]), new], axis=1)

    heads = lambda a, rows: a.reshape(-1, rows, N_KV_HEADS, HEAD_DIM)
    k_new = with_lead(heads(k32, tp)[:, :t], heads(lk32, BLK)[0] if n_lead else None)
    v_new = with_lead(heads(v32, tp)[:, :t], heads(lv32, BLK)[0] if n_lead else None)
    ki_new = with_lead(seq(sm)[:, :t, SM_KI:SM_KI + IDX_DIM], lsm[:, SM_KI:SM_KI + IDX_DIM] if n_lead else None)
    ssm_new = ssm_new.reshape(b, SSD_HEADS, SSD_HEAD_DIM, SSD_STATE)
    conv_new = conv_new8[:, HALO - (CONV_WIDTH - 1):]
    return y.reshape(b, tp, D_MODEL)[:, :t], k_new, v_new, ki_new, ssm_new, conv_new


def _prepare_params(l, norm1_w, w_in, conv_w, conv_b, dt_bias, a_log, d_skip, ssd_norm_w, q_norm_w, k_norm_w,
                    idx_k_norm_w, w_br_ssd, w_br_att, w_out, norm2_w, w_gate, w_up, w_down):
    offs = [0]
    for w in IN_WIDTHS:
        offs.append(offs[-1] + w)
    i_z, i_xbc, i_dt, i_q, i_k, i_v, i_qi, i_ki, i_wi, i_gs, i_ga = range(11)
    run = lambda first, last: w_in[l][:, offs[first]:offs[last + 1]].astype(BF16)
    pad = jnp.zeros((D_MODEL, LANES - IDX_DIM - SSD_HEADS - IDX_HEADS), BF16)
    w_perm = jnp.concatenate([run(i_z, i_xbc), run(i_q, i_qi), run(i_gs, i_ga),
                              run(i_ki, i_ki), run(i_dt, i_dt), run(i_wi, i_wi), pad], axis=1)

    def lanes_at(vec, start):
        return jnp.zeros((1, LANES), F32).at[0, start:start + vec.shape[0]].set(vec.astype(F32))

    dtb = lanes_at(dt_bias[l], SM_DT)
    alog = lanes_at(a_log[l], SM_DT)
    head_of_channel = jnp.arange(SSD_INNER) // SSD_HEAD_DIM
    expand = (jnp.arange(LANES)[:, None] == head_of_channel[None, :] + SM_DT).astype(BF16)
    row = lambda v: v.astype(F32).reshape(1, -1)
    return dict(
        norm1_w=row(norm1_w[l]), w_in=w_perm, conv_w=conv_w[l].astype(F32), conv_b=row(conv_b[l]),
        dtb=dtb, dtb_t=dtb.reshape(LANES, 1), alog=alog, alog_t=alog.reshape(LANES, 1),
        dskip_x=row(jnp.repeat(d_skip[l], SSD_HEAD_DIM)), ssd_norm_w=row(ssd_norm_w[l]), expand=expand,
        q_norm_w=row(q_norm_w[l]), k_norm_w=row(k_norm_w[l]),
        idx_k_norm_w=jnp.ones((1, LANES), F32).at[0, SM_KI:SM_KI + IDX_DIM].set(idx_k_norm_w[l].astype(F32)),
        w_br_ssd=w_br_ssd[l].astype(BF16), w_br_att=w_br_att[l].astype(BF16), w_out=w_out[l].astype(BF16),
        norm2_w=row(norm2_w[l]), w_gate=w_gate[l].astype(BF16), w_up=w_up[l].astype(BF16),
        w_down=w_down[l].astype(BF16))


def kernel(x_prompt, x_sample, cache_k, cache_v, cache_kidx, state_ssm, state_conv, meta_tokens, rel_bias, norm1_w,
           w_in, conv_w, conv_b, dt_bias, a_log, d_skip, ssd_norm_w, q_norm_w, k_norm_w, idx_k_norm_w, w_br_ssd,
           w_br_att, w_out, norm2_w, w_gate, w_up, w_down):
    bp, sp, _ = x_prompt.shape
    bs, ts, _ = x_sample.shape
    past = cache_k.shape[2]
    assert w_in.shape[0] == 1
    assert past % BLK == 0 and BLK % CHUNK == 0 and N_META <= CHUNK
    l = 0

    n_sel_p = min(TOPK_MAX, sp // 4)
    n_sel_s = min(TOPK_MAX, (past + ts) // 4)
    conv0 = jnp.zeros((bp, CONV_WIDTH - 1, CONV_DIM), F32)
    ssm0 = jnp.zeros((bp, SSD_HEADS, SSD_HEAD_DIM, SSD_STATE), F32)
    bias_tiles = _bias_tiles(rel_bias.astype(F32))
    p = _prepare_params(l, norm1_w, w_in, conv_w, conv_b, dt_bias, a_log, d_skip, ssd_norm_w, q_norm_w, k_norm_w,
                        idx_k_norm_w, w_br_ssd, w_br_att, w_out, norm2_w, w_gate, w_up, w_down)
    y_prompt, *rest_p = _layer(x_prompt, meta_tokens, conv0, ssm0, None, p, bias_tiles,
                               chunk_off=CHUNK - N_META, n_sel=n_sel_p)
    y_sample, *rest_s = _layer(x_sample, meta_tokens[:0], state_conv[l], state_ssm[l],
                               (cache_k[l], cache_v[l], cache_kidx[l]), p, bias_tiles,
                               chunk_off=0, n_sel=n_sel_s)

    dtypes = (x_prompt.dtype, x_prompt.dtype, x_prompt.dtype, state_ssm.dtype, x_prompt.dtype)
    return (y_prompt, y_sample, *(o[None].astype(dt) for o, dt in zip(rest_p, dtypes)),
            *(o[None].astype(dt) for o, dt in zip(rest_s, dtypes)))
```

```python
import functools

import jax
import jax.numpy as jnp
from jax import lax
from jax.experimental import pallas as pl
from jax.experimental.pallas import tpu as pltpu

F32 = jnp.float32
BF16 = jnp.bfloat16

D_MODEL = 1024
CHUNK = 64
N_META = 16
SSD_HEADS = 16
SSD_HEAD_DIM = 64
SSD_INNER = SSD_HEADS * SSD_HEAD_DIM
SSD_GROUPS = 4
SSD_STATE = 128
CONV_WIDTH = 4
CONV_DIM = SSD_INNER + 2 * SSD_GROUPS * SSD_STATE
N_HEADS = 8
N_KV_HEADS = 2
HEAD_DIM = 128
ATT_INNER = N_HEADS * HEAD_DIM
KV_DIM = N_KV_HEADS * HEAD_DIM
IDX_HEADS = 8
IDX_DIM = 64
TOPK_MAX = 256
REL_BUCKETS = 32
REL_MAX_DIST = 128
IN_WIDTHS = (SSD_INNER, CONV_DIM, SSD_HEADS, ATT_INNER, KV_DIM, KV_DIM, IDX_HEADS * IDX_DIM, IDX_DIM, IDX_HEADS,
             D_MODEL, D_MODEL)
EPS = 1e-6

LANES = 128
SUBLANES = 8
VMEM_LIMIT_BYTES = 56 * 1024 * 1024

BLK = LANES
ROW_TILE = 256
IN_PROJ_PASSES = 2
HALO = SUBLANES
SSD_SEQS_PER_STEP = 4

C_Z = 0
C_XBC = C_Z + SSD_INNER
C_Q = C_XBC + CONV_DIM
C_K = C_Q + ATT_INNER
C_V = C_K + KV_DIM
C_QI = C_V + KV_DIM
C_GS = C_QI + IDX_HEADS * IDX_DIM
C_GA = C_GS + D_MODEL
C_SM = C_GA + D_MODEL
IN_PAD = C_SM + LANES
SM_KI = 0
SM_DT = SM_KI + IDX_DIM
SM_WI = SM_DT + SSD_HEADS

SEARCH_STEPS = 16
SEARCH_CLAMP = 1.0 / 16
NEG_BIG = -1e30


def _cparams(sem):
    return pltpu.CompilerParams(dimension_semantics=sem, vmem_limit_bytes=VMEM_LIMIT_BYTES)


def _const_spec(shape):
    nd = len(shape)
    return pl.BlockSpec(shape, lambda *_: (0,) * nd)


def _rms(x, w):
    return x * lax.rsqrt(jnp.mean(x * x, axis=-1, keepdims=True) + EPS) * w


def _silu(x):
    return x * jax.nn.sigmoid(x)


def _in_proj_kernel(x_ref, hist_ref, n1_ref, w_ref, qn_ref, kn_ref, kin_ref, cw_ref, cb_ref,
                    z_ref, xc_ref, q_ref, k_ref, v_ref, kb_ref, vb_ref, qi_ref, sm_ref, smb_ref, gs_ref, ga_ref, tail_ref,
                    xpad_scr, *, tiles_per_seq, t_x):
    j = pl.program_id(1)
    tm = x_ref.shape[0]

    @pl.when(j == 0)
    def _():
        xpad_scr[0:HALO, :] = hist_ref[...]

    sub = ROW_TILE if tm % ROW_TILE == 0 else tm
    for r0 in range(0, tm, sub):
        rows, kv_rows = pl.ds(r0, sub), pl.ds(r0 * N_KV_HEADS, sub * N_KV_HEADS)
        _in_proj_rows(x_ref.at[rows], n1_ref, w_ref, qn_ref, kn_ref, kin_ref, cw_ref, cb_ref,
                      z_ref.at[rows], xc_ref.at[rows], q_ref.at[rows], k_ref.at[kv_rows], v_ref.at[kv_rows],
                      kb_ref.at[rows], vb_ref.at[rows], qi_ref.at[rows], sm_ref.at[rows], smb_ref.at[rows],
                      gs_ref.at[rows], ga_ref.at[rows], xpad_scr.at[pl.ds(r0, sub + 2 * HALO)])

    n_last = t_x - (tiles_per_seq - 1) * tm

    @pl.when(j == tiles_per_seq - 1)
    def _():
        tail_ref[...] = xpad_scr[n_last:n_last + HALO, :]

    xpad_scr[0:HALO, :] = xpad_scr[tm:tm + HALO, :]


def _in_proj_rows(x_ref, n1_ref, w_ref, qn_ref, kn_ref, kin_ref, cw_ref, cb_ref,
                  z_ref, xc_ref, q_ref, k_ref, v_ref, kb_ref, vb_ref, qi_ref, sm_ref, smb_ref, gs_ref, ga_ref, xpad_scr):
    hn = _rms(x_ref[...], n1_ref[...]).astype(BF16)
    tm = x_ref.shape[0]

    def mm(lo, hi):
        return jnp.dot(hn, w_ref[:, lo:hi], preferred_element_type=F32)

    step = 2 * LANES
    conv_slabs = iter(range(0, CONV_DIM, LANES))

    def conv_next():
        c0 = next(conv_slabs, None)
        if c0 is None:
            return
        if c0 % step == 0:
            xpad_scr[HALO:HALO + tm, c0:c0 + step] = mm(C_XBC + c0, C_XBC + c0 + step)
        sl = slice(c0, c0 + LANES)
        first = HALO - (CONV_WIDTH - 1)
        acc = xpad_scr[first:first + tm, sl] * cw_ref[0:1, sl]
        for i in range(1, CONV_WIDTH):
            acc = acc + xpad_scr[first + i:first + i + tm, sl] * cw_ref[i:i + 1, sl]
        xc_ref[:, sl] = _silu(cb_ref[:, sl] + acc)

    def project(lo, hi, out_ref, dtype=F32):
        for c0 in range(0, hi - lo, step):
            out_ref[:, c0:c0 + step] = mm(lo + c0, lo + c0 + step).astype(dtype)
            conv_next()

    project(C_Z, C_XBC, z_ref)
    for c0 in range(0, ATT_INNER, step):
        q = mm(C_Q + c0, C_Q + c0 + step)
        conv_next()
        for h in range(step // HEAD_DIM):
            sl = slice(h * HEAD_DIM, (h + 1) * HEAD_DIM)
            q_ref[:, c0 + h * HEAD_DIM:c0 + (h + 1) * HEAD_DIM] = _rms(q[:, sl], qn_ref[...]).astype(BF16)
    k = mm(C_K, C_V)
    conv_next()
    v = mm(C_V, C_QI)
    conv_next()
    for h in range(N_KV_HEADS):
        sl = slice(h * HEAD_DIM, (h + 1) * HEAD_DIM)
        kh = _rms(k[:, sl], kn_ref[...])
        k_ref[pl.ds(h, tm, stride=N_KV_HEADS), :] = kh
        kb_ref[:, sl] = kh.astype(BF16)
        v_ref[pl.ds(h, tm, stride=N_KV_HEADS), :] = v[:, sl]
    vb_ref[...] = v.astype(BF16)
    project(C_QI, C_GS, qi_ref, BF16)
    project(C_GS, C_GA, gs_ref)
    project(C_GA, C_SM, ga_ref)
    sm = mm(C_SM, IN_PAD)
    assert next(conv_slabs, None) is None
    lane = lax.broadcasted_iota(jnp.int32, sm.shape, 1)
    is_ki = lane < SM_KI + IDX_DIM
    ms = jnp.sum(jnp.where(is_ki, sm * sm, 0.0), axis=-1, keepdims=True) * (1.0 / IDX_DIM)
    ki = sm * lax.rsqrt(ms + EPS) * kin_ref[...]
    is_wi = (lane >= SM_WI) & (lane < SM_WI + IDX_HEADS)
    out = jnp.where(is_ki, ki, jnp.where(is_wi, sm * (IDX_HEADS ** -0.5), sm))
    sm_ref[...] = out
    smb_ref[...] = out.astype(BF16)


def _row_tile(n):
    return ROW_TILE if n % ROW_TILE == 0 else BLK


def _in_proj(x2d, hist, p, *, n_seq, t_x):
    n = x2d.shape[0]
    tp = n // n_seq
    tm = _row_tile(tp)
    if tp % (IN_PROJ_PASSES * tm) == 0:
        tm *= IN_PROJ_PASSES
    tiles = tp // tm
    shared_hist = hist.shape[0] == 1
    row = lambda width, per_token=1: pl.BlockSpec((tm * per_token, width), lambda s, j: (s * tiles + j, 0))
    per_seq = pl.BlockSpec((None, HALO, CONV_DIM), lambda s, j: (0 if shared_hist else s, 0, 0))
    outs = [
        (SSD_INNER, F32, 1), (CONV_DIM, F32, 1), (ATT_INNER, BF16, 1), (HEAD_DIM, F32, N_KV_HEADS),
        (HEAD_DIM, F32, N_KV_HEADS), (KV_DIM, BF16, 1), (KV_DIM, BF16, 1), (IDX_HEADS * IDX_DIM, BF16, 1),
        (LANES, F32, 1), (LANES, BF16, 1), (D_MODEL, F32, 1), (D_MODEL, F32, 1),
    ]
    kern = functools.partial(_in_proj_kernel, tiles_per_seq=tiles, t_x=t_x)
    return pl.pallas_call(
        kern,
        grid=(n_seq, tiles),
        in_specs=[row(D_MODEL), per_seq, _const_spec((1, D_MODEL)),
                  pl.BlockSpec((D_MODEL, IN_PAD), lambda s, j: (0, 0), pipeline_mode=pl.Buffered(1)),
                  _const_spec((1, HEAD_DIM)), _const_spec((1, HEAD_DIM)), _const_spec((1, LANES)),
                  _const_spec((CONV_WIDTH, CONV_DIM)), _const_spec((1, CONV_DIM))],
        out_specs=[row(w, per) for w, _, per in outs]
        + [pl.BlockSpec((None, HALO, CONV_DIM), lambda s, j: (s, 0, 0))],
        out_shape=[jax.ShapeDtypeStruct((n * per, w), dt) for w, dt, per in outs]
        + [jax.ShapeDtypeStruct((n_seq, HALO, CONV_DIM), F32)],
        scratch_shapes=[pltpu.VMEM((tm + 2 * HALO, CONV_DIM), F32)],
        compiler_params=_cparams(("parallel", "arbitrary")),
        name="in_proj",
    )(x2d, hist, p["norm1_w"], p["w_in"], p["q_norm_w"], p["k_norm_w"], p["idx_k_norm_w"], p["conv_w"], p["conv_b"])


def _softplus(x):
    return jnp.maximum(x, 0.0) + jnp.log1p(jnp.exp(-jnp.abs(x)))


def _split3(x):
    hi = x.astype(BF16)
    r1 = x - hi.astype(F32)
    mid = r1.astype(BF16)
    lo = (r1 - mid.astype(F32)).astype(BF16)
    return hi, mid, lo


def _ssd_kernel(*refs, n_chunks, n_lead, t_x):
    if n_lead:
        xbc_lead_ref, z_lead_ref, sm_lead_ref, *refs = refs
    (xbc_ref, z_ref, sm_ref, sprev_ref, dtb_ref, dtbt_ref, alog_ref, alogt_ref,
     dsk_ref, nw_ref, exp_ref, y_ref, snew_ref, s_scr, y_scr) = refs
    c = pl.program_id(1)
    lead_chunks = 1 if n_lead else 0
    seqs = range(xbc_ref.shape[0])

    @pl.when(c == 0)
    def _():
        for k in seqs:
            for g in range(SSD_GROUPS):
                s_scr[k, g] = sprev_ref[k, g].T

    def chunk(k, xc_ref, z_src_ref, sm_src_ref, n_valid):
        _ssd_chunk(xc_ref, z_src_ref, sm_src_ref, n_valid, dtb_ref, dtbt_ref, alog_ref, alogt_ref, dsk_ref, nw_ref,
                   exp_ref, y_ref.at[k], s_scr.at[k], y_scr.at[k])

    def x_chunks():
        for k in seqs:
            chunk(k, xbc_ref.at[k], z_ref.at[k], sm_ref.at[k], jnp.minimum(BLK, t_x - (c - lead_chunks) * BLK))

    if n_lead:
        @pl.when(c == 0)
        def _():
            for k in seqs:
                chunk(k, xbc_lead_ref, z_lead_ref, sm_lead_ref, n_lead)

        pl.when(c > 0)(x_chunks)
    else:
        x_chunks()

    @pl.when(c == n_chunks - 1)
    def _():
        for k in seqs:
            for g in range(SSD_GROUPS):
                snew_ref[k, g] = s_scr[k, g].T


def _ssd_chunk(xc_ref, z_ref, sm_ref, n_valid, dtb_ref, dtbt_ref, alog_ref, alogt_ref, dsk_ref, nw_ref, exp_ref,
               y_ref, s_scr, y_scr):
    q = BLK
    gw = SSD_HEADS // SSD_GROUPS * SSD_HEAD_DIM
    sm = sm_ref[...]

    heads = slice(SM_DT, SM_DT + SSD_HEADS)
    smt = sm.T[heads, :]
    lane = lax.broadcasted_iota(jnp.int32, (q, LANES), 1)
    row = lax.broadcasted_iota(jnp.int32, (q, LANES), 0)
    is_dt = (lane >= SM_DT) & (lane < SM_DT + SSD_HEADS) & (row < n_valid)
    time_t = lax.broadcasted_iota(jnp.int32, (SSD_HEADS, q), 1)
    dt = jnp.where(is_dt, _softplus(sm + dtb_ref[...]), 0.0)
    dtt = jnp.where(time_t < n_valid, _softplus(smt + dtbt_ref[heads, :]), 0.0)
    da = dt * (-jnp.exp(alog_ref[...]))
    dat = dtt * (-jnp.exp(alogt_ref[heads, :]))
    ii = lax.broadcasted_iota(jnp.int32, (q, q), 0)
    jj = lax.broadcasted_iota(jnp.int32, (q, q), 1)
    causal = jj <= ii
    acum = sum(jnp.dot(causal.astype(BF16), p, preferred_element_type=F32) for p in _split3(da))
    acumt = sum(jnp.dot(p, (ii <= jj).astype(BF16), preferred_element_type=F32) for p in _split3(dat))
    a_last = acum[q - 1:q, :]
    expand = exp_ref[...]
    stacked = jnp.concatenate([jnp.exp(acum), jnp.exp(a_last - acum) * dt,
                               jnp.broadcast_to(jnp.exp(a_last), (SUBLANES, LANES))], axis=0)
    stacked_x = sum(jnp.dot(p, expand, preferred_element_type=F32) for p in _split3(stacked))
    ea_x = stacked_x[0:q]
    wdt_x = stacked_x[q:2 * q]
    dec_x = stacked_x[2 * q:2 * q + 1]

    sq = None
    for g in range(SSD_GROUPS):
        gsl = slice(g * gw, (g + 1) * gw)
        bsl = slice(SSD_INNER + g * SSD_STATE, SSD_INNER + (g + 1) * SSD_STATE)
        csl = slice(SSD_INNER + SSD_GROUPS * SSD_STATE + g * SSD_STATE,
                    SSD_INNER + SSD_GROUPS * SSD_STATE + (g + 1) * SSD_STATE)
        bmf = xc_ref[:, bsl]
        bm = bmf.astype(BF16)
        cm = xc_ref[:, csl].astype(BF16)
        xg = xc_ref[:, gsl]
        xgb = xg.astype(BF16)
        cbm = lax.dot_general(cm, bm, (((1,), (1,)), ((), ())), preferred_element_type=F32)
        xw = (xg * wdt_x[:, gsl]).astype(BF16)
        st = jnp.dot(bmf.T.astype(BF16), xw, preferred_element_type=F32)
        s_in = s_scr[g]
        y_off = jnp.dot(cm, s_in.astype(BF16), preferred_element_type=F32) * ea_x[:, gsl]
        s_scr[g] = s_in * dec_x[:, gsl] + st
        first_of_pair = lax.broadcasted_iota(jnp.int32, (q, LANES), 1) < SSD_HEAD_DIM
        for rp in range(gw // LANES):
            psl = slice(rp * LANES, (rp + 1) * LANES)
            osl = slice(g * gw + rp * LANES, g * gw + (rp + 1) * LANES)
            pair = []
            for h in (g * (SSD_HEADS // SSD_GROUPS) + 2 * rp, g * (SSD_HEADS // SSD_GROUPS) + 2 * rp + 1):
                seg = acum[:, SM_DT + h:SM_DT + h + 1] - acumt[h:h + 1, :]
                lmat = jnp.exp(jnp.where(causal, seg, -jnp.inf))
                wmat = (cbm * lmat * dtt[h:h + 1, :]).astype(BF16)
                pair.append(jnp.dot(wmat, xgb[:, psl], preferred_element_type=F32))
            y_diag = jnp.where(first_of_pair, pair[0], pair[1])
            yp = (y_diag + y_off[:, psl] + xg[:, psl] * dsk_ref[:, osl]) * _silu(z_ref[:, osl])
            y_scr[:, osl] = yp
            sq = yp * yp if sq is None else sq + yp * yp

    ms = jnp.sum(sq, axis=-1, keepdims=True) * (1.0 / SSD_INNER)
    y_ref[...] = (y_scr[...] * lax.rsqrt(ms + EPS) * nw_ref[...]).astype(BF16)


def _ssd(lead, xbc, z, sm, ssm_prev, p, *, n_lead, t_x):
    b, tp, _ = xbc.shape
    lead_chunks = 1 if n_lead else 0
    assert n_lead % SUBLANES == 0 and n_lead <= BLK and 2 * SSD_HEAD_DIM == LANES
    nc = tp // BLK + lead_chunks
    gw = SSD_HEADS // SSD_GROUPS * SSD_HEAD_DIM
    bb = SSD_SEQS_PER_STEP if b % SSD_SEQS_PER_STEP == 0 else 1
    seq = lambda width: pl.BlockSpec((bb, BLK, width), lambda i, c: (i, jnp.maximum(c - lead_chunks, 0), 0))
    state = pl.BlockSpec((bb, SSD_GROUPS, gw, SSD_STATE), lambda i, c: (i, 0, 0, 0))
    kern = functools.partial(_ssd_kernel, n_chunks=nc, n_lead=n_lead, t_x=t_x)
    lead_specs = [_const_spec((BLK, CONV_DIM)), _const_spec((BLK, SSD_INNER)), _const_spec((BLK, LANES))]
    return pl.pallas_call(
        kern,
        grid=(b // bb, nc),
        in_specs=(lead_specs if n_lead else []) + [
                  seq(CONV_DIM), seq(SSD_INNER), seq(LANES), state,
                  _const_spec((1, LANES)), _const_spec((LANES, 1)), _const_spec((1, LANES)), _const_spec((LANES, 1)),
                  _const_spec((1, SSD_INNER)), _const_spec((1, SSD_INNER)), _const_spec((LANES, SSD_INNER))],
        out_specs=[seq(SSD_INNER), state],
        out_shape=[jax.ShapeDtypeStruct((b, tp, SSD_INNER), BF16),
                   jax.ShapeDtypeStruct((b, SSD_GROUPS, gw, SSD_STATE), F32)],
        scratch_shapes=[pltpu.VMEM((bb, SSD_GROUPS, SSD_STATE, gw), F32),
                        pltpu.VMEM((bb, BLK, SSD_INNER), F32)],
        compiler_params=_cparams(("parallel", "arbitrary")),
        name="ssd",
    )(*(lead if n_lead else ()), xbc, z, sm, ssm_prev, p["dtb"], p["dtb_t"],
      p["alog"], p["alog_t"], p["dskip_x"], p["ssd_norm_w"], p["expand"])


N_SHIFT_TILES = 5
LEAD_TILE = N_SHIFT_TILES
N_BIAS_TILES = N_SHIFT_TILES + 1


def _log_bucket_starts():
    nb = REL_BUCKETS // 2
    max_exact = nb // 2
    s = nb - max_exact
    starts = []
    for m in range(1, s):
        n = max_exact
        while n ** s * max_exact ** m < max_exact ** s * REL_MAX_DIST ** m:
            n += 1
        starts.append(n)
    return starts


def _bias_kernel(rb_ref, bt_ref):
    nb = REL_BUCKETS // 2
    max_exact = nb // 2
    qq = lax.broadcasted_iota(jnp.int32, (BLK, BLK), 0)
    kk = lax.broadcasted_iota(jnp.int32, (BLK, BLK), 1)
    for u in range(N_BIAS_TILES):
        rel = kk - qq + ((u - 2) * BLK if u < N_SHIFT_TILES else -N_META)
        n = jnp.abs(rel)
        large = max_exact + sum(jnp.where(n >= start, 1, 0) for start in _log_bucket_starts())
        bucket = jnp.where(rel > 0, nb, 0) + jnp.where(n < max_exact, n, large)
        for h in range(N_HEADS):
            acc = jnp.zeros((BLK, BLK), F32)
            for bkt in range(REL_BUCKETS):
                acc = jnp.where(bucket == bkt, rb_ref[bkt, h], acc)
            bt_ref[h * N_BIAS_TILES + u] = acc


def _bias_tiles(rel_bias):
    return pl.pallas_call(
        _bias_kernel,
        in_specs=[pl.BlockSpec(memory_space=pltpu.SMEM)],
        out_specs=pl.BlockSpec(memory_space=pltpu.VMEM),
        out_shape=jax.ShapeDtypeStruct((N_HEADS * N_BIAS_TILES, BLK, BLK), F32),
        name="bias_tiles",
    )(rel_bias)


KB = 2 * BLK


def _attn_kernel(*refs, nkp_total, n_past, n_lead, chunk_off, t_x, n_sel):
    if n_past:
        pki_ref, pk_ref, pv_ref, *refs = refs
    if n_lead:
        lki_ref, lk_ref, lv_ref, *refs = refs
    (qn_ref, qi_ref, sm_ref, ki_new_ref, k_new_ref, v_new_ref, bt_ref, o_ref,
     ki_scr, k_scr, v_scr, st_scr, m_scr, lg_scr, qis_scr, qs_scr, mrun_scr, lrun_scr, acc_scr) = refs
    i = pl.program_id(1)
    r = BLK
    n_prefix = n_past + (BLK if n_lead else 0)
    gap = BLK - n_lead if n_lead else 0
    l_valid = n_prefix - gap + t_x

    @pl.when(i == 0)
    def _():
        if n_lead:
            ki_scr[0:BLK, :] = lki_ref[:, SM_KI:SM_KI + IDX_DIM]
            k_scr[0:BLK, :] = lk_ref[...]
            v_scr[0:BLK, :] = lv_ref[...]
        if n_past:
            def load_past(c, carry):
                rows = pl.ds(pl.multiple_of(c * KB, KB), KB)
                ki_scr[rows, :] = pki_ref[rows, :].astype(BF16)
                for g in range(N_KV_HEADS):
                    src = pl.ds(pl.multiple_of(c * KB * N_KV_HEADS, KB) + g, KB, stride=N_KV_HEADS)
                    k_scr[rows, g * HEAD_DIM:(g + 1) * HEAD_DIM] = pk_ref[src, :].astype(BF16)
                    v_scr[rows, g * HEAD_DIM:(g + 1) * HEAD_DIM] = pv_ref[src, :].astype(BF16)
                return carry

            lax.fori_loop(0, n_past // KB, load_past, 0)
        n_new = k_new_ref.shape[0]
        ki_scr[n_prefix:n_prefix + n_new, :] = ki_new_ref[:, SM_KI:SM_KI + IDX_DIM]
        k_scr[n_prefix:n_prefix + n_new, :] = k_new_ref[...]
        v_scr[n_prefix:n_prefix + n_new, :] = v_new_ref[...]
        n_tail = nkp_total * KB - n_prefix - n_new
        if n_tail:
            for scr in (ki_scr, k_scr, v_scr):
                scr[n_prefix + n_new:, :] = jnp.zeros((n_tail, scr.shape[1]), BF16)

    qb = n_prefix // BLK + i
    q0 = qb * BLK - gap
    n_keys = nkp_total * KB

    def chunk_end(pos):
        return jnp.minimum(CHUNK * ((pos + chunk_off) // CHUNK + 1) - chunk_off, l_valid)

    nkp = jnp.minimum(nkp_total, (chunk_end(q0 + BLK - 1) + gap + KB - 1) // KB)

    nt = (((1,), (1,)), ((), ()))
    wit = sm_ref[...].T
    n_adm = chunk_end(q0 + lax.broadcasted_iota(jnp.int32, (1, r), 1))
    krow = lax.broadcasted_iota(jnp.int32, (KB, r), 0)

    fold_rows = 8 * SUBLANES

    def fold(x, op):
        return op(x.reshape(KB // fold_rows, fold_rows, r), axis=0)

    def for_key_steps(body, init):
        carry = lax.fori_loop(0, nkp // 2, lambda t, c: body(2 * t + 1, body(2 * t, c)), init)
        return lax.cond(nkp % 2 == 1, lambda c: body(nkp - 1, c), lambda c: c, carry)

    for h in range(IDX_HEADS):
        qis_scr[h * r:(h + 1) * r, :] = qi_ref[:, h * IDX_DIM:(h + 1) * IDX_DIM]

    def score_body(jp, carry):
        mn, mx = carry
        kij = ki_scr[pl.ds(pl.multiple_of(jp * KB, KB), KB), :]
        acc = jnp.zeros((KB, r), F32)
        for hp in range(IDX_HEADS // 2):
            sh = lax.dot_general(kij, qis_scr[2 * hp * r:2 * (hp + 1) * r, :], nt, preferred_element_type=F32)
            for e in range(2):
                h = 2 * hp + e
                acc = acc + wit[SM_WI + h:SM_WI + h + 1, :] * jnp.maximum(sh[:, e * r:(e + 1) * r], 0.0)
        adm = krow < n_adm + gap - jp * KB
        if gap:
            adm = adm & ((jp > 0) | (krow < n_lead) | (krow >= BLK))
        s = jnp.where(adm, acc * (IDX_DIM ** -0.5), -jnp.inf)
        st_scr[jp] = s
        mn = jnp.minimum(mn, fold(jnp.where(adm, s, jnp.inf), jnp.min))
        mx = jnp.maximum(mx, fold(s, jnp.max))
        return mn, mx

    init = (jnp.full((fold_rows, r), jnp.inf, F32), jnp.full((fold_rows, r), -jnp.inf, F32))
    mn, mx = for_key_steps(score_body, init)
    lo0 = jnp.min(mn, axis=0, keepdims=True)
    hi0 = jnp.max(mx, axis=0, keepdims=True)
    kk = jnp.minimum(n_adm, n_sel).astype(F32)

    def count(pred):
        def body(jp, acc):
            return acc + fold(jnp.where(pred(st_scr[jp], jp), 1.0, 0.0), jnp.sum)
        acc = lax.fori_loop(0, nkp, body, jnp.zeros((fold_rows, r), F32))
        return jnp.sum(acc, axis=0, keepdims=True)

    def search_body(_, carry):
        lo, hi, c_lo, c_hi, ub, c_ub, hit = carry
        frac = jnp.clip((c_lo - kk) / (c_lo - c_hi), SEARCH_CLAMP, 1.0 - SEARCH_CLAMP)
        mid = jnp.where(hit > 0.0, lo, lo * (1.0 - frac) + hi * frac)
        cnt = count(lambda s, jp: s >= mid)
        ok = cnt >= kk
        now = cnt == kk
        lo = jnp.where(ok, mid, lo)
        hi = jnp.where(ok & ~now, hi, mid)
        c_lo = jnp.where(ok, cnt, c_lo)
        c_hi = jnp.where(now, kk - 1.0, jnp.where(ok, c_hi, cnt))
        ub = jnp.where(ok, ub, mid)
        c_ub = jnp.where(ok, c_ub, cnt)
        return lo, hi, c_lo, c_hi, ub, c_ub, jnp.where(now, 1.0, 0.0)

    all_adm = n_adm.astype(F32) == kk
    zeros = jnp.zeros((1, r), F32)
    init = (lo0, jnp.where(all_adm, lo0, hi0), n_adm.astype(F32), jnp.where(all_adm, kk - 1.0, zeros),
            jnp.full((1, r), jnp.inf, F32), zeros, jnp.where(all_adm, 1.0, 0.0))
    lo_f, _, _, _, ub, c_ub, hit_f = lax.fori_loop(0, SEARCH_STEPS, search_body, init)
    hit = hit_f > 0.0
    pending = jnp.sum(jnp.where(hit, 0, 1))
    take_all = jnp.full((1, r), float(n_keys), F32)

    def exact_path():
        def next_below(ub):
            def body(jp, acc):
                s = st_scr[jp]
                return jnp.maximum(acc, fold(jnp.where(s < ub, s, -jnp.inf), jnp.max))
            acc = lax.fori_loop(0, nkp, body, jnp.full((fold_rows, r), -jnp.inf, F32))
            return jnp.max(acc, axis=0, keepdims=True)

        def descend_cond(carry):
            *_, todo, it = carry
            return (todo > 0) & (it < n_keys)

        def descend_body(carry):
            ub, c_ub, _, _, _, it = carry
            t = next_below(ub)
            c_t = count(lambda s, jp: s >= t)
            done = hit | (c_t >= kk)
            return (jnp.where(done, ub, t), jnp.where(done, c_ub, c_t), t, c_t,
                    jnp.sum(jnp.where(done, 0, 1)), it + 1)

        _, c_gt, t, c_t, _, _ = lax.while_loop(descend_cond, descend_body,
                                               (ub, c_ub, lo0, zeros, jnp.int32(1), jnp.int32(0)))
        ties_wanted = jnp.where(hit, take_all, kk - c_gt)
        extra_ties = jnp.sum(jnp.where(~hit & (c_t - c_gt > ties_wanted), 1, 0))
        return jnp.where(hit, lo_f, t), ties_wanted, extra_ties

    thr, ties_wanted, extra_ties = lax.cond(pending > 0, exact_path, lambda: (lo_f, take_all, jnp.int32(0)))

    def mask_with_ties():
        ii = lax.broadcasted_iota(jnp.int32, (KB, KB), 0)
        jj = lax.broadcasted_iota(jnp.int32, (KB, KB), 1)
        upto = (jj <= ii).astype(BF16)

        def body(jp, wanted):
            s = st_scr[jp]
            tied = s == thr
            rank = jnp.dot(upto, jnp.where(tied, 1.0, 0.0).astype(BF16), preferred_element_type=F32)
            m_scr[jp] = jnp.where((s > thr) | (tied & (rank <= wanted)), 0.0, NEG_BIG).T
            return wanted - rank[KB - 1:KB, :]

        lax.fori_loop(0, nkp, body, ties_wanted)

    scale = HEAD_DIM ** -0.5
    rep = N_HEADS // N_KV_HEADS
    rq = mrun_scr.shape[1]
    for h in range(N_HEADS):
        qs_scr[h // rep, (h % rep) * rq:(h % rep + 1) * rq, :] = qn_ref[0:rq, h * HEAD_DIM:(h + 1) * HEAD_DIM]
    mrun_scr[...] = jnp.full(mrun_scr.shape, -jnp.inf, F32)
    lrun_scr[...] = jnp.zeros(lrun_scr.shape, F32)
    acc_scr[...] = jnp.zeros(acc_scr.shape, F32)

    def logit_body(mask_of, jp, carry):
        u0 = jnp.clip(2 * jp - qb + 2, 0, N_SHIFT_TILES - 1)
        u1 = jnp.clip(2 * jp + 1 - qb + 2, 0, N_SHIFT_TILES - 1)
        if gap:
            u0 = jnp.where((jp == 0) & (qb == 1), LEAD_TILE, u0)
        madd = mask_of(jp)[0:rq, :]
        for g in range(N_KV_HEADS):
            lt = lax.dot_general(qs_scr[g], k_scr[pl.ds(pl.multiple_of(jp * KB, KB), KB),
                                                  g * HEAD_DIM:(g + 1) * HEAD_DIM], nt,
                                 preferred_element_type=F32)
            for e in range(rep):
                h = g * rep + e
                bias = jnp.concatenate([bt_ref[h * N_BIAS_TILES + u0][0:rq, :],
                                        bt_ref[h * N_BIAS_TILES + u1][0:rq, :]], axis=1)
                lg = lt[e * rq:(e + 1) * rq, :] * scale + bias + madd
                lg_scr[h, jp] = lg
                mrun_scr[h] = jnp.maximum(mrun_scr[h], jnp.maximum(lg[:, :BLK], lg[:, BLK:]))
        return carry

    def logits_with_ties():
        mask_with_ties()
        return for_key_steps(functools.partial(logit_body, lambda jp: m_scr[jp]), jnp.int32(0))

    def logits_plain():
        inline_mask = lambda jp: jnp.where(st_scr[jp] >= thr, 0.0, NEG_BIG).T
        return for_key_steps(functools.partial(logit_body, inline_mask), jnp.int32(0))

    lax.cond(extra_ties > 0, logits_with_ties, logits_plain)
    for h in range(N_HEADS):
        mrun_scr[h] = jnp.broadcast_to(jnp.max(mrun_scr[h], axis=1, keepdims=True), (rq, BLK))

    def pv_body(jp, carry):
        for g in range(N_KV_HEADS):
            es = []
            for e in range(rep):
                h = g * rep + e
                mrow = mrun_scr[h]
                ex = jnp.exp(lg_scr[h, jp] - jnp.concatenate([mrow, mrow], axis=1))
                lrun_scr[h] = lrun_scr[h] + (ex[:, :BLK] + ex[:, BLK:])
                es.append(ex.astype(BF16))
            acc_scr[g] = acc_scr[g] + jnp.dot(jnp.concatenate(es, axis=0),
                                              v_scr[pl.ds(pl.multiple_of(jp * KB, KB), KB),
                                                    g * HEAD_DIM:(g + 1) * HEAD_DIM],
                                              preferred_element_type=F32)
        return carry

    for_key_steps(pv_body, 0)
    for h in range(N_HEADS):
        g, e = h // rep, h % rep
        den = jnp.sum(lrun_scr[h], axis=1, keepdims=True)
        o_ref[0:rq, h * HEAD_DIM:(h + 1) * HEAD_DIM] = (acc_scr[g, e * rq:(e + 1) * rq, :] / den).astype(BF16)
    if rq < r:
        o_ref[rq:, :] = jnp.zeros((r - rq, ATT_INNER), BF16)


def _attn(qn, qi, sm, ki_new, k_new, v_new, past, lead, bias_tiles, *, n_lead, chunk_off, t_x, n_sel):
    b, tq, _ = qn.shape
    n_past = past[0].shape[1] if past is not None else 0
    assert n_past % KB == 0 and not (n_past and n_lead)
    n_prefix = n_past + (BLK if n_lead else 0)
    n_keys = -(-(n_prefix + tq) // KB) * KB
    nkp_total = n_keys // KB
    rep = N_HEADS // N_KV_HEADS
    rq = BLK if tq > BLK else min(BLK, -(-t_x // (2 * SUBLANES)) * 2 * SUBLANES)
    seq = lambda width: pl.BlockSpec((None, BLK, width), lambda bi, i: (bi, i, 0))
    rows = lambda n, width: pl.BlockSpec((None, n, width), lambda bi, i: (bi, 0, 0))
    kern = functools.partial(_attn_kernel, nkp_total=nkp_total, n_past=n_past, n_lead=n_lead, chunk_off=chunk_off,
                             t_x=t_x, n_sel=n_sel)
    past_specs = [rows(n_past, IDX_DIM), rows(n_past * N_KV_HEADS, HEAD_DIM),
                  rows(n_past * N_KV_HEADS, HEAD_DIM)] if n_past else []
    lead_specs = [_const_spec((BLK, LANES)), _const_spec((BLK, KV_DIM)), _const_spec((BLK, KV_DIM))] if n_lead else []
    return pl.pallas_call(
        kern,
        grid=(b, tq // BLK),
        in_specs=past_specs + lead_specs + [seq(ATT_INNER), seq(IDX_HEADS * IDX_DIM), seq(LANES),
                                            rows(tq, LANES), rows(tq, KV_DIM), rows(tq, KV_DIM),
                                            _const_spec((N_HEADS * N_BIAS_TILES, BLK, BLK))],
        out_specs=seq(ATT_INNER),
        out_shape=jax.ShapeDtypeStruct((b, tq, ATT_INNER), BF16),
        scratch_shapes=[pltpu.VMEM((n_keys, IDX_DIM), BF16),
                        pltpu.VMEM((n_keys, KV_DIM), BF16),
                        pltpu.VMEM((n_keys, KV_DIM), BF16),
                        pltpu.VMEM((nkp_total, KB, BLK), F32),
                        pltpu.VMEM((nkp_total, BLK, KB), F32),
                        pltpu.VMEM((N_HEADS, nkp_total, rq, KB), F32),
                        pltpu.VMEM((IDX_HEADS * BLK, IDX_DIM), BF16),
                        pltpu.VMEM((N_KV_HEADS, rep * rq, HEAD_DIM), BF16),
                        pltpu.VMEM((N_HEADS, rq, BLK), F32),
                        pltpu.VMEM((N_HEADS, rq, BLK), F32),
                        pltpu.VMEM((N_KV_HEADS, rep * rq, HEAD_DIM), F32)],
        compiler_params=_cparams(("parallel", "arbitrary")),
        name="attn",
    )(*(past or ()), *(lead or ()), qn, qi, sm, ki_new, k_new, v_new, bias_tiles)


def _out_ffn_kernel(x_ref, ys_ref, ya_ref, gs_ref, ga_ref, wbs_ref, wba_ref, wo_ref, n2_ref, wg_ref, wu_ref, wd_ref,
                    y_ref):
    dot = functools.partial(jnp.dot, preferred_element_type=F32)
    merged = (jax.nn.sigmoid(gs_ref[...]) * dot(ys_ref[...], wbs_ref[...])
              + jax.nn.sigmoid(ga_ref[...]) * dot(ya_ref[...], wba_ref[...]))
    h = x_ref[...] + dot(merged.astype(BF16), wo_ref[...])
    hn = _rms(h, n2_ref[...]).astype(BF16)
    act = (_silu(dot(hn, wg_ref[...])) * dot(hn, wu_ref[...])).astype(BF16)
    y_ref[...] = h + dot(act, wd_ref[...])


def _out_ffn(x2d, ys, ya, gs, ga, p):
    n = x2d.shape[0]
    tm = _row_tile(n)
    d_ff = p["w_gate"].shape[1]
    row = lambda width: pl.BlockSpec((tm, width), lambda i: (i, 0))
    wspec = lambda shape: pl.BlockSpec(shape, lambda i: (0, 0), pipeline_mode=pl.Buffered(1))
    return pl.pallas_call(
        _out_ffn_kernel,
        grid=(n // tm,),
        in_specs=[row(D_MODEL), row(SSD_INNER), row(ATT_INNER), row(D_MODEL), row(D_MODEL),
                  wspec((SSD_INNER, D_MODEL)), wspec((ATT_INNER, D_MODEL)), wspec((D_MODEL, D_MODEL)),
                  _const_spec((1, D_MODEL)), wspec((D_MODEL, d_ff)), wspec((D_MODEL, d_ff)), wspec((d_ff, D_MODEL))],
        out_specs=row(D_MODEL),
        out_shape=jax.ShapeDtypeStruct((n, D_MODEL), F32),
        compiler_params=_cparams(("parallel",)),
        name="out_ffn",
    )(x2d, ys, ya, gs, ga, p["w_br_ssd"], p["w_br_att"], p["w_out"], p["norm2_w"], p["w_gate"], p["w_up"],
      p["w_down"])


def _layer(x, lead_rows, conv_prev, ssm_prev, past, p, bias_tiles, *, chunk_off, n_sel):
    b, t, _ = x.shape
    n_lead = lead_rows.shape[0]
    tp = -(-t // BLK) * BLK
    x2d = jnp.pad(x, ((0, 0), (0, tp - t), (0, 0))).reshape(b * tp, D_MODEL)
    hist = jnp.pad(conv_prev.astype(F32), ((0, 0), (HALO - (CONV_WIDTH - 1), 0), (0, 0)))
    seq = lambda a: a.reshape(b, tp, a.shape[-1])
    ssd_lead = attn_lead = None
    if n_lead:
        lead = _in_proj(jnp.pad(lead_rows.astype(x.dtype), ((0, BLK - n_lead), (0, 0))), hist[:1], p,
                        n_seq=1, t_x=n_lead)
        lz, lxbc, _, lk32, lv32, lkb, lvb, _, lsm, lsmb, _, _, hist = lead
        ssd_lead, attn_lead = (lxbc, lz, lsm), (lsmb, lkb, lvb)
    z, xbc, qn, k32, v32, kb, vb, qi, sm, smb, gs, ga, conv_new8 = _in_proj(x2d, hist, p, n_seq=b, t_x=t)

    gw = SSD_HEADS // SSD_GROUPS * SSD_HEAD_DIM
    y_ssd, ssm_new = _ssd(ssd_lead, seq(xbc), seq(z), seq(sm),
                          ssm_prev.astype(F32).reshape(b, SSD_GROUPS, gw, SSD_STATE), p, n_lead=n_lead, t_x=t)

    if past is not None:
        pk, pv, pki = past
        n_past = pk.shape[1]
        past = (pki.astype(F32), pk.astype(F32).reshape(b, n_past * N_KV_HEADS, HEAD_DIM),
                pv.astype(F32).reshape(b, n_past * N_KV_HEADS, HEAD_DIM))
    y_att = _attn(seq(qn), seq(qi), seq(sm), seq(smb), seq(kb), seq(vb), past, attn_lead, bias_tiles,
                  n_lead=n_lead, chunk_off=chunk_off, t_x=t, n_sel=n_sel)

    y = _out_ffn(x2d, y_ssd.reshape(b * tp, SSD_INNER), y_att.reshape(b * tp, ATT_INNER), gs, ga, p)

    def with_lead(new, lead_part):
        if not n_lead:
            return new
        return jnp.concatenate([jnp.broadcast_to(lead_part[None, :n_lead], (b, n_lead) + new.shape[2:]), new], axis=1)

    heads = lambda a, rows: a.reshape(-1, rows, N_KV_HEADS, HEAD_DIM)
    k_new = with_lead(heads(k32, tp)[:, :t], heads(lk32, BLK)[0] if n_lead else None)
    v_new = with_lead(heads(v32, tp)[:, :t], heads(lv32, BLK)[0] if n_lead else None)
    ki_new = with_lead(seq(sm)[:, :t, SM_KI:SM_KI + IDX_DIM], lsm[:, SM_KI:SM_KI + IDX_DIM] if n_lead else None)
    ssm_new = ssm_new.reshape(b, SSD_HEADS, SSD_HEAD_DIM, SSD_STATE)
    conv_new = conv_new8[:, HALO - (CONV_WIDTH - 1):]
    return y.reshape(b, tp, D_MODEL)[:, :t], k_new, v_new, ki_new, ssm_new, conv_new


def _prepare_params(l, norm1_w, w_in, conv_w, conv_b, dt_bias, a_log, d_skip, ssd_norm_w, q_norm_w, k_norm_w,
                    idx_k_norm_w, w_br_ssd, w_br_att, w_out, norm2_w, w_gate, w_up, w_down):
    offs = [0]
    for w in IN_WIDTHS:
        offs.append(offs[-1] + w)
    i_z, i_xbc, i_dt, i_q, i_k, i_v, i_qi, i_ki, i_wi, i_gs, i_ga = range(11)
    run = lambda first, last: w_in[l][:, offs[first]:offs[last + 1]].astype(BF16)
    pad = jnp.zeros((D_MODEL, LANES - IDX_DIM - SSD_HEADS - IDX_HEADS), BF16)
    w_perm = jnp.concatenate([run(i_z, i_xbc), run(i_q, i_qi), run(i_gs, i_ga),
                              run(i_ki, i_ki), run(i_dt, i_dt), run(i_wi, i_wi), pad], axis=1)

    def lanes_at(vec, start):
        return jnp.zeros((1, LANES), F32).at[0, start:start + vec.shape[0]].set(vec.astype(F32))

    dtb = lanes_at(dt_bias[l], SM_DT)
    alog = lanes_at(a_log[l], SM_DT)
    head_of_channel = jnp.arange(SSD_INNER) // SSD_HEAD_DIM
    expand = (jnp.arange(LANES)[:, None] == head_of_channel[None, :] + SM_DT).astype(BF16)
    row = lambda v: v.astype(F32).reshape(1, -1)
    return dict(
        norm1_w=row(norm1_w[l]), w_in=w_perm, conv_w=conv_w[l].astype(F32), conv_b=row(conv_b[l]),
        dtb=dtb, dtb_t=dtb.reshape(LANES, 1), alog=alog, alog_t=alog.reshape(LANES, 1),
        dskip_x=row(jnp.repeat(d_skip[l], SSD_HEAD_DIM)), ssd_norm_w=row(ssd_norm_w[l]), expand=expand,
        q_norm_w=row(q_norm_w[l]), k_norm_w=row(k_norm_w[l]),
        idx_k_norm_w=jnp.ones((1, LANES), F32).at[0, SM_KI:SM_KI + IDX_DIM].set(idx_k_norm_w[l].astype(F32)),
        w_br_ssd=w_br_ssd[l].astype(BF16), w_br_att=w_br_att[l].astype(BF16), w_out=w_out[l].astype(BF16),
        norm2_w=row(norm2_w[l]), w_gate=w_gate[l].astype(BF16), w_up=w_up[l].astype(BF16),
        w_down=w_down[l].astype(BF16))


def kernel(x_prompt, x_sample, cache_k, cache_v, cache_kidx, state_ssm, state_conv, meta_tokens, rel_bias, norm1_w,
           w_in, conv_w, conv_b, dt_bias, a_log, d_skip, ssd_norm_w, q_norm_w, k_norm_w, idx_k_norm_w, w_br_ssd,
           w_br_att, w_out, norm2_w, w_gate, w_up, w_down):
    bp, sp, _ = x_prompt.shape
    bs, ts, _ = x_sample.shape
    past = cache_k.shape[2]
    assert w_in.shape[0] == 1
    assert past % BLK == 0 and BLK % CHUNK == 0 and N_META <= CHUNK
    l = 0

    n_sel_p = min(TOPK_MAX, sp // 4)
    n_sel_s = min(TOPK_MAX, (past + ts) // 4)
    conv0 = jnp.zeros((bp, CONV_WIDTH - 1, CONV_DIM), F32)
    ssm0 = jnp.zeros((bp, SSD_HEADS, SSD_HEAD_DIM, SSD_STATE), F32)
    bias_tiles = _bias_tiles(rel_bias.astype(F32))
    p = _prepare_params(l, norm1_w, w_in, conv_w, conv_b, dt_bias, a_log, d_skip, ssd_norm_w, q_norm_w, k_norm_w,
                        idx_k_norm_w, w_br_ssd, w_br_att, w_out, norm2_w, w_gate, w_up, w_down)
    y_prompt, *rest_p = _layer(x_prompt, meta_tokens, conv0, ssm0, None, p, bias_tiles,
                               chunk_off=CHUNK - N_META, n_sel=n_sel_p)
    y_sample, *rest_s = _layer(x_sample, meta_tokens[:0], state_conv[l], state_ssm[l],
                               (cache_k[l], cache_v[l], cache_kidx[l]), p, bias_tiles,
                               chunk_off=0, n_sel=n_sel_s)

    dtypes = (x_prompt.dtype, x_prompt.dtype, x_prompt.dtype, state_ssm.dtype, x_prompt.dtype)
    return (y_prompt, y_sample, *(o[None].astype(dt) for o, dt in zip(rest_p, dtypes)),
            *(o[None].astype(dt) for o, dt in zip(rest_s, dtypes)))
```

```python
import functools

import jax
import jax.numpy as jnp
from jax import lax
from jax.experimental import pallas as pl
from jax.experimental.pallas import tpu as pltpu

F32 = jnp.float32
BF16 = jnp.bfloat16

D_MODEL = 1024
CHUNK = 64
N_META = 16
SSD_HEADS = 16
SSD_HEAD_DIM = 64
SSD_INNER = SSD_HEADS * SSD_HEAD_DIM
SSD_GROUPS = 4
SSD_STATE = 128
CONV_WIDTH = 4
CONV_DIM = SSD_INNER + 2 * SSD_GROUPS * SSD_STATE
N_HEADS = 8
N_KV_HEADS = 2
HEAD_DIM = 128
ATT_INNER = N_HEADS * HEAD_DIM
KV_DIM = N_KV_HEADS * HEAD_DIM
IDX_HEADS = 8
IDX_DIM = 64
TOPK_MAX = 256
REL_BUCKETS = 32
REL_MAX_DIST = 128
IN_WIDTHS = (SSD_INNER, CONV_DIM, SSD_HEADS, ATT_INNER, KV_DIM, KV_DIM, IDX_HEADS * IDX_DIM, IDX_DIM, IDX_HEADS,
             D_MODEL, D_MODEL)
EPS = 1e-6

LANES = 128
SUBLANES = 8
VMEM_LIMIT_BYTES = 56 * 1024 * 1024

BLK = LANES
ROW_TILE = 256
IN_PROJ_PASSES = 2
HALO = SUBLANES
SSD_SEQS_PER_STEP = 4

C_Z = 0
C_XBC = C_Z + SSD_INNER
C_Q = C_XBC + CONV_DIM
C_K = C_Q + ATT_INNER
C_V = C_K + KV_DIM
C_QI = C_V + KV_DIM
C_GS = C_QI + IDX_HEADS * IDX_DIM
C_GA = C_GS + D_MODEL
C_SM = C_GA + D_MODEL
IN_PAD = C_SM + LANES
SM_KI = 0
SM_DT = SM_KI + IDX_DIM
SM_WI = SM_DT + SSD_HEADS

SEARCH_STEPS = 16
SEARCH_CLAMP = 1.0 / 16
NEG_BIG = -1e30


def _cparams(sem):
    return pltpu.CompilerParams(dimension_semantics=sem, vmem_limit_bytes=VMEM_LIMIT_BYTES)


def _const_spec(shape):
    nd = len(shape)
    return pl.BlockSpec(shape, lambda *_: (0,) * nd)


def _rms(x, w):
    return x * lax.rsqrt(jnp.mean(x * x, axis=-1, keepdims=True) + EPS) * w


def _silu(x):
    return x * jax.nn.sigmoid(x)


def _in_proj_kernel(x_ref, hist_ref, n1_ref, w_ref, qn_ref, kn_ref, kin_ref, cw_ref, cb_ref,
                    z_ref, xc_ref, q_ref, k_ref, v_ref, kb_ref, vb_ref, qi_ref, sm_ref, smb_ref, gs_ref, ga_ref, tail_ref,
                    xpad_scr, *, tiles_per_seq, t_x):
    j = pl.program_id(1)
    tm = x_ref.shape[0]

    @pl.when(j == 0)
    def _():
        xpad_scr[0:HALO, :] = hist_ref[...]

    sub = ROW_TILE if tm % ROW_TILE == 0 else tm
    for r0 in range(0, tm, sub):
        rows, kv_rows = pl.ds(r0, sub), pl.ds(r0 * N_KV_HEADS, sub * N_KV_HEADS)
        _in_proj_rows(x_ref.at[rows], n1_ref, w_ref, qn_ref, kn_ref, kin_ref, cw_ref, cb_ref,
                      z_ref.at[rows], xc_ref.at[rows], q_ref.at[rows], k_ref.at[kv_rows], v_ref.at[kv_rows],
                      kb_ref.at[rows], vb_ref.at[rows], qi_ref.at[rows], sm_ref.at[rows], smb_ref.at[rows],
                      gs_ref.at[rows], ga_ref.at[rows], xpad_scr.at[pl.ds(r0, sub + 2 * HALO)])

    n_last = t_x - (tiles_per_seq - 1) * tm

    @pl.when(j == tiles_per_seq - 1)
    def _():
        tail_ref[...] = xpad_scr[n_last:n_last + HALO, :]

    xpad_scr[0:HALO, :] = xpad_scr[tm:tm + HALO, :]


def _in_proj_rows(x_ref, n1_ref, w_ref, qn_ref, kn_ref, kin_ref, cw_ref, cb_ref,
                  z_ref, xc_ref, q_ref, k_ref, v_ref, kb_ref, vb_ref, qi_ref, sm_ref, smb_ref, gs_ref, ga_ref, xpad_scr):
    hn = _rms(x_ref[...], n1_ref[...]).astype(BF16)
    tm = x_ref.shape[0]

    def mm(lo, hi):
        return jnp.dot(hn, w_ref[:, lo:hi], preferred_element_type=F32)

    step = 2 * LANES
    conv_slabs = iter(range(0, CONV_DIM, LANES))

    def conv_next():
        c0 = next(conv_slabs, None)
        if c0 is None:
            return
        if c0 % step == 0:
            xpad_scr[HALO:HALO + tm, c0:c0 + step] = mm(C_XBC + c0, C_XBC + c0 + step)
        sl = slice(c0, c0 + LANES)
        first = HALO - (CONV_WIDTH - 1)
        acc = xpad_scr[first:first + tm, sl] * cw_ref[0:1, sl]
        for i in range(1, CONV_WIDTH):
            acc = acc + xpad_scr[first + i:first + i + tm, sl] * cw_ref[i:i + 1, sl]
        xc_ref[:, sl] = _silu(cb_ref[:, sl] + acc)

    def project(lo, hi, out_ref, dtype=F32):
        for c0 in range(0, hi - lo, step):
            out_ref[:, c0:c0 + step] = mm(lo + c0, lo + c0 + step).astype(dtype)
            conv_next()

    project(C_Z, C_XBC, z_ref)
    for c0 in range(0, ATT_INNER, step):
        q = mm(C_Q + c0, C_Q + c0 + step)
        conv_next()
        for h in range(step // HEAD_DIM):
            sl = slice(h * HEAD_DIM, (h + 1) * HEAD_DIM)
            q_ref[:, c0 + h * HEAD_DIM:c0 + (h + 1) * HEAD_DIM] = _rms(q[:, sl], qn_ref[...]).astype(BF16)
    k = mm(C_K, C_V)
    conv_next()
    v = mm(C_V, C_QI)
    conv_next()
    for h in range(N_KV_HEADS):
        sl = slice(h * HEAD_DIM, (h + 1) * HEAD_DIM)
        kh = _rms(k[:, sl], kn_ref[...])
        k_ref[pl.ds(h, tm, stride=N_KV_HEADS), :] = kh
        kb_ref[:, sl] = kh.astype(BF16)
        v_ref[pl.ds(h, tm, stride=N_KV_HEADS), :] = v[:, sl]
    vb_ref[...] = v.astype(BF16)
    project(C_QI, C_GS, qi_ref, BF16)
    project(C_GS, C_GA, gs_ref)
    project(C_GA, C_SM, ga_ref)
    sm = mm(C_SM, IN_PAD)
    assert next(conv_slabs, None) is None
    lane = lax.broadcasted_iota(jnp.int32, sm.shape, 1)
    is_ki = lane < SM_KI + IDX_DIM
    ms = jnp.sum(jnp.where(is_ki, sm * sm, 0.0), axis=-1, keepdims=True) * (1.0 / IDX_DIM)
    ki = sm * lax.rsqrt(ms + EPS) * kin_ref[...]
    is_wi = (lane >= SM_WI) & (lane < SM_WI + IDX_HEADS)
    out = jnp.where(is_ki, ki, jnp.where(is_wi, sm * (IDX_HEADS ** -0.5), sm))
    sm_ref[...] = out
    smb_ref[...] = out.astype(BF16)


def _row_tile(n):
    return ROW_TILE if n % ROW_TILE == 0 else BLK


def _in_proj(x2d, hist, p, *, n_seq, t_x):
    n = x2d.shape[0]
    tp = n // n_seq
    tm = _row_tile(tp)
    if tp % (IN_PROJ_PASSES * tm) == 0:
        tm *= IN_PROJ_PASSES
    tiles = tp // tm
    shared_hist = hist.shape[0] == 1
    row = lambda width, per_token=1: pl.BlockSpec((tm * per_token, width), lambda s, j: (s * tiles + j, 0))
    per_seq = pl.BlockSpec((None, HALO, CONV_DIM), lambda s, j: (0 if shared_hist else s, 0, 0))
    outs = [
        (SSD_INNER, F32, 1), (CONV_DIM, F32, 1), (ATT_INNER, BF16, 1), (HEAD_DIM, F32, N_KV_HEADS),
        (HEAD_DIM, F32, N_KV_HEADS), (KV_DIM, BF16, 1), (KV_DIM, BF16, 1), (IDX_HEADS * IDX_DIM, BF16, 1),
        (LANES, F32, 1), (LANES, BF16, 1), (D_MODEL, F32, 1), (D_MODEL, F32, 1),
    ]
    kern = functools.partial(_in_proj_kernel, tiles_per_seq=tiles, t_x=t_x)
    return pl.pallas_call(
        kern,
        grid=(n_seq, tiles),
        in_specs=[row(D_MODEL), per_seq, _const_spec((1, D_MODEL)),
                  pl.BlockSpec((D_MODEL, IN_PAD), lambda s, j: (0, 0), pipeline_mode=pl.Buffered(1)),
                  _const_spec((1, HEAD_DIM)), _const_spec((1, HEAD_DIM)), _const_spec((1, LANES)),
                  _const_spec((CONV_WIDTH, CONV_DIM)), _const_spec((1, CONV_DIM))],
        out_specs=[row(w, per) for w, _, per in outs]
        + [pl.BlockSpec((None, HALO, CONV_DIM), lambda s, j: (s, 0, 0))],
        out_shape=[jax.ShapeDtypeStruct((n * per, w), dt) for w, dt, per in outs]
        + [jax.ShapeDtypeStruct((n_seq, HALO, CONV_DIM), F32)],
        scratch_shapes=[pltpu.VMEM((tm + 2 * HALO, CONV_DIM), F32)],
        compiler_params=_cparams(("parallel", "arbitrary")),
        name="in_proj",
    )(x2d, hist, p["norm1_w"], p["w_in"], p["q_norm_w"], p["k_norm_w"], p["idx_k_norm_w"], p["conv_w"], p["conv_b"])


def _softplus(x):
    return jnp.maximum(x, 0.0) + jnp.log1p(jnp.exp(-jnp.abs(x)))


def _split3(x):
    hi = x.astype(BF16)
    r1 = x - hi.astype(F32)
    mid = r1.astype(BF16)
    lo = (r1 - mid.astype(F32)).astype(BF16)
    return hi, mid, lo


def _ssd_kernel(*refs, n_chunks, n_lead, t_x):
    if n_lead:
        xbc_lead_ref, z_lead_ref, sm_lead_ref, *refs = refs
    (xbc_ref, z_ref, sm_ref, sprev_ref, dtb_ref, dtbt_ref, alog_ref, alogt_ref,
     dsk_ref, nw_ref, exp_ref, y_ref, snew_ref, s_scr, y_scr) = refs
    c = pl.program_id(1)
    lead_chunks = 1 if n_lead else 0
    seqs = range(xbc_ref.shape[0])

    @pl.when(c == 0)
    def _():
        for k in seqs:
            for g in range(SSD_GROUPS):
                s_scr[k, g] = sprev_ref[k, g].T

    def chunk(k, xc_ref, z_src_ref, sm_src_ref, n_valid):
        _ssd_chunk(xc_ref, z_src_ref, sm_src_ref, n_valid, dtb_ref, dtbt_ref, alog_ref, alogt_ref, dsk_ref, nw_ref,
                   exp_ref, y_ref.at[k], s_scr.at[k], y_scr.at[k])

    def x_chunks():
        for k in seqs:
            chunk(k, xbc_ref.at[k], z_ref.at[k], sm_ref.at[k], jnp.minimum(BLK, t_x - (c - lead_chunks) * BLK))

    if n_lead:
        @pl.when(c == 0)
        def _():
            for k in seqs:
                chunk(k, xbc_lead_ref, z_lead_ref, sm_lead_ref, n_lead)

        pl.when(c > 0)(x_chunks)
    else:
        x_chunks()

    @pl.when(c == n_chunks - 1)
    def _():
        for k in seqs:
            for g in range(SSD_GROUPS):
                snew_ref[k, g] = s_scr[k, g].T


def _ssd_chunk(xc_ref, z_ref, sm_ref, n_valid, dtb_ref, dtbt_ref, alog_ref, alogt_ref, dsk_ref, nw_ref, exp_ref,
               y_ref, s_scr, y_scr):
    q = BLK
    gw = SSD_HEADS // SSD_GROUPS * SSD_HEAD_DIM
    sm = sm_ref[...]

    heads = slice(SM_DT, SM_DT + SSD_HEADS)
    smt = sm.T[heads, :]
    lane = lax.broadcasted_iota(jnp.int32, (q, LANES), 1)
    row = lax.broadcasted_iota(jnp.int32, (q, LANES), 0)
    is_dt = (lane >= SM_DT) & (lane < SM_DT + SSD_HEADS) & (row < n_valid)
    time_t = lax.broadcasted_iota(jnp.int32, (SSD_HEADS, q), 1)
    dt = jnp.where(is_dt, _softplus(sm + dtb_ref[...]), 0.0)
    dtt = jnp.where(time_t < n_valid, _softplus(smt + dtbt_ref[heads, :]), 0.0)
    da = dt * (-jnp.exp(alog_ref[...]))
    dat = dtt * (-jnp.exp(alogt_ref[heads, :]))
    ii = lax.broadcasted_iota(jnp.int32, (q, q), 0)
    jj = lax.broadcasted_iota(jnp.int32, (q, q), 1)
    causal = jj <= ii
    acum = sum(jnp.dot(causal.astype(BF16), p, preferred_element_type=F32) for p in _split3(da))
    acumt = sum(jnp.dot(p, (ii <= jj).astype(BF16), preferred_element_type=F32) for p in _split3(dat))
    a_last = acum[q - 1:q, :]
    expand = exp_ref[...]
    stacked = jnp.concatenate([jnp.exp(acum), jnp.exp(a_last - acum) * dt,
                               jnp.broadcast_to(jnp.exp(a_last), (SUBLANES, LANES))], axis=0)
    stacked_x = sum(jnp.dot(p, expand, preferred_element_type=F32) for p in _split3(stacked))
    ea_x = stacked_x[0:q]
    wdt_x = stacked_x[q:2 * q]
    dec_x = stacked_x[2 * q:2 * q + 1]

    sq = None
    for g in range(SSD_GROUPS):
        gsl = slice(g * gw, (g + 1) * gw)
        bsl = slice(SSD_INNER + g * SSD_STATE, SSD_INNER + (g + 1) * SSD_STATE)
        csl = slice(SSD_INNER + SSD_GROUPS * SSD_STATE + g * SSD_STATE,
                    SSD_INNER + SSD_GROUPS * SSD_STATE + (g + 1) * SSD_STATE)
        bmf = xc_ref[:, bsl]
        bm = bmf.astype(BF16)
        cm = xc_ref[:, csl].astype(BF16)
        xg = xc_ref[:, gsl]
        xgb = xg.astype(BF16)
        cbm = lax.dot_general(cm, bm, (((1,), (1,)), ((), ())), preferred_element_type=F32)
        xw = (xg * wdt_x[:, gsl]).astype(BF16)
        st = jnp.dot(bmf.T.astype(BF16), xw, preferred_element_type=F32)
        s_in = s_scr[g]
        y_off = jnp.dot(cm, s_in.astype(BF16), preferred_element_type=F32) * ea_x[:, gsl]
        s_scr[g] = s_in * dec_x[:, gsl] + st
        first_of_pair = lax.broadcasted_iota(jnp.int32, (q, LANES), 1) < SSD_HEAD_DIM
        for rp in range(gw // LANES):
            psl = slice(rp * LANES, (rp + 1) * LANES)
            osl = slice(g * gw + rp * LANES, g * gw + (rp + 1) * LANES)
            pair = []
            for h in (g * (SSD_HEADS // SSD_GROUPS) + 2 * rp, g * (SSD_HEADS // SSD_GROUPS) + 2 * rp + 1):
                seg = acum[:, SM_DT + h:SM_DT + h + 1] - acumt[h:h + 1, :]
                lmat = jnp.exp(jnp.where(causal, seg, -jnp.inf))
                wmat = (cbm * lmat * dtt[h:h + 1, :]).astype(BF16)
                pair.append(jnp.dot(wmat, xgb[:, psl], preferred_element_type=F32))
            y_diag = jnp.where(first_of_pair, pair[0], pair[1])
            yp = (y_diag + y_off[:, psl] + xg[:, psl] * dsk_ref[:, osl]) * _silu(z_ref[:, osl])
            y_scr[:, osl] = yp
            sq = yp * yp if sq is None else sq + yp * yp

    ms = jnp.sum(sq, axis=-1, keepdims=True) * (1.0 / SSD_INNER)
    y_ref[...] = (y_scr[...] * lax.rsqrt(ms + EPS) * nw_ref[...]).astype(BF16)


def _ssd(lead, xbc, z, sm, ssm_prev, p, *, n_lead, t_x):
    b, tp, _ = xbc.shape
    lead_chunks = 1 if n_lead else 0
    assert n_lead % SUBLANES == 0 and n_lead <= BLK and 2 * SSD_HEAD_DIM == LANES
    nc = tp // BLK + lead_chunks
    gw = SSD_HEADS // SSD_GROUPS * SSD_HEAD_DIM
    bb = SSD_SEQS_PER_STEP if b % SSD_SEQS_PER_STEP == 0 else 1
    seq = lambda width: pl.BlockSpec((bb, BLK, width), lambda i, c: (i, jnp.maximum(c - lead_chunks, 0), 0))
    state = pl.BlockSpec((bb, SSD_GROUPS, gw, SSD_STATE), lambda i, c: (i, 0, 0, 0))
    kern = functools.partial(_ssd_kernel, n_chunks=nc, n_lead=n_lead, t_x=t_x)
    lead_specs = [_const_spec((BLK, CONV_DIM)), _const_spec((BLK, SSD_INNER)), _const_spec((BLK, LANES))]
    return pl.pallas_call(
        kern,
        grid=(b // bb, nc),
        in_specs=(lead_specs if n_lead else []) + [
                  seq(CONV_DIM), seq(SSD_INNER), seq(LANES), state,
                  _const_spec((1, LANES)), _const_spec((LANES, 1)), _const_spec((1, LANES)), _const_spec((LANES, 1)),
                  _const_spec((1, SSD_INNER)), _const_spec((1, SSD_INNER)), _const_spec((LANES, SSD_INNER))],
        out_specs=[seq(SSD_INNER), state],
        out_shape=[jax.ShapeDtypeStruct((b, tp, SSD_INNER), BF16),
                   jax.ShapeDtypeStruct((b, SSD_GROUPS, gw, SSD_STATE), F32)],
        scratch_shapes=[pltpu.VMEM((bb, SSD_GROUPS, SSD_STATE, gw), F32),
                        pltpu.VMEM((bb, BLK, SSD_INNER), F32)],
        compiler_params=_cparams(("parallel", "arbitrary")),
        name="ssd",
    )(*(lead if n_lead else ()), xbc, z, sm, ssm_prev, p["dtb"], p["dtb_t"],
      p["alog"], p["alog_t"], p["dskip_x"], p["ssd_norm_w"], p["expand"])


N_SHIFT_TILES = 5
LEAD_TILE = N_SHIFT_TILES
N_BIAS_TILES = N_SHIFT_TILES + 1


def _log_bucket_starts():
    nb = REL_BUCKETS // 2
    max_exact = nb // 2
    s = nb - max_exact
    starts = []
    for m in range(1, s):
        n = max_exact
        while n ** s * max_exact ** m < max_exact ** s * REL_MAX_DIST ** m:
            n += 1
        starts.append(n)
    return starts


def _bias_kernel(rb_ref, bt_ref):
    nb = REL_BUCKETS // 2
    max_exact = nb // 2
    qq = lax.broadcasted_iota(jnp.int32, (BLK, BLK), 0)
    kk = lax.broadcasted_iota(jnp.int32, (BLK, BLK), 1)
    for u in range(N_BIAS_TILES):
        rel = kk - qq + ((u - 2) * BLK if u < N_SHIFT_TILES else -N_META)
        n = jnp.abs(rel)
        large = max_exact + sum(jnp.where(n >= start, 1, 0) for start in _log_bucket_starts())
        bucket = jnp.where(rel > 0, nb, 0) + jnp.where(n < max_exact, n, large)
        for h in range(N_HEADS):
            acc = jnp.zeros((BLK, BLK), F32)
            for bkt in range(REL_BUCKETS):
                acc = jnp.where(bucket == bkt, rb_ref[bkt, h], acc)
            bt_ref[h * N_BIAS_TILES + u] = acc


def _bias_tiles(rel_bias):
    return pl.pallas_call(
        _bias_kernel,
        in_specs=[pl.BlockSpec(memory_space=pltpu.SMEM)],
        out_specs=pl.BlockSpec(memory_space=pltpu.VMEM),
        out_shape=jax.ShapeDtypeStruct((N_HEADS * N_BIAS_TILES, BLK, BLK), F32),
        name="bias_tiles",
    )(rel_bias)


KB = 2 * BLK


def _attn_kernel(*refs, nkp_total, n_past, n_lead, chunk_off, t_x, n_sel):
    if n_past:
        pki_ref, pk_ref, pv_ref, *refs = refs
    if n_lead:
        lki_ref, lk_ref, lv_ref, *refs = refs
    (qn_ref, qi_ref, sm_ref, ki_new_ref, k_new_ref, v_new_ref, bt_ref, o_ref,
     ki_scr, k_scr, v_scr, st_scr, m_scr, lg_scr, qis_scr, qs_scr, mrun_scr, lrun_scr, acc_scr) = refs
    i = pl.program_id(1)
    r = BLK
    n_prefix = n_past + (BLK if n_lead else 0)
    gap = BLK - n_lead if n_lead else 0
    l_valid = n_prefix - gap + t_x

    @pl.when(i == 0)
    def _():
        if n_lead:
            ki_scr[0:BLK, :] = lki_ref[:, SM_KI:SM_KI + IDX_DIM]
            k_scr[0:BLK, :] = lk_ref[...]
            v_scr[0:BLK, :] = lv_ref[...]
        if n_past:
            def load_past(c, carry):
                rows = pl.ds(pl.multiple_of(c * KB, KB), KB)
                ki_scr[rows, :] = pki_ref[rows, :].astype(BF16)
                for g in range(N_KV_HEADS):
                    src = pl.ds(pl.multiple_of(c * KB * N_KV_HEADS, KB) + g, KB, stride=N_KV_HEADS)
                    k_scr[rows, g * HEAD_DIM:(g + 1) * HEAD_DIM] = pk_ref[src, :].astype(BF16)
                    v_scr[rows, g * HEAD_DIM:(g + 1) * HEAD_DIM] = pv_ref[src, :].astype(BF16)
                return carry

            lax.fori_loop(0, n_past // KB, load_past, 0)
        n_new = k_new_ref.shape[0]
        ki_scr[n_prefix:n_prefix + n_new, :] = ki_new_ref[:, SM_KI:SM_KI + IDX_DIM]
        k_scr[n_prefix:n_prefix + n_new, :] = k_new_ref[...]
        v_scr[n_prefix:n_prefix + n_new, :] = v_new_ref[...]
        n_tail = nkp_total * KB - n_prefix - n_new
        if n_tail:
            for scr in (ki_scr, k_scr, v_scr):
                scr[n_prefix + n_new:, :] = jnp.zeros((n_tail, scr.shape[1]), BF16)

    qb = n_prefix // BLK + i
    q0 = qb * BLK - gap
    n_keys = nkp_total * KB

    def chunk_end(pos):
        return jnp.minimum(CHUNK * ((pos + chunk_off) // CHUNK + 1) - chunk_off, l_valid)

    nkp = jnp.minimum(nkp_total, (chunk_end(q0 + BLK - 1) + gap + KB - 1) // KB)

    nt = (((1,), (1,)), ((), ()))
    wit = sm_ref[...].T
    n_adm = chunk_end(q0 + lax.broadcasted_iota(jnp.int32, (1, r), 1))
    krow = lax.broadcasted_iota(jnp.int32, (KB, r), 0)

    fold_rows = 8 * SUBLANES

    def fold(x, op):
        return op(x.reshape(KB // fold_rows, fold_rows, r), axis=0)

    def for_key_steps(body, init):
        def several(n, first, c):
            for s in range(n):
                c = body(first + s, c)
            return c

        carry = lax.fori_loop(0, nkp // 4, lambda t, c: several(4, 4 * t, c), init)
        carry = lax.cond(nkp % 4 >= 2, lambda c: several(2, nkp - nkp % 4, c), lambda c: c, carry)
        return lax.cond(nkp % 2 == 1, lambda c: body(nkp - 1, c), lambda c: c, carry)

    for h in range(IDX_HEADS):
        qis_scr[h * r:(h + 1) * r, :] = qi_ref[:, h * IDX_DIM:(h + 1) * IDX_DIM]

    def score_body(jp, carry):
        mn, mx = carry
        kij = ki_scr[pl.ds(pl.multiple_of(jp * KB, KB), KB), :]
        acc = jnp.zeros((KB, r), F32)
        for hp in range(IDX_HEADS // 2):
            sh = lax.dot_general(kij, qis_scr[2 * hp * r:2 * (hp + 1) * r, :], nt, preferred_element_type=F32)
            for e in range(2):
                h = 2 * hp + e
                acc = acc + wit[SM_WI + h:SM_WI + h + 1, :] * jnp.maximum(sh[:, e * r:(e + 1) * r], 0.0)
        adm = krow < n_adm + gap - jp * KB
        if gap:
            adm = adm & ((jp > 0) | (krow < n_lead) | (krow >= BLK))
        s = jnp.where(adm, acc * (IDX_DIM ** -0.5), -jnp.inf)
        st_scr[jp] = s
        mn = jnp.minimum(mn, fold(jnp.where(adm, s, jnp.inf), jnp.min))
        mx = jnp.maximum(mx, fold(s, jnp.max))
        return mn, mx

    init = (jnp.full((fold_rows, r), jnp.inf, F32), jnp.full((fold_rows, r), -jnp.inf, F32))
    mn, mx = for_key_steps(score_body, init)
    lo0 = jnp.min(mn, axis=0, keepdims=True)
    hi0 = jnp.max(mx, axis=0, keepdims=True)
    kk = jnp.minimum(n_adm, n_sel).astype(F32)

    def count(pred):
        def body(jp, acc):
            return acc + fold(jnp.where(pred(st_scr[jp], jp), 1.0, 0.0), jnp.sum)
        acc = lax.fori_loop(0, nkp, body, jnp.zeros((fold_rows, r), F32))
        return jnp.sum(acc, axis=0, keepdims=True)

    def search_body(_, carry):
        lo, hi, c_lo, c_hi, ub, c_ub, hit = carry
        frac = jnp.clip((c_lo - kk) / (c_lo - c_hi), SEARCH_CLAMP, 1.0 - SEARCH_CLAMP)
        mid = jnp.where(hit > 0.0, lo, lo * (1.0 - frac) + hi * frac)
        cnt = count(lambda s, jp: s >= mid)
        ok = cnt >= kk
        now = cnt == kk
        lo = jnp.where(ok, mid, lo)
        hi = jnp.where(ok & ~now, hi, mid)
        c_lo = jnp.where(ok, cnt, c_lo)
        c_hi = jnp.where(now, kk - 1.0, jnp.where(ok, c_hi, cnt))
        ub = jnp.where(ok, ub, mid)
        c_ub = jnp.where(ok, c_ub, cnt)
        return lo, hi, c_lo, c_hi, ub, c_ub, jnp.where(now, 1.0, 0.0)

    all_adm = n_adm.astype(F32) == kk
    zeros = jnp.zeros((1, r), F32)
    init = (lo0, jnp.where(all_adm, lo0, hi0), n_adm.astype(F32), jnp.where(all_adm, kk - 1.0, zeros),
            jnp.full((1, r), jnp.inf, F32), zeros, jnp.where(all_adm, 1.0, 0.0))
    lo_f, _, _, _, ub, c_ub, hit_f = lax.fori_loop(0, SEARCH_STEPS, search_body, init)
    hit = hit_f > 0.0
    pending = jnp.sum(jnp.where(hit, 0, 1))
    take_all = jnp.full((1, r), float(n_keys), F32)

    def exact_path():
        def next_below(ub):
            def body(jp, acc):
                s = st_scr[jp]
                return jnp.maximum(acc, fold(jnp.where(s < ub, s, -jnp.inf), jnp.max))
            acc = lax.fori_loop(0, nkp, body, jnp.full((fold_rows, r), -jnp.inf, F32))
            return jnp.max(acc, axis=0, keepdims=True)

        def descend_cond(carry):
            *_, todo, it = carry
            return (todo > 0) & (it < n_keys)

        def descend_body(carry):
            ub, c_ub, _, _, _, it = carry
            t = next_below(ub)
            c_t = count(lambda s, jp: s >= t)
            done = hit | (c_t >= kk)
            return (jnp.where(done, ub, t), jnp.where(done, c_ub, c_t), t, c_t,
                    jnp.sum(jnp.where(done, 0, 1)), it + 1)

        _, c_gt, t, c_t, _, _ = lax.while_loop(descend_cond, descend_body,
                                               (ub, c_ub, lo0, zeros, jnp.int32(1), jnp.int32(0)))
        ties_wanted = jnp.where(hit, take_all, kk - c_gt)
        extra_ties = jnp.sum(jnp.where(~hit & (c_t - c_gt > ties_wanted), 1, 0))
        return jnp.where(hit, lo_f, t), ties_wanted, extra_ties

    thr, ties_wanted, extra_ties = lax.cond(pending > 0, exact_path, lambda: (lo_f, take_all, jnp.int32(0)))

    def mask_with_ties():
        ii = lax.broadcasted_iota(jnp.int32, (KB, KB), 0)
        jj = lax.broadcasted_iota(jnp.int32, (KB, KB), 1)
        upto = (jj <= ii).astype(BF16)

        def body(jp, wanted):
            s = st_scr[jp]
            tied = s == thr
            rank = jnp.dot(upto, jnp.where(tied, 1.0, 0.0).astype(BF16), preferred_element_type=F32)
            m_scr[jp] = jnp.where((s > thr) | (tied & (rank <= wanted)), 0.0, NEG_BIG).T
            return wanted - rank[KB - 1:KB, :]

        lax.fori_loop(0, nkp, body, ties_wanted)

    scale = HEAD_DIM ** -0.5
    rep = N_HEADS // N_KV_HEADS
    rq = mrun_scr.shape[1]
    for h in range(N_HEADS):
        qs_scr[h // rep, (h % rep) * rq:(h % rep + 1) * rq, :] = qn_ref[0:rq, h * HEAD_DIM:(h + 1) * HEAD_DIM]
    mrun_scr[...] = jnp.full(mrun_scr.shape, -jnp.inf, F32)
    lrun_scr[...] = jnp.zeros(lrun_scr.shape, F32)
    acc_scr[...] = jnp.zeros(acc_scr.shape, F32)

    def logit_body(mask_of, jp, carry):
        u0 = jnp.clip(2 * jp - qb + 2, 0, N_SHIFT_TILES - 1)
        u1 = jnp.clip(2 * jp + 1 - qb + 2, 0, N_SHIFT_TILES - 1)
        if gap:
            u0 = jnp.where((jp == 0) & (qb == 1), LEAD_TILE, u0)
        madd = mask_of(jp)[0:rq, :]
        for g in range(N_KV_HEADS):
            lt = lax.dot_general(qs_scr[g], k_scr[pl.ds(pl.multiple_of(jp * KB, KB), KB),
                                                  g * HEAD_DIM:(g + 1) * HEAD_DIM], nt,
                                 preferred_element_type=F32)
            for e in range(rep):
                h = g * rep + e
                bias = jnp.concatenate([bt_ref[h * N_BIAS_TILES + u0][0:rq, :],
                                        bt_ref[h * N_BIAS_TILES + u1][0:rq, :]], axis=1)
                lg = lt[e * rq:(e + 1) * rq, :] * scale + bias + madd
                lg_scr[h, jp] = lg
                mrun_scr[h] = jnp.maximum(mrun_scr[h], jnp.maximum(lg[:, :BLK], lg[:, BLK:]))
        return carry

    def logits_with_ties():
        mask_with_ties()
        return for_key_steps(functools.partial(logit_body, lambda jp: m_scr[jp]), jnp.int32(0))

    def logits_plain():
        inline_mask = lambda jp: jnp.where(st_scr[jp] >= thr, 0.0, NEG_BIG).T
        return for_key_steps(functools.partial(logit_body, inline_mask), jnp.int32(0))

    lax.cond(extra_ties > 0, logits_with_ties, logits_plain)
    for h in range(N_HEADS):
        mrun_scr[h] = jnp.broadcast_to(jnp.max(mrun_scr[h], axis=1, keepdims=True), (rq, BLK))

    def pv_body(jp, carry):
        for g in range(N_KV_HEADS):
            es = []
            for e in range(rep):
                h = g * rep + e
                mrow = mrun_scr[h]
                ex = jnp.exp(lg_scr[h, jp] - jnp.concatenate([mrow, mrow], axis=1))
                lrun_scr[h] = lrun_scr[h] + (ex[:, :BLK] + ex[:, BLK:])
                es.append(ex.astype(BF16))
            acc_scr[g] = acc_scr[g] + jnp.dot(jnp.concatenate(es, axis=0),
                                              v_scr[pl.ds(pl.multiple_of(jp * KB, KB), KB),
                                                    g * HEAD_DIM:(g + 1) * HEAD_DIM],
                                              preferred_element_type=F32)
        return carry

    for_key_steps(pv_body, 0)
    for h in range(N_HEADS):
        g, e = h // rep, h % rep
        den = jnp.sum(lrun_scr[h], axis=1, keepdims=True)
        o_ref[0:rq, h * HEAD_DIM:(h + 1) * HEAD_DIM] = (acc_scr[g, e * rq:(e + 1) * rq, :] / den).astype(BF16)
    if rq < r:
        o_ref[rq:, :] = jnp.zeros((r - rq, ATT_INNER), BF16)


def _attn(qn, qi, sm, ki_new, k_new, v_new, past, lead, bias_tiles, *, n_lead, chunk_off, t_x, n_sel):
    b, tq, _ = qn.shape
    n_past = past[0].shape[1] if past is not None else 0
    assert n_past % KB == 0 and not (n_past and n_lead)
    n_prefix = n_past + (BLK if n_lead else 0)
    n_keys = -(-(n_prefix + tq) // KB) * KB
    nkp_total = n_keys // KB
    rep = N_HEADS // N_KV_HEADS
    rq = BLK if tq > BLK else min(BLK, -(-t_x // (2 * SUBLANES)) * 2 * SUBLANES)
    seq = lambda width: pl.BlockSpec((None, BLK, width), lambda bi, i: (bi, i, 0))
    rows = lambda n, width: pl.BlockSpec((None, n, width), lambda bi, i: (bi, 0, 0))
    kern = functools.partial(_attn_kernel, nkp_total=nkp_total, n_past=n_past, n_lead=n_lead, chunk_off=chunk_off,
                             t_x=t_x, n_sel=n_sel)
    past_specs = [rows(n_past, IDX_DIM), rows(n_past * N_KV_HEADS, HEAD_DIM),
                  rows(n_past * N_KV_HEADS, HEAD_DIM)] if n_past else []
    lead_specs = [_const_spec((BLK, LANES)), _const_spec((BLK, KV_DIM)), _const_spec((BLK, KV_DIM))] if n_lead else []
    return pl.pallas_call(
        kern,
        grid=(b, tq // BLK),
        in_specs=past_specs + lead_specs + [seq(ATT_INNER), seq(IDX_HEADS * IDX_DIM), seq(LANES),
                                            rows(tq, LANES), rows(tq, KV_DIM), rows(tq, KV_DIM),
                                            _const_spec((N_HEADS * N_BIAS_TILES, BLK, BLK))],
        out_specs=seq(ATT_INNER),
        out_shape=jax.ShapeDtypeStruct((b, tq, ATT_INNER), BF16),
        scratch_shapes=[pltpu.VMEM((n_keys, IDX_DIM), BF16),
                        pltpu.VMEM((n_keys, KV_DIM), BF16),
                        pltpu.VMEM((n_keys, KV_DIM), BF16),
                        pltpu.VMEM((nkp_total, KB, BLK), F32),
                        pltpu.VMEM((nkp_total, BLK, KB), F32),
                        pltpu.VMEM((N_HEADS, nkp_total, rq, KB), F32),
                        pltpu.VMEM((IDX_HEADS * BLK, IDX_DIM), BF16),
                        pltpu.VMEM((N_KV_HEADS, rep * rq, HEAD_DIM), BF16),
                        pltpu.VMEM((N_HEADS, rq, BLK), F32),
                        pltpu.VMEM((N_HEADS, rq, BLK), F32),
                        pltpu.VMEM((N_KV_HEADS, rep * rq, HEAD_DIM), F32)],
        compiler_params=_cparams(("parallel", "arbitrary")),
        name="attn",
    )(*(past or ()), *(lead or ()), qn, qi, sm, ki_new, k_new, v_new, bias_tiles)


def _out_ffn_kernel(x_ref, ys_ref, ya_ref, gs_ref, ga_ref, wbs_ref, wba_ref, wo_ref, n2_ref, wg_ref, wu_ref, wd_ref,
                    y_ref):
    dot = functools.partial(jnp.dot, preferred_element_type=F32)
    merged = (jax.nn.sigmoid(gs_ref[...]) * dot(ys_ref[...], wbs_ref[...])
              + jax.nn.sigmoid(ga_ref[...]) * dot(ya_ref[...], wba_ref[...]))
    h = x_ref[...] + dot(merged.astype(BF16), wo_ref[...])
    hn = _rms(h, n2_ref[...]).astype(BF16)
    act = (_silu(dot(hn, wg_ref[...])) * dot(hn, wu_ref[...])).astype(BF16)
    y_ref[...] = h + dot(act, wd_ref[...])


def _out_ffn(x2d, ys, ya, gs, ga, p):
    n = x2d.shape[0]
    tm = _row_tile(n)
    d_ff = p["w_gate"].shape[1]
    row = lambda width: pl.BlockSpec((tm, width), lambda i: (i, 0))
    wspec = lambda shape: pl.BlockSpec(shape, lambda i: (0, 0), pipeline_mode=pl.Buffered(1))
    return pl.pallas_call(
        _out_ffn_kernel,
        grid=(n // tm,),
        in_specs=[row(D_MODEL), row(SSD_INNER), row(ATT_INNER), row(D_MODEL), row(D_MODEL),
                  wspec((SSD_INNER, D_MODEL)), wspec((ATT_INNER, D_MODEL)), wspec((D_MODEL, D_MODEL)),
                  _const_spec((1, D_MODEL)), wspec((D_MODEL, d_ff)), wspec((D_MODEL, d_ff)), wspec((d_ff, D_MODEL))],
        out_specs=row(D_MODEL),
        out_shape=jax.ShapeDtypeStruct((n, D_MODEL), F32),
        compiler_params=_cparams(("parallel",)),
        name="out_ffn",
    )(x2d, ys, ya, gs, ga, p["w_br_ssd"], p["w_br_att"], p["w_out"], p["norm2_w"], p["w_gate"], p["w_up"],
      p["w_down"])


def _layer(x, lead_rows, conv_prev, ssm_prev, past, p, bias_tiles, *, chunk_off, n_sel):
    b, t, _ = x.shape
    n_lead = lead_rows.shape[0]
    tp = -(-t // BLK) * BLK
    x2d = jnp.pad(x, ((0, 0), (0, tp - t), (0, 0))).reshape(b * tp, D_MODEL)
    hist = jnp.pad(conv_prev.astype(F32), ((0, 0), (HALO - (CONV_WIDTH - 1), 0), (0, 0)))
    seq = lambda a: a.reshape(b, tp, a.shape[-1])
    ssd_lead = attn_lead = None
    if n_lead:
        lead = _in_proj(jnp.pad(lead_rows.astype(x.dtype), ((0, BLK - n_lead), (0, 0))), hist[:1], p,
                        n_seq=1, t_x=n_lead)
        lz, lxbc, _, lk32, lv32, lkb, lvb, _, lsm, lsmb, _, _, hist = lead
        ssd_lead, attn_lead = (lxbc, lz, lsm), (lsmb, lkb, lvb)
    z, xbc, qn, k32, v32, kb, vb, qi, sm, smb, gs, ga, conv_new8 = _in_proj(x2d, hist, p, n_seq=b, t_x=t)

    gw = SSD_HEADS // SSD_GROUPS * SSD_HEAD_DIM
    y_ssd, ssm_new = _ssd(ssd_lead, seq(xbc), seq(z), seq(sm),
                          ssm_prev.astype(F32).reshape(b, SSD_GROUPS, gw, SSD_STATE), p, n_lead=n_lead, t_x=t)

    if past is not None:
        pk, pv, pki = past
        n_past = pk.shape[1]
        past = (pki.astype(F32), pk.astype(F32).reshape(b, n_past * N_KV_HEADS, HEAD_DIM),
                pv.astype(F32).reshape(b, n_past * N_KV_HEADS, HEAD_DIM))
    y_att = _attn(seq(qn), seq(qi), seq(sm), seq(smb), seq(kb), seq(vb), past, attn_lead, bias_tiles,
                  n_lead=n_lead, chunk_off=chunk_off, t_x=t, n_sel=n_sel)

    y = _out_ffn(x2d, y_ssd.reshape(b * tp, SSD_INNER), y_att.reshape(b * tp, ATT_INNER), gs, ga, p)

    def with_lead(new, lead_part):
        if not n_lead:
            return new
        return jnp.concatenate([jnp.broadcast_to(lead_part[None, :n_lead], (b, n_lead) + new.shape[2:]), new], axis=1)

    heads = lambda a, rows: a.reshape(-1, rows, N_KV_HEADS, HEAD_DIM)
    k_new = with_lead(heads(k32, tp)[:, :t], heads(lk32, BLK)[0] if n_lead else None)
    v_new = with_lead(heads(v32, tp)[:, :t], heads(lv32, BLK)[0] if n_lead else None)
    ki_new = with_lead(seq(sm)[:, :t, SM_KI:SM_KI + IDX_DIM], lsm[:, SM_KI:SM_KI + IDX_DIM] if n_lead else None)
    ssm_new = ssm_new.reshape(b, SSD_HEADS, SSD_HEAD_DIM, SSD_STATE)
    conv_new = conv_new8[:, HALO - (CONV_WIDTH - 1):]
    return y.reshape(b, tp, D_MODEL)[:, :t], k_new, v_new, ki_new, ssm_new, conv_new


def _prepare_params(l, norm1_w, w_in, conv_w, conv_b, dt_bias, a_log, d_skip, ssd_norm_w, q_norm_w, k_norm_w,
                    idx_k_norm_w, w_br_ssd, w_br_att, w_out, norm2_w, w_gate, w_up, w_down):
    offs = [0]
    for w in IN_WIDTHS:
        offs.append(offs[-1] + w)
    i_z, i_xbc, i_dt, i_q, i_k, i_v, i_qi, i_ki, i_wi, i_gs, i_ga = range(11)
    run = lambda first, last: w_in[l][:, offs[first]:offs[last + 1]].astype(BF16)
    pad = jnp.zeros((D_MODEL, LANES - IDX_DIM - SSD_HEADS - IDX_HEADS), BF16)
    w_perm = jnp.concatenate([run(i_z, i_xbc), run(i_q, i_qi), run(i_gs, i_ga),
                              run(i_ki, i_ki), run(i_dt, i_dt), run(i_wi, i_wi), pad], axis=1)

    def lanes_at(vec, start):
        return jnp.zeros((1, LANES), F32).at[0, start:start + vec.shape[0]].set(vec.astype(F32))

    dtb = lanes_at(dt_bias[l], SM_DT)
    alog = lanes_at(a_log[l], SM_DT)
    head_of_channel = jnp.arange(SSD_INNER) // SSD_HEAD_DIM
    expand = (jnp.arange(LANES)[:, None] == head_of_channel[None, :] + SM_DT).astype(BF16)
    row = lambda v: v.astype(F32).reshape(1, -1)
    return dict(
        norm1_w=row(norm1_w[l]), w_in=w_perm, conv_w=conv_w[l].astype(F32), conv_b=row(conv_b[l]),
        dtb=dtb, dtb_t=dtb.reshape(LANES, 1), alog=alog, alog_t=alog.reshape(LANES, 1),
        dskip_x=row(jnp.repeat(d_skip[l], SSD_HEAD_DIM)), ssd_norm_w=row(ssd_norm_w[l]), expand=expand,
        q_norm_w=row(q_norm_w[l]), k_norm_w=row(k_norm_w[l]),
        idx_k_norm_w=jnp.ones((1, LANES), F32).at[0, SM_KI:SM_KI + IDX_DIM].set(idx_k_norm_w[l].astype(F32)),
        w_br_ssd=w_br_ssd[l].astype(BF16), w_br_att=w_br_att[l].astype(BF16), w_out=w_out[l].astype(BF16),
        norm2_w=row(norm2_w[l]), w_gate=w_gate[l].astype(BF16), w_up=w_up[l].astype(BF16),
        w_down=w_down[l].astype(BF16))


def kernel(x_prompt, x_sample, cache_k, cache_v, cache_kidx, state_ssm, state_conv, meta_tokens, rel_bias, norm1_w,
           w_in, conv_w, conv_b, dt_bias, a_log, d_skip, ssd_norm_w, q_norm_w, k_norm_w, idx_k_norm_w, w_br_ssd,
           w_br_att, w_out, norm2_w, w_gate, w_up, w_down):
    bp, sp, _ = x_prompt.shape
    bs, ts, _ = x_sample.shape
    past = cache_k.shape[2]
    assert w_in.shape[0] == 1
    assert past % BLK == 0 and BLK % CHUNK == 0 and N_META <= CHUNK
    l = 0

    n_sel_p = min(TOPK_MAX, sp // 4)
    n_sel_s = min(TOPK_MAX, (past + ts) // 4)
    conv0 = jnp.zeros((bp, CONV_WIDTH - 1, CONV_DIM), F32)
    ssm0 = jnp.zeros((bp, SSD_HEADS, SSD_HEAD_DIM, SSD_STATE), F32)
    bias_tiles = _bias_tiles(rel_bias.astype(F32))
    p = _prepare_params(l, norm1_w, w_in, conv_w, conv_b, dt_bias, a_log, d_skip, ssd_norm_w, q_norm_w, k_norm_w,
                        idx_k_norm_w, w_br_ssd, w_br_att, w_out, norm2_w, w_gate, w_up, w_down)
    y_prompt, *rest_p = _layer(x_prompt, meta_tokens, conv0, ssm0, None, p, bias_tiles,
                               chunk_off=CHUNK - N_META, n_sel=n_sel_p)
    y_sample, *rest_s = _layer(x_sample, meta_tokens[:0], state_conv[l], state_ssm[l],
                               (cache_k[l], cache_v[l], cache_kidx[l]), p, bias_tiles,
                               chunk_off=0, n_sel=n_sel_s)

    dtypes = (x_prompt.dtype, x_prompt.dtype, x_prompt.dtype, state_ssm.dtype, x_prompt.dtype)
    return (y_prompt, y_sample, *(o[None].astype(dt) for o, dt in zip(rest_p, dtypes)),
            *(o[None].astype(dt) for o, dt in zip(rest_s, dtypes)))
```

```python
import functools

import jax
import jax.numpy as jnp
from jax import lax
from jax.experimental import pallas as pl
from jax.experimental.pallas import tpu as pltpu

F32 = jnp.float32
BF16 = jnp.bfloat16

D_MODEL = 1024
CHUNK = 64
N_META = 16
SSD_HEADS = 16
SSD_HEAD_DIM = 64
SSD_INNER = SSD_HEADS * SSD_HEAD_DIM
SSD_GROUPS = 4
SSD_STATE = 128
CONV_WIDTH = 4
CONV_DIM = SSD_INNER + 2 * SSD_GROUPS * SSD_STATE
N_HEADS = 8
N_KV_HEADS = 2
HEAD_DIM = 128
ATT_INNER = N_HEADS * HEAD_DIM
KV_DIM = N_KV_HEADS * HEAD_DIM
IDX_HEADS = 8
IDX_DIM = 64
TOPK_MAX = 256
REL_BUCKETS = 32
REL_MAX_DIST = 128
IN_WIDTHS = (SSD_INNER, CONV_DIM, SSD_HEADS, ATT_INNER, KV_DIM, KV_DIM, IDX_HEADS * IDX_DIM, IDX_DIM, IDX_HEADS,
             D_MODEL, D_MODEL)
EPS = 1e-6

LANES = 128
SUBLANES = 8
VMEM_LIMIT_BYTES = 56 * 1024 * 1024

BLK = LANES
ROW_TILE = 256
IN_PROJ_PASSES = 2
HALO = SUBLANES
SSD_SEQS_PER_STEP = 4

C_Z = 0
C_XBC = C_Z + SSD_INNER
C_Q = C_XBC + CONV_DIM
C_K = C_Q + ATT_INNER
C_V = C_K + KV_DIM
C_QI = C_V + KV_DIM
C_GS = C_QI + IDX_HEADS * IDX_DIM
C_GA = C_GS + D_MODEL
C_SM = C_GA + D_MODEL
IN_PAD = C_SM + LANES
SM_KI = 0
SM_DT = SM_KI + IDX_DIM
SM_WI = SM_DT + SSD_HEADS

SEARCH_STEPS = 16
SEARCH_CLAMP = 1.0 / 16
NEG_BIG = -1e30


def _cparams(sem):
    return pltpu.CompilerParams(dimension_semantics=sem, vmem_limit_bytes=VMEM_LIMIT_BYTES)


def _const_spec(shape):
    nd = len(shape)
    return pl.BlockSpec(shape, lambda *_: (0,) * nd)


def _rms(x, w):
    return x * lax.rsqrt(jnp.mean(x * x, axis=-1, keepdims=True) + EPS) * w


def _silu(x):
    return x * jax.nn.sigmoid(x)


def _in_proj_kernel(x_ref, hist_ref, n1_ref, w_ref, qn_ref, kn_ref, kin_ref, cw_ref, cb_ref,
                    z_ref, xc_ref, q_ref, k_ref, v_ref, kb_ref, vb_ref, qi_ref, sm_ref, smb_ref, gs_ref, ga_ref, tail_ref,
                    xpad_scr, *, tiles_per_seq, t_x):
    j = pl.program_id(1)
    tm = x_ref.shape[0]

    @pl.when(j == 0)
    def _():
        xpad_scr[0:HALO, :] = hist_ref[...]

    sub = ROW_TILE if tm % ROW_TILE == 0 else tm
    for r0 in range(0, tm, sub):
        rows, kv_rows = pl.ds(r0, sub), pl.ds(r0 * N_KV_HEADS, sub * N_KV_HEADS)
        _in_proj_rows(x_ref.at[rows], n1_ref, w_ref, qn_ref, kn_ref, kin_ref, cw_ref, cb_ref,
                      z_ref.at[rows], xc_ref.at[rows], q_ref.at[rows], k_ref.at[kv_rows], v_ref.at[kv_rows],
                      kb_ref.at[rows], vb_ref.at[rows], qi_ref.at[rows], sm_ref.at[rows], smb_ref.at[rows],
                      gs_ref.at[rows], ga_ref.at[rows], xpad_scr.at[pl.ds(r0, sub + 2 * HALO)])

    n_last = t_x - (tiles_per_seq - 1) * tm

    @pl.when(j == tiles_per_seq - 1)
    def _():
        tail_ref[...] = xpad_scr[n_last:n_last + HALO, :]

    xpad_scr[0:HALO, :] = xpad_scr[tm:tm + HALO, :]


def _in_proj_rows(x_ref, n1_ref, w_ref, qn_ref, kn_ref, kin_ref, cw_ref, cb_ref,
                  z_ref, xc_ref, q_ref, k_ref, v_ref, kb_ref, vb_ref, qi_ref, sm_ref, smb_ref, gs_ref, ga_ref, xpad_scr):
    hn = _rms(x_ref[...], n1_ref[...]).astype(BF16)
    tm = x_ref.shape[0]

    def mm(lo, hi):
        return jnp.dot(hn, w_ref[:, lo:hi], preferred_element_type=F32)

    step = 2 * LANES
    conv_slabs = iter(range(0, CONV_DIM, LANES))

    def conv_next():
        c0 = next(conv_slabs, None)
        if c0 is None:
            return
        if c0 % step == 0:
            xpad_scr[HALO:HALO + tm, c0:c0 + step] = mm(C_XBC + c0, C_XBC + c0 + step)
        sl = slice(c0, c0 + LANES)
        first = HALO - (CONV_WIDTH - 1)
        acc = xpad_scr[first:first + tm, sl] * cw_ref[0:1, sl]
        for i in range(1, CONV_WIDTH):
            acc = acc + xpad_scr[first + i:first + i + tm, sl] * cw_ref[i:i + 1, sl]
        xc_ref[:, sl] = _silu(cb_ref[:, sl] + acc)

    def project(lo, hi, out_ref, dtype=F32):
        for c0 in range(0, hi - lo, step):
            out_ref[:, c0:c0 + step] = mm(lo + c0, lo + c0 + step).astype(dtype)
            conv_next()

    project(C_Z, C_XBC, z_ref)
    for c0 in range(0, ATT_INNER, step):
        q = mm(C_Q + c0, C_Q + c0 + step)
        conv_next()
        for h in range(step // HEAD_DIM):
            sl = slice(h * HEAD_DIM, (h + 1) * HEAD_DIM)
            q_ref[:, c0 + h * HEAD_DIM:c0 + (h + 1) * HEAD_DIM] = _rms(q[:, sl], qn_ref[...]).astype(BF16)
    k = mm(C_K, C_V)
    conv_next()
    v = mm(C_V, C_QI)
    conv_next()
    for h in range(N_KV_HEADS):
        sl = slice(h * HEAD_DIM, (h + 1) * HEAD_DIM)
        kh = _rms(k[:, sl], kn_ref[...])
        k_ref[pl.ds(h, tm, stride=N_KV_HEADS), :] = kh
        kb_ref[:, sl] = kh.astype(BF16)
        v_ref[pl.ds(h, tm, stride=N_KV_HEADS), :] = v[:, sl]
    vb_ref[...] = v.astype(BF16)
    project(C_QI, C_GS, qi_ref, BF16)
    project(C_GS, C_GA, gs_ref)
    project(C_GA, C_SM, ga_ref)
    sm = mm(C_SM, IN_PAD)
    assert next(conv_slabs, None) is None
    lane = lax.broadcasted_iota(jnp.int32, sm.shape, 1)
    is_ki = lane < SM_KI + IDX_DIM
    ms = jnp.sum(jnp.where(is_ki, sm * sm, 0.0), axis=-1, keepdims=True) * (1.0 / IDX_DIM)
    ki = sm * lax.rsqrt(ms + EPS) * kin_ref[...]
    is_wi = (lane >= SM_WI) & (lane < SM_WI + IDX_HEADS)
    out = jnp.where(is_ki, ki, jnp.where(is_wi, sm * (IDX_HEADS ** -0.5), sm))
    sm_ref[...] = out
    smb_ref[...] = out.astype(BF16)


def _row_tile(n):
    return ROW_TILE if n % ROW_TILE == 0 else BLK


def _in_proj(x2d, hist, p, *, n_seq, t_x):
    n = x2d.shape[0]
    tp = n // n_seq
    tm = _row_tile(tp)
    if tp % (IN_PROJ_PASSES * tm) == 0:
        tm *= IN_PROJ_PASSES
    tiles = tp // tm
    shared_hist = hist.shape[0] == 1
    row = lambda width, per_token=1: pl.BlockSpec((tm * per_token, width), lambda s, j: (s * tiles + j, 0))
    per_seq = pl.BlockSpec((None, HALO, CONV_DIM), lambda s, j: (0 if shared_hist else s, 0, 0))
    outs = [
        (SSD_INNER, F32, 1), (CONV_DIM, F32, 1), (ATT_INNER, BF16, 1), (HEAD_DIM, F32, N_KV_HEADS),
        (HEAD_DIM, F32, N_KV_HEADS), (KV_DIM, BF16, 1), (KV_DIM, BF16, 1), (IDX_HEADS * IDX_DIM, BF16, 1),
        (LANES, F32, 1), (LANES, BF16, 1), (D_MODEL, F32, 1), (D_MODEL, F32, 1),
    ]
    kern = functools.partial(_in_proj_kernel, tiles_per_seq=tiles, t_x=t_x)
    return pl.pallas_call(
        kern,
        grid=(n_seq, tiles),
        in_specs=[row(D_MODEL), per_seq, _const_spec((1, D_MODEL)),
                  pl.BlockSpec((D_MODEL, IN_PAD), lambda s, j: (0, 0), pipeline_mode=pl.Buffered(1)),
                  _const_spec((1, HEAD_DIM)), _const_spec((1, HEAD_DIM)), _const_spec((1, LANES)),
                  _const_spec((CONV_WIDTH, CONV_DIM)), _const_spec((1, CONV_DIM))],
        out_specs=[row(w, per) for w, _, per in outs]
        + [pl.BlockSpec((None, HALO, CONV_DIM), lambda s, j: (s, 0, 0))],
        out_shape=[jax.ShapeDtypeStruct((n * per, w), dt) for w, dt, per in outs]
        + [jax.ShapeDtypeStruct((n_seq, HALO, CONV_DIM), F32)],
        scratch_shapes=[pltpu.VMEM((tm + 2 * HALO, CONV_DIM), F32)],
        compiler_params=_cparams(("parallel", "arbitrary")),
        name="in_proj",
    )(x2d, hist, p["norm1_w"], p["w_in"], p["q_norm_w"], p["k_norm_w"], p["idx_k_norm_w"], p["conv_w"], p["conv_b"])


def _softplus(x):
    return jnp.maximum(x, 0.0) + jnp.log1p(jnp.exp(-jnp.abs(x)))


def _split3(x):
    hi = x.astype(BF16)
    r1 = x - hi.astype(F32)
    mid = r1.astype(BF16)
    lo = (r1 - mid.astype(F32)).astype(BF16)
    return hi, mid, lo


def _ssd_kernel(*refs, n_chunks, n_lead, t_x):
    if n_lead:
        xbc_lead_ref, z_lead_ref, sm_lead_ref, *refs = refs
    (xbc_ref, z_ref, sm_ref, sprev_ref, dtb_ref, dtbt_ref, alog_ref, alogt_ref,
     dsk_ref, nw_ref, exp_ref, y_ref, snew_ref, s_scr, y_scr) = refs
    c = pl.program_id(1)
    lead_chunks = 1 if n_lead else 0
    seqs = range(xbc_ref.shape[0])

    @pl.when(c == 0)
    def _():
        for k in seqs:
            for g in range(SSD_GROUPS):
                s_scr[k, g] = sprev_ref[k, g].T

    def chunk(k, xc_ref, z_src_ref, sm_src_ref, n_valid):
        _ssd_chunk(xc_ref, z_src_ref, sm_src_ref, n_valid, dtb_ref, dtbt_ref, alog_ref, alogt_ref, dsk_ref, nw_ref,
                   exp_ref, y_ref.at[k], s_scr.at[k], y_scr.at[k])

    def x_chunks():
        for k in seqs:
            chunk(k, xbc_ref.at[k], z_ref.at[k], sm_ref.at[k], jnp.minimum(BLK, t_x - (c - lead_chunks) * BLK))

    if n_lead:
        @pl.when(c == 0)
        def _():
            for k in seqs:
                chunk(k, xbc_lead_ref, z_lead_ref, sm_lead_ref, n_lead)

        pl.when(c > 0)(x_chunks)
    else:
        x_chunks()

    @pl.when(c == n_chunks - 1)
    def _():
        for k in seqs:
            for g in range(SSD_GROUPS):
                snew_ref[k, g] = s_scr[k, g].T


def _ssd_chunk(xc_ref, z_ref, sm_ref, n_valid, dtb_ref, dtbt_ref, alog_ref, alogt_ref, dsk_ref, nw_ref, exp_ref,
               y_ref, s_scr, y_scr):
    q = BLK
    gw = SSD_HEADS // SSD_GROUPS * SSD_HEAD_DIM
    sm = sm_ref[...]

    heads = slice(SM_DT, SM_DT + SSD_HEADS)
    smt = sm.T[heads, :]
    lane = lax.broadcasted_iota(jnp.int32, (q, LANES), 1)
    row = lax.broadcasted_iota(jnp.int32, (q, LANES), 0)
    is_dt = (lane >= SM_DT) & (lane < SM_DT + SSD_HEADS) & (row < n_valid)
    time_t = lax.broadcasted_iota(jnp.int32, (SSD_HEADS, q), 1)
    dt = jnp.where(is_dt, _softplus(sm + dtb_ref[...]), 0.0)
    dtt = jnp.where(time_t < n_valid, _softplus(smt + dtbt_ref[heads, :]), 0.0)
    da = dt * (-jnp.exp(alog_ref[...]))
    dat = dtt * (-jnp.exp(alogt_ref[heads, :]))
    ii = lax.broadcasted_iota(jnp.int32, (q, q), 0)
    jj = lax.broadcasted_iota(jnp.int32, (q, q), 1)
    causal = jj <= ii
    acum = sum(jnp.dot(causal.astype(BF16), p, preferred_element_type=F32) for p in _split3(da))
    acumt = sum(jnp.dot(p, (ii <= jj).astype(BF16), preferred_element_type=F32) for p in _split3(dat))
    a_last = acum[q - 1:q, :]
    expand = exp_ref[...]
    stacked = jnp.concatenate([jnp.exp(acum), jnp.exp(a_last - acum) * dt,
                               jnp.broadcast_to(jnp.exp(a_last), (SUBLANES, LANES))], axis=0)
    stacked_x = sum(jnp.dot(p, expand, preferred_element_type=F32) for p in _split3(stacked))
    ea_x = stacked_x[0:q]
    wdt_x = stacked_x[q:2 * q]
    dec_x = stacked_x[2 * q:2 * q + 1]

    sq = None
    for g in range(SSD_GROUPS):
        gsl = slice(g * gw, (g + 1) * gw)
        bsl = slice(SSD_INNER + g * SSD_STATE, SSD_INNER + (g + 1) * SSD_STATE)
        csl = slice(SSD_INNER + SSD_GROUPS * SSD_STATE + g * SSD_STATE,
                    SSD_INNER + SSD_GROUPS * SSD_STATE + (g + 1) * SSD_STATE)
        bmf = xc_ref[:, bsl]
        bm = bmf.astype(BF16)
        cm = xc_ref[:, csl].astype(BF16)
        xg = xc_ref[:, gsl]
        xgb = xg.astype(BF16)
        cbm = lax.dot_general(cm, bm, (((1,), (1,)), ((), ())), preferred_element_type=F32)
        xw = (xg * wdt_x[:, gsl]).astype(BF16)
        st = jnp.dot(bmf.T.astype(BF16), xw, preferred_element_type=F32)
        s_in = s_scr[g]
        y_off = jnp.dot(cm, s_in.astype(BF16), preferred_element_type=F32) * ea_x[:, gsl]
        s_scr[g] = s_in * dec_x[:, gsl] + st
        first_of_pair = lax.broadcasted_iota(jnp.int32, (q, LANES), 1) < SSD_HEAD_DIM
        for rp in range(gw // LANES):
            psl = slice(rp * LANES, (rp + 1) * LANES)
            osl = slice(g * gw + rp * LANES, g * gw + (rp + 1) * LANES)
            pair = []
            for h in (g * (SSD_HEADS // SSD_GROUPS) + 2 * rp, g * (SSD_HEADS // SSD_GROUPS) + 2 * rp + 1):
                seg = acum[:, SM_DT + h:SM_DT + h + 1] - acumt[h:h + 1, :]
                lmat = jnp.exp(jnp.where(causal, seg, -jnp.inf))
                wmat = (cbm * lmat * dtt[h:h + 1, :]).astype(BF16)
                pair.append(jnp.dot(wmat, xgb[:, psl], preferred_element_type=F32))
            y_diag = jnp.where(first_of_pair, pair[0], pair[1])
            yp = (y_diag + y_off[:, psl] + xg[:, psl] * dsk_ref[:, osl]) * _silu(z_ref[:, osl])
            y_scr[:, osl] = yp
            sq = yp * yp if sq is None else sq + yp * yp

    ms = jnp.sum(sq, axis=-1, keepdims=True) * (1.0 / SSD_INNER)
    y_ref[...] = (y_scr[...] * lax.rsqrt(ms + EPS) * nw_ref[...]).astype(BF16)


def _ssd(lead, xbc, z, sm, ssm_prev, p, *, n_lead, t_x):
    b, tp, _ = xbc.shape
    lead_chunks = 1 if n_lead else 0
    assert n_lead % SUBLANES == 0 and n_lead <= BLK and 2 * SSD_HEAD_DIM == LANES
    nc = tp // BLK + lead_chunks
    gw = SSD_HEADS // SSD_GROUPS * SSD_HEAD_DIM
    bb = SSD_SEQS_PER_STEP if b % SSD_SEQS_PER_STEP == 0 else 1
    seq = lambda width: pl.BlockSpec((bb, BLK, width), lambda i, c: (i, jnp.maximum(c - lead_chunks, 0), 0))
    state = pl.BlockSpec((bb, SSD_GROUPS, gw, SSD_STATE), lambda i, c: (i, 0, 0, 0))
    kern = functools.partial(_ssd_kernel, n_chunks=nc, n_lead=n_lead, t_x=t_x)
    lead_specs = [_const_spec((BLK, CONV_DIM)), _const_spec((BLK, SSD_INNER)), _const_spec((BLK, LANES))]
    return pl.pallas_call(
        kern,
        grid=(b // bb, nc),
        in_specs=(lead_specs if n_lead else []) + [
                  seq(CONV_DIM), seq(SSD_INNER), seq(LANES), state,
                  _const_spec((1, LANES)), _const_spec((LANES, 1)), _const_spec((1, LANES)), _const_spec((LANES, 1)),
                  _const_spec((1, SSD_INNER)), _const_spec((1, SSD_INNER)), _const_spec((LANES, SSD_INNER))],
        out_specs=[seq(SSD_INNER), state],
        out_shape=[jax.ShapeDtypeStruct((b, tp, SSD_INNER), BF16),
                   jax.ShapeDtypeStruct((b, SSD_GROUPS, gw, SSD_STATE), F32)],
        scratch_shapes=[pltpu.VMEM((bb, SSD_GROUPS, SSD_STATE, gw), F32),
                        pltpu.VMEM((bb, BLK, SSD_INNER), F32)],
        compiler_params=_cparams(("parallel", "arbitrary")),
        name="ssd",
    )(*(lead if n_lead else ()), xbc, z, sm, ssm_prev, p["dtb"], p["dtb_t"],
      p["alog"], p["alog_t"], p["dskip_x"], p["ssd_norm_w"], p["expand"])


N_SHIFT_TILES = 5
LEAD_TILE = N_SHIFT_TILES
N_BIAS_TILES = N_SHIFT_TILES + 1


def _log_bucket_starts():
    nb = REL_BUCKETS // 2
    max_exact = nb // 2
    s = nb - max_exact
    starts = []
    for m in range(1, s):
        n = max_exact
        while n ** s * max_exact ** m < max_exact ** s * REL_MAX_DIST ** m:
            n += 1
        starts.append(n)
    return starts


def _bias_kernel(rb_ref, bt_ref):
    nb = REL_BUCKETS // 2
    max_exact = nb // 2
    qq = lax.broadcasted_iota(jnp.int32, (BLK, BLK), 0)
    kk = lax.broadcasted_iota(jnp.int32, (BLK, BLK), 1)
    for u in range(N_BIAS_TILES):
        rel = kk - qq + ((u - 2) * BLK if u < N_SHIFT_TILES else -N_META)
        n = jnp.abs(rel)
        large = max_exact + sum(jnp.where(n >= start, 1, 0) for start in _log_bucket_starts())
        bucket = jnp.where(rel > 0, nb, 0) + jnp.where(n < max_exact, n, large)
        for h in range(N_HEADS):
            acc = jnp.zeros((BLK, BLK), F32)
            for bkt in range(REL_BUCKETS):
                acc = jnp.where(bucket == bkt, rb_ref[bkt, h], acc)
            bt_ref[h * N_BIAS_TILES + u] = acc


def _bias_tiles(rel_bias):
    return pl.pallas_call(
        _bias_kernel,
        in_specs=[pl.BlockSpec(memory_space=pltpu.SMEM)],
        out_specs=pl.BlockSpec(memory_space=pltpu.VMEM),
        out_shape=jax.ShapeDtypeStruct((N_HEADS * N_BIAS_TILES, BLK, BLK), F32),
        name="bias_tiles",
    )(rel_bias)


KB = 2 * BLK


def _attn_kernel(*refs, nkp_total, n_past, n_lead, chunk_off, t_x, n_sel):
    if n_past:
        pki_ref, pk_ref, pv_ref, *refs = refs
    if n_lead:
        lki_ref, lk_ref, lv_ref, *refs = refs
    (qn_ref, qi_ref, sm_ref, ki_new_ref, k_new_ref, v_new_ref, bt_ref, o_ref,
     ki_scr, k_scr, v_scr, st_scr, m_scr, lg_scr, qis_scr, qs_scr, mrun_scr, lrun_scr, acc_scr) = refs
    i = pl.program_id(1)
    r = BLK
    n_prefix = n_past + (BLK if n_lead else 0)
    gap = BLK - n_lead if n_lead else 0
    l_valid = n_prefix - gap + t_x

    @pl.when(i == 0)
    def _():
        if n_lead:
            ki_scr[0:BLK, :] = lki_ref[:, SM_KI:SM_KI + IDX_DIM]
            k_scr[0:BLK, :] = lk_ref[...]
            v_scr[0:BLK, :] = lv_ref[...]
        if n_past:
            def load_past(c, carry):
                rows = pl.ds(pl.multiple_of(c * KB, KB), KB)
                ki_scr[rows, :] = pki_ref[rows, :].astype(BF16)
                for g in range(N_KV_HEADS):
                    src = pl.ds(pl.multiple_of(c * KB * N_KV_HEADS, KB) + g, KB, stride=N_KV_HEADS)
                    k_scr[rows, g * HEAD_DIM:(g + 1) * HEAD_DIM] = pk_ref[src, :].astype(BF16)
                    v_scr[rows, g * HEAD_DIM:(g + 1) * HEAD_DIM] = pv_ref[src, :].astype(BF16)
                return carry

            lax.fori_loop(0, n_past // KB, load_past, 0)
        n_new = k_new_ref.shape[0]
        ki_scr[n_prefix:n_prefix + n_new, :] = ki_new_ref[:, SM_KI:SM_KI + IDX_DIM]
        k_scr[n_prefix:n_prefix + n_new, :] = k_new_ref[...]
        v_scr[n_prefix:n_prefix + n_new, :] = v_new_ref[...]
        n_tail = nkp_total * KB - n_prefix - n_new
        if n_tail:
            for scr in (ki_scr, k_scr, v_scr):
                scr[n_prefix + n_new:, :] = jnp.zeros((n_tail, scr.shape[1]), BF16)

    qb = n_prefix // BLK + i
    q0 = qb * BLK - gap
    n_keys = nkp_total * KB

    def chunk_end(pos):
        return jnp.minimum(CHUNK * ((pos + chunk_off) // CHUNK + 1) - chunk_off, l_valid)

    nkp = jnp.minimum(nkp_total, (chunk_end(q0 + BLK - 1) + gap + KB - 1) // KB)

    nt = (((1,), (1,)), ((), ()))
    wit = sm_ref[...].T
    n_adm = chunk_end(q0 + lax.broadcasted_iota(jnp.int32, (1, r), 1))
    krow = lax.broadcasted_iota(jnp.int32, (KB, r), 0)

    fold_rows = 8 * SUBLANES

    def fold(x, op):
        return op(x.reshape(KB // fold_rows, fold_rows, r), axis=0)

    def for_key_steps(body, init):
        def several(n, first, c):
            for s in range(n):
                c = body(first + s, c)
            return c

        carry = lax.fori_loop(0, nkp // 4, lambda t, c: several(4, 4 * t, c), init)
        carry = lax.cond(nkp % 4 >= 2, lambda c: several(2, nkp - nkp % 4, c), lambda c: c, carry)
        return lax.cond(nkp % 2 == 1, lambda c: body(nkp - 1, c), lambda c: c, carry)

    for h in range(IDX_HEADS):
        qis_scr[h * r:(h + 1) * r, :] = qi_ref[:, h * IDX_DIM:(h + 1) * IDX_DIM]

    def score_body(jp, carry):
        mn, mx = carry
        kij = ki_scr[pl.ds(pl.multiple_of(jp * KB, KB), KB), :]
        acc = jnp.zeros((KB, r), F32)
        for hp in range(IDX_HEADS // 2):
            sh = lax.dot_general(kij, qis_scr[2 * hp * r:2 * (hp + 1) * r, :], nt, preferred_element_type=F32)
            for e in range(2):
                h = 2 * hp + e
                acc = acc + wit[SM_WI + h:SM_WI + h + 1, :] * jnp.maximum(sh[:, e * r:(e + 1) * r], 0.0)
        adm = krow < n_adm + gap - jp * KB
        if gap:
            adm = adm & ((jp > 0) | (krow < n_lead) | (krow >= BLK))
        s = jnp.where(adm, acc * (IDX_DIM ** -0.5), -jnp.inf)
        st_scr[jp] = s
        mn = jnp.minimum(mn, fold(jnp.where(adm, s, jnp.inf), jnp.min))
        mx = jnp.maximum(mx, fold(s, jnp.max))
        return mn, mx

    init = (jnp.full((fold_rows, r), jnp.inf, F32), jnp.full((fold_rows, r), -jnp.inf, F32))
    mn, mx = for_key_steps(score_body, init)
    lo0 = jnp.min(mn, axis=0, keepdims=True)
    hi0 = jnp.max(mx, axis=0, keepdims=True)
    kk = jnp.minimum(n_adm, n_sel).astype(F32)

    def count(pred):
        def body(jp, acc):
            return acc + fold(jnp.where(pred(st_scr[jp], jp), 1.0, 0.0), jnp.sum)
        acc = lax.fori_loop(0, nkp, body, jnp.zeros((fold_rows, r), F32))
        return jnp.sum(acc, axis=0, keepdims=True)

    def search_body(_, carry):
        lo, hi, c_lo, c_hi, ub, c_ub, hit = carry
        frac = jnp.clip((c_lo - kk) / (c_lo - c_hi), SEARCH_CLAMP, 1.0 - SEARCH_CLAMP)
        mid = jnp.where(hit > 0.0, lo, lo * (1.0 - frac) + hi * frac)
        cnt = count(lambda s, jp: s >= mid)
        ok = cnt >= kk
        now = cnt == kk
        lo = jnp.where(ok, mid, lo)
        hi = jnp.where(ok & ~now, hi, mid)
        c_lo = jnp.where(ok, cnt, c_lo)
        c_hi = jnp.where(now, kk - 1.0, jnp.where(ok, c_hi, cnt))
        ub = jnp.where(ok, ub, mid)
        c_ub = jnp.where(ok, c_ub, cnt)
        return lo, hi, c_lo, c_hi, ub, c_ub, jnp.where(now, 1.0, 0.0)

    all_adm = n_adm.astype(F32) == kk
    zeros = jnp.zeros((1, r), F32)
    init = (lo0, jnp.where(all_adm, lo0, hi0), n_adm.astype(F32), jnp.where(all_adm, kk - 1.0, zeros),
            jnp.full((1, r), jnp.inf, F32), zeros, jnp.where(all_adm, 1.0, 0.0))
    lo_f, _, _, _, ub, c_ub, hit_f = lax.fori_loop(0, SEARCH_STEPS, search_body, init)
    hit = hit_f > 0.0
    pending = jnp.sum(jnp.where(hit, 0, 1))
    take_all = jnp.full((1, r), float(n_keys), F32)

    def exact_path():
        def next_below(ub):
            def body(jp, acc):
                s = st_scr[jp]
                return jnp.maximum(acc, fold(jnp.where(s < ub, s, -jnp.inf), jnp.max))
            acc = lax.fori_loop(0, nkp, body, jnp.full((fold_rows, r), -jnp.inf, F32))
            return jnp.max(acc, axis=0, keepdims=True)

        def descend_cond(carry):
            *_, todo, it = carry
            return (todo > 0) & (it < n_keys)

        def descend_body(carry):
            ub, c_ub, _, _, _, it = carry
            t = next_below(ub)
            c_t = count(lambda s, jp: s >= t)
            done = hit | (c_t >= kk)
            return (jnp.where(done, ub, t), jnp.where(done, c_ub, c_t), t, c_t,
                    jnp.sum(jnp.where(done, 0, 1)), it + 1)

        _, c_gt, t, c_t, _, _ = lax.while_loop(descend_cond, descend_body,
                                               (ub, c_ub, lo0, zeros, jnp.int32(1), jnp.int32(0)))
        ties_wanted = jnp.where(hit, take_all, kk - c_gt)
        extra_ties = jnp.sum(jnp.where(~hit & (c_t - c_gt > ties_wanted), 1, 0))
        return jnp.where(hit, lo_f, t), ties_wanted, extra_ties

    thr, ties_wanted, extra_ties = lax.cond(pending > 0, exact_path, lambda: (lo_f, take_all, jnp.int32(0)))

    def mask_with_ties():
        ii = lax.broadcasted_iota(jnp.int32, (KB, KB), 0)
        jj = lax.broadcasted_iota(jnp.int32, (KB, KB), 1)
        upto = (jj <= ii).astype(BF16)

        def body(jp, wanted):
            s = st_scr[jp]
            tied = s == thr
            rank = jnp.dot(upto, jnp.where(tied, 1.0, 0.0).astype(BF16), preferred_element_type=F32)
            m_scr[jp] = jnp.where((s > thr) | (tied & (rank <= wanted)), 0.0, NEG_BIG).T
            return wanted - rank[KB - 1:KB, :]

        for_key_steps(body, ties_wanted)

    scale = HEAD_DIM ** -0.5
    rep = N_HEADS // N_KV_HEADS
    rq = mrun_scr.shape[1]
    for h in range(N_HEADS):
        qs_scr[h // rep, (h % rep) * rq:(h % rep + 1) * rq, :] = qn_ref[0:rq, h * HEAD_DIM:(h + 1) * HEAD_DIM]
    mrun_scr[...] = jnp.full(mrun_scr.shape, -jnp.inf, F32)
    lrun_scr[...] = jnp.zeros(lrun_scr.shape, F32)
    acc_scr[...] = jnp.zeros(acc_scr.shape, F32)

    def logit_body(mask_of, jp, carry):
        u0 = jnp.clip(2 * jp - qb + 2, 0, N_SHIFT_TILES - 1)
        u1 = jnp.clip(2 * jp + 1 - qb + 2, 0, N_SHIFT_TILES - 1)
        if gap:
            u0 = jnp.where((jp == 0) & (qb == 1), LEAD_TILE, u0)
        madd = mask_of(jp)[0:rq, :]
        for g in range(N_KV_HEADS):
            lt = lax.dot_general(qs_scr[g], k_scr[pl.ds(pl.multiple_of(jp * KB, KB), KB),
                                                  g * HEAD_DIM:(g + 1) * HEAD_DIM], nt,
                                 preferred_element_type=F32)
            for e in range(rep):
                h = g * rep + e
                bias = jnp.concatenate([bt_ref[h * N_BIAS_TILES + u0][0:rq, :],
                                        bt_ref[h * N_BIAS_TILES + u1][0:rq, :]], axis=1)
                lg = lt[e * rq:(e + 1) * rq, :] * scale + bias + madd
                lg_scr[h, jp] = lg
                mrun_scr[h] = jnp.maximum(mrun_scr[h], jnp.maximum(lg[:, :BLK], lg[:, BLK:]))
        return carry

    def logits_with_ties():
        mask_with_ties()
        return for_key_steps(functools.partial(logit_body, lambda jp: m_scr[jp]), jnp.int32(0))

    def logits_plain():
        inline_mask = lambda jp: jnp.where(st_scr[jp] >= thr, 0.0, NEG_BIG).T
        return for_key_steps(functools.partial(logit_body, inline_mask), jnp.int32(0))

    lax.cond(extra_ties > 0, logits_with_ties, logits_plain)
    for h in range(N_HEADS):
        mrun_scr[h] = jnp.broadcast_to(jnp.max(mrun_scr[h], axis=1, keepdims=True), (rq, BLK))

    def pv_body(jp, carry):
        for g in range(N_KV_HEADS):
            es = []
            for e in range(rep):
                h = g * rep + e
                mrow = mrun_scr[h]
                ex = jnp.exp(lg_scr[h, jp] - jnp.concatenate([mrow, mrow], axis=1))
                lrun_scr[h] = lrun_scr[h] + (ex[:, :BLK] + ex[:, BLK:])
                es.append(ex.astype(BF16))
            acc_scr[g] = acc_scr[g] + jnp.dot(jnp.concatenate(es, axis=0),
                                              v_scr[pl.ds(pl.multiple_of(jp * KB, KB), KB),
                                                    g * HEAD_DIM:(g + 1) * HEAD_DIM],
                                              preferred_element_type=F32)
        return carry

    for_key_steps(pv_body, 0)
    for h in range(N_HEADS):
        g, e = h // rep, h % rep
        den = jnp.sum(lrun_scr[h], axis=1, keepdims=True)
        o_ref[0:rq, h * HEAD_DIM:(h + 1) * HEAD_DIM] = (acc_scr[g, e * rq:(e + 1) * rq, :] / den).astype(BF16)
    if rq < r:
        o_ref[rq:, :] = jnp.zeros((r - rq, ATT_INNER), BF16)


def _attn(qn, qi, sm, ki_new, k_new, v_new, past, lead, bias_tiles, *, n_lead, chunk_off, t_x, n_sel):
    b, tq, _ = qn.shape
    n_past = past[0].shape[1] if past is not None else 0
    assert n_past % KB == 0 and not (n_past and n_lead)
    n_prefix = n_past + (BLK if n_lead else 0)
    n_keys = -(-(n_prefix + tq) // KB) * KB
    nkp_total = n_keys // KB
    rep = N_HEADS // N_KV_HEADS
    rq = BLK if tq > BLK else min(BLK, -(-t_x // (2 * SUBLANES)) * 2 * SUBLANES)
    seq = lambda width: pl.BlockSpec((None, BLK, width), lambda bi, i: (bi, i, 0))
    rows = lambda n, width: pl.BlockSpec((None, n, width), lambda bi, i: (bi, 0, 0))
    kern = functools.partial(_attn_kernel, nkp_total=nkp_total, n_past=n_past, n_lead=n_lead, chunk_off=chunk_off,
                             t_x=t_x, n_sel=n_sel)
    past_specs = [rows(n_past, IDX_DIM), rows(n_past * N_KV_HEADS, HEAD_DIM),
                  rows(n_past * N_KV_HEADS, HEAD_DIM)] if n_past else []
    lead_specs = [_const_spec((BLK, LANES)), _const_spec((BLK, KV_DIM)), _const_spec((BLK, KV_DIM))] if n_lead else []
    return pl.pallas_call(
        kern,
        grid=(b, tq // BLK),
        in_specs=past_specs + lead_specs + [seq(ATT_INNER), seq(IDX_HEADS * IDX_DIM), seq(LANES),
                                            rows(tq, LANES), rows(tq, KV_DIM), rows(tq, KV_DIM),
                                            _const_spec((N_HEADS * N_BIAS_TILES, BLK, BLK))],
        out_specs=seq(ATT_INNER),
        out_shape=jax.ShapeDtypeStruct((b, tq, ATT_INNER), BF16),
        scratch_shapes=[pltpu.VMEM((n_keys, IDX_DIM), BF16),
                        pltpu.VMEM((n_keys, KV_DIM), BF16),
                        pltpu.VMEM((n_keys, KV_DIM), BF16),
                        pltpu.VMEM((nkp_total, KB, BLK), F32),
                        pltpu.VMEM((nkp_total, BLK, KB), F32),
                        pltpu.VMEM((N_HEADS, nkp_total, rq, KB), F32),
                        pltpu.VMEM((IDX_HEADS * BLK, IDX_DIM), BF16),
                        pltpu.VMEM((N_KV_HEADS, rep * rq, HEAD_DIM), BF16),
                        pltpu.VMEM((N_HEADS, rq, BLK), F32),
                        pltpu.VMEM((N_HEADS, rq, BLK), F32),
                        pltpu.VMEM((N_KV_HEADS, rep * rq, HEAD_DIM), F32)],
        compiler_params=_cparams(("parallel", "arbitrary")),
        name="attn",
    )(*(past or ()), *(lead or ()), qn, qi, sm, ki_new, k_new, v_new, bias_tiles)


def _out_ffn_kernel(x_ref, ys_ref, ya_ref, gs_ref, ga_ref, wbs_ref, wba_ref, wo_ref, n2_ref, wg_ref, wu_ref, wd_ref,
                    y_ref):
    dot = functools.partial(jnp.dot, preferred_element_type=F32)
    merged = (jax.nn.sigmoid(gs_ref[...]) * dot(ys_ref[...], wbs_ref[...])
              + jax.nn.sigmoid(ga_ref[...]) * dot(ya_ref[...], wba_ref[...]))
    h = x_ref[...] + dot(merged.astype(BF16), wo_ref[...])
    hn = _rms(h, n2_ref[...]).astype(BF16)
    act = (_silu(dot(hn, wg_ref[...])) * dot(hn, wu_ref[...])).astype(BF16)
    y_ref[...] = h + dot(act, wd_ref[...])


def _out_ffn(x2d, ys, ya, gs, ga, p):
    n = x2d.shape[0]
    tm = _row_tile(n)
    d_ff = p["w_gate"].shape[1]
    row = lambda width: pl.BlockSpec((tm, width), lambda i: (i, 0))
    wspec = lambda shape: pl.BlockSpec(shape, lambda i: (0, 0), pipeline_mode=pl.Buffered(1))
    return pl.pallas_call(
        _out_ffn_kernel,
        grid=(n // tm,),
        in_specs=[row(D_MODEL), row(SSD_INNER), row(ATT_INNER), row(D_MODEL), row(D_MODEL),
                  wspec((SSD_INNER, D_MODEL)), wspec((ATT_INNER, D_MODEL)), wspec((D_MODEL, D_MODEL)),
                  _const_spec((1, D_MODEL)), wspec((D_MODEL, d_ff)), wspec((D_MODEL, d_ff)), wspec((d_ff, D_MODEL))],
        out_specs=row(D_MODEL),
        out_shape=jax.ShapeDtypeStruct((n, D_MODEL), F32),
        compiler_params=_cparams(("parallel",)),
        name="out_ffn",
    )(x2d, ys, ya, gs, ga, p["w_br_ssd"], p["w_br_att"], p["w_out"], p["norm2_w"], p["w_gate"], p["w_up"],
      p["w_down"])


def _layer(x, lead_rows, conv_prev, ssm_prev, past, p, bias_tiles, *, chunk_off, n_sel):
    b, t, _ = x.shape
    n_lead = lead_rows.shape[0]
    tp = -(-t // BLK) * BLK
    x2d = jnp.pad(x, ((0, 0), (0, tp - t), (0, 0))).reshape(b * tp, D_MODEL)
    hist = jnp.pad(conv_prev.astype(F32), ((0, 0), (HALO - (CONV_WIDTH - 1), 0), (0, 0)))
    seq = lambda a: a.reshape(b, tp, a.shape[-1])
    ssd_lead = attn_lead = None
    if n_lead:
        lead = _in_proj(jnp.pad(lead_rows.astype(x.dtype), ((0, BLK - n_lead), (0, 0))), hist[:1], p,
                        n_seq=1, t_x=n_lead)
        lz, lxbc, _, lk32, lv32, lkb, lvb, _, lsm, lsmb, _, _, hist = lead
        ssd_lead, attn_lead = (lxbc, lz, lsm), (lsmb, lkb, lvb)
    z, xbc, qn, k32, v32, kb, vb, qi, sm, smb, gs, ga, conv_new8 = _in_proj(x2d, hist, p, n_seq=b, t_x=t)

    gw = SSD_HEADS // SSD_GROUPS * SSD_HEAD_DIM
    y_ssd, ssm_new = _ssd(ssd_lead, seq(xbc), seq(z), seq(sm),
                          ssm_prev.astype(F32).reshape(b, SSD_GROUPS, gw, SSD_STATE), p, n_lead=n_lead, t_x=t)

    if past is not None:
        pk, pv, pki = past
        n_past = pk.shape[1]
        past = (pki.astype(F32), pk.astype(F32).reshape(b, n_past * N_KV_HEADS, HEAD_DIM),
                pv.astype(F32).reshape(b, n_past * N_KV_HEADS, HEAD_DIM))
    y_att = _attn(seq(qn), seq(qi), seq(sm), seq(smb), seq(kb), seq(vb), past, attn_lead, bias_tiles,
                  n_lead=n_lead, chunk_off=chunk_off, t_x=t, n_sel=n_sel)

    y = _out_ffn(x2d, y_ssd.reshape(b * tp, SSD_INNER), y_att.reshape(b * tp, ATT_INNER), gs, ga, p)

    def with_lead(new, lead_part):
        if not n_lead:
            return new
        return jnp.concatenate([jnp.broadcast_to(lead_part[None, :n_lead], (b, n_lead) + new.shape[2:]), new], axis=1)

    heads = lambda a, rows: a.reshape(-1, rows, N_KV_HEADS, HEAD_DIM)
    k_new = with_lead(heads(k32, tp)[:, :t], heads(lk32, BLK)[0] if n_lead else None)
    v_new = with_lead(heads(v32, tp)[:, :t], heads(lv32, BLK)[0] if n_lead else None)
    ki_new = with_lead(seq(sm)[:, :t, SM_KI:SM_KI + IDX_DIM], lsm[:, SM_KI:SM_KI + IDX_DIM] if n_lead else None)
    ssm_new = ssm_new.reshape(b, SSD_HEADS, SSD_HEAD_DIM, SSD_STATE)
    conv_new = conv_new8[:, HALO - (CONV_WIDTH - 1):]
    return y.reshape(b, tp, D_MODEL)[:, :t], k_new, v_new, ki_new, ssm_new, conv_new


def _prepare_params(l, norm1_w, w_in, conv_w, conv_b, dt_bias, a_log, d_skip, ssd_norm_w, q_norm_w, k_norm_w,
                    idx_k_norm_w, w_br_ssd, w_br_att, w_out, norm2_w, w_gate, w_up, w_down):
    offs = [0]
    for w in IN_WIDTHS:
        offs.append(offs[-1] + w)
    i_z, i_xbc, i_dt, i_q, i_k, i_v, i_qi, i_ki, i_wi, i_gs, i_ga = range(11)
    run = lambda first, last: w_in[l][:, offs[first]:offs[last + 1]].astype(BF16)
    pad = jnp.zeros((D_MODEL, LANES - IDX_DIM - SSD_HEADS - IDX_HEADS), BF16)
    w_perm = jnp.concatenate([run(i_z, i_xbc), run(i_q, i_qi), run(i_gs, i_ga),
                              run(i_ki, i_ki), run(i_dt, i_dt), run(i_wi, i_wi), pad], axis=1)

    def lanes_at(vec, start):
        return jnp.zeros((1, LANES), F32).at[0, start:start + vec.shape[0]].set(vec.astype(F32))

    dtb = lanes_at(dt_bias[l], SM_DT)
    alog = lanes_at(a_log[l], SM_DT)
    head_of_channel = jnp.arange(SSD_INNER) // SSD_HEAD_DIM
    expand = (jnp.arange(LANES)[:, None] == head_of_channel[None, :] + SM_DT).astype(BF16)
    row = lambda v: v.astype(F32).reshape(1, -1)
    return dict(
        norm1_w=row(norm1_w[l]), w_in=w_perm, conv_w=conv_w[l].astype(F32), conv_b=row(conv_b[l]),
        dtb=dtb, dtb_t=dtb.reshape(LANES, 1), alog=alog, alog_t=alog.reshape(LANES, 1),
        dskip_x=row(jnp.repeat(d_skip[l], SSD_HEAD_DIM)), ssd_norm_w=row(ssd_norm_w[l]), expand=expand,
        q_norm_w=row(q_norm_w[l]), k_norm_w=row(k_norm_w[l]),
        idx_k_norm_w=jnp.ones((1, LANES), F32).at[0, SM_KI:SM_KI + IDX_DIM].set(idx_k_norm_w[l].astype(F32)),
        w_br_ssd=w_br_ssd[l].astype(BF16), w_br_att=w_br_att[l].astype(BF16), w_out=w_out[l].astype(BF16),
        norm2_w=row(norm2_w[l]), w_gate=w_gate[l].astype(BF16), w_up=w_up[l].astype(BF16),
        w_down=w_down[l].astype(BF16))


def kernel(x_prompt, x_sample, cache_k, cache_v, cache_kidx, state_ssm, state_conv, meta_tokens, rel_bias, norm1_w,
           w_in, conv_w, conv_b, dt_bias, a_log, d_skip, ssd_norm_w, q_norm_w, k_norm_w, idx_k_norm_w, w_br_ssd,
           w_br_att, w_out, norm2_w, w_gate, w_up, w_down):
    bp, sp, _ = x_prompt.shape
    bs, ts, _ = x_sample.shape
    past = cache_k.shape[2]
    assert w_in.shape[0] == 1
    assert past % BLK == 0 and BLK % CHUNK == 0 and N_META <= CHUNK
    l = 0

    n_sel_p = min(TOPK_MAX, sp // 4)
    n_sel_s = min(TOPK_MAX, (past + ts) // 4)
    conv0 = jnp.zeros((bp, CONV_WIDTH - 1, CONV_DIM), F32)
    ssm0 = jnp.zeros((bp, SSD_HEADS, SSD_HEAD_DIM, SSD_STATE), F32)
    bias_tiles = _bias_tiles(rel_bias.astype(F32))
    p = _prepare_params(l, norm1_w, w_in, conv_w, conv_b, dt_bias, a_log, d_skip, ssd_norm_w, q_norm_w, k_norm_w,
                        idx_k_norm_w, w_br_ssd, w_br_att, w_out, norm2_w, w_gate, w_up, w_down)
    y_prompt, *rest_p = _layer(x_prompt, meta_tokens, conv0, ssm0, None, p, bias_tiles,
                               chunk_off=CHUNK - N_META, n_sel=n_sel_p)
    y_sample, *rest_s = _layer(x_sample, meta_tokens[:0], state_conv[l], state_ssm[l],
                               (cache_k[l], cache_v[l], cache_kidx[l]), p, bias_tiles,
                               chunk_off=0, n_sel=n_sel_s)

    dtypes = (x_prompt.dtype, x_prompt.dtype, x_prompt.dtype, state_ssm.dtype, x_prompt.dtype)
    return (y_prompt, y_sample, *(o[None].astype(dt) for o, dt in zip(rest_p, dtypes)),
            *(o[None].astype(dt) for o, dt in zip(rest_s, dtypes)))
```

```python
import functools

import jax
import jax.numpy as jnp
from jax import lax
from jax.experimental import pallas as pl
from jax.experimental.pallas import tpu as pltpu

F32 = jnp.float32
BF16 = jnp.bfloat16

D_MODEL = 1024
CHUNK = 64
N_META = 16
SSD_HEADS = 16
SSD_HEAD_DIM = 64
SSD_INNER = SSD_HEADS * SSD_HEAD_DIM
SSD_GROUPS = 4
SSD_STATE = 128
CONV_WIDTH = 4
CONV_DIM = SSD_INNER + 2 * SSD_GROUPS * SSD_STATE
N_HEADS = 8
N_KV_HEADS = 2
HEAD_DIM = 128
ATT_INNER = N_HEADS * HEAD_DIM
KV_DIM = N_KV_HEADS * HEAD_DIM
IDX_HEADS = 8
IDX_DIM = 64
TOPK_MAX = 256
REL_BUCKETS = 32
REL_MAX_DIST = 128
IN_WIDTHS = (SSD_INNER, CONV_DIM, SSD_HEADS, ATT_INNER, KV_DIM, KV_DIM, IDX_HEADS * IDX_DIM, IDX_DIM, IDX_HEADS,
             D_MODEL, D_MODEL)
EPS = 1e-6

LANES = 128
SUBLANES = 8
VMEM_LIMIT_BYTES = 56 * 1024 * 1024

BLK = LANES
ROW_TILE = 256
IN_PROJ_PASSES = 2
HALO = SUBLANES
SSD_SEQS_PER_STEP = 4

C_Z = 0
C_XBC = C_Z + SSD_INNER
C_Q = C_XBC + CONV_DIM
C_K = C_Q + ATT_INNER
C_V = C_K + KV_DIM
C_QI = C_V + KV_DIM
C_GS = C_QI + IDX_HEADS * IDX_DIM
C_GA = C_GS + D_MODEL
C_SM = C_GA + D_MODEL
IN_PAD = C_SM + LANES
SM_KI = 0
SM_DT = SM_KI + IDX_DIM
SM_WI = SM_DT + SSD_HEADS

KEY_STEPS_PER_TRIP = 8
SEARCH_STEPS = 16
SEARCH_CLAMP = 1.0 / 16
NEG_BIG = -1e30


def _cparams(sem):
    return pltpu.CompilerParams(dimension_semantics=sem, vmem_limit_bytes=VMEM_LIMIT_BYTES)


def _const_spec(shape):
    nd = len(shape)
    return pl.BlockSpec(shape, lambda *_: (0,) * nd)


def _rms(x, w):
    return x * lax.rsqrt(jnp.mean(x * x, axis=-1, keepdims=True) + EPS) * w


def _silu(x):
    return x * jax.nn.sigmoid(x)


def _in_proj_kernel(x_ref, hist_ref, n1_ref, w_ref, qn_ref, kn_ref, kin_ref, cw_ref, cb_ref,
                    z_ref, xc_ref, q_ref, k_ref, v_ref, kb_ref, vb_ref, qi_ref, sm_ref, smb_ref, gs_ref, ga_ref, tail_ref,
                    xpad_scr, *, tiles_per_seq, t_x):
    j = pl.program_id(1)
    tm = x_ref.shape[0]

    @pl.when(j == 0)
    def _():
        xpad_scr[0:HALO, :] = hist_ref[...]

    sub = ROW_TILE if tm % ROW_TILE == 0 else tm
    for r0 in range(0, tm, sub):
        rows, kv_rows = pl.ds(r0, sub), pl.ds(r0 * N_KV_HEADS, sub * N_KV_HEADS)
        _in_proj_rows(x_ref.at[rows], n1_ref, w_ref, qn_ref, kn_ref, kin_ref, cw_ref, cb_ref,
                      z_ref.at[rows], xc_ref.at[rows], q_ref.at[rows], k_ref.at[kv_rows], v_ref.at[kv_rows],
                      kb_ref.at[rows], vb_ref.at[rows], qi_ref.at[rows], sm_ref.at[rows], smb_ref.at[rows],
                      gs_ref.at[rows], ga_ref.at[rows], xpad_scr.at[pl.ds(r0, sub + 2 * HALO)])

    n_last = t_x - (tiles_per_seq - 1) * tm

    @pl.when(j == tiles_per_seq - 1)
    def _():
        tail_ref[...] = xpad_scr[n_last:n_last + HALO, :]

    xpad_scr[0:HALO, :] = xpad_scr[tm:tm + HALO, :]


def _in_proj_rows(x_ref, n1_ref, w_ref, qn_ref, kn_ref, kin_ref, cw_ref, cb_ref,
                  z_ref, xc_ref, q_ref, k_ref, v_ref, kb_ref, vb_ref, qi_ref, sm_ref, smb_ref, gs_ref, ga_ref, xpad_scr):
    hn = _rms(x_ref[...], n1_ref[...]).astype(BF16)
    tm = x_ref.shape[0]

    def mm(lo, hi):
        return jnp.dot(hn, w_ref[:, lo:hi], preferred_element_type=F32)

    step = 2 * LANES
    conv_slabs = iter(range(0, CONV_DIM, LANES))

    def conv_next():
        c0 = next(conv_slabs, None)
        if c0 is None:
            return
        if c0 % step == 0:
            xpad_scr[HALO:HALO + tm, c0:c0 + step] = mm(C_XBC + c0, C_XBC + c0 + step)
        sl = slice(c0, c0 + LANES)
        first = HALO - (CONV_WIDTH - 1)
        acc = xpad_scr[first:first + tm, sl] * cw_ref[0:1, sl]
        for i in range(1, CONV_WIDTH):
            acc = acc + xpad_scr[first + i:first + i + tm, sl] * cw_ref[i:i + 1, sl]
        xc_ref[:, sl] = _silu(cb_ref[:, sl] + acc)

    def project(lo, hi, out_ref, dtype=F32):
        for c0 in range(0, hi - lo, step):
            out_ref[:, c0:c0 + step] = mm(lo + c0, lo + c0 + step).astype(dtype)
            conv_next()

    project(C_Z, C_XBC, z_ref)
    for c0 in range(0, ATT_INNER, step):
        q = mm(C_Q + c0, C_Q + c0 + step)
        conv_next()
        for h in range(step // HEAD_DIM):
            sl = slice(h * HEAD_DIM, (h + 1) * HEAD_DIM)
            q_ref[:, c0 + h * HEAD_DIM:c0 + (h + 1) * HEAD_DIM] = _rms(q[:, sl], qn_ref[...]).astype(BF16)
    k = mm(C_K, C_V)
    conv_next()
    v = mm(C_V, C_QI)
    conv_next()
    for h in range(N_KV_HEADS):
        sl = slice(h * HEAD_DIM, (h + 1) * HEAD_DIM)
        kh = _rms(k[:, sl], kn_ref[...])
        k_ref[pl.ds(h, tm, stride=N_KV_HEADS), :] = kh
        kb_ref[:, sl] = kh.astype(BF16)
        v_ref[pl.ds(h, tm, stride=N_KV_HEADS), :] = v[:, sl]
    vb_ref[...] = v.astype(BF16)
    project(C_QI, C_GS, qi_ref, BF16)
    project(C_GS, C_GA, gs_ref)
    project(C_GA, C_SM, ga_ref)
    sm = mm(C_SM, IN_PAD)
    assert next(conv_slabs, None) is None
    lane = lax.broadcasted_iota(jnp.int32, sm.shape, 1)
    is_ki = lane < SM_KI + IDX_DIM
    ms = jnp.sum(jnp.where(is_ki, sm * sm, 0.0), axis=-1, keepdims=True) * (1.0 / IDX_DIM)
    ki = sm * lax.rsqrt(ms + EPS) * kin_ref[...]
    is_wi = (lane >= SM_WI) & (lane < SM_WI + IDX_HEADS)
    out = jnp.where(is_ki, ki, jnp.where(is_wi, sm * (IDX_HEADS ** -0.5), sm))
    sm_ref[...] = out
    smb_ref[...] = out.astype(BF16)


def _row_tile(n):
    return ROW_TILE if n % ROW_TILE == 0 else BLK


def _in_proj(x2d, hist, p, *, n_seq, t_x):
    n = x2d.shape[0]
    tp = n // n_seq
    tm = _row_tile(tp)
    if tp % (IN_PROJ_PASSES * tm) == 0:
        tm *= IN_PROJ_PASSES
    tiles = tp // tm
    shared_hist = hist.shape[0] == 1
    row = lambda width, per_token=1: pl.BlockSpec((tm * per_token, width), lambda s, j: (s * tiles + j, 0))
    per_seq = pl.BlockSpec((None, HALO, CONV_DIM), lambda s, j: (0 if shared_hist else s, 0, 0))
    outs = [
        (SSD_INNER, F32, 1), (CONV_DIM, F32, 1), (ATT_INNER, BF16, 1), (HEAD_DIM, F32, N_KV_HEADS),
        (HEAD_DIM, F32, N_KV_HEADS), (KV_DIM, BF16, 1), (KV_DIM, BF16, 1), (IDX_HEADS * IDX_DIM, BF16, 1),
        (LANES, F32, 1), (LANES, BF16, 1), (D_MODEL, F32, 1), (D_MODEL, F32, 1),
    ]
    kern = functools.partial(_in_proj_kernel, tiles_per_seq=tiles, t_x=t_x)
    return pl.pallas_call(
        kern,
        grid=(n_seq, tiles),
        in_specs=[row(D_MODEL), per_seq, _const_spec((1, D_MODEL)),
                  pl.BlockSpec((D_MODEL, IN_PAD), lambda s, j: (0, 0), pipeline_mode=pl.Buffered(1)),
                  _const_spec((1, HEAD_DIM)), _const_spec((1, HEAD_DIM)), _const_spec((1, LANES)),
                  _const_spec((CONV_WIDTH, CONV_DIM)), _const_spec((1, CONV_DIM))],
        out_specs=[row(w, per) for w, _, per in outs]
        + [pl.BlockSpec((None, HALO, CONV_DIM), lambda s, j: (s, 0, 0))],
        out_shape=[jax.ShapeDtypeStruct((n * per, w), dt) for w, dt, per in outs]
        + [jax.ShapeDtypeStruct((n_seq, HALO, CONV_DIM), F32)],
        scratch_shapes=[pltpu.VMEM((tm + 2 * HALO, CONV_DIM), F32)],
        compiler_params=_cparams(("parallel", "arbitrary")),
        name="in_proj",
    )(x2d, hist, p["norm1_w"], p["w_in"], p["q_norm_w"], p["k_norm_w"], p["idx_k_norm_w"], p["conv_w"], p["conv_b"])


def _softplus(x):
    return jnp.maximum(x, 0.0) + jnp.log1p(jnp.exp(-jnp.abs(x)))


def _split3(x):
    hi = x.astype(BF16)
    r1 = x - hi.astype(F32)
    mid = r1.astype(BF16)
    lo = (r1 - mid.astype(F32)).astype(BF16)
    return hi, mid, lo


def _ssd_kernel(*refs, n_chunks, n_lead, t_x):
    if n_lead:
        xbc_lead_ref, z_lead_ref, sm_lead_ref, *refs = refs
    (xbc_ref, z_ref, sm_ref, sprev_ref, dtb_ref, dtbt_ref, alog_ref, alogt_ref,
     dsk_ref, nw_ref, exp_ref, y_ref, snew_ref, s_scr, y_scr) = refs
    c = pl.program_id(1)
    lead_chunks = 1 if n_lead else 0
    seqs = range(xbc_ref.shape[0])

    @pl.when(c == 0)
    def _():
        for k in seqs:
            for g in range(SSD_GROUPS):
                s_scr[k, g] = sprev_ref[k, g].T

    def chunk(k, xc_ref, z_src_ref, sm_src_ref, n_valid):
        _ssd_chunk(xc_ref, z_src_ref, sm_src_ref, n_valid, dtb_ref, dtbt_ref, alog_ref, alogt_ref, dsk_ref, nw_ref,
                   exp_ref, y_ref.at[k], s_scr.at[k], y_scr.at[k])

    def x_chunks():
        for k in seqs:
            chunk(k, xbc_ref.at[k], z_ref.at[k], sm_ref.at[k], jnp.minimum(BLK, t_x - (c - lead_chunks) * BLK))

    if n_lead:
        @pl.when(c == 0)
        def _():
            for k in seqs:
                chunk(k, xbc_lead_ref, z_lead_ref, sm_lead_ref, n_lead)

        pl.when(c > 0)(x_chunks)
    else:
        x_chunks()

    @pl.when(c == n_chunks - 1)
    def _():
        for k in seqs:
            for g in range(SSD_GROUPS):
                snew_ref[k, g] = s_scr[k, g].T


def _ssd_chunk(xc_ref, z_ref, sm_ref, n_valid, dtb_ref, dtbt_ref, alog_ref, alogt_ref, dsk_ref, nw_ref, exp_ref,
               y_ref, s_scr, y_scr):
    q = BLK
    gw = SSD_HEADS // SSD_GROUPS * SSD_HEAD_DIM
    sm = sm_ref[...]

    heads = slice(SM_DT, SM_DT + SSD_HEADS)
    smt = sm.T[heads, :]
    lane = lax.broadcasted_iota(jnp.int32, (q, LANES), 1)
    row = lax.broadcasted_iota(jnp.int32, (q, LANES), 0)
    is_dt = (lane >= SM_DT) & (lane < SM_DT + SSD_HEADS) & (row < n_valid)
    time_t = lax.broadcasted_iota(jnp.int32, (SSD_HEADS, q), 1)
    dt = jnp.where(is_dt, _softplus(sm + dtb_ref[...]), 0.0)
    dtt = jnp.where(time_t < n_valid, _softplus(smt + dtbt_ref[heads, :]), 0.0)
    da = dt * (-jnp.exp(alog_ref[...]))
    dat = dtt * (-jnp.exp(alogt_ref[heads, :]))
    ii = lax.broadcasted_iota(jnp.int32, (q, q), 0)
    jj = lax.broadcasted_iota(jnp.int32, (q, q), 1)
    causal = jj <= ii
    acum = sum(jnp.dot(causal.astype(BF16), p, preferred_element_type=F32) for p in _split3(da))
    acumt = sum(jnp.dot(p, (ii <= jj).astype(BF16), preferred_element_type=F32) for p in _split3(dat))
    a_last = acum[q - 1:q, :]
    expand = exp_ref[...]
    stacked = jnp.concatenate([jnp.exp(acum), jnp.exp(a_last - acum) * dt,
                               jnp.broadcast_to(jnp.exp(a_last), (SUBLANES, LANES))], axis=0)
    stacked_x = sum(jnp.dot(p, expand, preferred_element_type=F32) for p in _split3(stacked))
    ea_x = stacked_x[0:q]
    wdt_x = stacked_x[q:2 * q]
    dec_x = stacked_x[2 * q:2 * q + 1]

    sq = None
    for g in range(SSD_GROUPS):
        gsl = slice(g * gw, (g + 1) * gw)
        bsl = slice(SSD_INNER + g * SSD_STATE, SSD_INNER + (g + 1) * SSD_STATE)
        csl = slice(SSD_INNER + SSD_GROUPS * SSD_STATE + g * SSD_STATE,
                    SSD_INNER + SSD_GROUPS * SSD_STATE + (g + 1) * SSD_STATE)
        bmf = xc_ref[:, bsl]
        bm = bmf.astype(BF16)
        cm = xc_ref[:, csl].astype(BF16)
        xg = xc_ref[:, gsl]
        xgb = xg.astype(BF16)
        cbm = lax.dot_general(cm, bm, (((1,), (1,)), ((), ())), preferred_element_type=F32)
        xw = (xg * wdt_x[:, gsl]).astype(BF16)
        st = jnp.dot(bmf.T.astype(BF16), xw, preferred_element_type=F32)
        s_in = s_scr[g]
        y_off = jnp.dot(cm, s_in.astype(BF16), preferred_element_type=F32) * ea_x[:, gsl]
        s_scr[g] = s_in * dec_x[:, gsl] + st
        first_of_pair = lax.broadcasted_iota(jnp.int32, (q, LANES), 1) < SSD_HEAD_DIM
        for rp in range(gw // LANES):
            psl = slice(rp * LANES, (rp + 1) * LANES)
            osl = slice(g * gw + rp * LANES, g * gw + (rp + 1) * LANES)
            pair = []
            for h in (g * (SSD_HEADS // SSD_GROUPS) + 2 * rp, g * (SSD_HEADS // SSD_GROUPS) + 2 * rp + 1):
                seg = acum[:, SM_DT + h:SM_DT + h + 1] - acumt[h:h + 1, :]
                lmat = jnp.exp(jnp.where(causal, seg, -jnp.inf))
                wmat = (cbm * lmat * dtt[h:h + 1, :]).astype(BF16)
                pair.append(jnp.dot(wmat, xgb[:, psl], preferred_element_type=F32))
            y_diag = jnp.where(first_of_pair, pair[0], pair[1])
            yp = (y_diag + y_off[:, psl] + xg[:, psl] * dsk_ref[:, osl]) * _silu(z_ref[:, osl])
            y_scr[:, osl] = yp
            sq = yp * yp if sq is None else sq + yp * yp

    ms = jnp.sum(sq, axis=-1, keepdims=True) * (1.0 / SSD_INNER)
    y_ref[...] = (y_scr[...] * lax.rsqrt(ms + EPS) * nw_ref[...]).astype(BF16)


def _ssd(lead, xbc, z, sm, ssm_prev, p, *, n_lead, t_x):
    b, tp, _ = xbc.shape
    lead_chunks = 1 if n_lead else 0
    assert n_lead % SUBLANES == 0 and n_lead <= BLK and 2 * SSD_HEAD_DIM == LANES
    nc = tp // BLK + lead_chunks
    gw = SSD_HEADS // SSD_GROUPS * SSD_HEAD_DIM
    bb = SSD_SEQS_PER_STEP if b % SSD_SEQS_PER_STEP == 0 else 1
    seq = lambda width: pl.BlockSpec((bb, BLK, width), lambda i, c: (i, jnp.maximum(c - lead_chunks, 0), 0))
    state = pl.BlockSpec((bb, SSD_GROUPS, gw, SSD_STATE), lambda i, c: (i, 0, 0, 0))
    kern = functools.partial(_ssd_kernel, n_chunks=nc, n_lead=n_lead, t_x=t_x)
    lead_specs = [_const_spec((BLK, CONV_DIM)), _const_spec((BLK, SSD_INNER)), _const_spec((BLK, LANES))]
    return pl.pallas_call(
        kern,
        grid=(b // bb, nc),
        in_specs=(lead_specs if n_lead else []) + [
                  seq(CONV_DIM), seq(SSD_INNER), seq(LANES), state,
                  _const_spec((1, LANES)), _const_spec((LANES, 1)), _const_spec((1, LANES)), _const_spec((LANES, 1)),
                  _const_spec((1, SSD_INNER)), _const_spec((1, SSD_INNER)), _const_spec((LANES, SSD_INNER))],
        out_specs=[seq(SSD_INNER), state],
        out_shape=[jax.ShapeDtypeStruct((b, tp, SSD_INNER), BF16),
                   jax.ShapeDtypeStruct((b, SSD_GROUPS, gw, SSD_STATE), F32)],
        scratch_shapes=[pltpu.VMEM((bb, SSD_GROUPS, SSD_STATE, gw), F32),
                        pltpu.VMEM((bb, BLK, SSD_INNER), F32)],
        compiler_params=_cparams(("parallel", "arbitrary")),
        name="ssd",
    )(*(lead if n_lead else ()), xbc, z, sm, ssm_prev, p["dtb"], p["dtb_t"],
      p["alog"], p["alog_t"], p["dskip_x"], p["ssd_norm_w"], p["expand"])


N_SHIFT_TILES = 5
LEAD_TILE = N_SHIFT_TILES
N_BIAS_TILES = N_SHIFT_TILES + 1


def _log_bucket_starts():
    nb = REL_BUCKETS // 2
    max_exact = nb // 2
    s = nb - max_exact
    starts = []
    for m in range(1, s):
        n = max_exact
        while n ** s * max_exact ** m < max_exact ** s * REL_MAX_DIST ** m:
            n += 1
        starts.append(n)
    return starts


def _bias_kernel(rb_ref, bt_ref):
    nb = REL_BUCKETS // 2
    max_exact = nb // 2
    qq = lax.broadcasted_iota(jnp.int32, (BLK, BLK), 0)
    kk = lax.broadcasted_iota(jnp.int32, (BLK, BLK), 1)
    for u in range(N_BIAS_TILES):
        rel = kk - qq + ((u - 2) * BLK if u < N_SHIFT_TILES else -N_META)
        n = jnp.abs(rel)
        large = max_exact + sum(jnp.where(n >= start, 1, 0) for start in _log_bucket_starts())
        bucket = jnp.where(rel > 0, nb, 0) + jnp.where(n < max_exact, n, large)
        for h in range(N_HEADS):
            acc = jnp.zeros((BLK, BLK), F32)
            for bkt in range(REL_BUCKETS):
                acc = jnp.where(bucket == bkt, rb_ref[bkt, h], acc)
            bt_ref[h * N_BIAS_TILES + u] = acc


def _bias_tiles(rel_bias):
    return pl.pallas_call(
        _bias_kernel,
        in_specs=[pl.BlockSpec(memory_space=pltpu.SMEM)],
        out_specs=pl.BlockSpec(memory_space=pltpu.VMEM),
        out_shape=jax.ShapeDtypeStruct((N_HEADS * N_BIAS_TILES, BLK, BLK), F32),
        name="bias_tiles",
    )(rel_bias)


KB = 2 * BLK


def _attn_kernel(*refs, nkp_total, n_past, n_lead, chunk_off, t_x, n_sel):
    if n_past:
        pki_ref, pk_ref, pv_ref, *refs = refs
    if n_lead:
        lki_ref, lk_ref, lv_ref, *refs = refs
    (qn_ref, qi_ref, sm_ref, ki_new_ref, k_new_ref, v_new_ref, bt_ref, o_ref,
     ki_scr, k_scr, v_scr, st_scr, m_scr, lg_scr, qis_scr, qs_scr, mrun_scr, lrun_scr, acc_scr) = refs
    i = pl.program_id(1)
    r = BLK
    n_prefix = n_past + (BLK if n_lead else 0)
    gap = BLK - n_lead if n_lead else 0
    l_valid = n_prefix - gap + t_x

    @pl.when(i == 0)
    def _():
        if n_lead:
            ki_scr[0:BLK, :] = lki_ref[:, SM_KI:SM_KI + IDX_DIM]
            k_scr[0:BLK, :] = lk_ref[...]
            v_scr[0:BLK, :] = lv_ref[...]
        if n_past:
            def load_past(c, carry):
                rows = pl.ds(pl.multiple_of(c * KB, KB), KB)
                ki_scr[rows, :] = pki_ref[rows, :].astype(BF16)
                for g in range(N_KV_HEADS):
                    src = pl.ds(pl.multiple_of(c * KB * N_KV_HEADS, KB) + g, KB, stride=N_KV_HEADS)
                    k_scr[rows, g * HEAD_DIM:(g + 1) * HEAD_DIM] = pk_ref[src, :].astype(BF16)
                    v_scr[rows, g * HEAD_DIM:(g + 1) * HEAD_DIM] = pv_ref[src, :].astype(BF16)
                return carry

            lax.fori_loop(0, n_past // KB, load_past, 0)
        n_new = k_new_ref.shape[0]
        ki_scr[n_prefix:n_prefix + n_new, :] = ki_new_ref[:, SM_KI:SM_KI + IDX_DIM]
        k_scr[n_prefix:n_prefix + n_new, :] = k_new_ref[...]
        v_scr[n_prefix:n_prefix + n_new, :] = v_new_ref[...]
        n_tail = nkp_total * KB - n_prefix - n_new
        if n_tail:
            for scr in (ki_scr, k_scr, v_scr):
                scr[n_prefix + n_new:, :] = jnp.zeros((n_tail, scr.shape[1]), BF16)

    qb = n_prefix // BLK + i
    q0 = qb * BLK - gap
    n_keys = nkp_total * KB

    def chunk_end(pos):
        return jnp.minimum(CHUNK * ((pos + chunk_off) // CHUNK + 1) - chunk_off, l_valid)

    nkp = jnp.minimum(nkp_total, (chunk_end(q0 + BLK - 1) + gap + KB - 1) // KB)

    nt = (((1,), (1,)), ((), ()))
    wit = sm_ref[...].T
    n_adm = chunk_end(q0 + lax.broadcasted_iota(jnp.int32, (1, r), 1))
    krow = lax.broadcasted_iota(jnp.int32, (KB, r), 0)

    fold_rows = 8 * SUBLANES

    def fold(x, op):
        return op(x.reshape(KB // fold_rows, fold_rows, r), axis=0)

    def for_key_steps(body, init):
        def several(n, first, c):
            for s in range(n):
                c = body(first + s, c)
            return c

        trip = KEY_STEPS_PER_TRIP
        carry = lax.fori_loop(0, nkp // trip, lambda t, c: several(trip, trip * t, c), init)
        done = nkp - nkp % trip
        n = trip // 2
        while n >= 1:
            take = nkp % (2 * n) >= n
            carry = lax.cond(take, lambda c, n=n, done=done: several(n, done, c), lambda c: c, carry)
            done = done + jnp.where(take, n, 0)
            n //= 2
        return carry

    for h in range(IDX_HEADS):
        qis_scr[h * r:(h + 1) * r, :] = qi_ref[:, h * IDX_DIM:(h + 1) * IDX_DIM]

    def score_body(jp, carry):
        mn, mx = carry
        kij = ki_scr[pl.ds(pl.multiple_of(jp * KB, KB), KB), :]
        acc = jnp.zeros((KB, r), F32)
        for hp in range(IDX_HEADS // 2):
            sh = lax.dot_general(kij, qis_scr[2 * hp * r:2 * (hp + 1) * r, :], nt, preferred_element_type=F32)
            for e in range(2):
                h = 2 * hp + e
                acc = acc + wit[SM_WI + h:SM_WI + h + 1, :] * jnp.maximum(sh[:, e * r:(e + 1) * r], 0.0)
        adm = krow < n_adm + gap - jp * KB
        if gap:
            adm = adm & ((jp > 0) | (krow < n_lead) | (krow >= BLK))
        s = jnp.where(adm, acc * (IDX_DIM ** -0.5), -jnp.inf)
        st_scr[jp] = s
        mn = jnp.minimum(mn, fold(jnp.where(adm, s, jnp.inf), jnp.min))
        mx = jnp.maximum(mx, fold(s, jnp.max))
        return mn, mx

    init = (jnp.full((fold_rows, r), jnp.inf, F32), jnp.full((fold_rows, r), -jnp.inf, F32))
    mn, mx = for_key_steps(score_body, init)
    lo0 = jnp.min(mn, axis=0, keepdims=True)
    hi0 = jnp.max(mx, axis=0, keepdims=True)
    kk = jnp.minimum(n_adm, n_sel).astype(F32)

    def count(pred):
        def body(jp, acc):
            return acc + fold(jnp.where(pred(st_scr[jp], jp), 1.0, 0.0), jnp.sum)
        acc = lax.fori_loop(0, nkp, body, jnp.zeros((fold_rows, r), F32))
        return jnp.sum(acc, axis=0, keepdims=True)

    def search_body(_, carry):
        lo, hi, c_lo, c_hi, ub, c_ub, hit = carry
        frac = jnp.clip((c_lo - kk) / (c_lo - c_hi), SEARCH_CLAMP, 1.0 - SEARCH_CLAMP)
        mid = jnp.where(hit > 0.0, lo, lo * (1.0 - frac) + hi * frac)
        cnt = count(lambda s, jp: s >= mid)
        ok = cnt >= kk
        now = cnt == kk
        lo = jnp.where(ok, mid, lo)
        hi = jnp.where(ok & ~now, hi, mid)
        c_lo = jnp.where(ok, cnt, c_lo)
        c_hi = jnp.where(now, kk - 1.0, jnp.where(ok, c_hi, cnt))
        ub = jnp.where(ok, ub, mid)
        c_ub = jnp.where(ok, c_ub, cnt)
        return lo, hi, c_lo, c_hi, ub, c_ub, jnp.where(now, 1.0, 0.0)

    all_adm = n_adm.astype(F32) == kk
    zeros = jnp.zeros((1, r), F32)
    init = (lo0, jnp.where(all_adm, lo0, hi0), n_adm.astype(F32), jnp.where(all_adm, kk - 1.0, zeros),
            jnp.full((1, r), jnp.inf, F32), zeros, jnp.where(all_adm, 1.0, 0.0))
    lo_f, _, _, _, ub, c_ub, hit_f = lax.fori_loop(0, SEARCH_STEPS, search_body, init)
    hit = hit_f > 0.0
    pending = jnp.sum(jnp.where(hit, 0, 1))
    take_all = jnp.full((1, r), float(n_keys), F32)

    def exact_path():
        def next_below(ub):
            def body(jp, acc):
                s = st_scr[jp]
                return jnp.maximum(acc, fold(jnp.where(s < ub, s, -jnp.inf), jnp.max))
            acc = lax.fori_loop(0, nkp, body, jnp.full((fold_rows, r), -jnp.inf, F32))
            return jnp.max(acc, axis=0, keepdims=True)

        def descend_cond(carry):
            *_, todo, it = carry
            return (todo > 0) & (it < n_keys)

        def descend_body(carry):
            ub, c_ub, _, _, _, it = carry
            t = next_below(ub)
            c_t = count(lambda s, jp: s >= t)
            done = hit | (c_t >= kk)
            return (jnp.where(done, ub, t), jnp.where(done, c_ub, c_t), t, c_t,
                    jnp.sum(jnp.where(done, 0, 1)), it + 1)

        _, c_gt, t, c_t, _, _ = lax.while_loop(descend_cond, descend_body,
                                               (ub, c_ub, lo0, zeros, jnp.int32(1), jnp.int32(0)))
        ties_wanted = jnp.where(hit, take_all, kk - c_gt)
        extra_ties = jnp.sum(jnp.where(~hit & (c_t - c_gt > ties_wanted), 1, 0))
        return jnp.where(hit, lo_f, t), ties_wanted, extra_ties

    thr, ties_wanted, extra_ties = lax.cond(pending > 0, exact_path, lambda: (lo_f, take_all, jnp.int32(0)))

    def mask_with_ties():
        ii = lax.broadcasted_iota(jnp.int32, (KB, KB), 0)
        jj = lax.broadcasted_iota(jnp.int32, (KB, KB), 1)
        upto = (jj <= ii).astype(BF16)

        def body(jp, wanted):
            s = st_scr[jp]
            tied = s == thr
            rank = jnp.dot(upto, jnp.where(tied, 1.0, 0.0).astype(BF16), preferred_element_type=F32)
            m_scr[jp] = jnp.where((s > thr) | (tied & (rank <= wanted)), 0.0, NEG_BIG).T
            return wanted - rank[KB - 1:KB, :]

        for_key_steps(body, ties_wanted)

    scale = HEAD_DIM ** -0.5
    rep = N_HEADS // N_KV_HEADS
    rq = mrun_scr.shape[1]
    for h in range(N_HEADS):
        qs_scr[h // rep, (h % rep) * rq:(h % rep + 1) * rq, :] = qn_ref[0:rq, h * HEAD_DIM:(h + 1) * HEAD_DIM]
    mrun_scr[...] = jnp.full(mrun_scr.shape, -jnp.inf, F32)
    lrun_scr[...] = jnp.zeros(lrun_scr.shape, F32)
    acc_scr[...] = jnp.zeros(acc_scr.shape, F32)

    def logit_body(mask_of, jp, carry):
        u0 = jnp.clip(2 * jp - qb + 2, 0, N_SHIFT_TILES - 1)
        u1 = jnp.clip(2 * jp + 1 - qb + 2, 0, N_SHIFT_TILES - 1)
        if gap:
            u0 = jnp.where((jp == 0) & (qb == 1), LEAD_TILE, u0)
        madd = mask_of(jp)[0:rq, :]
        for g in range(N_KV_HEADS):
            lt = lax.dot_general(qs_scr[g], k_scr[pl.ds(pl.multiple_of(jp * KB, KB), KB),
                                                  g * HEAD_DIM:(g + 1) * HEAD_DIM], nt,
                                 preferred_element_type=F32)
            for e in range(rep):
                h = g * rep + e
                bias = jnp.concatenate([bt_ref[h * N_BIAS_TILES + u0][0:rq, :],
                                        bt_ref[h * N_BIAS_TILES + u1][0:rq, :]], axis=1)
                lg = lt[e * rq:(e + 1) * rq, :] * scale + bias + madd
                lg_scr[h, jp] = lg
                mrun_scr[h] = jnp.maximum(mrun_scr[h], jnp.maximum(lg[:, :BLK], lg[:, BLK:]))
        return carry

    def logits_with_ties():
        mask_with_ties()
        return for_key_steps(functools.partial(logit_body, lambda jp: m_scr[jp]), jnp.int32(0))

    def logits_plain():
        inline_mask = lambda jp: jnp.where(st_scr[jp] >= thr, 0.0, NEG_BIG).T
        return for_key_steps(functools.partial(logit_body, inline_mask), jnp.int32(0))

    lax.cond(extra_ties > 0, logits_with_ties, logits_plain)
    for h in range(N_HEADS):
        mrun_scr[h] = jnp.broadcast_to(jnp.max(mrun_scr[h], axis=1, keepdims=True), (rq, BLK))

    def pv_body(jp, carry):
        for g in range(N_KV_HEADS):
            es = []
            for e in range(rep):
                h = g * rep + e
                mrow = mrun_scr[h]
                ex = jnp.exp(lg_scr[h, jp] - jnp.concatenate([mrow, mrow], axis=1))
                lrun_scr[h] = lrun_scr[h] + (ex[:, :BLK] + ex[:, BLK:])
                es.append(ex.astype(BF16))
            acc_scr[g] = acc_scr[g] + jnp.dot(jnp.concatenate(es, axis=0),
                                              v_scr[pl.ds(pl.multiple_of(jp * KB, KB), KB),
                                                    g * HEAD_DIM:(g + 1) * HEAD_DIM],
                                              preferred_element_type=F32)
        return carry

    for_key_steps(pv_body, 0)
    for h in range(N_HEADS):
        g, e = h // rep, h % rep
        den = jnp.sum(lrun_scr[h], axis=1, keepdims=True)
        o_ref[0:rq, h * HEAD_DIM:(h + 1) * HEAD_DIM] = (acc_scr[g, e * rq:(e + 1) * rq, :] / den).astype(BF16)
    if rq < r:
        o_ref[rq:, :] = jnp.zeros((r - rq, ATT_INNER), BF16)


def _attn(qn, qi, sm, ki_new, k_new, v_new, past, lead, bias_tiles, *, n_lead, chunk_off, t_x, n_sel):
    b, tq, _ = qn.shape
    n_past = past[0].shape[1] if past is not None else 0
    assert n_past % KB == 0 and not (n_past and n_lead)
    n_prefix = n_past + (BLK if n_lead else 0)
    n_keys = -(-(n_prefix + tq) // KB) * KB
    nkp_total = n_keys // KB
    rep = N_HEADS // N_KV_HEADS
    rq = BLK if tq > BLK else min(BLK, -(-t_x // (2 * SUBLANES)) * 2 * SUBLANES)
    seq = lambda width: pl.BlockSpec((None, BLK, width), lambda bi, i: (bi, i, 0))
    rows = lambda n, width: pl.BlockSpec((None, n, width), lambda bi, i: (bi, 0, 0))
    kern = functools.partial(_attn_kernel, nkp_total=nkp_total, n_past=n_past, n_lead=n_lead, chunk_off=chunk_off,
                             t_x=t_x, n_sel=n_sel)
    past_specs = [rows(n_past, IDX_DIM), rows(n_past * N_KV_HEADS, HEAD_DIM),
                  rows(n_past * N_KV_HEADS, HEAD_DIM)] if n_past else []
    lead_specs = [_const_spec((BLK, LANES)), _const_spec((BLK, KV_DIM)), _const_spec((BLK, KV_DIM))] if n_lead else []
    return pl.pallas_call(
        kern,
        grid=(b, tq // BLK),
        in_specs=past_specs + lead_specs + [seq(ATT_INNER), seq(IDX_HEADS * IDX_DIM), seq(LANES),
                                            rows(tq, LANES), rows(tq, KV_DIM), rows(tq, KV_DIM),
                                            _const_spec((N_HEADS * N_BIAS_TILES, BLK, BLK))],
        out_specs=seq(ATT_INNER),
        out_shape=jax.ShapeDtypeStruct((b, tq, ATT_INNER), BF16),
        scratch_shapes=[pltpu.VMEM((n_keys, IDX_DIM), BF16),
                        pltpu.VMEM((n_keys, KV_DIM), BF16),
                        pltpu.VMEM((n_keys, KV_DIM), BF16),
                        pltpu.VMEM((nkp_total, KB, BLK), F32),
                        pltpu.VMEM((nkp_total, BLK, KB), F32),
                        pltpu.VMEM((N_HEADS, nkp_total, rq, KB), F32),
                        pltpu.VMEM((IDX_HEADS * BLK, IDX_DIM), BF16),
                        pltpu.VMEM((N_KV_HEADS, rep * rq, HEAD_DIM), BF16),
                        pltpu.VMEM((N_HEADS, rq, BLK), F32),
                        pltpu.VMEM((N_HEADS, rq, BLK), F32),
                        pltpu.VMEM((N_KV_HEADS, rep * rq, HEAD_DIM), F32)],
        compiler_params=_cparams(("parallel", "arbitrary")),
        name="attn",
    )(*(past or ()), *(lead or ()), qn, qi, sm, ki_new, k_new, v_new, bias_tiles)


def _out_ffn_kernel(x_ref, ys_ref, ya_ref, gs_ref, ga_ref, wbs_ref, wba_ref, wo_ref, n2_ref, wg_ref, wu_ref, wd_ref,
                    y_ref):
    dot = functools.partial(jnp.dot, preferred_element_type=F32)
    merged = (jax.nn.sigmoid(gs_ref[...]) * dot(ys_ref[...], wbs_ref[...])
              + jax.nn.sigmoid(ga_ref[...]) * dot(ya_ref[...], wba_ref[...]))
    h = x_ref[...] + dot(merged.astype(BF16), wo_ref[...])
    hn = _rms(h, n2_ref[...]).astype(BF16)
    act = (_silu(dot(hn, wg_ref[...])) * dot(hn, wu_ref[...])).astype(BF16)
    y_ref[...] = h + dot(act, wd_ref[...])


def _out_ffn(x2d, ys, ya, gs, ga, p):
    n = x2d.shape[0]
    tm = _row_tile(n)
    d_ff = p["w_gate"].shape[1]
    row = lambda width: pl.BlockSpec((tm, width), lambda i: (i, 0))
    wspec = lambda shape: pl.BlockSpec(shape, lambda i: (0, 0), pipeline_mode=pl.Buffered(1))
    return pl.pallas_call(
        _out_ffn_kernel,
        grid=(n // tm,),
        in_specs=[row(D_MODEL), row(SSD_INNER), row(ATT_INNER), row(D_MODEL), row(D_MODEL),
                  wspec((SSD_INNER, D_MODEL)), wspec((ATT_INNER, D_MODEL)), wspec((D_MODEL, D_MODEL)),
                  _const_spec((1, D_MODEL)), wspec((D_MODEL, d_ff)), wspec((D_MODEL, d_ff)), wspec((d_ff, D_MODEL))],
        out_specs=row(D_MODEL),
        out_shape=jax.ShapeDtypeStruct((n, D_MODEL), F32),
        compiler_params=_cparams(("parallel",)),
        name="out_ffn",
    )(x2d, ys, ya, gs, ga, p["w_br_ssd"], p["w_br_att"], p["w_out"], p["norm2_w"], p["w_gate"], p["w_up"],
      p["w_down"])


def _layer(x, lead_rows, conv_prev, ssm_prev, past, p, bias_tiles, *, chunk_off, n_sel):
    b, t, _ = x.shape
    n_lead = lead_rows.shape[0]
    tp = -(-t // BLK) * BLK
    x2d = jnp.pad(x, ((0, 0), (0, tp - t), (0, 0))).reshape(b * tp, D_MODEL)
    hist = jnp.pad(conv_prev.astype(F32), ((0, 0), (HALO - (CONV_WIDTH - 1), 0), (0, 0)))
    seq = lambda a: a.reshape(b, tp, a.shape[-1])
    ssd_lead = attn_lead = None
    if n_lead:
        lead = _in_proj(jnp.pad(lead_rows.astype(x.dtype), ((0, BLK - n_lead), (0, 0))), hist[:1], p,
                        n_seq=1, t_x=n_lead)
        lz, lxbc, _, lk32, lv32, lkb, lvb, _, lsm, lsmb, _, _, hist = lead
        ssd_lead, attn_lead = (lxbc, lz, lsm), (lsmb, lkb, lvb)
    z, xbc, qn, k32, v32, kb, vb, qi, sm, smb, gs, ga, conv_new8 = _in_proj(x2d, hist, p, n_seq=b, t_x=t)

    gw = SSD_HEADS // SSD_GROUPS * SSD_HEAD_DIM
    y_ssd, ssm_new = _ssd(ssd_lead, seq(xbc), seq(z), seq(sm),
                          ssm_prev.astype(F32).reshape(b, SSD_GROUPS, gw, SSD_STATE), p, n_lead=n_lead, t_x=t)

    if past is not None:
        pk, pv, pki = past
        n_past = pk.shape[1]
        past = (pki.astype(F32), pk.astype(F32).reshape(b, n_past * N_KV_HEADS, HEAD_DIM),
                pv.astype(F32).reshape(b, n_past * N_KV_HEADS, HEAD_DIM))
    y_att = _attn(seq(qn), seq(qi), seq(sm), seq(smb), seq(kb), seq(vb), past, attn_lead, bias_tiles,
                  n_lead=n_lead, chunk_off=chunk_off, t_x=t, n_sel=n_sel)

    y = _out_ffn(x2d, y_ssd.reshape(b * tp, SSD_INNER), y_att.reshape(b * tp, ATT_INNER), gs, ga, p)

    def with_lead(new, lead_part):
        if not n_lead:
            return new
        return jnp.concatenate([jnp.broadcast_to(lead_part[None, :n_lead], (b, n_lead) + new.shape[2:]), new], axis=1)

    heads = lambda a, rows: a.reshape(-1, rows, N_KV_HEADS, HEAD_DIM)
    k_new = with_lead(heads(k32, tp)[:, :t], heads(lk32, BLK)[0] if n_lead else None)
    v_new = with_lead(heads(v32, tp)[:, :t], heads(lv32, BLK)[0] if n_lead else None)
    ki_new = with_lead(seq(sm)[:, :t, SM_KI:SM_KI + IDX_DIM], lsm[:, SM_KI:SM_KI + IDX_DIM] if n_lead else None)
    ssm_new = ssm_new.reshape(b, SSD_HEADS, SSD_HEAD_DIM, SSD_STATE)
    conv_new = conv_new8[:, HALO - (CONV_WIDTH - 1):]
    return y.reshape(b, tp, D_MODEL)[:, :t], k_new, v_new, ki_new, ssm_new, conv_new


def _prepare_params(l, norm1_w, w_in, conv_w, conv_b, dt_bias, a_log, d_skip, ssd_norm_w, q_norm_w, k_norm_w,
                    idx_k_norm_w, w_br_ssd, w_br_att, w_out, norm2_w, w_gate, w_up, w_down):
    offs = [0]
    for w in IN_WIDTHS:
        offs.append(offs[-1] + w)
    i_z, i_xbc, i_dt, i_q, i_k, i_v, i_qi, i_ki, i_wi, i_gs, i_ga = range(11)
    run = lambda first, last: w_in[l][:, offs[first]:offs[last + 1]].astype(BF16)
    pad = jnp.zeros((D_MODEL, LANES - IDX_DIM - SSD_HEADS - IDX_HEADS), BF16)
    w_perm = jnp.concatenate([run(i_z, i_xbc), run(i_q, i_qi), run(i_gs, i_ga),
                              run(i_ki, i_ki), run(i_dt, i_dt), run(i_wi, i_wi), pad], axis=1)

    def lanes_at(vec, start):
        return jnp.zeros((1, LANES), F32).at[0, start:start + vec.shape[0]].set(vec.astype(F32))

    dtb = lanes_at(dt_bias[l], SM_DT)
    alog = lanes_at(a_log[l], SM_DT)
    head_of_channel = jnp.arange(SSD_INNER) // SSD_HEAD_DIM
    expand = (jnp.arange(LANES)[:, None] == head_of_channel[None, :] + SM_DT).astype(BF16)
    row = lambda v: v.astype(F32).reshape(1, -1)
    return dict(
        norm1_w=row(norm1_w[l]), w_in=w_perm, conv_w=conv_w[l].astype(F32), conv_b=row(conv_b[l]),
        dtb=dtb, dtb_t=dtb.reshape(LANES, 1), alog=alog, alog_t=alog.reshape(LANES, 1),
        dskip_x=row(jnp.repeat(d_skip[l], SSD_HEAD_DIM)), ssd_norm_w=row(ssd_norm_w[l]), expand=expand,
        q_norm_w=row(q_norm_w[l]), k_norm_w=row(k_norm_w[l]),
        idx_k_norm_w=jnp.ones((1, LANES), F32).at[0, SM_KI:SM_KI + IDX_DIM].set(idx_k_norm_w[l].astype(F32)),
        w_br_ssd=w_br_ssd[l].astype(BF16), w_br_att=w_br_att[l].astype(BF16), w_out=w_out[l].astype(BF16),
        norm2_w=row(norm2_w[l]), w_gate=w_gate[l].astype(BF16), w_up=w_up[l].astype(BF16),
        w_down=w_down[l].astype(BF16))


def kernel(x_prompt, x_sample, cache_k, cache_v, cache_kidx, state_ssm, state_conv, meta_tokens, rel_bias, norm1_w,
           w_in, conv_w, conv_b, dt_bias, a_log, d_skip, ssd_norm_w, q_norm_w, k_norm_w, idx_k_norm_w, w_br_ssd,
           w_br_att, w_out, norm2_w, w_gate, w_up, w_down):
    bp, sp, _ = x_prompt.shape
    bs, ts, _ = x_sample.shape
    past = cache_k.shape[2]
    assert w_in.shape[0] == 1
    assert past % BLK == 0 and BLK % CHUNK == 0 and N_META <= CHUNK
    l = 0

    n_sel_p = min(TOPK_MAX, sp // 4)
    n_sel_s = min(TOPK_MAX, (past + ts) // 4)
    conv0 = jnp.zeros((bp, CONV_WIDTH - 1, CONV_DIM), F32)
    ssm0 = jnp.zeros((bp, SSD_HEADS, SSD_HEAD_DIM, SSD_STATE), F32)
    bias_tiles = _bias_tiles(rel_bias.astype(F32))
    p = _prepare_params(l, norm1_w, w_in, conv_w, conv_b, dt_bias, a_log, d_skip, ssd_norm_w, q_norm_w, k_norm_w,
                        idx_k_norm_w, w_br_ssd, w_br_att, w_out, norm2_w, w_gate, w_up, w_down)
    y_prompt, *rest_p = _layer(x_prompt, meta_tokens, conv0, ssm0, None, p, bias_tiles,
                               chunk_off=CHUNK - N_META, n_sel=n_sel_p)
    y_sample, *rest_s = _layer(x_sample, meta_tokens[:0], state_conv[l], state_ssm[l],
                               (cache_k[l], cache_v[l], cache_kidx[l]), p, bias_tiles,
                               chunk_off=0, n_sel=n_sel_s)

    dtypes = (x_prompt.dtype, x_prompt.dtype, x_prompt.dtype, state_ssm.dtype, x_prompt.dtype)
    return (y_prompt, y_sample, *(o[None].astype(dt) for o, dt in zip(rest_p, dtypes)),
            *(o[None].astype(dt) for o, dt in zip(rest_s, dtypes)))
```
